```python
import math
import jax, jax.numpy as jnp
from jax import lax
import numpy as np

D_MODEL = 1024
BATCH = 8
SEQ = 4096
DEPTH = 1
DEC_BATCH = 8
DEC_SEQ = 64
PAST_LEN = 4096

CHUNK = 64
HEAD_DIM = 64
SSM_WIDTH = D_MODEL // 2
SSM_HEADS = SSM_WIDTH // HEAD_DIM
SSM_GROUPS = 2
D_STATE = 128
CONV_W = 4
CONV_DIM = SSM_WIDTH + 2 * SSM_GROUPS * D_STATE
ATTN_WIDTH = D_MODEL - SSM_WIDTH
ATTN_HEADS = ATTN_WIDTH // HEAD_DIM
KV_HEADS = 2
Q_PER_KV = ATTN_HEADS // KV_HEADS
WINDOW = 128
WIN_CHUNKS = WINDOW // CHUNK
D_FF = 256 * math.ceil(8 * D_MODEL / (3 * 256))
IN_COLS = SSM_WIDTH + CONV_DIM + SSM_HEADS + ATTN_WIDTH + 2 * KV_HEADS * HEAD_DIM
V_COL_START = IN_COLS - KV_HEADS * HEAD_DIM
LN_EPS = 1e-5
RMS_EPS = 1e-5
DEEPNORM_ALPHA = (2.0 * DEPTH) ** 0.25
DEEPNORM_BETA = (8.0 * DEPTH) ** -0.25

kernel_name = 'hymba_ssd_swa_sink_alibi_deepnorm_stream_step'


def in_proj_split_points():
    sizes = [SSM_WIDTH, CONV_DIM, SSM_HEADS, ATTN_WIDTH, KV_HEADS * HEAD_DIM]
    return [int(s) for s in np.cumsum(sizes)]


def alibi_slopes():
    return jnp.exp2(-8.0 * jnp.arange(1, ATTN_HEADS + 1, dtype=jnp.float32) / ATTN_HEADS)


def layer_norm(x, g, b):
    xf = x.astype(jnp.float32)
    mu = jnp.mean(xf, axis=-1, keepdims=True)
    var = jnp.mean(jnp.square(xf - mu), axis=-1, keepdims=True)
    out = (xf - mu) * lax.rsqrt(var + LN_EPS) * g.astype(jnp.float32) + b.astype(jnp.float32)
    return out.astype(x.dtype)


def gated_rms_norm(y, z, w):
    g = y * jax.nn.silu(z.astype(jnp.float32))
    g = g.reshape(*g.shape[:-1], SSM_GROUPS, SSM_WIDTH // SSM_GROUPS)
    g = g * lax.rsqrt(jnp.mean(jnp.square(g), axis=-1, keepdims=True) + RMS_EPS)
    return g.reshape(*y.shape) * w.astype(jnp.float32)


def ssd_scan(xs, dt, a, bm, cm, s0):
    bsz, seq = xs.shape[:2]
    cl = min(CHUNK, seq)
    nc = seq // cl
    r = SSM_HEADS // SSM_GROUPS
    xdt = (xs * dt[..., None]).reshape(bsz, nc, cl, SSM_GROUPS, r, HEAD_DIM)
    cum = jnp.cumsum((dt * a).reshape(bsz, nc, cl, SSM_GROUPS, r), axis=2)
    bm = bm.reshape(bsz, nc, cl, SSM_GROUPS, D_STATE)
    cm = cm.reshape(bsz, nc, cl, SSM_GROUPS, D_STATE)
    seg = cum[:, :, :, None] - cum[:, :, None, :]
    causal = jnp.tril(jnp.ones((cl, cl), dtype=bool))[:, :, None, None]
    decay = jnp.exp(jnp.where(causal, seg, -jnp.inf))
    cb = jnp.einsum('bclgn,bcsgn->bclsg', cm, bm)
    y_diag = jnp.einsum('bclsgr,bcsgrp->bclgrp', cb[..., None] * decay, xdt)
    to_end = jnp.exp(cum[:, :, -1:] - cum)
    block_states = jnp.einsum('bclgn,bclgrp->bcgrpn', bm, xdt * to_end[..., None])
    block_decay = jnp.exp(cum[:, :, -1])

    def carry_step(s, inp):
        st, dec = inp
        return dec[..., None, None] * s + st, s

    s_final, s_prev = lax.scan(
        carry_step, s0.reshape(bsz, SSM_GROUPS, r, HEAD_DIM, D_STATE),
        (jnp.moveaxis(block_states, 1, 0), jnp.moveaxis(block_decay, 1, 0)))
    s_prev = jnp.moveaxis(s_prev, 0, 1)
    y_off = jnp.einsum('bclgn,bcgrpn->bclgrp', cm, s_prev) * jnp.exp(cum)[..., None]
    y = (y_diag + y_off).reshape(bsz, seq, SSM_HEADS, HEAD_DIM)
    return y, s_final.reshape(bsz, SSM_HEADS, HEAD_DIM, D_STATE)


def sink_attention(q, k, v, dist, valid, slopes, sinks):
    f32 = jnp.float32
    s = jnp.einsum('...qgrd,...sgd->...grqs', q.astype(f32), k.astype(f32)) * (HEAD_DIM ** -0.5)
    s = s - slopes.reshape(KV_HEADS, Q_PER_KV, 1, 1) * dist
    s = jnp.where(valid, s, -jnp.inf)
    sink = sinks.astype(f32).reshape(KV_HEADS, Q_PER_KV, 1, 1)
    m = jnp.maximum(jnp.max(s, axis=-1, keepdims=True), sink)
    p = jnp.exp(s - m)
    denom = jnp.sum(p, axis=-1, keepdims=True) + jnp.exp(sink - m)
    return jnp.einsum('...grqs,...sgd->...qgrd', p / denom, v.astype(f32))


def band_attention(q, k, v, slopes, sinks):
    bsz, seq = q.shape[:2]
    nc = seq // CHUNK
    band = (WIN_CHUNKS + 1) * CHUNK
    qc = q.reshape(bsz, nc, CHUNK, KV_HEADS, Q_PER_KV, HEAD_DIM)

    def banded(t):
        tp = jnp.pad(t, ((0, 0), (WIN_CHUNKS * CHUNK, 0), (0, 0), (0, 0)))
        tp = tp.reshape(bsz, nc + WIN_CHUNKS, CHUNK, KV_HEADS, HEAD_DIM)
        return jnp.concatenate([tp[:, j:j + nc] for j in range(WIN_CHUNKS + 1)], axis=2)

    kb, vb = banded(k), banded(v)
    qi = jnp.arange(CHUNK)[:, None] + WIN_CHUNKS * CHUNK
    kj = jnp.arange(band)[None, :]
    dist = jnp.abs(qi - kj).astype(jnp.float32)
    key_chunk = jnp.arange(nc)[:, None] - WIN_CHUNKS + kj // CHUNK
    valid = (key_chunk >= 0)[:, None, None, None, :]
    return sink_attention(qc, kb, vb, dist, valid, slopes, sinks)


def window_step_attention(q, kk, vv, slopes, sinks):
    t = q.shape[1]
    dist = jnp.abs(jnp.arange(t)[:, None] + WINDOW - jnp.arange(WINDOW + t)[None, :]).astype(jnp.float32)
    valid = jnp.ones((t, WINDOW + t), dtype=bool)
    return sink_attention(q, kk, vv, dist, valid, slopes, sinks)


def token_mixers(x, conv_hist, ssm_state, win_k, win_v, slopes, w_in, conv_w, conv_b, dt_bias, a_log,
                 d_skip, ssm_norm_w, attn_sinks, w_out):
    f32 = jnp.float32
    bsz, seq = x.shape[:2]
    proj = jnp.einsum('bld,de->ble', x, w_in)
    z, xbc, dt_raw, q, k, v = jnp.split(proj, in_proj_split_points(), axis=-1)
    xbc_hist = jnp.concatenate([conv_hist.astype(xbc.dtype), xbc], axis=1)
    conv = conv_b
    for i in range(CONV_W):
        conv = conv + conv_w[i] * xbc_hist[:, i:i + seq]
    xbc = jax.nn.silu(conv)
    new_conv = xbc_hist[:, -(CONV_W - 1):]
    xs, bm, cm = jnp.split(xbc.astype(f32), [SSM_WIDTH, SSM_WIDTH + SSM_GROUPS * D_STATE], axis=-1)
    xs = xs.reshape(bsz, seq, SSM_HEADS, HEAD_DIM)
    dt = jax.nn.softplus(dt_raw.astype(f32) + dt_bias.astype(f32))
    a = -jnp.exp(a_log.astype(f32))
    y, new_ssm = ssd_scan(xs, dt, a, bm.reshape(bsz, seq, SSM_GROUPS, D_STATE),
                          cm.reshape(bsz, seq, SSM_GROUPS, D_STATE), ssm_state.astype(f32))
    y = y + d_skip.astype(f32)[:, None] * xs
    y = gated_rms_norm(y.reshape(bsz, seq, SSM_WIDTH), z, ssm_norm_w)
    q = q.reshape(bsz, seq, KV_HEADS, Q_PER_KV, HEAD_DIM)
    k = k.reshape(bsz, seq, KV_HEADS, HEAD_DIM)
    v = v.reshape(bsz, seq, KV_HEADS, HEAD_DIM)
    if win_k is None:
        o = band_attention(q, k, v, slopes, attn_sinks)
        new_k, new_v = k[:, -WINDOW:], v[:, -WINDOW:]
    else:
        kk = jnp.concatenate([win_k.astype(k.dtype), k], axis=1)
        vv = jnp.concatenate([win_v.astype(v.dtype), v], axis=1)
        o = window_step_attention(q, kk, vv, slopes, attn_sinks)
        new_k, new_v = kk[:, -WINDOW:], vv[:, -WINDOW:]
    o = o.reshape(bsz, seq, ATTN_WIDTH)
    mixed = jnp.concatenate([y.astype(x.dtype), o.astype(x.dtype)], axis=-1)
    out = jnp.einsum('ble,ed->bld', mixed, w_out)
    return out, new_conv, new_ssm, new_k, new_v


def swiglu(h, w_gate, w_up, w_down):
    g = jnp.einsum('bld,df->blf', h, w_gate)
    u = jnp.einsum('bld,df->blf', h, w_up)
    return jnp.einsum('blf,fd->bld', jax.nn.silu(g) * u, w_down)


def trunk_layer(x, conv_hist, ssm_state, win_k, win_v, slopes, w_in, conv_w, conv_b, dt_bias, a_log,
                d_skip, ssm_norm_w, attn_sinks, w_out, ln1_g, ln1_b, w_gate, w_up, w_down, ln2_g, ln2_b):
    mix, new_conv, new_ssm, new_k, new_v = token_mixers(
        x, conv_hist, ssm_state, win_k, win_v, slopes, w_in, conv_w, conv_b, dt_bias, a_log,
        d_skip, ssm_norm_w, attn_sinks, w_out)
    h = layer_norm(DEEPNORM_ALPHA * x + mix, ln1_g, ln1_b)
    y = layer_norm(DEEPNORM_ALPHA * h + swiglu(h, w_gate, w_up, w_down), ln2_g, ln2_b)
    return y, new_conv, new_ssm, new_k, new_v


def setup_inputs(seed: int = 0) -> dict:
    key = jax.random.key(seed)
    ks = jax.random.split(key, 24)
    f32 = jnp.float32

    def nrm(k, shape, scale):
        return scale * jax.random.normal(k, shape, f32)

    x_prompt = nrm(ks[0], (BATCH, SEQ, D_MODEL), 1.0)
    x_sample = nrm(ks[1], (DEC_BATCH, DEC_SEQ, D_MODEL), 1.0)
    state_conv = nrm(ks[2], (DEPTH, DEC_BATCH, CONV_W - 1, CONV_DIM), 1.0)
    state_ssm = nrm(ks[3], (DEPTH, DEC_BATCH, SSM_HEADS, HEAD_DIM, D_STATE), 0.1)
    cache_k = nrm(ks[4], (DEPTH, DEC_BATCH, WINDOW, KV_HEADS, HEAD_DIM), 1.0)
    cache_v = nrm(ks[5], (DEPTH, DEC_BATCH, WINDOW, KV_HEADS, HEAD_DIM), 1.0)
    w_in = nrm(ks[6], (DEPTH, D_MODEL, IN_COLS), D_MODEL ** -0.5)
    w_in = w_in * jnp.where(jnp.arange(IN_COLS) >= V_COL_START, DEEPNORM_BETA, 1.0).astype(f32)
    conv_w = nrm(ks[7], (DEPTH, CONV_W, CONV_DIM), CONV_W ** -0.5)
    conv_b = nrm(ks[8], (DEPTH, CONV_DIM), 0.01)
    dt0 = jnp.exp(jax.random.uniform(ks[9], (DEPTH, SSM_HEADS), f32, math.log(1e-3), math.log(1e-1)))
    dt_bias = dt0 + jnp.log(-jnp.expm1(-dt0))
    a_log = jnp.log(jax.random.uniform(ks[10], (DEPTH, SSM_HEADS), f32, 1.0, 16.0))
    d_skip = 1.0 + nrm(ks[11], (DEPTH, SSM_HEADS), 0.1)
    ssm_norm_w = 1.0 + nrm(ks[12], (DEPTH, SSM_WIDTH), 0.02)
    attn_sinks = nrm(ks[13], (DEPTH, ATTN_HEADS), 0.5)
    w_out = nrm(ks[14], (DEPTH, D_MODEL, D_MODEL), DEEPNORM_BETA * D_MODEL ** -0.5)
    ln1_g = 1.0 + nrm(ks[15], (DEPTH, D_MODEL), 0.02)
    ln1_b = nrm(ks[16], (DEPTH, D_MODEL), 0.02)
    w_gate = nrm(ks[17], (DEPTH, D_MODEL, D_FF), D_MODEL ** -0.5)
    w_up = nrm(ks[18], (DEPTH, D_MODEL, D_FF), D_MODEL ** -0.5)
    w_down = nrm(ks[19], (DEPTH, D_FF, D_MODEL), DEEPNORM_BETA * D_FF ** -0.5)
    ln2_g = 1.0 + nrm(ks[20], (DEPTH, D_MODEL), 0.02)
    ln2_b = nrm(ks[21], (DEPTH, D_MODEL), 0.02)
    return {'x_prompt': x_prompt, 'x_sample': x_sample, 'state_conv': state_conv, 'state_ssm': state_ssm,
            'cache_k': cache_k, 'cache_v': cache_v, 'w_in': w_in, 'conv_w': conv_w, 'conv_b': conv_b,
            'dt_bias': dt_bias, 'a_log': a_log, 'd_skip': d_skip, 'ssm_norm_w': ssm_norm_w,
            'attn_sinks': attn_sinks, 'w_out': w_out, 'ln1_g': ln1_g, 'ln1_b': ln1_b, 'w_gate': w_gate,
            'w_up': w_up, 'w_down': w_down, 'ln2_g': ln2_g, 'ln2_b': ln2_b}


def reference(x_prompt, x_sample, state_conv, state_ssm, cache_k, cache_v, w_in, conv_w, conv_b, dt_bias,
              a_log, d_skip, ssm_norm_w, attn_sinks, w_out, ln1_g, ln1_b, w_gate, w_up, w_down, ln2_g, ln2_b):
    slopes = alibi_slopes()
    bsz = x_prompt.shape[0]
    h_p, h_s = x_prompt, x_sample
    new_p, new_s = [], []
    for layer in range(DEPTH):
        lw = (w_in[layer], conv_w[layer], conv_b[layer], dt_bias[layer], a_log[layer], d_skip[layer],
              ssm_norm_w[layer], attn_sinks[layer], w_out[layer], ln1_g[layer], ln1_b[layer],
              w_gate[layer], w_up[layer], w_down[layer], ln2_g[layer], ln2_b[layer])
        zero_conv = jnp.zeros((bsz, CONV_W - 1, CONV_DIM), x_prompt.dtype)
        zero_ssm = jnp.zeros((bsz, SSM_HEADS, HEAD_DIM, D_STATE), jnp.float32)
        h_p, *st_p = trunk_layer(h_p, zero_conv, zero_ssm, None, None, slopes, *lw)
        h_s, *st_s = trunk_layer(h_s, state_conv[layer], state_ssm[layer], cache_k[layer], cache_v[layer],
                                 slopes, *lw)
        new_p.append(st_p)
        new_s.append(st_s)
    conv_p = jnp.stack([s[0] for s in new_p])
    ssm_p = jnp.stack([s[1] for s in new_p])
    k_p = jnp.stack([s[2] for s in new_p])
    v_p = jnp.stack([s[3] for s in new_p])
    conv_s = jnp.stack([s[0] for s in new_s])
    ssm_s = jnp.stack([s[1] for s in new_s])
    k_s = jnp.stack([s[2] for s in new_s])
    v_s = jnp.stack([s[3] for s in new_s])
    return (h_p, h_s, conv_p, ssm_p, k_p, v_p, conv_s, ssm_s, k_s, v_s)
```

```python
import functools
import math

import jax
import jax.numpy as jnp
import numpy as np
from jax import lax
from jax.experimental import pallas as pl
from jax.experimental.pallas import tpu as pltpu

D_MODEL = 1024
CHUNK = 64
HEAD_DIM = 64
SSM_WIDTH = 512
SSM_HEADS = 8
SSM_GROUPS = 2
D_STATE = 128
CONV_W = 4
CONV_DIM = SSM_WIDTH + 2 * SSM_GROUPS * D_STATE
ATTN_WIDTH = 512
ATTN_HEADS = 8
KV_HEADS = 2
WINDOW = 128
D_FF = 2816
LN_EPS = 1e-5
RMS_EPS = 1e-5
DEEPNORM_ALPHA = 2.0 ** 0.25

LANES = 128
GROUP_W = SSM_WIDTH // SSM_GROUPS
KV_W = KV_HEADS * HEAD_DIM
BAND = WINDOW + CHUNK
HEADS_PER_STACK = 4
STACK_ROWS = HEADS_PER_STACK * CHUNK
COL_Z, COL_XBC, COL_Q, COL_K, COL_V, COL_END = 0, 512, 1536, 2048, 2176, 2304
STACK_A_HEADS = (0, 2, 5, 7)
STACK_B_HEADS = (1, 3, 4, 6)
EXPAND_K = 128

VMEM_LIMIT = 56 * 1024 * 1024

F32 = jnp.float32
BF16 = jnp.bfloat16


def _dot(a, b):
    return jnp.dot(a, b, preferred_element_type=F32)


def _dot_nt(a, b):
    return lax.dot_general(a, b, (((1,), (1,)), ((), ())), preferred_element_type=F32)


def _dot_tn(a, b):
    return lax.dot_general(a, b, (((0,), (0,)), ((), ())), preferred_element_type=F32)


def _sigmoid(x):
    return 1.0 / (1.0 + jnp.exp(-x))


def _silu(x):
    return x * _sigmoid(x)


def _layer_norm(x, g, b):
    mu = jnp.mean(x, axis=-1, keepdims=True)
    xc = x - mu
    var = jnp.mean(xc * xc, axis=-1, keepdims=True)
    return xc * lax.rsqrt(var + LN_EPS) * g + b


def _split3(x):
    hi = x.astype(BF16).astype(F32)
    r = x - hi
    mid = r.astype(BF16).astype(F32)
    lo = r - mid
    return hi, mid, lo


def _mixer_kernel(hist_all_valid, tile, n_tiles,
                  x_ref, conv0_ref, ssm0_ref, k0_ref, v0_ref,
                  w_in_ref, w_dt_ref, conv_w_ref, conv_b_ref, dt_bias_ref, a_log_ref, d_ref, norm_w_ref,
                  expand_ref, bias_ref, sink_ref, w_out_ref, ln_g_ref, ln_b_ref,
                  h_ref, conv_out_ref, ssm_out_ref, k_out_ref, v_out_ref,
                  convbuf, st_ref, kf_ref, vf_ref, kb_ref, ksb_ref, vb_ref, vsb_ref, ybuf, mixed_ref):
    t = pl.program_id(1)
    n_chunks = tile // CHUNK

    @pl.when(t == 0)
    def _():
        convbuf[0:8, :] = jnp.zeros((8, CONV_DIM), F32)
        convbuf[8 - (CONV_W - 1):8, :] = conv0_ref[0]
        st_ref[...] = ssm0_ref[0]
        k0 = k0_ref[0]
        v0 = v0_ref[0]
        kf_ref[0:WINDOW, :] = k0
        vf_ref[0:WINDOW, :] = v0
        kb_ref[0:WINDOW, :] = k0.astype(BF16)
        vb_ref[0:WINDOW, :] = v0.astype(BF16)
        ksb_ref[0:WINDOW, :] = pltpu.roll(k0, HEAD_DIM, axis=1).astype(BF16)
        vsb_ref[0:WINDOW, :] = pltpu.roll(v0, HEAD_DIM, axis=1).astype(BF16)

    x = x_ref[0]
    xb = x.astype(BF16)

    z = _dot(xb, w_in_ref[:, COL_Z:COL_XBC])
    convbuf[8:8 + tile, :] = _dot(xb, w_in_ref[:, COL_XBC:COL_Q])
    q = _dot(xb, w_in_ref[:, COL_Q:COL_K])
    kv = _dot(xb, w_in_ref[:, COL_K:COL_END])
    dt_raw = _dot(xb, w_dt_ref[...])

    k_new = kv[:, :KV_W]
    v_new = kv[:, KV_W:]
    kf_ref[WINDOW:WINDOW + tile, :] = k_new
    vf_ref[WINDOW:WINDOW + tile, :] = v_new
    kb_ref[WINDOW:WINDOW + tile, :] = k_new.astype(BF16)
    vb_ref[WINDOW:WINDOW + tile, :] = v_new.astype(BF16)
    ksb_ref[WINDOW:WINDOW + tile, :] = pltpu.roll(k_new, HEAD_DIM, axis=1).astype(BF16)
    vsb_ref[WINDOW:WINDOW + tile, :] = pltpu.roll(v_new, HEAD_DIM, axis=1).astype(BF16)

    conv = conv_b_ref[...]
    for i in range(CONV_W):
        conv = conv + conv_w_ref[i:i + 1, :] * convbuf[8 - (CONV_W - 1) + i:8 - (CONV_W - 1) + i + tile, :]
    xc = _silu(conv)
    conv_tail = convbuf[8 + tile - (CONV_W - 1):8 + tile, :]
    conv_out_ref[0] = conv_tail
    convbuf[8 - (CONV_W - 1):8, :] = conv_tail
    xs = xc[:, :SSM_WIDTH]
    bm = xc[:, SSM_WIDTH:SSM_WIDTH + GROUP_W].astype(BF16)
    cm = xc[:, SSM_WIDTH + GROUP_W:].astype(BF16)

    dt_in = dt_raw + dt_bias_ref[...]
    dt = jnp.maximum(dt_in, 0.0) + jnp.log1p(jnp.exp(-jnp.abs(dt_in)))
    da = dt * (-jnp.exp(a_log_ref[...]))
    row_in_chunk = lax.broadcasted_iota(jnp.int32, (tile, LANES), 0) % CHUNK
    cum = da
    step = 1
    while step < CHUNK:
        shifted = pltpu.roll(cum, step, axis=0)
        cum = cum + jnp.where(row_in_chunk >= step, shifted, 0.0)
        step *= 2

    lane = lax.broadcasted_iota(jnp.int32, (tile, LANES), 1)
    c_hi, c_mid, c_lo = _split3(cum)
    d_hi, d_mid, d_lo = _split3(dt)
    parts = (c_hi, c_mid, c_lo, d_hi, d_mid, d_lo)
    packed = jnp.zeros((tile, LANES), F32)
    for i, part in enumerate(parts):
        packed = jnp.where((lane >= SSM_HEADS * i) & (lane < SSM_HEADS * (i + 1)), part, packed)
    expanded = _dot(packed.astype(BF16), expand_ref[...])
    cum_x = expanded[:, :SSM_WIDTH]
    dt_x = expanded[:, SSM_WIDTH:]
    xdt = xs * dt_x

    li = lax.broadcasted_iota(jnp.int32, (CHUNK, GROUP_W), 0)
    si = lax.broadcasted_iota(jnp.int32, (CHUNK, GROUP_W), 1) % CHUNK
    diag_mask = li == si
    causal_mask = si <= li
    bd_mask = (lax.broadcasted_iota(jnp.int32, (GROUP_W, GROUP_W), 0) // CHUNK
               == lax.broadcasted_iota(jnp.int32, (GROUP_W, GROUP_W), 1) // CHUNK)
    d_row = d_ref[...]
    for ci in range(n_chunks):
        r0 = ci * CHUNK
        for g in range(SSM_GROUPS):
            c0 = g * GROUP_W
            colb = cum_x[r0:r0 + CHUNK, c0:c0 + GROUP_W]
            rowb = jnp.sum(jnp.where(diag_mask, colb, 0.0), axis=0, keepdims=True)
            lmat = jnp.exp(jnp.where(causal_mask, colb - rowb, -jnp.inf))
            b_g = bm[r0:r0 + CHUNK, g * D_STATE:(g + 1) * D_STATE]
            c_g = cm[r0:r0 + CHUNK, g * D_STATE:(g + 1) * D_STATE]
            cb = _dot_nt(c_g, jnp.concatenate([b_g] * HEADS_PER_STACK, axis=0))
            m_g = (cb * lmat).astype(BF16)
            xdt_g = xdt[r0:r0 + CHUNK, c0:c0 + GROUP_W]
            xdt_b = xdt_g.astype(BF16)
            bd = jnp.where(bd_mask, jnp.concatenate([xdt_b] * HEADS_PER_STACK, axis=0), jnp.zeros((), BF16))
            y_diag = _dot(m_g, bd)
            s_prev = st_ref[:, c0:c0 + GROUP_W]
            y_off = _dot(c_g, s_prev.astype(BF16)) * jnp.exp(colb)
            last = colb[CHUNK - 1:CHUNK, :]
            xw = (xdt_g * jnp.exp(last - colb)).astype(BF16)
            st_ref[:, c0:c0 + GROUP_W] = jnp.exp(last) * s_prev + _dot_tn(b_g, xw)
            xs_g = xs[r0:r0 + CHUNK, c0:c0 + GROUP_W]
            ybuf[r0:r0 + CHUNK, c0:c0 + GROUP_W] = y_diag + y_off + d_row[:, c0:c0 + GROUP_W] * xs_g
    ssm_out_ref[0] = st_ref[...]

    gated = ybuf[...] * _silu(z)
    for g in range(SSM_GROUPS):
        c0 = g * GROUP_W
        gg = gated[:, c0:c0 + GROUP_W]
        ms = jnp.mean(gg * gg, axis=-1, keepdims=True)
        mixed_ref[:, c0:c0 + GROUP_W] = (gg * lax.rsqrt(ms + RMS_EPS) * norm_w_ref[:, c0:c0 + GROUP_W]).astype(BF16)

    lane_q = lax.broadcasted_iota(jnp.int32, (CHUNK, LANES), 1)
    low_half = lane_q < HEAD_DIM
    for ci in range(n_chunks):
        r0 = ci * CHUNK
        if hist_all_valid:
            variant = WINDOW // CHUNK
        else:
            variant = jnp.minimum(t * n_chunks + ci, WINDOW // CHUNK)
        q_c = q[r0:r0 + CHUNK, :]
        blocks = [q_c[:, j * LANES:(j + 1) * LANES] for j in range(ATTN_HEADS // 2)]
        evens = [jnp.where(low_half, blk, 0.0).astype(BF16) for blk in blocks]
        odds = [jnp.where(low_half, 0.0, blk).astype(BF16) for blk in blocks]
        q_a = jnp.concatenate([evens[0], evens[1], odds[2], odds[3]], axis=0)
        q_b = jnp.concatenate([odds[0], odds[1], evens[2], evens[3]], axis=0)
        outs = []
        for s_idx, (q_s, k_ref, v_ref) in enumerate(((q_a, kb_ref, vb_ref), (q_b, ksb_ref, vsb_ref))):
            kband = k_ref[r0:r0 + BAND, :]
            vband = v_ref[r0:r0 + BAND, :]
            s = _dot_nt(q_s, kband) + bias_ref[variant, s_idx]
            sink = sink_ref[s_idx]
            m = jnp.maximum(jnp.max(s, axis=-1, keepdims=True), sink)
            p = jnp.exp(s - m)
            denom = jnp.sum(p, axis=-1, keepdims=True) + jnp.exp(sink - m)
            outs.append(_dot(p.astype(BF16), vband) / denom)
        o_a, o_b = outs
        sel = [(o_a, o_b), (o_a, o_b), (o_b, o_a), (o_b, o_a)]
        for j in range(ATTN_HEADS // 2):
            ev, od = sel[j]
            blk = jnp.where(low_half, ev[j * CHUNK:(j + 1) * CHUNK], od[j * CHUNK:(j + 1) * CHUNK])
            mixed_ref[r0:r0 + CHUNK, SSM_WIDTH + j * LANES:SSM_WIDTH + (j + 1) * LANES] = blk.astype(BF16)

    k_tail = kf_ref[tile:tile + WINDOW, :]
    v_tail = vf_ref[tile:tile + WINDOW, :]
    k_out_ref[0] = k_tail
    v_out_ref[0] = v_tail
    kb_tail, ksb_tail = kb_ref[tile:tile + WINDOW, :], ksb_ref[tile:tile + WINDOW, :]
    vb_tail, vsb_tail = vb_ref[tile:tile + WINDOW, :], vsb_ref[tile:tile + WINDOW, :]
    kf_ref[0:WINDOW, :] = k_tail
    vf_ref[0:WINDOW, :] = v_tail
    kb_ref[0:WINDOW, :] = kb_tail
    ksb_ref[0:WINDOW, :] = ksb_tail
    vb_ref[0:WINDOW, :] = vb_tail
    vsb_ref[0:WINDOW, :] = vsb_tail

    mix = _dot(mixed_ref[...], w_out_ref[...])
    h_ref[0] = _layer_norm(DEEPNORM_ALPHA * x + mix, ln_g_ref[...], ln_b_ref[...])


def _ffn_kernel(f_chunk, h_ref, w_gate_ref, w_up_ref, w_down_ref, ln_g_ref, ln_b_ref, y_ref):
    h = h_ref[...]
    hb = h.astype(BF16)
    acc = DEEPNORM_ALPHA * h
    for c in range(D_FF // f_chunk):
        c0 = c * f_chunk
        gate = _dot(hb, w_gate_ref[:, c0:c0 + f_chunk])
        up = _dot(hb, w_up_ref[:, c0:c0 + f_chunk])
        act = (_silu(gate) * up).astype(BF16)
        acc = acc + _dot(act, w_down_ref[c0:c0 + f_chunk, :])
    y_ref[...] = _layer_norm(acc, ln_g_ref[...], ln_b_ref[...])


def _const_spec(shape):
    zeros = (0,) * len(shape)
    return pl.BlockSpec(shape, lambda *_: zeros)


def _attention_tables(attn_sinks):
    slopes = np.exp2(-8.0 * np.arange(1, ATTN_HEADS + 1, dtype=np.float64) / ATTN_HEADS).astype(np.float32)
    qi = np.arange(CHUNK)[:, None] + WINDOW
    kj = np.arange(BAND)[None, :]
    dist = np.abs(qi - kj).astype(np.float32)
    key_chunk = kj // CHUNK
    tables = np.zeros((WINDOW // CHUNK + 1, 2, STACK_ROWS, BAND), np.float32)
    for variant in range(WINDOW // CHUNK + 1):
        valid = key_chunk >= (WINDOW // CHUNK - variant)
        for s_idx, heads in enumerate((STACK_A_HEADS, STACK_B_HEADS)):
            for i, h in enumerate(heads):
                tables[variant, s_idx, i * CHUNK:(i + 1) * CHUNK] = np.where(valid, -slopes[h] * dist, -np.inf)
    order = np.array(STACK_A_HEADS + STACK_B_HEADS)
    sink_rows = jnp.repeat(attn_sinks.astype(F32)[order], CHUNK).reshape(2, STACK_ROWS, 1)
    return jnp.asarray(tables), sink_rows


def _expansion_matrix():
    e = np.zeros((EXPAND_K, 2 * SSM_WIDTH), np.float32)
    for part in range(6):
        for h in range(SSM_HEADS):
            c0 = (part // 3) * SSM_WIDTH + h * HEAD_DIM
            e[part * SSM_HEADS + h, c0:c0 + HEAD_DIM] = 1.0
    return jnp.asarray(e, BF16)


def _mixer_call(x, conv0, ssm0_t, k0, v0, weights, hist_all_valid, tile):
    bsz, seq, _ = x.shape
    n_tiles = seq // tile
    kernel = functools.partial(_mixer_kernel, hist_all_valid, tile, n_tiles)
    per_batch = lambda shape: pl.BlockSpec((1,) + shape, lambda b, t: (b, 0, 0))
    in_specs = [pl.BlockSpec((1, tile, D_MODEL), lambda b, t: (b, t, 0)),
                per_batch((CONV_W - 1, CONV_DIM)), per_batch((D_STATE, SSM_WIDTH)),
                per_batch((WINDOW, KV_W)), per_batch((WINDOW, KV_W))]
    in_specs += [_const_spec(w.shape) for w in weights]
    out_shape = (jax.ShapeDtypeStruct((bsz, seq, D_MODEL), F32),
                 jax.ShapeDtypeStruct((bsz, CONV_W - 1, CONV_DIM), F32),
                 jax.ShapeDtypeStruct((bsz, D_STATE, SSM_WIDTH), F32),
                 jax.ShapeDtypeStruct((bsz, WINDOW, KV_W), F32),
                 jax.ShapeDtypeStruct((bsz, WINDOW, KV_W), F32))
    out_specs = (pl.BlockSpec((1, tile, D_MODEL), lambda b, t: (b, t, 0)),
                 per_batch((CONV_W - 1, CONV_DIM)), per_batch((D_STATE, SSM_WIDTH)),
                 per_batch((WINDOW, KV_W)), per_batch((WINDOW, KV_W)))
    scratch = [pltpu.VMEM((8 + tile, CONV_DIM), F32),
               pltpu.VMEM((D_STATE, SSM_WIDTH), F32),
               pltpu.VMEM((WINDOW + tile, KV_W), F32), pltpu.VMEM((WINDOW + tile, KV_W), F32),
               pltpu.VMEM((WINDOW + tile, KV_W), BF16), pltpu.VMEM((WINDOW + tile, KV_W), BF16),
               pltpu.VMEM((WINDOW + tile, KV_W), BF16), pltpu.VMEM((WINDOW + tile, KV_W), BF16),
               pltpu.VMEM((tile, SSM_WIDTH), F32),
               pltpu.VMEM((tile, D_MODEL), BF16)]
    return pl.pallas_call(
        kernel, grid=(bsz, n_tiles), in_specs=in_specs, out_specs=out_specs, out_shape=out_shape,
        scratch_shapes=scratch, name="mixer_prompt" if not hist_all_valid else "mixer_sample",
        compiler_params=pltpu.CompilerParams(dimension_semantics=("arbitrary", "arbitrary"),
                                             vmem_limit_bytes=VMEM_LIMIT),
    )(x, conv0, ssm0_t, k0, v0, *weights)


def _ffn_call(h2d, w_gate, w_up, w_down, ln_g, ln_b, tile, name):
    n_tok = h2d.shape[0]
    return pl.pallas_call(
        functools.partial(_ffn_kernel, 256),
        grid=(n_tok // tile,),
        in_specs=[pl.BlockSpec((tile, D_MODEL), lambda i: (i, 0)),
                  _const_spec(w_gate.shape), _const_spec(w_up.shape), _const_spec(w_down.shape),
                  _const_spec(ln_g.shape), _const_spec(ln_b.shape)],
        out_specs=pl.BlockSpec((tile, D_MODEL), lambda i: (i, 0)),
        out_shape=jax.ShapeDtypeStruct((n_tok, D_MODEL), F32),
        name=name,
        compiler_params=pltpu.CompilerParams(dimension_semantics=("arbitrary",), vmem_limit_bytes=VMEM_LIMIT),
    )(h2d, w_gate, w_up, w_down, ln_g, ln_b)


def _state_to_kernel_layout(ssm):
    b = ssm.shape[0]
    return jnp.transpose(ssm, (0, 3, 1, 2)).reshape(b, D_STATE, SSM_WIDTH)


def _state_from_kernel_layout(ssm_t):
    b = ssm_t.shape[0]
    return jnp.transpose(ssm_t.reshape(b, D_STATE, SSM_HEADS, HEAD_DIM), (0, 2, 3, 1))


def kernel(x_prompt, x_sample, state_conv, state_ssm, cache_k, cache_v, w_in, conv_w, conv_b, dt_bias, a_log,
           d_skip, ssm_norm_w, attn_sinks, w_out, ln1_g, ln1_b, w_gate, w_up, w_down, ln2_g, ln2_b):
    w = w_in[0]
    split = np.cumsum([SSM_WIDTH, CONV_DIM, SSM_HEADS, ATTN_WIDTH, KV_W])
    w_z, w_xbc, w_dt, w_q, w_k, w_v = jnp.split(w, split, axis=1)
    w_in_r = jnp.concatenate([w_z, w_xbc, w_q * (HEAD_DIM ** -0.5), w_k, w_v], axis=1).astype(BF16)
    rep = LANES // SSM_HEADS
    w_dt_rep = jnp.tile(w_dt, (1, rep)).astype(BF16)
    dt_bias_rep = jnp.tile(dt_bias[0].astype(F32), rep).reshape(1, LANES)
    a_log_rep = jnp.tile(a_log[0].astype(F32), rep).reshape(1, LANES)
    d_row = jnp.repeat(d_skip[0].astype(F32), HEAD_DIM).reshape(1, SSM_WIDTH)
    bias_tables, sink_rows = _attention_tables(attn_sinks[0])
    row = lambda v: v.astype(F32).reshape(1, -1)
    mixer_weights = (w_in_r, w_dt_rep, conv_w[0].astype(F32), row(conv_b[0]), dt_bias_rep, a_log_rep, d_row,
                     row(ssm_norm_w[0]), _expansion_matrix(), bias_tables, sink_rows,
                     w_out[0].astype(BF16), row(ln1_g[0]), row(ln1_b[0]))
    ffn_weights = (w_gate[0].astype(BF16), w_up[0].astype(BF16), w_down[0].astype(BF16), row(ln2_g[0]), row(ln2_b[0]))

    bsz, seq, _ = x_prompt.shape
    dbsz, dseq, _ = x_sample.shape

    zeros = lambda *shape: jnp.zeros(shape, F32)
    h_p, conv_p, ssm_p, k_p, v_p = _mixer_call(
        x_prompt, zeros(bsz, CONV_W - 1, CONV_DIM), zeros(bsz, D_STATE, SSM_WIDTH),
        zeros(bsz, WINDOW, KV_W), zeros(bsz, WINDOW, KV_W), mixer_weights, False, 256)
    y_p = _ffn_call(h_p.reshape(bsz * seq, D_MODEL), *ffn_weights, 512, "ffn_prompt").reshape(bsz, seq, D_MODEL)

    h_s, conv_s, ssm_s, k_s, v_s = _mixer_call(
        x_sample, state_conv[0].astype(F32), _state_to_kernel_layout(state_ssm[0].astype(F32)),
        cache_k[0].astype(F32).reshape(dbsz, WINDOW, KV_W), cache_v[0].astype(F32).reshape(dbsz, WINDOW, KV_W),
        mixer_weights, True, dseq)
    y_s = _ffn_call(h_s.reshape(dbsz * dseq, D_MODEL), *ffn_weights, 512, "ffn_sample").reshape(dbsz, dseq, D_MODEL)

    kv_shape = lambda a: a.reshape(1, a.shape[0], WINDOW, KV_HEADS, HEAD_DIM)
    return (y_p, y_s,
            conv_p[None], _state_from_kernel_layout(ssm_p)[None], kv_shape(k_p), kv_shape(v_p),
            conv_s[None], _state_from_kernel_layout(ssm_s)[None], kv_shape(k_s), kv_shape(v_s))
```

```python
import functools
import math

import jax
import jax.numpy as jnp
import numpy as np
from jax import lax
from jax.experimental import pallas as pl
from jax.experimental.pallas import tpu as pltpu

D_MODEL = 1024
CHUNK = 64
HEAD_DIM = 64
SSM_WIDTH = 512
SSM_HEADS = 8
SSM_GROUPS = 2
D_STATE = 128
CONV_W = 4
CONV_DIM = SSM_WIDTH + 2 * SSM_GROUPS * D_STATE
ATTN_WIDTH = 512
ATTN_HEADS = 8
KV_HEADS = 2
WINDOW = 128
D_FF = 2816
LN_EPS = 1e-5
RMS_EPS = 1e-5
DEEPNORM_ALPHA = 2.0 ** 0.25
LOG2E = math.log2(math.e)

LANES = 128
GROUP_W = SSM_WIDTH // SSM_GROUPS
KV_W = KV_HEADS * HEAD_DIM
BAND = WINDOW + CHUNK
HEADS_PER_STACK = 4
STACK_ROWS = HEADS_PER_STACK * CHUNK
COL_Z, COL_XBC, COL_Q, COL_K, COL_V, COL_END = 0, 512, 1536, 2048, 2176, 2304
STACK_A_HEADS = (0, 2, 5, 7)
STACK_B_HEADS = (1, 3, 4, 6)
EXPAND_K = 128

VMEM_LIMIT = 56 * 1024 * 1024

F32 = jnp.float32
BF16 = jnp.bfloat16


def _dot(a, b):
    return jnp.dot(a, b, preferred_element_type=F32)


def _dot_nt(a, b):
    return lax.dot_general(a, b, (((1,), (1,)), ((), ())), preferred_element_type=F32)


def _dot_tn(a, b):
    return lax.dot_general(a, b, (((0,), (0,)), ((), ())), preferred_element_type=F32)


def _sigmoid(x):
    return 1.0 / (1.0 + jnp.exp(-x))


def _silu(x):
    return x * _sigmoid(x)


def _layer_norm(x, g, b):
    mu = jnp.mean(x, axis=-1, keepdims=True)
    xc = x - mu
    var = jnp.mean(xc * xc, axis=-1, keepdims=True)
    return xc * lax.rsqrt(var + LN_EPS) * g + b


def _split3(x):
    hi = x.astype(BF16).astype(F32)
    r = x - hi
    mid = r.astype(BF16).astype(F32)
    lo = r - mid
    return hi, mid, lo


def _mixer_kernel(hist_all_valid, tile, n_tiles,
                  x_ref, conv0_ref, ssm0_ref, k0_ref, v0_ref,
                  w_in_ref, w_dt_ref, conv_w_ref, conv_b_ref, dt_bias_ref, a_log_ref, d_ref, norm_w_ref,
                  expand_ref, bias_ref, sink_ref, w_out_ref, ln_g_ref, ln_b_ref,
                  h_ref, conv_out_ref, ssm_out_ref, k_out_ref, v_out_ref,
                  convbuf, st_ref, kf_ref, vf_ref, kb_ref, ksb_ref, vb_ref, vsb_ref, ybuf, mixed_ref):
    t = pl.program_id(1)
    n_chunks = tile // CHUNK

    @pl.when(t == 0)
    def _():
        convbuf[0:8, :] = jnp.zeros((8, CONV_DIM), F32)
        convbuf[8 - (CONV_W - 1):8, :] = conv0_ref[0]
        st_ref[...] = ssm0_ref[0]
        k0 = k0_ref[0]
        v0 = v0_ref[0]
        kf_ref[0:WINDOW, :] = k0
        vf_ref[0:WINDOW, :] = v0
        kb_ref[0:WINDOW, :] = k0.astype(BF16)
        vb_ref[0:WINDOW, :] = v0.astype(BF16)
        ksb_ref[0:WINDOW, :] = pltpu.roll(k0, HEAD_DIM, axis=1).astype(BF16)
        vsb_ref[0:WINDOW, :] = pltpu.roll(v0, HEAD_DIM, axis=1).astype(BF16)

    x = x_ref[0]
    xb = x.astype(BF16)

    z = _dot(xb, w_in_ref[:, COL_Z:COL_XBC])
    convbuf[8:8 + tile, :] = _dot(xb, w_in_ref[:, COL_XBC:COL_Q])
    q = _dot(xb, w_in_ref[:, COL_Q:COL_K])
    kv = _dot(xb, w_in_ref[:, COL_K:COL_END])
    dt_raw = _dot(xb, w_dt_ref[...])

    k_new = kv[:, :KV_W]
    v_new = kv[:, KV_W:]
    kf_ref[WINDOW:WINDOW + tile, :] = k_new
    vf_ref[WINDOW:WINDOW + tile, :] = v_new
    kb_ref[WINDOW:WINDOW + tile, :] = k_new.astype(BF16)
    vb_ref[WINDOW:WINDOW + tile, :] = v_new.astype(BF16)
    ksb_ref[WINDOW:WINDOW + tile, :] = pltpu.roll(k_new, HEAD_DIM, axis=1).astype(BF16)
    vsb_ref[WINDOW:WINDOW + tile, :] = pltpu.roll(v_new, HEAD_DIM, axis=1).astype(BF16)

    conv = conv_b_ref[...]
    for i in range(CONV_W):
        conv = conv + conv_w_ref[i:i + 1, :] * convbuf[8 - (CONV_W - 1) + i:8 - (CONV_W - 1) + i + tile, :]
    xc = _silu(conv)
    conv_tail = convbuf[8 + tile - (CONV_W - 1):8 + tile, :]
    conv_out_ref[0] = conv_tail
    convbuf[8 - (CONV_W - 1):8, :] = conv_tail
    xs = xc[:, :SSM_WIDTH]
    bm = xc[:, SSM_WIDTH:SSM_WIDTH + GROUP_W].astype(BF16)
    cm = xc[:, SSM_WIDTH + GROUP_W:].astype(BF16)

    dt_in = dt_raw + dt_bias_ref[...]
    dt = jnp.maximum(dt_in, 0.0) + jnp.log1p(jnp.exp(-jnp.abs(dt_in)))
    da = dt * (-LOG2E * jnp.exp(a_log_ref[...]))
    row_in_chunk = lax.broadcasted_iota(jnp.int32, (tile, LANES), 0) % CHUNK
    cum = da
    step = 1
    while step < CHUNK:
        shifted = pltpu.roll(cum, step, axis=0)
        cum = cum + jnp.where(row_in_chunk >= step, shifted, 0.0)
        step *= 2

    lane = lax.broadcasted_iota(jnp.int32, (tile, LANES), 1)
    c_hi, c_mid, c_lo = _split3(cum)
    d_hi, d_mid, d_lo = _split3(dt)
    parts = (c_hi, c_mid, c_lo, d_hi, d_mid, d_lo)
    packed = jnp.zeros((tile, LANES), F32)
    for i, part in enumerate(parts):
        packed = jnp.where((lane >= SSM_HEADS * i) & (lane < SSM_HEADS * (i + 1)), part, packed)
    expanded = _dot(packed.astype(BF16), expand_ref[...])
    cum_x = expanded[:, :SSM_WIDTH]
    dt_x = expanded[:, SSM_WIDTH:]
    xdt = xs * dt_x

    li = lax.broadcasted_iota(jnp.int32, (CHUNK, GROUP_W), 0)
    si = lax.broadcasted_iota(jnp.int32, (CHUNK, GROUP_W), 1) % CHUNK
    diag_mask = li == si
    causal_mask = si <= li
    bd_mask = (lax.broadcasted_iota(jnp.int32, (GROUP_W, GROUP_W), 0) // CHUNK
               == lax.broadcasted_iota(jnp.int32, (GROUP_W, GROUP_W), 1) // CHUNK)
    d_row = d_ref[...]
    for ci in range(n_chunks):
        r0 = ci * CHUNK
        for g in range(SSM_GROUPS):
            c0 = g * GROUP_W
            colb = cum_x[r0:r0 + CHUNK, c0:c0 + GROUP_W]
            rowb = jnp.sum(jnp.where(diag_mask, colb, 0.0), axis=0, keepdims=True)
            lmat = jnp.exp2(jnp.where(causal_mask, colb - rowb, -jnp.inf))
            b_g = bm[r0:r0 + CHUNK, g * D_STATE:(g + 1) * D_STATE]
            c_g = cm[r0:r0 + CHUNK, g * D_STATE:(g + 1) * D_STATE]
            cb = _dot_nt(c_g, jnp.concatenate([b_g] * HEADS_PER_STACK, axis=0))
            m_g = (cb * lmat).astype(BF16)
            xdt_g = xdt[r0:r0 + CHUNK, c0:c0 + GROUP_W]
            xdt_b = xdt_g.astype(BF16)
            bd = jnp.where(bd_mask, jnp.concatenate([xdt_b] * HEADS_PER_STACK, axis=0), jnp.zeros((), BF16))
            y_diag = _dot(m_g, bd)
            s_prev = st_ref[:, c0:c0 + GROUP_W]
            y_off = _dot(c_g, s_prev.astype(BF16)) * jnp.exp2(colb)
            last = colb[CHUNK - 1:CHUNK, :]
            xw = (xdt_g * jnp.exp2(last - colb)).astype(BF16)
            st_ref[:, c0:c0 + GROUP_W] = jnp.exp2(last) * s_prev + _dot_tn(b_g, xw)
            xs_g = xs[r0:r0 + CHUNK, c0:c0 + GROUP_W]
            ybuf[r0:r0 + CHUNK, c0:c0 + GROUP_W] = y_diag + y_off + d_row[:, c0:c0 + GROUP_W] * xs_g
    ssm_out_ref[0] = st_ref[...]

    gated = ybuf[...] * _silu(z)
    for g in range(SSM_GROUPS):
        c0 = g * GROUP_W
        gg = gated[:, c0:c0 + GROUP_W]
        ms = jnp.mean(gg * gg, axis=-1, keepdims=True)
        mixed_ref[:, c0:c0 + GROUP_W] = (gg * lax.rsqrt(ms + RMS_EPS) * norm_w_ref[:, c0:c0 + GROUP_W]).astype(BF16)

    lane_q = lax.broadcasted_iota(jnp.int32, (CHUNK, LANES), 1)
    low_half = lane_q < HEAD_DIM
    for ci in range(n_chunks):
        r0 = ci * CHUNK
        if hist_all_valid:
            variant = WINDOW // CHUNK
        else:
            variant = jnp.minimum(t * n_chunks + ci, WINDOW // CHUNK)
        q_c = q[r0:r0 + CHUNK, :]
        blocks = [q_c[:, j * LANES:(j + 1) * LANES] for j in range(ATTN_HEADS // 2)]
        evens = [jnp.where(low_half, blk, 0.0).astype(BF16) for blk in blocks]
        odds = [jnp.where(low_half, 0.0, blk).astype(BF16) for blk in blocks]
        q_a = jnp.concatenate([evens[0], evens[1], odds[2], odds[3]], axis=0)
        q_b = jnp.concatenate([odds[0], odds[1], evens[2], evens[3]], axis=0)
        outs = []
        for s_idx, (q_s, k_ref, v_ref) in enumerate(((q_a, kb_ref, vb_ref), (q_b, ksb_ref, vsb_ref))):
            kband = k_ref[r0:r0 + BAND, :]
            vband = v_ref[r0:r0 + BAND, :]
            s = _dot_nt(kband, q_s) + bias_ref[variant, s_idx]
            sink = sink_ref[s_idx]
            m = jnp.maximum(jnp.max(s, axis=0, keepdims=True), sink)
            p = jnp.exp2(s - m)
            denom = jnp.sum(p, axis=0, keepdims=True) + jnp.exp2(sink - m)
            outs.append(_dot_tn((p * (1.0 / denom)).astype(BF16), vband))
        o_a, o_b = outs
        sel = [(o_a, o_b), (o_a, o_b), (o_b, o_a), (o_b, o_a)]
        for j in range(ATTN_HEADS // 2):
            ev, od = sel[j]
            blk = jnp.where(low_half, ev[j * CHUNK:(j + 1) * CHUNK], od[j * CHUNK:(j + 1) * CHUNK])
            mixed_ref[r0:r0 + CHUNK, SSM_WIDTH + j * LANES:SSM_WIDTH + (j + 1) * LANES] = blk.astype(BF16)

    k_tail = kf_ref[tile:tile + WINDOW, :]
    v_tail = vf_ref[tile:tile + WINDOW, :]
    k_out_ref[0] = k_tail
    v_out_ref[0] = v_tail
    kb_tail, ksb_tail = kb_ref[tile:tile + WINDOW, :], ksb_ref[tile:tile + WINDOW, :]
    vb_tail, vsb_tail = vb_ref[tile:tile + WINDOW, :], vsb_ref[tile:tile + WINDOW, :]
    kf_ref[0:WINDOW, :] = k_tail
    vf_ref[0:WINDOW, :] = v_tail
    kb_ref[0:WINDOW, :] = kb_tail
    ksb_ref[0:WINDOW, :] = ksb_tail
    vb_ref[0:WINDOW, :] = vb_tail
    vsb_ref[0:WINDOW, :] = vsb_tail

    mix = _dot(mixed_ref[...], w_out_ref[...])
    h_ref[0] = _layer_norm(DEEPNORM_ALPHA * x + mix, ln_g_ref[...], ln_b_ref[...])


def _ffn_kernel(f_chunk, h_ref, w_gate_ref, w_up_ref, w_down_ref, ln_g_ref, ln_b_ref, y_ref):
    h = h_ref[...]
    hb = h.astype(BF16)
    acc = DEEPNORM_ALPHA * h
    for c in range(D_FF // f_chunk):
        c0 = c * f_chunk
        gate = _dot(hb, w_gate_ref[:, c0:c0 + f_chunk])
        up = _dot(hb, w_up_ref[:, c0:c0 + f_chunk])
        act = (_silu(gate) * up).astype(BF16)
        acc = acc + _dot(act, w_down_ref[c0:c0 + f_chunk, :])
    y_ref[...] = _layer_norm(acc, ln_g_ref[...], ln_b_ref[...])


def _const_spec(shape):
    zeros = (0,) * len(shape)
    return pl.BlockSpec(shape, lambda *_: zeros)


def _attention_tables(attn_sinks):
    slopes = np.exp2(-8.0 * np.arange(1, ATTN_HEADS + 1, dtype=np.float64) / ATTN_HEADS)
    qi = np.arange(CHUNK)[None, :] + WINDOW
    kj = np.arange(BAND)[:, None]
    dist = np.abs(qi - kj).astype(np.float64)
    key_chunk = kj // CHUNK
    tables = np.zeros((WINDOW // CHUNK + 1, 2, BAND, STACK_ROWS), np.float32)
    for variant in range(WINDOW // CHUNK + 1):
        valid = key_chunk >= (WINDOW // CHUNK - variant)
        for s_idx, heads in enumerate((STACK_A_HEADS, STACK_B_HEADS)):
            for i, h in enumerate(heads):
                tables[variant, s_idx, :, i * CHUNK:(i + 1) * CHUNK] = np.where(
                    valid, -slopes[h] * dist * LOG2E, -np.inf)
    order = np.array(STACK_A_HEADS + STACK_B_HEADS)
    sink_rows = jnp.repeat(attn_sinks.astype(F32)[order] * LOG2E, CHUNK).reshape(2, 1, STACK_ROWS)
    return jnp.asarray(tables), sink_rows


def _expansion_matrix():
    e = np.zeros((EXPAND_K, 2 * SSM_WIDTH), np.float32)
    for part in range(6):
        for h in range(SSM_HEADS):
            c0 = (part // 3) * SSM_WIDTH + h * HEAD_DIM
            e[part * SSM_HEADS + h, c0:c0 + HEAD_DIM] = 1.0
    return jnp.asarray(e, BF16)


def _mixer_call(x, conv0, ssm0_t, k0, v0, weights, hist_all_valid, tile):
    bsz, seq, _ = x.shape
    n_tiles = seq // tile
    kernel = functools.partial(_mixer_kernel, hist_all_valid, tile, n_tiles)
    per_batch = lambda shape: pl.BlockSpec((1,) + shape, lambda b, t: (b, 0, 0))
    in_specs = [pl.BlockSpec((1, tile, D_MODEL), lambda b, t: (b, t, 0)),
                per_batch((CONV_W - 1, CONV_DIM)), per_batch((D_STATE, SSM_WIDTH)),
                per_batch((WINDOW, KV_W)), per_batch((WINDOW, KV_W))]
    in_specs += [_const_spec(w.shape) for w in weights]
    out_shape = (jax.ShapeDtypeStruct((bsz, seq, D_MODEL), F32),
                 jax.ShapeDtypeStruct((bsz, CONV_W - 1, CONV_DIM), F32),
                 jax.ShapeDtypeStruct((bsz, D_STATE, SSM_WIDTH), F32),
                 jax.ShapeDtypeStruct((bsz, WINDOW, KV_W), F32),
                 jax.ShapeDtypeStruct((bsz, WINDOW, KV_W), F32))
    out_specs = (pl.BlockSpec((1, tile, D_MODEL), lambda b, t: (b, t, 0)),
                 per_batch((CONV_W - 1, CONV_DIM)), per_batch((D_STATE, SSM_WIDTH)),
                 per_batch((WINDOW, KV_W)), per_batch((WINDOW, KV_W)))
    scratch = [pltpu.VMEM((8 + tile, CONV_DIM), F32),
               pltpu.VMEM((D_STATE, SSM_WIDTH), F32),
               pltpu.VMEM((WINDOW + tile, KV_W), F32), pltpu.VMEM((WINDOW + tile, KV_W), F32),
               pltpu.VMEM((WINDOW + tile, KV_W), BF16), pltpu.VMEM((WINDOW + tile, KV_W), BF16),
               pltpu.VMEM((WINDOW + tile, KV_W), BF16), pltpu.VMEM((WINDOW + tile, KV_W), BF16),
               pltpu.VMEM((tile, SSM_WIDTH), F32),
               pltpu.VMEM((tile, D_MODEL), BF16)]
    return pl.pallas_call(
        kernel, grid=(bsz, n_tiles), in_specs=in_specs, out_specs=out_specs, out_shape=out_shape,
        scratch_shapes=scratch, name="mixer_prompt" if not hist_all_valid else "mixer_sample",
        compiler_params=pltpu.CompilerParams(dimension_semantics=("arbitrary", "arbitrary"),
                                             vmem_limit_bytes=VMEM_LIMIT),
    )(x, conv0, ssm0_t, k0, v0, *weights)


def _ffn_call(h2d, w_gate, w_up, w_down, ln_g, ln_b, tile, name):
    n_tok = h2d.shape[0]
    return pl.pallas_call(
        functools.partial(_ffn_kernel, 256),
        grid=(n_tok // tile,),
        in_specs=[pl.BlockSpec((tile, D_MODEL), lambda i: (i, 0)),
                  _const_spec(w_gate.shape), _const_spec(w_up.shape), _const_spec(w_down.shape),
                  _const_spec(ln_g.shape), _const_spec(ln_b.shape)],
        out_specs=pl.BlockSpec((tile, D_MODEL), lambda i: (i, 0)),
        out_shape=jax.ShapeDtypeStruct((n_tok, D_MODEL), F32),
        name=name,
        compiler_params=pltpu.CompilerParams(dimension_semantics=("arbitrary",), vmem_limit_bytes=VMEM_LIMIT),
    )(h2d, w_gate, w_up, w_down, ln_g, ln_b)


def _state_to_kernel_layout(ssm):
    b = ssm.shape[0]
    return jnp.transpose(ssm, (0, 3, 1, 2)).reshape(b, D_STATE, SSM_WIDTH)


def _state_from_kernel_layout(ssm_t):
    b = ssm_t.shape[0]
    return jnp.transpose(ssm_t.reshape(b, D_STATE, SSM_HEADS, HEAD_DIM), (0, 2, 3, 1))


def kernel(x_prompt, x_sample, state_conv, state_ssm, cache_k, cache_v, w_in, conv_w, conv_b, dt_bias, a_log,
           d_skip, ssm_norm_w, attn_sinks, w_out, ln1_g, ln1_b, w_gate, w_up, w_down, ln2_g, ln2_b):
    w = w_in[0]
    split = np.cumsum([SSM_WIDTH, CONV_DIM, SSM_HEADS, ATTN_WIDTH, KV_W])
    w_z, w_xbc, w_dt, w_q, w_k, w_v = jnp.split(w, split, axis=1)
    w_in_r = jnp.concatenate([w_z, w_xbc, w_q * (HEAD_DIM ** -0.5 * LOG2E), w_k, w_v], axis=1).astype(BF16)
    rep = LANES // SSM_HEADS
    w_dt_rep = jnp.tile(w_dt, (1, rep)).astype(BF16)
    dt_bias_rep = jnp.tile(dt_bias[0].astype(F32), rep).reshape(1, LANES)
    a_log_rep = jnp.tile(a_log[0].astype(F32), rep).reshape(1, LANES)
    d_row = jnp.repeat(d_skip[0].astype(F32), HEAD_DIM).reshape(1, SSM_WIDTH)
    bias_tables, sink_rows = _attention_tables(attn_sinks[0])
    row = lambda v: v.astype(F32).reshape(1, -1)
    mixer_weights = (w_in_r, w_dt_rep, conv_w[0].astype(F32), row(conv_b[0]), dt_bias_rep, a_log_rep, d_row,
                     row(ssm_norm_w[0]), _expansion_matrix(), bias_tables, sink_rows,
                     w_out[0].astype(BF16), row(ln1_g[0]), row(ln1_b[0]))
    ffn_weights = (w_gate[0].astype(BF16), w_up[0].astype(BF16), w_down[0].astype(BF16), row(ln2_g[0]), row(ln2_b[0]))

    bsz, seq, _ = x_prompt.shape
    dbsz, dseq, _ = x_sample.shape

    zeros = lambda *shape: jnp.zeros(shape, F32)
    h_p, conv_p, ssm_p, k_p, v_p = _mixer_call(
        x_prompt, zeros(bsz, CONV_W - 1, CONV_DIM), zeros(bsz, D_STATE, SSM_WIDTH),
        zeros(bsz, WINDOW, KV_W), zeros(bsz, WINDOW, KV_W), mixer_weights, False, 256)
    y_p = _ffn_call(h_p.reshape(bsz * seq, D_MODEL), *ffn_weights, 512, "ffn_prompt").reshape(bsz, seq, D_MODEL)

    h_s, conv_s, ssm_s, k_s, v_s = _mixer_call(
        x_sample, state_conv[0].astype(F32), _state_to_kernel_layout(state_ssm[0].astype(F32)),
        cache_k[0].astype(F32).reshape(dbsz, WINDOW, KV_W), cache_v[0].astype(F32).reshape(dbsz, WINDOW, KV_W),
        mixer_weights, True, dseq)
    y_s = _ffn_call(h_s.reshape(dbsz * dseq, D_MODEL), *ffn_weights, 512, "ffn_sample").reshape(dbsz, dseq, D_MODEL)

    kv_shape = lambda a: a.reshape(1, a.shape[0], WINDOW, KV_HEADS, HEAD_DIM)
    return (y_p, y_s,
            conv_p[None], _state_from_kernel_layout(ssm_p)[None], kv_shape(k_p), kv_shape(v_p),
            conv_s[None], _state_from_kernel_layout(ssm_s)[None], kv_shape(k_s), kv_shape(v_s))
```

```python
import functools
import math

import jax
import jax.numpy as jnp
import numpy as np
from jax import lax
from jax.experimental import pallas as pl
from jax.experimental.pallas import tpu as pltpu

D_MODEL = 1024
CHUNK = 64
HEAD_DIM = 64
SSM_WIDTH = 512
SSM_HEADS = 8
SSM_GROUPS = 2
D_STATE = 128
CONV_W = 4
CONV_DIM = SSM_WIDTH + 2 * SSM_GROUPS * D_STATE
ATTN_WIDTH = 512
ATTN_HEADS = 8
KV_HEADS = 2
WINDOW = 128
D_FF = 2816
LN_EPS = 1e-5
RMS_EPS = 1e-5
DEEPNORM_ALPHA = 2.0 ** 0.25
LOG2E = math.log2(math.e)

LANES = 128
SUBLANES = 8
GROUP_W = SSM_WIDTH // SSM_GROUPS
KV_W = KV_HEADS * HEAD_DIM
BAND = WINDOW + CHUNK
HIST_CHUNKS = WINDOW // CHUNK
HEADS_PER_STACK = 4
STACK_ROWS = HEADS_PER_STACK * CHUNK
COL_Z, COL_XBC, COL_Q, COL_K, COL_V, COL_END = 0, 512, 1536, 2048, 2176, 2304
STACK_A_HEADS = (0, 2, 5, 7)
STACK_B_HEADS = (1, 3, 4, 6)
EXPAND_K = 128
CONV_PAD = SUBLANES

PROMPT_TILE = 256
FFN_TILE = 512
FFN_CHUNK = 256
N_FFN_CHUNKS = D_FF // FFN_CHUNK
VMEM_LIMIT = 56 * 1024 * 1024

F32 = jnp.float32
BF16 = jnp.bfloat16


def _dot(a, b):
    return jnp.dot(a, b, preferred_element_type=F32)


def _dot_nt(a, b):
    return lax.dot_general(a, b, (((1,), (1,)), ((), ())), preferred_element_type=F32)


def _dot_tn(a, b):
    return lax.dot_general(a, b, (((0,), (0,)), ((), ())), preferred_element_type=F32)


def _sigmoid(x):
    return 1.0 / (1.0 + jnp.exp(-x))


def _silu(x):
    return x * _sigmoid(x)


def _layer_norm(x, g, b):
    mu = jnp.mean(x, axis=-1, keepdims=True)
    xc = x - mu
    var = jnp.mean(xc * xc, axis=-1, keepdims=True)
    return xc * lax.rsqrt(var + LN_EPS) * g + b


def _split3(x):
    hi = x.astype(BF16).astype(F32)
    r = x - hi
    mid = r.astype(BF16).astype(F32)
    lo = r - mid
    return hi, mid, lo


def _init_stream_state(conv0_ref, ssm0_ref, k0_ref, v0_ref,
                       convbuf, st_ref, kf_ref, vf_ref, kb_ref, ksb_ref, vb_ref, vsb_ref):
    convbuf[CONV_PAD - (CONV_W - 1):CONV_PAD, :] = conv0_ref[0]
    st_ref[...] = ssm0_ref[0]
    k0 = k0_ref[0]
    v0 = v0_ref[0]
    kf_ref[0:WINDOW, :] = k0
    vf_ref[0:WINDOW, :] = v0
    kb_ref[0:WINDOW, :] = k0.astype(BF16)
    vb_ref[0:WINDOW, :] = v0.astype(BF16)
    ksb_ref[0:WINDOW, :] = pltpu.roll(k0, HEAD_DIM, axis=1).astype(BF16)
    vsb_ref[0:WINDOW, :] = pltpu.roll(v0, HEAD_DIM, axis=1).astype(BF16)


def _write_stream_state(conv_out_ref, ssm_out_ref, k_out_ref, v_out_ref, convbuf, st_ref, kf_ref, vf_ref):
    conv_out_ref[0] = convbuf[CONV_PAD - (CONV_W - 1):CONV_PAD, :]
    ssm_out_ref[0] = st_ref[...]
    k_out_ref[0] = kf_ref[0:WINDOW, :]
    v_out_ref[0] = vf_ref[0:WINDOW, :]


def _mixer_tile(hist_all_valid, tile, first_chunk, x_ref,
                w_in_ref, w_dt_ref, conv_w_ref, conv_b_ref, dt_bias_ref, a_log_ref, d_ref, norm_w_ref,
                expand_ref, bias_ref, sink_ref, w_out_ref, ln_g_ref, ln_b_ref,
                convbuf, st_ref, kf_ref, vf_ref, kb_ref, ksb_ref, vb_ref, vsb_ref, mixed_ref,
                zbuf, qbuf, dtbuf, xsbuf, bcbuf, cumx_ref, xdt_ref):
    n_chunks = tile // CHUNK
    xb = x_ref[0].astype(BF16)

    zbuf[...] = _dot(xb, w_in_ref[:, COL_Z:COL_XBC])
    convbuf[CONV_PAD:CONV_PAD + tile, :] = _dot(xb, w_in_ref[:, COL_XBC:COL_Q])
    qbuf[...] = _dot(xb, w_in_ref[:, COL_Q:COL_K])
    kv = _dot(xb, w_in_ref[:, COL_K:COL_END])
    dtbuf[...] = _dot(xb, w_dt_ref[...])
    k_new = kv[:, :KV_W]
    v_new = kv[:, KV_W:]
    kf_ref[WINDOW:WINDOW + tile, :] = k_new
    vf_ref[WINDOW:WINDOW + tile, :] = v_new
    kb_ref[WINDOW:WINDOW + tile, :] = k_new.astype(BF16)
    vb_ref[WINDOW:WINDOW + tile, :] = v_new.astype(BF16)
    ksb_ref[WINDOW:WINDOW + tile, :] = pltpu.roll(k_new, HEAD_DIM, axis=1).astype(BF16)
    vsb_ref[WINDOW:WINDOW + tile, :] = pltpu.roll(v_new, HEAD_DIM, axis=1).astype(BF16)
    yield 2

    row_in_chunk = lax.broadcasted_iota(jnp.int32, (CHUNK, LANES), 0)
    lane = lax.broadcasted_iota(jnp.int32, (CHUNK, LANES), 1)
    low_half = lane < HEAD_DIM
    li = lax.broadcasted_iota(jnp.int32, (CHUNK, GROUP_W), 0)
    si = lax.broadcasted_iota(jnp.int32, (CHUNK, GROUP_W), 1) % CHUNK
    diag_mask = li == si
    causal_mask = si <= li
    bd_mask = (lax.broadcasted_iota(jnp.int32, (GROUP_W, GROUP_W), 0) // CHUNK
               == lax.broadcasted_iota(jnp.int32, (GROUP_W, GROUP_W), 1) // CHUNK)

    def conv_and_decay(ci):
        r0 = ci * CHUNK
        rows = slice(r0, r0 + CHUNK)

        conv = conv_b_ref[...]
        for i in range(CONV_W):
            r = CONV_PAD - (CONV_W - 1) + i + r0
            conv = conv + conv_w_ref[i:i + 1, :] * convbuf[r:r + CHUNK, :]
        xc = _silu(conv)
        xsbuf[rows, :] = xc[:, :SSM_WIDTH]
        bcbuf[rows, :] = xc[:, SSM_WIDTH:].astype(BF16)

        dt_in = dtbuf[rows, :] + dt_bias_ref[...]
        dt = jnp.maximum(dt_in, 0.0) + jnp.log1p(jnp.exp(-jnp.abs(dt_in)))
        cum = dt * (-LOG2E * jnp.exp(a_log_ref[...]))
        step = 1
        while step < CHUNK:
            cum = cum + jnp.where(row_in_chunk >= step, pltpu.roll(cum, step, axis=0), 0.0)
            step *= 2
        packed = jnp.zeros((CHUNK, LANES), F32)
        for i, part in enumerate(_split3(cum) + _split3(dt)):
            packed = jnp.where((lane >= SSM_HEADS * i) & (lane < SSM_HEADS * (i + 1)), part, packed)
        return packed.astype(BF16)

    packed_next = conv_and_decay(0)
    for ci in range(n_chunks):
        r0 = ci * CHUNK
        rows = slice(r0, r0 + CHUNK)
        packed = packed_next

        expanded = _dot(packed, expand_ref[...])
        if hist_all_valid:
            variant = HIST_CHUNKS
        else:
            variant = jnp.minimum(first_chunk + ci, HIST_CHUNKS)
        group_cols = [slice(g * GROUP_W, (g + 1) * GROUP_W) for g in range(SSM_GROUPS)]
        b_gs = [bcbuf[rows, g * D_STATE:(g + 1) * D_STATE] for g in range(SSM_GROUPS)]
        c_gs = [bcbuf[rows, GROUP_W + g * D_STATE:GROUP_W + (g + 1) * D_STATE] for g in range(SSM_GROUPS)]
        cbs = [_dot_nt(c_gs[g], jnp.concatenate([b_gs[g]] * HEADS_PER_STACK, axis=0)) for g in range(SSM_GROUPS)]
        y_offs = [_dot(c_gs[g], st_ref[:, group_cols[g]].astype(BF16)) for g in range(SSM_GROUPS)]
        blocks = [qbuf[rows, j * LANES:(j + 1) * LANES] for j in range(ATTN_HEADS // 2)]
        evens = [jnp.where(low_half, blk, 0.0).astype(BF16) for blk in blocks]
        odds = [jnp.where(low_half, 0.0, blk).astype(BF16) for blk in blocks]
        q_a = jnp.concatenate([evens[0], evens[1], odds[2], odds[3]], axis=0)
        q_b = jnp.concatenate([odds[0], odds[1], evens[2], evens[3]], axis=0)
        scores = [_dot_nt(k_ref[r0:r0 + BAND, :], q_s) for q_s, k_ref in ((q_a, kb_ref), (q_b, ksb_ref))]
        yield 3

        if ci + 1 < n_chunks:
            packed_next = conv_and_decay(ci + 1)
        cumx_ref[...] = expanded[:, :SSM_WIDTH]
        xdt_ref[...] = xsbuf[rows, :] * expanded[:, SSM_WIDTH:]
        m_gs, bds, xws, probs = [], [], [], []
        for g in range(SSM_GROUPS):
            colb = cumx_ref[:, group_cols[g]]
            rowb = jnp.sum(jnp.where(diag_mask, colb, 0.0), axis=0, keepdims=True)
            lmat = jnp.exp2(jnp.where(causal_mask, colb - rowb, -jnp.inf))
            m_gs.append((cbs[g] * lmat).astype(BF16))
            xdt_g = xdt_ref[:, group_cols[g]]
            xdt_b = xdt_g.astype(BF16)
            bds.append(jnp.where(bd_mask, jnp.concatenate([xdt_b] * HEADS_PER_STACK, axis=0), jnp.zeros((), BF16)))
            last = colb[CHUNK - 1:CHUNK, :]
            xws.append((xdt_g * jnp.exp2(last - colb)).astype(BF16))
        for s_idx in range(2):
            s = scores[s_idx] + bias_ref[variant, s_idx]
            sink = sink_ref[s_idx]
            m = jnp.maximum(jnp.max(s, axis=0, keepdims=True), sink)
            p = jnp.exp2(s - m)
            denom = jnp.sum(p, axis=0, keepdims=True) + jnp.exp2(sink - m)
            probs.append((p * (1.0 / denom)).astype(BF16).T)
        b_ts = [b_g.T for b_g in b_gs]
        yield 2

        y_diags = [_dot(m_gs[g], bds[g]) for g in range(SSM_GROUPS)]
        updates = [_dot(b_ts[g], xws[g]) for g in range(SSM_GROUPS)]
        outs = [_dot(probs[s_idx], v_ref[r0:r0 + BAND, :])
                for s_idx, v_ref in enumerate((vb_ref, vsb_ref))]
        yield 2

        for g in range(SSM_GROUPS):
            cols = group_cols[g]
            colb = cumx_ref[:, cols]
            last = colb[CHUNK - 1:CHUNK, :]
            st_ref[:, cols] = jnp.exp2(last) * st_ref[:, cols] + updates[g]
            y = y_diags[g] + y_offs[g] * jnp.exp2(colb) + d_ref[:, cols] * xsbuf[rows, cols]
            gg = y * _silu(zbuf[rows, cols])
            ms = jnp.mean(gg * gg, axis=-1, keepdims=True)
            mixed_ref[rows, cols] = (gg * lax.rsqrt(ms + RMS_EPS) * norm_w_ref[:, cols]).astype(BF16)
        o_a, o_b = outs
        sel = [(o_a, o_b), (o_a, o_b), (o_b, o_a), (o_b, o_a)]
        for j in range(ATTN_HEADS // 2):
            ev, od = sel[j]
            blk = jnp.where(low_half, ev[j * CHUNK:(j + 1) * CHUNK], od[j * CHUNK:(j + 1) * CHUNK])
            mixed_ref[rows, SSM_WIDTH + j * LANES:SSM_WIDTH + (j + 1) * LANES] = blk.astype(BF16)

    convbuf[CONV_PAD - (CONV_W - 1):CONV_PAD, :] = convbuf[CONV_PAD + tile - (CONV_W - 1):CONV_PAD + tile, :]
    for ref in (kf_ref, vf_ref, kb_ref, ksb_ref, vb_ref, vsb_ref):
        ref[0:WINDOW, :] = ref[tile:tile + WINDOW, :]

    yield

    mix = _dot(mixed_ref[...], w_out_ref[...])
    return _layer_norm(DEEPNORM_ALPHA * x_ref[0] + mix, ln_g_ref[...], ln_b_ref[...])


FFN_UNITS_PER_CHUNK = 3


def _ffn_tile(h, hb_ref, w_gate_ref, w_up_ref, w_down_ref, ln_g_ref, ln_b_ref):
    acc = DEEPNORM_ALPHA * h
    for c in range(N_FFN_CHUNKS):
        c0 = c * FFN_CHUNK
        gate = _dot(hb_ref[...], w_gate_ref[:, c0:c0 + FFN_CHUNK])
        yield
        up = _dot(hb_ref[...], w_up_ref[:, c0:c0 + FFN_CHUNK])
        yield
        act = (_silu(gate) * up).astype(BF16)
        acc = acc + _dot(act, w_down_ref[c0:c0 + FFN_CHUNK, :])
        yield
    return _layer_norm(acc, ln_g_ref[...], ln_b_ref[...])


def _run(gen):
    while True:
        try:
            next(gen)
        except StopIteration as stop:
            return stop.value


def _interleave(mixer, ffn):
    remaining = FFN_UNITS_PER_CHUNK * N_FFN_CHUNKS
    while True:
        try:
            wanted = next(mixer)
        except StopIteration as stop:
            return stop.value, y
        last = wanted is None
        for _ in range(remaining if last else min(wanted, remaining)):
            next(ffn)
            remaining -= 1
        if last:
            y = _run(ffn)


N_STATE_IN = 4
N_MIXER_W = 14
N_FFN_W = 5
N_STATE_OUT = 4
N_MIXER_SCRATCH = 16


def _fused_kernel(tile, n_tiles, n_total, *refs):
    x_ref = refs[0]
    state_in = refs[1:1 + N_STATE_IN]
    o = 1 + N_STATE_IN
    mixer_w = refs[o:o + N_MIXER_W]
    o += N_MIXER_W
    ffn_w = refs[o:o + N_FFN_W]
    o += N_FFN_W
    y_ref = refs[o]
    state_out = refs[o + 1:o + 1 + N_STATE_OUT]
    o += 1 + N_STATE_OUT
    mixer_s = refs[o:o + N_MIXER_SCRATCH]
    h_s, hb_s = refs[o + N_MIXER_SCRATCH:]
    state_s = mixer_s[:8]

    g = pl.program_id(0)
    t = jnp.minimum(g, n_total - 1) % n_tiles

    @pl.when(g == 0)
    def _():
        h_s[...] = jnp.zeros(h_s.shape, F32)
        hb_s[...] = jnp.zeros(hb_s.shape, BF16)

    @pl.when(t == 0)
    def _():
        _init_stream_state(*state_in, *state_s)

    n_chunks = tile // CHUNK
    ffn = _ffn_tile(h_s[...], hb_s, *ffn_w)
    mixer = _mixer_tile(False, tile, t * n_chunks, x_ref, *mixer_w, *mixer_s)
    h, y = _interleave(mixer, ffn)
    y_ref[0] = y
    h_s[...] = h
    hb_s[...] = h.astype(BF16)

    @pl.when((t == n_tiles - 1) & (g < n_total))
    def _():
        _write_stream_state(*state_out, convbuf=state_s[0], st_ref=state_s[1], kf_ref=state_s[2], vf_ref=state_s[3])


def _mixer_kernel(tile, *refs):
    x_ref = refs[0]
    state_in = refs[1:1 + N_STATE_IN]
    o = 1 + N_STATE_IN
    mixer_w = refs[o:o + N_MIXER_W]
    o += N_MIXER_W
    h_ref = refs[o]
    state_out = refs[o + 1:o + 1 + N_STATE_OUT]
    mixer_s = refs[o + 1 + N_STATE_OUT:]
    state_s = mixer_s[:8]
    _init_stream_state(*state_in, *state_s)
    h_ref[0] = _run(_mixer_tile(True, tile, 0, x_ref, *mixer_w, *mixer_s))
    _write_stream_state(*state_out, convbuf=state_s[0], st_ref=state_s[1], kf_ref=state_s[2], vf_ref=state_s[3])


def _ffn_kernel(h_ref, *refs):
    ffn_w, y_ref = refs[:N_FFN_W], refs[N_FFN_W]
    hb_s = refs[N_FFN_W + 1]
    h = h_ref[...]
    hb_s[...] = h.astype(BF16)
    y_ref[...] = _run(_ffn_tile(h, hb_s, *ffn_w))


def _const_spec(shape):
    zeros = (0,) * len(shape)
    return pl.BlockSpec(shape, lambda *_: zeros, pipeline_mode=pl.Buffered(1))


def _mixer_scratch(tile):
    return [pltpu.VMEM((CONV_PAD + tile, CONV_DIM), F32),
            pltpu.VMEM((D_STATE, SSM_WIDTH), F32),
            pltpu.VMEM((WINDOW + tile, KV_W), F32), pltpu.VMEM((WINDOW + tile, KV_W), F32),
            pltpu.VMEM((WINDOW + tile, KV_W), BF16), pltpu.VMEM((WINDOW + tile, KV_W), BF16),
            pltpu.VMEM((WINDOW + tile, KV_W), BF16), pltpu.VMEM((WINDOW + tile, KV_W), BF16),
            pltpu.VMEM((tile, D_MODEL), BF16),
            pltpu.VMEM((tile, SSM_WIDTH), F32),
            pltpu.VMEM((tile, ATTN_WIDTH), F32),
            pltpu.VMEM((tile, LANES), F32),
            pltpu.VMEM((tile, SSM_WIDTH), F32),
            pltpu.VMEM((tile, CONV_DIM - SSM_WIDTH), BF16),
            pltpu.VMEM((CHUNK, SSM_WIDTH), F32),
            pltpu.VMEM((CHUNK, SSM_WIDTH), F32)]


_STATE_SHAPES = ((CONV_W - 1, CONV_DIM), (D_STATE, SSM_WIDTH), (WINDOW, KV_W), (WINDOW, KV_W))


def _attention_tables(attn_sinks):
    slopes = np.exp2(-8.0 * np.arange(1, ATTN_HEADS + 1, dtype=np.float64) / ATTN_HEADS)
    qi = np.arange(CHUNK)[None, :] + WINDOW
    kj = np.arange(BAND)[:, None]
    dist = np.abs(qi - kj).astype(np.float64)
    key_chunk = kj // CHUNK
    tables = np.zeros((HIST_CHUNKS + 1, 2, BAND, STACK_ROWS), np.float32)
    for variant in range(HIST_CHUNKS + 1):
        valid = key_chunk >= (HIST_CHUNKS - variant)
        for s_idx, heads in enumerate((STACK_A_HEADS, STACK_B_HEADS)):
            for i, h in enumerate(heads):
                tables[variant, s_idx, :, i * CHUNK:(i + 1) * CHUNK] = np.where(
                    valid, -slopes[h] * dist * LOG2E, -np.inf)
    order = np.array(STACK_A_HEADS + STACK_B_HEADS)
    sink_rows = jnp.repeat(attn_sinks.astype(F32)[order] * LOG2E, CHUNK).reshape(2, 1, STACK_ROWS)
    return jnp.asarray(tables), sink_rows


def _expansion_matrix():
    e = np.zeros((EXPAND_K, 2 * SSM_WIDTH), np.float32)
    for part in range(6):
        for h in range(SSM_HEADS):
            c0 = (part // 3) * SSM_WIDTH + h * HEAD_DIM
            e[part * SSM_HEADS + h, c0:c0 + HEAD_DIM] = 1.0
    return jnp.asarray(e, BF16)


def _prompt_call(x, state0, mixer_w, ffn_w):
    bsz, seq, _ = x.shape
    tile = PROMPT_TILE
    n_tiles = seq // tile
    n_total = bsz * n_tiles

    def mixer_tile_idx(g):
        gc = jnp.minimum(g, n_total - 1)
        return gc // n_tiles, gc % n_tiles

    def ffn_tile_idx(g):
        gp = jnp.maximum(g - 1, 0)
        return gp // n_tiles, gp % n_tiles

    tile_spec = lambda idx: pl.BlockSpec((1, tile, D_MODEL), lambda g: (*idx(g), 0))
    stream_spec = lambda shape: pl.BlockSpec((1,) + shape, lambda g: (mixer_tile_idx(g)[0], 0, 0))
    in_specs = ([tile_spec(mixer_tile_idx)] + [stream_spec(s) for s in _STATE_SHAPES]
                + [_const_spec(w.shape) for w in mixer_w + ffn_w])
    out_specs = [tile_spec(ffn_tile_idx)] + [stream_spec(s) for s in _STATE_SHAPES]
    out_shape = ([jax.ShapeDtypeStruct((bsz, seq, D_MODEL), F32)]
                 + [jax.ShapeDtypeStruct((bsz,) + s, F32) for s in _STATE_SHAPES])
    scratch = _mixer_scratch(tile) + [pltpu.VMEM((tile, D_MODEL), F32), pltpu.VMEM((tile, D_MODEL), BF16)]
    return pl.pallas_call(
        functools.partial(_fused_kernel, tile, n_tiles, n_total),
        grid=(n_total + 1,), in_specs=in_specs, out_specs=out_specs, out_shape=out_shape,
        scratch_shapes=scratch, name="prompt_layer",
        compiler_params=pltpu.CompilerParams(dimension_semantics=("arbitrary",), vmem_limit_bytes=VMEM_LIMIT),
    )(x, *state0, *mixer_w, *ffn_w)


def _sample_mixer_call(x, state0, mixer_w):
    bsz, seq, _ = x.shape
    stream_spec = lambda shape: pl.BlockSpec((1,) + shape, lambda b: (b, 0, 0))
    in_specs = ([stream_spec((seq, D_MODEL))] + [stream_spec(s) for s in _STATE_SHAPES]
                + [_const_spec(w.shape) for w in mixer_w])
    out_specs = [stream_spec((seq, D_MODEL))] + [stream_spec(s) for s in _STATE_SHAPES]
    out_shape = ([jax.ShapeDtypeStruct((bsz, seq, D_MODEL), F32)]
                 + [jax.ShapeDtypeStruct((bsz,) + s, F32) for s in _STATE_SHAPES])
    return pl.pallas_call(
        functools.partial(_mixer_kernel, seq),
        grid=(bsz,), in_specs=in_specs, out_specs=out_specs, out_shape=out_shape,
        scratch_shapes=_mixer_scratch(seq), name="sample_mixer",
        compiler_params=pltpu.CompilerParams(dimension_semantics=("arbitrary",), vmem_limit_bytes=VMEM_LIMIT),
    )(x, *state0, *mixer_w)


def _ffn_call(h2d, ffn_w):
    n_tok = h2d.shape[0]
    tile = min(FFN_TILE, n_tok)
    tok_spec = pl.BlockSpec((tile, D_MODEL), lambda i: (i, 0))
    return pl.pallas_call(
        _ffn_kernel, grid=(n_tok // tile,),
        in_specs=[tok_spec] + [_const_spec(w.shape) for w in ffn_w],
        out_specs=tok_spec, out_shape=jax.ShapeDtypeStruct((n_tok, D_MODEL), F32), name="sample_ffn",
        scratch_shapes=[pltpu.VMEM((tile, D_MODEL), BF16)],
        compiler_params=pltpu.CompilerParams(dimension_semantics=("arbitrary",), vmem_limit_bytes=VMEM_LIMIT),
    )(h2d, *ffn_w)


def _state_to_kernel_layout(ssm):
    b = ssm.shape[0]
    return jnp.transpose(ssm, (0, 3, 1, 2)).reshape(b, D_STATE, SSM_WIDTH)


def _state_from_kernel_layout(ssm_t):
    b = ssm_t.shape[0]
    return jnp.transpose(ssm_t.reshape(b, D_STATE, SSM_HEADS, HEAD_DIM), (0, 2, 3, 1))


def kernel(x_prompt, x_sample, state_conv, state_ssm, cache_k, cache_v, w_in, conv_w, conv_b, dt_bias, a_log,
           d_skip, ssm_norm_w, attn_sinks, w_out, ln1_g, ln1_b, w_gate, w_up, w_down, ln2_g, ln2_b):
    w = w_in[0]
    split = np.cumsum([SSM_WIDTH, CONV_DIM, SSM_HEADS, ATTN_WIDTH, KV_W])
    w_z, w_xbc, w_dt, w_q, w_k, w_v = jnp.split(w, split, axis=1)
    w_in_r = jnp.concatenate([w_z, w_xbc, w_q * (HEAD_DIM ** -0.5 * LOG2E), w_k, w_v], axis=1).astype(BF16)
    rep = LANES // SSM_HEADS
    w_dt_rep = jnp.tile(w_dt, (1, rep)).astype(BF16)
    dt_bias_rep = jnp.tile(dt_bias[0].astype(F32), rep).reshape(1, LANES)
    a_log_rep = jnp.tile(a_log[0].astype(F32), rep).reshape(1, LANES)
    d_row = jnp.repeat(d_skip[0].astype(F32), HEAD_DIM).reshape(1, SSM_WIDTH)
    bias_tables, sink_rows = _attention_tables(attn_sinks[0])
    row = lambda v: v.astype(F32).reshape(1, -1)
    mixer_w = (w_in_r, w_dt_rep, conv_w[0].astype(F32), row(conv_b[0]), dt_bias_rep, a_log_rep, d_row,
               row(ssm_norm_w[0]), _expansion_matrix(), bias_tables, sink_rows,
               w_out[0].astype(BF16), row(ln1_g[0]), row(ln1_b[0]))
    ffn_w = (w_gate[0].astype(BF16), w_up[0].astype(BF16), w_down[0].astype(BF16), row(ln2_g[0]), row(ln2_b[0]))
    assert len(mixer_w) == N_MIXER_W and len(ffn_w) == N_FFN_W

    bsz = x_prompt.shape[0]
    dbsz, dseq, _ = x_sample.shape

    zero_state = tuple(jnp.zeros((bsz,) + s, F32) for s in _STATE_SHAPES)
    y_p, conv_p, ssm_p, k_p, v_p = _prompt_call(x_prompt, zero_state, mixer_w, ffn_w)

    sample_state = (state_conv[0].astype(F32), _state_to_kernel_layout(state_ssm[0].astype(F32)),
                    cache_k[0].astype(F32).reshape(dbsz, WINDOW, KV_W), cache_v[0].astype(F32).reshape(dbsz, WINDOW, KV_W))
    h_s, conv_s, ssm_s, k_s, v_s = _sample_mixer_call(x_sample, sample_state, mixer_w)
    y_s = _ffn_call(h_s.reshape(dbsz * dseq, D_MODEL), ffn_w).reshape(dbsz, dseq, D_MODEL)

    kv_shape = lambda a: a.reshape(1, a.shape[0], WINDOW, KV_HEADS, HEAD_DIM)
    return (y_p, y_s,
            conv_p[None], _state_from_kernel_layout(ssm_p)[None], kv_shape(k_p), kv_shape(v_p),
            conv_s[None], _state_from_kernel_layout(ssm_s)[None], kv_shape(k_s), kv_shape(v_s))
```

```python
import functools
import math

import jax
import jax.numpy as jnp
import numpy as np
from jax import lax
from jax.experimental import pallas as pl
from jax.experimental.pallas import tpu as pltpu

D_MODEL = 1024
CHUNK = 64
HEAD_DIM = 64
SSM_WIDTH = 512
SSM_HEADS = 8
SSM_GROUPS = 2
D_STATE = 128
CONV_W = 4
CONV_DIM = SSM_WIDTH + 2 * SSM_GROUPS * D_STATE
ATTN_WIDTH = 512
ATTN_HEADS = 8
KV_HEADS = 2
WINDOW = 128
D_FF = 2816
LN_EPS = 1e-5
RMS_EPS = 1e-5
DEEPNORM_ALPHA = 2.0 ** 0.25
LOG2E = math.log2(math.e)

LANES = 128
SUBLANES = 8
GROUP_W = SSM_WIDTH // SSM_GROUPS
KV_W = KV_HEADS * HEAD_DIM
BAND = WINDOW + CHUNK
HIST_CHUNKS = WINDOW // CHUNK
HEADS_PER_STACK = 4
STACK_ROWS = HEADS_PER_STACK * CHUNK
COL_Z, COL_XBC, COL_Q, COL_K, COL_V, COL_END = 0, 512, 1536, 2048, 2176, 2304
STACK_A_HEADS = (0, 2, 5, 7)
STACK_B_HEADS = (1, 3, 4, 6)
EXPAND_K = 128
CONV_PAD = SUBLANES

PROMPT_TILE = 256
FFN_TILE = 512
FFN_CHUNK = 256
N_FFN_CHUNKS = D_FF // FFN_CHUNK
VMEM_LIMIT = 56 * 1024 * 1024

F32 = jnp.float32
BF16 = jnp.bfloat16


def _dot(a, b):
    return jnp.dot(a, b, preferred_element_type=F32)


def _dot_nt(a, b):
    return lax.dot_general(a, b, (((1,), (1,)), ((), ())), preferred_element_type=F32)


def _dot_tn(a, b):
    return lax.dot_general(a, b, (((0,), (0,)), ((), ())), preferred_element_type=F32)


def _sigmoid(x):
    return 1.0 / (1.0 + jnp.exp2(x * -LOG2E))


def _silu(x):
    return x * _sigmoid(x)


def _layer_norm(x, g, b):
    mu = jnp.mean(x, axis=-1, keepdims=True)
    xc = x - mu
    var = jnp.mean(xc * xc, axis=-1, keepdims=True)
    return xc * lax.rsqrt(var + LN_EPS) * g + b


def _split3(x):
    hi = x.astype(BF16).astype(F32)
    r = x - hi
    mid = r.astype(BF16).astype(F32)
    lo = r - mid
    return hi, mid, lo


def _init_stream_state(conv0_ref, ssm0_ref, k0_ref, v0_ref,
                       convbuf, st_ref, kf_ref, vf_ref, kb_ref, ksb_ref, vb_ref, vsb_ref):
    convbuf[CONV_PAD - (CONV_W - 1):CONV_PAD, :] = conv0_ref[0]
    st_ref[...] = ssm0_ref[0]
    k0 = k0_ref[0]
    v0 = v0_ref[0]
    kf_ref[0:WINDOW, :] = k0
    vf_ref[0:WINDOW, :] = v0
    kb_ref[0:WINDOW, :] = k0.astype(BF16)
    vb_ref[0:WINDOW, :] = v0.astype(BF16)
    ksb_ref[0:WINDOW, :] = pltpu.roll(k0, HEAD_DIM, axis=1).astype(BF16)
    vsb_ref[0:WINDOW, :] = pltpu.roll(v0, HEAD_DIM, axis=1).astype(BF16)


def _write_stream_state(conv_out_ref, ssm_out_ref, k_out_ref, v_out_ref, convbuf, st_ref, kf_ref, vf_ref):
    conv_out_ref[0] = convbuf[CONV_PAD - (CONV_W - 1):CONV_PAD, :]
    ssm_out_ref[0] = st_ref[...]
    k_out_ref[0] = kf_ref[0:WINDOW, :]
    v_out_ref[0] = vf_ref[0:WINDOW, :]


def _in_projection(tile, x_ref, w_in_ref, w_dt_ref,
                   convbuf, kf_ref, vf_ref, kb_ref, ksb_ref, vb_ref, vsb_ref, zbuf, qbuf, dtbuf, xb_ref):
    if x_ref is not None:
        xb_ref[...] = x_ref[0].astype(BF16)
    xb = xb_ref[...]
    zbuf[...] = _dot(xb, w_in_ref[:, COL_Z:COL_XBC])
    convbuf[CONV_PAD:CONV_PAD + tile, :] = _dot(xb, w_in_ref[:, COL_XBC:COL_Q])
    qbuf[...] = _dot(xb, w_in_ref[:, COL_Q:COL_K])
    kv = _dot(xb, w_in_ref[:, COL_K:COL_END])
    dtbuf[...] = _dot(xb, w_dt_ref[...])
    k_new = kv[:, :KV_W]
    v_new = kv[:, KV_W:]
    kf_ref[WINDOW:WINDOW + tile, :] = k_new
    vf_ref[WINDOW:WINDOW + tile, :] = v_new
    kb_ref[WINDOW:WINDOW + tile, :] = k_new.astype(BF16)
    vb_ref[WINDOW:WINDOW + tile, :] = v_new.astype(BF16)
    ksb_ref[WINDOW:WINDOW + tile, :] = pltpu.roll(k_new, HEAD_DIM, axis=1).astype(BF16)
    vsb_ref[WINDOW:WINDOW + tile, :] = pltpu.roll(v_new, HEAD_DIM, axis=1).astype(BF16)


def _mixer_tile(hist_all_valid, tile, first_chunk, x_ref, x_next_ref,
                w_in_ref, w_dt_ref, conv_w_ref, conv_b_ref, dt_bias_ref, a_log_ref, d_ref, norm_w_ref,
                expand_ref, bias_ref, sink_ref, w_out_ref, ln_g_ref, ln_b_ref,
                convbuf, st_ref, kf_ref, vf_ref, kb_ref, ksb_ref, vb_ref, vsb_ref, mixed_ref,
                zbuf, qbuf, dtbuf, xsbuf, bcbuf, cumx_ref, xdt_ref, xb_ref):
    n_chunks = tile // CHUNK
    if x_next_ref is not None:
        xb_ref[...] = x_next_ref[0].astype(BF16)
    yield 4

    row_in_chunk = lax.broadcasted_iota(jnp.int32, (CHUNK, LANES), 0)
    lane = lax.broadcasted_iota(jnp.int32, (CHUNK, LANES), 1)
    low_half = lane < HEAD_DIM
    li = lax.broadcasted_iota(jnp.int32, (CHUNK, GROUP_W), 0)
    si = lax.broadcasted_iota(jnp.int32, (CHUNK, GROUP_W), 1) % CHUNK
    diag_mask = li == si
    causal_mask = si <= li
    bd_mask = (lax.broadcasted_iota(jnp.int32, (GROUP_W, GROUP_W), 0) // CHUNK
               == lax.broadcasted_iota(jnp.int32, (GROUP_W, GROUP_W), 1) // CHUNK)

    def conv_and_decay(ci):
        r0 = ci * CHUNK
        rows = slice(r0, r0 + CHUNK)

        conv = conv_b_ref[...]
        for i in range(CONV_W):
            r = CONV_PAD - (CONV_W - 1) + i + r0
            conv = conv + conv_w_ref[i:i + 1, :] * convbuf[r:r + CHUNK, :]
        xc = _silu(conv)
        xsbuf[rows, :] = xc[:, :SSM_WIDTH]
        bcbuf[rows, :] = xc[:, SSM_WIDTH:].astype(BF16)

        dt_in = dtbuf[rows, :] + dt_bias_ref[...]
        dt = jnp.maximum(dt_in, 0.0) + jnp.log1p(jnp.exp(-jnp.abs(dt_in)))
        cum = dt * (-LOG2E * jnp.exp(a_log_ref[...]))
        step = 1
        while step < CHUNK:
            cum = cum + jnp.where(row_in_chunk >= step, pltpu.roll(cum, step, axis=0), 0.0)
            step *= 2
        packed = jnp.zeros((CHUNK, LANES), F32)
        for i, part in enumerate(_split3(cum) + _split3(dt)):
            packed = jnp.where((lane >= SSM_HEADS * i) & (lane < SSM_HEADS * (i + 1)), part, packed)
        return packed.astype(BF16)

    packed_next = conv_and_decay(0)
    for ci in range(n_chunks):
        r0 = ci * CHUNK
        rows = slice(r0, r0 + CHUNK)
        packed = packed_next

        expanded = _dot(packed, expand_ref[...])
        if hist_all_valid:
            variant = HIST_CHUNKS
        else:
            variant = jnp.minimum(first_chunk + ci, HIST_CHUNKS)
        group_cols = [slice(g * GROUP_W, (g + 1) * GROUP_W) for g in range(SSM_GROUPS)]
        b_gs = [bcbuf[rows, g * D_STATE:(g + 1) * D_STATE] for g in range(SSM_GROUPS)]
        c_gs = [bcbuf[rows, GROUP_W + g * D_STATE:GROUP_W + (g + 1) * D_STATE] for g in range(SSM_GROUPS)]
        cbs = [_dot_nt(c_gs[g], jnp.concatenate([b_gs[g]] * HEADS_PER_STACK, axis=0)) for g in range(SSM_GROUPS)]
        y_offs = [_dot(c_gs[g], st_ref[:, group_cols[g]].astype(BF16)) for g in range(SSM_GROUPS)]
        blocks = [qbuf[rows, j * LANES:(j + 1) * LANES] for j in range(ATTN_HEADS // 2)]
        evens = [jnp.where(low_half, blk, 0.0).astype(BF16) for blk in blocks]
        odds = [jnp.where(low_half, 0.0, blk).astype(BF16) for blk in blocks]
        q_a = jnp.concatenate([evens[0], evens[1], odds[2], odds[3]], axis=0)
        q_b = jnp.concatenate([odds[0], odds[1], evens[2], evens[3]], axis=0)
        scores = [_dot_nt(k_ref[r0:r0 + BAND, :], q_s) for q_s, k_ref in ((q_a, kb_ref), (q_b, ksb_ref))]
        yield 2

        if ci + 1 < n_chunks:
            packed_next = conv_and_decay(ci + 1)
        cumx_ref[...] = expanded[:, :SSM_WIDTH]
        xdt_ref[...] = xsbuf[rows, :] * expanded[:, SSM_WIDTH:]
        m_gs, bds, xws, probs = [], [], [], []
        for g in range(SSM_GROUPS):
            colb = cumx_ref[:, group_cols[g]]
            rowb = jnp.sum(jnp.where(diag_mask, colb, 0.0), axis=0, keepdims=True)
            lmat = jnp.exp2(jnp.where(causal_mask, colb - rowb, -jnp.inf))
            m_gs.append((cbs[g] * lmat).astype(BF16))
            xdt_g = xdt_ref[:, group_cols[g]]
            xdt_b = xdt_g.astype(BF16)
            bds.append(jnp.where(bd_mask, jnp.concatenate([xdt_b] * HEADS_PER_STACK, axis=0), jnp.zeros((), BF16)))
            last = colb[CHUNK - 1:CHUNK, :]
            xws.append((xdt_g * jnp.exp2(last - colb)).astype(BF16))
        for s_idx in range(2):
            s = scores[s_idx] + bias_ref[variant, s_idx]
            sink = sink_ref[s_idx]
            m = jnp.maximum(jnp.max(s, axis=0, keepdims=True), sink)
            p = jnp.exp2(s - m)
            denom = jnp.sum(p, axis=0, keepdims=True) + jnp.exp2(sink - m)
            probs.append((p * (1.0 / denom)).astype(BF16).T)
        b_ts = [b_g.T for b_g in b_gs]
        yield 2

        y_diags = [_dot(m_gs[g], bds[g]) for g in range(SSM_GROUPS)]
        updates = [_dot(b_ts[g], xws[g]) for g in range(SSM_GROUPS)]
        outs = [_dot(probs[s_idx], v_ref[r0:r0 + BAND, :])
                for s_idx, v_ref in enumerate((vb_ref, vsb_ref))]
        yield 2

        for g in range(SSM_GROUPS):
            cols = group_cols[g]
            colb = cumx_ref[:, cols]
            last = colb[CHUNK - 1:CHUNK, :]
            st_ref[:, cols] = jnp.exp2(last) * st_ref[:, cols] + updates[g]
            y = y_diags[g] + y_offs[g] * jnp.exp2(colb) + d_ref[:, cols] * xsbuf[rows, cols]
            gg = y * _silu(zbuf[rows, cols])
            ms = jnp.mean(gg * gg, axis=-1, keepdims=True)
            mixed_ref[rows, cols] = (gg * lax.rsqrt(ms + RMS_EPS) * norm_w_ref[:, cols]).astype(BF16)
        o_a, o_b = outs
        sel = [(o_a, o_b), (o_a, o_b), (o_b, o_a), (o_b, o_a)]
        for j in range(ATTN_HEADS // 2):
            ev, od = sel[j]
            blk = jnp.where(low_half, ev[j * CHUNK:(j + 1) * CHUNK], od[j * CHUNK:(j + 1) * CHUNK])
            mixed_ref[rows, SSM_WIDTH + j * LANES:SSM_WIDTH + (j + 1) * LANES] = blk.astype(BF16)

    convbuf[CONV_PAD - (CONV_W - 1):CONV_PAD, :] = convbuf[CONV_PAD + tile - (CONV_W - 1):CONV_PAD + tile, :]
    for ref in (kf_ref, vf_ref, kb_ref, ksb_ref, vb_ref, vsb_ref):
        ref[0:WINDOW, :] = ref[tile:tile + WINDOW, :]

    yield

    mix = _dot(mixed_ref[...], w_out_ref[...])
    if x_next_ref is not None:
        _in_projection(tile, None, w_in_ref, w_dt_ref,
                       convbuf, kf_ref, vf_ref, kb_ref, ksb_ref, vb_ref, vsb_ref, zbuf, qbuf, dtbuf, xb_ref)
    return _layer_norm(DEEPNORM_ALPHA * x_ref[0] + mix, ln_g_ref[...], ln_b_ref[...])


FFN_DOWN_AFTER = (6, N_FFN_CHUNKS)
N_FFN_UNITS = 2 * N_FFN_CHUNKS + len(FFN_DOWN_AFTER) * (D_MODEL // FFN_CHUNK)


def _ffn_tile(h_ref, hb_ref, act_ref, w_gate_ref, w_up_ref, w_down_ref, ln_g_ref, ln_b_ref):
    out_cols = [slice(n * FFN_CHUNK, (n + 1) * FFN_CHUNK) for n in range(D_MODEL // FFN_CHUNK)]
    pre = [DEEPNORM_ALPHA * h_ref[:, cols] for cols in out_cols]
    done = 0
    for c in range(N_FFN_CHUNKS):
        cols = slice(c * FFN_CHUNK, (c + 1) * FFN_CHUNK)
        gate = _dot(hb_ref[...], w_gate_ref[:, cols])
        yield
        up = _dot(hb_ref[...], w_up_ref[:, cols])
        act_ref[:, cols] = (_silu(gate) * up).astype(BF16)
        yield
        if c + 1 in FFN_DOWN_AFTER:
            k_rows = slice(done * FFN_CHUNK, (c + 1) * FFN_CHUNK)
            done = c + 1
            for n, cols in enumerate(out_cols):
                pre[n] = pre[n] + _dot(act_ref[:, k_rows], w_down_ref[k_rows, cols])
                yield
    return _layer_norm(jnp.concatenate(pre, axis=1), ln_g_ref[...], ln_b_ref[...])


def _run(gen):
    while True:
        try:
            next(gen)
        except StopIteration as stop:
            return stop.value


def _interleave(mixer, ffn):
    remaining = N_FFN_UNITS
    while True:
        try:
            wanted = next(mixer)
        except StopIteration as stop:
            return stop.value, y
        last = wanted is None
        for _ in range(remaining if last else min(wanted, remaining)):
            next(ffn)
            remaining -= 1
        if last:
            y = _run(ffn)


N_STATE_IN = 4
N_MIXER_W = 14
N_FFN_W = 5
N_STATE_OUT = 4
N_MIXER_SCRATCH = 17


def _fused_kernel(tile, n_tiles, n_total, *refs):
    x_ref, x_next_ref = refs[:2]
    state_in = refs[2:2 + N_STATE_IN]
    o = 2 + N_STATE_IN
    mixer_w = refs[o:o + N_MIXER_W]
    o += N_MIXER_W
    ffn_w = refs[o:o + N_FFN_W]
    o += N_FFN_W
    y_ref = refs[o]
    state_out = refs[o + 1:o + 1 + N_STATE_OUT]
    o += 1 + N_STATE_OUT
    mixer_s = refs[o:o + N_MIXER_SCRATCH]
    h_s, hb_s, act_s = refs[o + N_MIXER_SCRATCH:]
    state_s = mixer_s[:8]

    g = pl.program_id(0)
    t = jnp.minimum(g, n_total - 1) % n_tiles

    @pl.when(g == 0)
    def _():
        h_s[...] = jnp.zeros(h_s.shape, F32)
        hb_s[...] = jnp.zeros(hb_s.shape, BF16)
        _in_projection(tile, x_ref, *mixer_w[:2], *_in_projection_scratch(mixer_s))

    @pl.when(t == 0)
    def _():
        _init_stream_state(*state_in, *state_s)

    n_chunks = tile // CHUNK
    ffn = _ffn_tile(h_s, hb_s, act_s, *ffn_w)
    mixer = _mixer_tile(False, tile, t * n_chunks, x_ref, x_next_ref, *mixer_w, *mixer_s)
    h, y = _interleave(mixer, ffn)
    y_ref[0] = y
    h_s[...] = h
    hb_s[...] = h.astype(BF16)

    @pl.when((t == n_tiles - 1) & (g < n_total))
    def _():
        _write_stream_state(*state_out, convbuf=state_s[0], st_ref=state_s[1], kf_ref=state_s[2], vf_ref=state_s[3])


def _in_projection_scratch(mixer_s):
    return (mixer_s[0],) + tuple(mixer_s[2:8]) + tuple(mixer_s[9:12]) + (mixer_s[16],)


def _mixer_kernel(tile, *refs):
    x_ref = refs[0]
    state_in = refs[1:1 + N_STATE_IN]
    o = 1 + N_STATE_IN
    mixer_w = refs[o:o + N_MIXER_W]
    o += N_MIXER_W
    h_ref = refs[o]
    state_out = refs[o + 1:o + 1 + N_STATE_OUT]
    mixer_s = refs[o + 1 + N_STATE_OUT:]
    state_s = mixer_s[:8]
    _init_stream_state(*state_in, *state_s)
    _in_projection(tile, x_ref, *mixer_w[:2], *_in_projection_scratch(mixer_s))
    h_ref[0] = _run(_mixer_tile(True, tile, 0, x_ref, None, *mixer_w, *mixer_s))
    _write_stream_state(*state_out, convbuf=state_s[0], st_ref=state_s[1], kf_ref=state_s[2], vf_ref=state_s[3])


def _ffn_kernel(h_ref, *refs):
    ffn_w, y_ref = refs[:N_FFN_W], refs[N_FFN_W]
    hb_s, act_s = refs[N_FFN_W + 1:]
    hb_s[...] = h_ref[...].astype(BF16)
    y_ref[...] = _run(_ffn_tile(h_ref, hb_s, act_s, *ffn_w))


def _const_spec(shape):
    zeros = (0,) * len(shape)
    return pl.BlockSpec(shape, lambda *_: zeros, pipeline_mode=pl.Buffered(1))


def _mixer_scratch(tile):
    return [pltpu.VMEM((CONV_PAD + tile, CONV_DIM), F32),
            pltpu.VMEM((D_STATE, SSM_WIDTH), F32),
            pltpu.VMEM((WINDOW + tile, KV_W), F32), pltpu.VMEM((WINDOW + tile, KV_W), F32),
            pltpu.VMEM((WINDOW + tile, KV_W), BF16), pltpu.VMEM((WINDOW + tile, KV_W), BF16),
            pltpu.VMEM((WINDOW + tile, KV_W), BF16), pltpu.VMEM((WINDOW + tile, KV_W), BF16),
            pltpu.VMEM((tile, D_MODEL), BF16),
            pltpu.VMEM((tile, SSM_WIDTH), F32),
            pltpu.VMEM((tile, ATTN_WIDTH), F32),
            pltpu.VMEM((tile, LANES), F32),
            pltpu.VMEM((tile, SSM_WIDTH), F32),
            pltpu.VMEM((tile, CONV_DIM - SSM_WIDTH), BF16),
            pltpu.VMEM((CHUNK, SSM_WIDTH), F32),
            pltpu.VMEM((CHUNK, SSM_WIDTH), F32),
            pltpu.VMEM((tile, D_MODEL), BF16)]


_STATE_SHAPES = ((CONV_W - 1, CONV_DIM), (D_STATE, SSM_WIDTH), (WINDOW, KV_W), (WINDOW, KV_W))


def _attention_tables(attn_sinks):
    slopes = np.exp2(-8.0 * np.arange(1, ATTN_HEADS + 1, dtype=np.float64) / ATTN_HEADS)
    qi = np.arange(CHUNK)[None, :] + WINDOW
    kj = np.arange(BAND)[:, None]
    dist = np.abs(qi - kj).astype(np.float64)
    key_chunk = kj // CHUNK
    tables = np.zeros((HIST_CHUNKS + 1, 2, BAND, STACK_ROWS), np.float32)
    for variant in range(HIST_CHUNKS + 1):
        valid = key_chunk >= (HIST_CHUNKS - variant)
        for s_idx, heads in enumerate((STACK_A_HEADS, STACK_B_HEADS)):
            for i, h in enumerate(heads):
                tables[variant, s_idx, :, i * CHUNK:(i + 1) * CHUNK] = np.where(
                    valid, -slopes[h] * dist * LOG2E, -np.inf)
    order = np.array(STACK_A_HEADS + STACK_B_HEADS)
    sink_rows = jnp.repeat(attn_sinks.astype(F32)[order] * LOG2E, CHUNK).reshape(2, 1, STACK_ROWS)
    return jnp.asarray(tables), sink_rows


def _expansion_matrix():
    e = np.zeros((EXPAND_K, 2 * SSM_WIDTH), np.float32)
    for part in range(6):
        for h in range(SSM_HEADS):
            c0 = (part // 3) * SSM_WIDTH + h * HEAD_DIM
            e[part * SSM_HEADS + h, c0:c0 + HEAD_DIM] = 1.0
    return jnp.asarray(e, BF16)


def _prompt_call(x, state0, mixer_w, ffn_w):
    bsz, seq, _ = x.shape
    tile = PROMPT_TILE
    n_tiles = seq // tile
    n_total = bsz * n_tiles

    def mixer_tile_idx(g):
        gc = jnp.minimum(g, n_total - 1)
        return gc // n_tiles, gc % n_tiles

    def next_tile_idx(g):
        return mixer_tile_idx(g + 1)

    def ffn_tile_idx(g):
        gp = jnp.maximum(g - 1, 0)
        return gp // n_tiles, gp % n_tiles

    tile_spec = lambda idx: pl.BlockSpec((1, tile, D_MODEL), lambda g: (*idx(g), 0))
    stream_spec = lambda shape: pl.BlockSpec((1,) + shape, lambda g: (mixer_tile_idx(g)[0], 0, 0))
    in_specs = ([tile_spec(mixer_tile_idx), tile_spec(next_tile_idx)] + [stream_spec(s) for s in _STATE_SHAPES]
                + [_const_spec(w.shape) for w in mixer_w + ffn_w])
    out_specs = [tile_spec(ffn_tile_idx)] + [stream_spec(s) for s in _STATE_SHAPES]
    out_shape = ([jax.ShapeDtypeStruct((bsz, seq, D_MODEL), F32)]
                 + [jax.ShapeDtypeStruct((bsz,) + s, F32) for s in _STATE_SHAPES])
    scratch = _mixer_scratch(tile) + [pltpu.VMEM((tile, D_MODEL), F32), pltpu.VMEM((tile, D_MODEL), BF16),
                                      pltpu.VMEM((tile, D_FF), BF16)]
    return pl.pallas_call(
        functools.partial(_fused_kernel, tile, n_tiles, n_total),
        grid=(n_total + 1,), in_specs=in_specs, out_specs=out_specs, out_shape=out_shape,
        scratch_shapes=scratch, name="prompt_layer",
        compiler_params=pltpu.CompilerParams(dimension_semantics=("arbitrary",), vmem_limit_bytes=VMEM_LIMIT),
    )(x, x, *state0, *mixer_w, *ffn_w)


def _sample_mixer_call(x, state0, mixer_w):
    bsz, seq, _ = x.shape
    stream_spec = lambda shape: pl.BlockSpec((1,) + shape, lambda b: (b, 0, 0))
    in_specs = ([stream_spec((seq, D_MODEL))] + [stream_spec(s) for s in _STATE_SHAPES]
                + [_const_spec(w.shape) for w in mixer_w])
    out_specs = [stream_spec((seq, D_MODEL))] + [stream_spec(s) for s in _STATE_SHAPES]
    out_shape = ([jax.ShapeDtypeStruct((bsz, seq, D_MODEL), F32)]
                 + [jax.ShapeDtypeStruct((bsz,) + s, F32) for s in _STATE_SHAPES])
    return pl.pallas_call(
        functools.partial(_mixer_kernel, seq),
        grid=(bsz,), in_specs=in_specs, out_specs=out_specs, out_shape=out_shape,
        scratch_shapes=_mixer_scratch(seq), name="sample_mixer",
        compiler_params=pltpu.CompilerParams(dimension_semantics=("arbitrary",), vmem_limit_bytes=VMEM_LIMIT),
    )(x, *state0, *mixer_w)


def _ffn_call(h2d, ffn_w):
    n_tok = h2d.shape[0]
    tile = min(FFN_TILE, n_tok)
    tok_spec = pl.BlockSpec((tile, D_MODEL), lambda i: (i, 0))
    return pl.pallas_call(
        _ffn_kernel, grid=(n_tok // tile,),
        in_specs=[tok_spec] + [_const_spec(w.shape) for w in ffn_w],
        out_specs=tok_spec, out_shape=jax.ShapeDtypeStruct((n_tok, D_MODEL), F32), name="sample_ffn",
        scratch_shapes=[pltpu.VMEM((tile, D_MODEL), BF16), pltpu.VMEM((tile, D_FF), BF16)],
        compiler_params=pltpu.CompilerParams(dimension_semantics=("arbitrary",), vmem_limit_bytes=VMEM_LIMIT),
    )(h2d, *ffn_w)


def _state_to_kernel_layout(ssm):
    b = ssm.shape[0]
    return jnp.transpose(ssm, (0, 3, 1, 2)).reshape(b, D_STATE, SSM_WIDTH)


def _state_from_kernel_layout(ssm_t):
    b = ssm_t.shape[0]
    return jnp.transpose(ssm_t.reshape(b, D_STATE, SSM_HEADS, HEAD_DIM), (0, 2, 3, 1))


def kernel(x_prompt, x_sample, state_conv, state_ssm, cache_k, cache_v, w_in, conv_w, conv_b, dt_bias, a_log,
           d_skip, ssm_norm_w, attn_sinks, w_out, ln1_g, ln1_b, w_gate, w_up, w_down, ln2_g, ln2_b):
    w = w_in[0]
    split = np.cumsum([SSM_WIDTH, CONV_DIM, SSM_HEADS, ATTN_WIDTH, KV_W])
    w_z, w_xbc, w_dt, w_q, w_k, w_v = jnp.split(w, split, axis=1)
    w_in_r = jnp.concatenate([w_z, w_xbc, w_q * (HEAD_DIM ** -0.5 * LOG2E), w_k, w_v], axis=1).astype(BF16)
    rep = LANES // SSM_HEADS
    w_dt_rep = jnp.tile(w_dt, (1, rep)).astype(BF16)
    dt_bias_rep = jnp.tile(dt_bias[0].astype(F32), rep).reshape(1, LANES)
    a_log_rep = jnp.tile(a_log[0].astype(F32), rep).reshape(1, LANES)
    d_row = jnp.repeat(d_skip[0].astype(F32), HEAD_DIM).reshape(1, SSM_WIDTH)
    bias_tables, sink_rows = _attention_tables(attn_sinks[0])
    row = lambda v: v.astype(F32).reshape(1, -1)
    mixer_w = (w_in_r, w_dt_rep, conv_w[0].astype(F32), row(conv_b[0]), dt_bias_rep, a_log_rep, d_row,
               row(ssm_norm_w[0]), _expansion_matrix(), bias_tables, sink_rows,
               w_out[0].astype(BF16), row(ln1_g[0]), row(ln1_b[0]))
    ffn_w = (w_gate[0].astype(BF16), w_up[0].astype(BF16), w_down[0].astype(BF16), row(ln2_g[0]), row(ln2_b[0]))
    assert len(mixer_w) == N_MIXER_W and len(ffn_w) == N_FFN_W

    bsz = x_prompt.shape[0]
    dbsz, dseq, _ = x_sample.shape

    zero_state = tuple(jnp.zeros((bsz,) + s, F32) for s in _STATE_SHAPES)
    y_p, conv_p, ssm_p, k_p, v_p = _prompt_call(x_prompt, zero_state, mixer_w, ffn_w)

    sample_state = (state_conv[0].astype(F32), _state_to_kernel_layout(state_ssm[0].astype(F32)),
                    cache_k[0].astype(F32).reshape(dbsz, WINDOW, KV_W), cache_v[0].astype(F32).reshape(dbsz, WINDOW, KV_W))
    h_s, conv_s, ssm_s, k_s, v_s = _sample_mixer_call(x_sample, sample_state, mixer_w)
    y_s = _ffn_call(h_s.reshape(dbsz * dseq, D_MODEL), ffn_w).reshape(dbsz, dseq, D_MODEL)

    kv_shape = lambda a: a.reshape(1, a.shape[0], WINDOW, KV_HEADS, HEAD_DIM)
    return (y_p, y_s,
            conv_p[None], _state_from_kernel_layout(ssm_p)[None], kv_shape(k_p), kv_shape(v_p),
            conv_s[None], _state_from_kernel_layout(ssm_s)[None], kv_shape(k_s), kv_shape(v_s))
```

```python
import functools
import math

import jax
import jax.numpy as jnp
import numpy as np
from jax import lax
from jax.experimental import pallas as pl
from jax.experimental.pallas import tpu as pltpu

D_MODEL = 1024
CHUNK = 64
HEAD_DIM = 64
SSM_WIDTH = 512
SSM_HEADS = 8
SSM_GROUPS = 2
D_STATE = 128
CONV_W = 4
CONV_DIM = SSM_WIDTH + 2 * SSM_GROUPS * D_STATE
ATTN_WIDTH = 512
ATTN_HEADS = 8
KV_HEADS = 2
WINDOW = 128
D_FF = 2816
LN_EPS = 1e-5
RMS_EPS = 1e-5
DEEPNORM_ALPHA = 2.0 ** 0.25
LOG2E = math.log2(math.e)

LANES = 128
SUBLANES = 8
GROUP_W = SSM_WIDTH // SSM_GROUPS
KV_W = KV_HEADS * HEAD_DIM
BAND = WINDOW + CHUNK
HIST_CHUNKS = WINDOW // CHUNK
HEADS_PER_STACK = 4
STACK_ROWS = HEADS_PER_STACK * CHUNK
COL_Z, COL_XBC, COL_Q, COL_K, COL_V, COL_END = 0, 512, 1536, 2048, 2176, 2304
STACK_A_HEADS = (0, 2, 5, 7)
STACK_B_HEADS = (1, 3, 4, 6)
EXPAND_K = 128
CONV_PAD = SUBLANES

PROMPT_TILE = 256
FFN_TILE = 512
FFN_CHUNK = 256
N_FFN_CHUNKS = D_FF // FFN_CHUNK
VMEM_LIMIT = 56 * 1024 * 1024
UNITS_AT_START = 2
UNITS_AFTER_FIRST_MATMULS = 4
UNITS_AFTER_VECTOR_WORK = 2
UNITS_AFTER_SECOND_MATMULS = 1

F32 = jnp.float32
BF16 = jnp.bfloat16


def _dot(a, b):
    return jnp.dot(a, b, preferred_element_type=F32)


def _dot_nt(a, b):
    return lax.dot_general(a, b, (((1,), (1,)), ((), ())), preferred_element_type=F32)


def _dot_tn(a, b):
    return lax.dot_general(a, b, (((0,), (0,)), ((), ())), preferred_element_type=F32)


def _sigmoid(x):
    return 1.0 / (1.0 + jnp.exp2(x * -LOG2E))


def _silu(x):
    return x * _sigmoid(x)


def _layer_norm(x, g, b):
    mu = jnp.mean(x, axis=-1, keepdims=True)
    xc = x - mu
    var = jnp.mean(xc * xc, axis=-1, keepdims=True)
    return xc * lax.rsqrt(var + LN_EPS) * g + b


def _split3(x):
    hi = x.astype(BF16).astype(F32)
    r = x - hi
    mid = r.astype(BF16).astype(F32)
    lo = r - mid
    return hi, mid, lo


CONV_HIST = slice(CONV_PAD - (CONV_W - 1), CONV_PAD)


def _init_stream_state(ssm0_ref, k0_ref, v0_ref, st_ref, kf_ref, vf_ref, kb_ref, ksb_ref, vb_ref, vsb_ref):
    st_ref[...] = ssm0_ref[0]
    k0 = k0_ref[0]
    v0 = v0_ref[0]
    kf_ref[0:WINDOW, :] = k0
    vf_ref[0:WINDOW, :] = v0
    kb_ref[0:WINDOW, :] = k0.astype(BF16)
    vb_ref[0:WINDOW, :] = v0.astype(BF16)
    ksb_ref[0:WINDOW, :] = pltpu.roll(k0, HEAD_DIM, axis=1).astype(BF16)
    vsb_ref[0:WINDOW, :] = pltpu.roll(v0, HEAD_DIM, axis=1).astype(BF16)


def _write_stream_state(ssm_out_ref, k_out_ref, v_out_ref, st_ref, kf_ref, vf_ref):
    ssm_out_ref[0] = st_ref[...]
    k_out_ref[0] = kf_ref[0:WINDOW, :]
    v_out_ref[0] = vf_ref[0:WINDOW, :]


def _in_projection(tile, xb_ref, w_in_ref, w_dt_ref,
                   convbuf, kf_ref, vf_ref, kb_ref, ksb_ref, vb_ref, vsb_ref, zbuf, qbuf, dtbuf):
    xb = xb_ref[...]
    zbuf[...] = _dot(xb, w_in_ref[:, COL_Z:COL_XBC])
    convbuf[CONV_PAD:CONV_PAD + tile, :] = _dot(xb, w_in_ref[:, COL_XBC:COL_Q])
    qbuf[...] = _dot(xb, w_in_ref[:, COL_Q:COL_K])
    kv = _dot(xb, w_in_ref[:, COL_K:COL_END])
    dtbuf[...] = _dot(xb, w_dt_ref[...])
    k_new = kv[:, :KV_W]
    v_new = kv[:, KV_W:]
    kf_ref[WINDOW:WINDOW + tile, :] = k_new
    vf_ref[WINDOW:WINDOW + tile, :] = v_new
    kb_ref[WINDOW:WINDOW + tile, :] = k_new.astype(BF16)
    vb_ref[WINDOW:WINDOW + tile, :] = v_new.astype(BF16)
    ksb_ref[WINDOW:WINDOW + tile, :] = pltpu.roll(k_new, HEAD_DIM, axis=1).astype(BF16)
    vsb_ref[WINDOW:WINDOW + tile, :] = pltpu.roll(v_new, HEAD_DIM, axis=1).astype(BF16)


def _conv_and_decay(ci, conv_w_ref, conv_b_ref, dt_bias_ref, a_log_ref, convbuf, dtbuf, xsbuf, bcbuf):
    r0 = ci * CHUNK
    rows = slice(r0, r0 + CHUNK)

    conv = conv_b_ref[...]
    for i in range(CONV_W):
        r = CONV_PAD - (CONV_W - 1) + i + r0
        conv = conv + conv_w_ref[i:i + 1, :] * convbuf[r:r + CHUNK, :]
    xc = _silu(conv)
    xsbuf[rows, :] = xc[:, :SSM_WIDTH]
    bcbuf[rows, :] = xc[:, SSM_WIDTH:].astype(BF16)

    row_in_chunk = lax.broadcasted_iota(jnp.int32, (CHUNK, LANES), 0)
    lane = lax.broadcasted_iota(jnp.int32, (CHUNK, LANES), 1)
    dt_in = dtbuf[rows, :] + dt_bias_ref[...]
    dt = jnp.maximum(dt_in, 0.0) + jnp.log1p(jnp.exp(-jnp.abs(dt_in)))
    cum = dt * (-LOG2E * jnp.exp(a_log_ref[...]))
    step = 1
    while step < CHUNK:
        cum = cum + jnp.where(row_in_chunk >= step, pltpu.roll(cum, step, axis=0), 0.0)
        step *= 2
    packed = jnp.zeros((CHUNK, LANES), F32)
    for i, part in enumerate(_split3(cum) + _split3(dt)):
        packed = jnp.where((lane >= SSM_HEADS * i) & (lane < SSM_HEADS * (i + 1)), part, packed)
    return packed.astype(BF16)


def _prepare_tile(tile, x_ref, mixer_w, mixer_s):
    (w_in_ref, w_dt_ref, conv_w_ref, conv_b_ref, dt_bias_ref, a_log_ref) = mixer_w[:6]
    (convbuf, _, kf_ref, vf_ref, kb_ref, ksb_ref, vb_ref, vsb_ref, _, zbuf, qbuf, dtbuf, xsbuf, bcbuf, _, _,
     xb_ref, packbuf) = mixer_s
    if x_ref is not None:
        xb_ref[...] = x_ref[0].astype(BF16)
    _in_projection(tile, xb_ref, w_in_ref, w_dt_ref,
                   convbuf, kf_ref, vf_ref, kb_ref, ksb_ref, vb_ref, vsb_ref, zbuf, qbuf, dtbuf)
    packbuf[...] = _conv_and_decay(0, conv_w_ref, conv_b_ref, dt_bias_ref, a_log_ref, convbuf, dtbuf, xsbuf, bcbuf)


def _mixer_tile(hist_all_valid, tile, first_chunk, x_ref, x_next_ref, switch_stream, mixer_w, mixer_s):
    (_, _, conv_w_ref, conv_b_ref, dt_bias_ref, a_log_ref, d_ref, norm_w_ref,
     expand_ref, bias_ref, sink_ref, w_out_ref, ln_g_ref, ln_b_ref) = mixer_w
    (convbuf, st_ref, kf_ref, vf_ref, kb_ref, ksb_ref, vb_ref, vsb_ref, mixed_ref,
     zbuf, qbuf, dtbuf, xsbuf, bcbuf, cumx_ref, xdt_ref, xb_ref, packbuf) = mixer_s
    n_chunks = tile // CHUNK
    if x_next_ref is not None:
        xb_ref[...] = x_next_ref[0].astype(BF16)
    yield UNITS_AT_START

    lane = lax.broadcasted_iota(jnp.int32, (CHUNK, LANES), 1)
    low_half = lane < HEAD_DIM
    li = lax.broadcasted_iota(jnp.int32, (CHUNK, GROUP_W), 0)
    si = lax.broadcasted_iota(jnp.int32, (CHUNK, GROUP_W), 1) % CHUNK
    diag_mask = li == si
    causal_mask = si <= li
    bd_mask = (lax.broadcasted_iota(jnp.int32, (GROUP_W, GROUP_W), 0) // CHUNK
               == lax.broadcasted_iota(jnp.int32, (GROUP_W, GROUP_W), 1) // CHUNK)

    def conv_and_decay(ci):
        return _conv_and_decay(ci, conv_w_ref, conv_b_ref, dt_bias_ref, a_log_ref, convbuf, dtbuf, xsbuf, bcbuf)

    packed_next = packbuf[...]
    for ci in range(n_chunks):
        r0 = ci * CHUNK
        rows = slice(r0, r0 + CHUNK)
        packed = packed_next

        expanded = _dot(packed, expand_ref[...])
        if hist_all_valid:
            variant = HIST_CHUNKS
        else:
            variant = jnp.minimum(first_chunk + ci, HIST_CHUNKS)
        group_cols = [slice(g * GROUP_W, (g + 1) * GROUP_W) for g in range(SSM_GROUPS)]
        b_gs = [bcbuf[rows, g * D_STATE:(g + 1) * D_STATE] for g in range(SSM_GROUPS)]
        c_gs = [bcbuf[rows, GROUP_W + g * D_STATE:GROUP_W + (g + 1) * D_STATE] for g in range(SSM_GROUPS)]
        cbs = [_dot_nt(c_gs[g], jnp.concatenate([b_gs[g]] * HEADS_PER_STACK, axis=0)) for g in range(SSM_GROUPS)]
        y_offs = [_dot(c_gs[g], st_ref[:, group_cols[g]].astype(BF16)) for g in range(SSM_GROUPS)]
        blocks = [qbuf[rows, j * LANES:(j + 1) * LANES] for j in range(ATTN_HEADS // 2)]
        evens = [jnp.where(low_half, blk, 0.0).astype(BF16) for blk in blocks]
        odds = [jnp.where(low_half, 0.0, blk).astype(BF16) for blk in blocks]
        q_a = jnp.concatenate([evens[0], evens[1], odds[2], odds[3]], axis=0)
        q_b = jnp.concatenate([odds[0], odds[1], evens[2], evens[3]], axis=0)
        scores = [_dot_nt(k_ref[r0:r0 + BAND, :], q_s) for q_s, k_ref in ((q_a, kb_ref), (q_b, ksb_ref))]
        yield UNITS_AFTER_FIRST_MATMULS

        if ci + 1 < n_chunks:
            packed_next = conv_and_decay(ci + 1)
        cumx_ref[...] = expanded[:, :SSM_WIDTH]
        xdt_ref[...] = xsbuf[rows, :] * expanded[:, SSM_WIDTH:]
        m_gs, bds, xws, probs = [], [], [], []
        for g in range(SSM_GROUPS):
            colb = cumx_ref[:, group_cols[g]]
            rowb = jnp.sum(jnp.where(diag_mask, colb, 0.0), axis=0, keepdims=True)
            lmat = jnp.exp2(jnp.where(causal_mask, colb - rowb, -jnp.inf))
            m_gs.append((cbs[g] * lmat).astype(BF16))
            xdt_g = xdt_ref[:, group_cols[g]]
            xdt_b = xdt_g.astype(BF16)
            bds.append(jnp.where(bd_mask, jnp.concatenate([xdt_b] * HEADS_PER_STACK, axis=0), jnp.zeros((), BF16)))
            last = colb[CHUNK - 1:CHUNK, :]
            xws.append((xdt_g * jnp.exp2(last - colb)).astype(BF16))
        for s_idx in range(2):
            s = scores[s_idx] + bias_ref[variant, s_idx]
            sink = sink_ref[s_idx]
            m = jnp.maximum(jnp.max(s, axis=0, keepdims=True), sink)
            p = jnp.exp2(s - m)
            denom = jnp.sum(p, axis=0, keepdims=True) + jnp.exp2(sink - m)
            probs.append((p * (1.0 / denom)).astype(BF16).T)
        b_ts = [b_g.T for b_g in b_gs]
        yield UNITS_AFTER_VECTOR_WORK

        y_diags = [_dot(m_gs[g], bds[g]) for g in range(SSM_GROUPS)]
        updates = [_dot(b_ts[g], xws[g]) for g in range(SSM_GROUPS)]
        outs = [_dot(probs[s_idx], v_ref[r0:r0 + BAND, :])
                for s_idx, v_ref in enumerate((vb_ref, vsb_ref))]
        yield UNITS_AFTER_SECOND_MATMULS

        for g in range(SSM_GROUPS):
            cols = group_cols[g]
            colb = cumx_ref[:, cols]
            last = colb[CHUNK - 1:CHUNK, :]
            st_ref[:, cols] = jnp.exp2(last) * st_ref[:, cols] + updates[g]
            y = y_diags[g] + y_offs[g] * jnp.exp2(colb) + d_ref[:, cols] * xsbuf[rows, cols]
            gg = y * _silu(zbuf[rows, cols])
            ms = jnp.mean(gg * gg, axis=-1, keepdims=True)
            mixed_ref[rows, cols] = (gg * lax.rsqrt(ms + RMS_EPS) * norm_w_ref[:, cols]).astype(BF16)
        o_a, o_b = outs
        sel = [(o_a, o_b), (o_a, o_b), (o_b, o_a), (o_b, o_a)]
        for j in range(ATTN_HEADS // 2):
            ev, od = sel[j]
            blk = jnp.where(low_half, ev[j * CHUNK:(j + 1) * CHUNK], od[j * CHUNK:(j + 1) * CHUNK])
            mixed_ref[rows, SSM_WIDTH + j * LANES:SSM_WIDTH + (j + 1) * LANES] = blk.astype(BF16)

    convbuf[CONV_HIST, :] = convbuf[CONV_PAD + tile - (CONV_W - 1):CONV_PAD + tile, :]
    for ref in (kf_ref, vf_ref, kb_ref, ksb_ref, vb_ref, vsb_ref):
        ref[0:WINDOW, :] = ref[tile:tile + WINDOW, :]

    yield

    mix = _dot(mixed_ref[...], w_out_ref[...])
    if x_next_ref is not None:
        switch_stream()
        _prepare_tile(tile, None, mixer_w, mixer_s)
    return _layer_norm(DEEPNORM_ALPHA * x_ref[0] + mix, ln_g_ref[...], ln_b_ref[...])


FFN_DOWN_AFTER = (6, N_FFN_CHUNKS)
N_FFN_UNITS = 2 * N_FFN_CHUNKS + len(FFN_DOWN_AFTER) * (D_MODEL // FFN_CHUNK)


def _ffn_tile(h_ref, hb_ref, act_ref, w_gate_ref, w_up_ref, w_down_ref, ln_g_ref, ln_b_ref):
    out_cols = [slice(n * FFN_CHUNK, (n + 1) * FFN_CHUNK) for n in range(D_MODEL // FFN_CHUNK)]
    pre = [DEEPNORM_ALPHA * h_ref[:, cols] for cols in out_cols]
    done = 0
    for c in range(N_FFN_CHUNKS):
        cols = slice(c * FFN_CHUNK, (c + 1) * FFN_CHUNK)
        gate = _dot(hb_ref[...], w_gate_ref[:, cols])
        yield
        up = _dot(hb_ref[...], w_up_ref[:, cols])
        act_ref[:, cols] = (_silu(gate) * up).astype(BF16)
        yield
        if c + 1 in FFN_DOWN_AFTER:
            k_rows = slice(done * FFN_CHUNK, (c + 1) * FFN_CHUNK)
            done = c + 1
            for n, cols in enumerate(out_cols):
                pre[n] = pre[n] + _dot(act_ref[:, k_rows], w_down_ref[k_rows, cols])
                yield
    return _layer_norm(jnp.concatenate(pre, axis=1), ln_g_ref[...], ln_b_ref[...])


def _run(gen):
    while True:
        try:
            next(gen)
        except StopIteration as stop:
            return stop.value


def _interleave(mixer, ffn):
    remaining = N_FFN_UNITS
    while True:
        try:
            wanted = next(mixer)
        except StopIteration as stop:
            return stop.value, y
        last = wanted is None
        for _ in range(remaining if last else min(wanted, remaining)):
            next(ffn)
            remaining -= 1
        if last:
            y = _run(ffn)


N_MIXER_W = 14
N_FFN_W = 5
N_CARRIED = 3
N_MIXER_SCRATCH = 18


def _fused_kernel(tile, n_tiles, n_total, *refs):
    x_ref, x_next_ref, conv0_ref, conv0_next_ref = refs[:4]
    o = 4
    carried_in = refs[o:o + N_CARRIED]
    o += N_CARRIED
    mixer_w = refs[o:o + N_MIXER_W]
    o += N_MIXER_W
    ffn_w = refs[o:o + N_FFN_W]
    o += N_FFN_W
    y_ref, conv_out_ref = refs[o:o + 2]
    o += 2
    carried_out = refs[o:o + N_CARRIED]
    o += N_CARRIED
    mixer_s = refs[o:o + N_MIXER_SCRATCH]
    h_s, hb_s, act_s = refs[o + N_MIXER_SCRATCH:]
    convbuf, st_ref, kf_ref, vf_ref = mixer_s[:4]

    g = pl.program_id(0)
    t = jnp.minimum(g, n_total - 1) % n_tiles
    next_starts_stream = (g + 1 < n_total) & ((g + 1) % n_tiles == 0)

    @pl.when(g == 0)
    def _():
        h_s[...] = jnp.zeros(h_s.shape, F32)
        hb_s[...] = jnp.zeros(hb_s.shape, BF16)
        convbuf[CONV_HIST, :] = conv0_ref[0]
        _prepare_tile(tile, x_ref, mixer_w, mixer_s)

    @pl.when(t == 0)
    def _():
        _init_stream_state(*carried_in, *mixer_s[1:8])

    def switch_stream():
        hist = convbuf[CONV_HIST, :]
        conv_out_ref[0] = hist
        convbuf[CONV_HIST, :] = jnp.where(next_starts_stream, conv0_next_ref[0], hist)

    n_chunks = tile // CHUNK
    ffn = _ffn_tile(h_s, hb_s, act_s, *ffn_w)
    mixer = _mixer_tile(False, tile, t * n_chunks, x_ref, x_next_ref, switch_stream, mixer_w, mixer_s)
    h, y = _interleave(mixer, ffn)
    y_ref[0] = y
    h_s[...] = h
    hb_s[...] = h.astype(BF16)

    @pl.when((t == n_tiles - 1) & (g < n_total))
    def _():
        _write_stream_state(*carried_out, st_ref, kf_ref, vf_ref)


def _mixer_kernel(tile, *refs):
    x_ref, conv0_ref = refs[:2]
    o = 2
    carried_in = refs[o:o + N_CARRIED]
    o += N_CARRIED
    mixer_w = refs[o:o + N_MIXER_W]
    o += N_MIXER_W
    h_ref, conv_out_ref = refs[o:o + 2]
    o += 2
    carried_out = refs[o:o + N_CARRIED]
    mixer_s = refs[o + N_CARRIED:]
    convbuf, st_ref, kf_ref, vf_ref = mixer_s[:4]
    convbuf[CONV_HIST, :] = conv0_ref[0]
    _init_stream_state(*carried_in, *mixer_s[1:8])
    _prepare_tile(tile, x_ref, mixer_w, mixer_s)
    h_ref[0] = _run(_mixer_tile(True, tile, 0, x_ref, None, None, mixer_w, mixer_s))
    conv_out_ref[0] = convbuf[CONV_HIST, :]
    _write_stream_state(*carried_out, st_ref, kf_ref, vf_ref)


def _ffn_kernel(h_ref, *refs):
    ffn_w, y_ref = refs[:N_FFN_W], refs[N_FFN_W]
    hb_s, act_s = refs[N_FFN_W + 1:]
    hb_s[...] = h_ref[...].astype(BF16)
    y_ref[...] = _run(_ffn_tile(h_ref, hb_s, act_s, *ffn_w))


def _const_spec(shape):
    zeros = (0,) * len(shape)
    return pl.BlockSpec(shape, lambda *_: zeros, pipeline_mode=pl.Buffered(1))


def _mixer_scratch(tile):
    return [pltpu.VMEM((CONV_PAD + tile, CONV_DIM), F32),
            pltpu.VMEM((D_STATE, SSM_WIDTH), F32),
            pltpu.VMEM((WINDOW + tile, KV_W), F32), pltpu.VMEM((WINDOW + tile, KV_W), F32),
            pltpu.VMEM((WINDOW + tile, KV_W), BF16), pltpu.VMEM((WINDOW + tile, KV_W), BF16),
            pltpu.VMEM((WINDOW + tile, KV_W), BF16), pltpu.VMEM((WINDOW + tile, KV_W), BF16),
            pltpu.VMEM((tile, D_MODEL), BF16),
            pltpu.VMEM((tile, SSM_WIDTH), F32),
            pltpu.VMEM((tile, ATTN_WIDTH), F32),
            pltpu.VMEM((tile, LANES), F32),
            pltpu.VMEM((tile, SSM_WIDTH), F32),
            pltpu.VMEM((tile, CONV_DIM - SSM_WIDTH), BF16),
            pltpu.VMEM((CHUNK, SSM_WIDTH), F32),
            pltpu.VMEM((CHUNK, SSM_WIDTH), F32),
            pltpu.VMEM((tile, D_MODEL), BF16),
            pltpu.VMEM((CHUNK, LANES), BF16)]


_STATE_SHAPES = ((CONV_W - 1, CONV_DIM), (D_STATE, SSM_WIDTH), (WINDOW, KV_W), (WINDOW, KV_W))


def _attention_tables(attn_sinks):
    slopes = np.exp2(-8.0 * np.arange(1, ATTN_HEADS + 1, dtype=np.float64) / ATTN_HEADS)
    qi = np.arange(CHUNK)[None, :] + WINDOW
    kj = np.arange(BAND)[:, None]
    dist = np.abs(qi - kj).astype(np.float64)
    key_chunk = kj // CHUNK
    tables = np.zeros((HIST_CHUNKS + 1, 2, BAND, STACK_ROWS), np.float32)
    for variant in range(HIST_CHUNKS + 1):
        valid = key_chunk >= (HIST_CHUNKS - variant)
        for s_idx, heads in enumerate((STACK_A_HEADS, STACK_B_HEADS)):
            for i, h in enumerate(heads):
                tables[variant, s_idx, :, i * CHUNK:(i + 1) * CHUNK] = np.where(
                    valid, -slopes[h] * dist * LOG2E, -np.inf)
    order = np.array(STACK_A_HEADS + STACK_B_HEADS)
    sink_rows = jnp.repeat(attn_sinks.astype(F32)[order] * LOG2E, CHUNK).reshape(2, 1, STACK_ROWS)
    return jnp.asarray(tables), sink_rows


def _expansion_matrix():
    e = np.zeros((EXPAND_K, 2 * SSM_WIDTH), np.float32)
    for part in range(6):
        for h in range(SSM_HEADS):
            c0 = (part // 3) * SSM_WIDTH + h * HEAD_DIM
            e[part * SSM_HEADS + h, c0:c0 + HEAD_DIM] = 1.0
    return jnp.asarray(e, BF16)


def _prompt_call(x, state0, mixer_w, ffn_w):
    bsz, seq, _ = x.shape
    tile = PROMPT_TILE
    n_tiles = seq // tile
    n_total = bsz * n_tiles

    def mixer_tile_idx(g):
        gc = jnp.minimum(g, n_total - 1)
        return gc // n_tiles, gc % n_tiles

    def next_tile_idx(g):
        return mixer_tile_idx(g + 1)

    def ffn_tile_idx(g):
        gp = jnp.maximum(g - 1, 0)
        return gp // n_tiles, gp % n_tiles

    tile_spec = lambda idx: pl.BlockSpec((1, tile, D_MODEL), lambda g: (*idx(g), 0))
    stream_spec = lambda shape: pl.BlockSpec((1,) + shape, lambda g: (mixer_tile_idx(g)[0], 0, 0))
    next_stream_spec = lambda shape: pl.BlockSpec((1,) + shape, lambda g: (next_tile_idx(g)[0], 0, 0))
    in_specs = ([tile_spec(mixer_tile_idx), tile_spec(next_tile_idx),
                 stream_spec(_STATE_SHAPES[0]), next_stream_spec(_STATE_SHAPES[0])]
                + [stream_spec(s) for s in _STATE_SHAPES[1:]] + [_const_spec(w.shape) for w in mixer_w + ffn_w])
    out_specs = [tile_spec(ffn_tile_idx)] + [stream_spec(s) for s in _STATE_SHAPES]
    out_shape = ([jax.ShapeDtypeStruct((bsz, seq, D_MODEL), F32)]
                 + [jax.ShapeDtypeStruct((bsz,) + s, F32) for s in _STATE_SHAPES])
    scratch = _mixer_scratch(tile) + [pltpu.VMEM((tile, D_MODEL), F32), pltpu.VMEM((tile, D_MODEL), BF16),
                                      pltpu.VMEM((tile, D_FF), BF16)]
    return pl.pallas_call(
        functools.partial(_fused_kernel, tile, n_tiles, n_total),
        grid=(n_total + 1,), in_specs=in_specs, out_specs=out_specs, out_shape=out_shape,
        scratch_shapes=scratch, name="prompt_layer",
        compiler_params=pltpu.CompilerParams(dimension_semantics=("arbitrary",), vmem_limit_bytes=VMEM_LIMIT),
    )(x, x, state0[0], *state0, *mixer_w, *ffn_w)


def _sample_mixer_call(x, state0, mixer_w):
    bsz, seq, _ = x.shape
    stream_spec = lambda shape: pl.BlockSpec((1,) + shape, lambda b: (b, 0, 0))
    in_specs = ([stream_spec((seq, D_MODEL))] + [stream_spec(s) for s in _STATE_SHAPES]
                + [_const_spec(w.shape) for w in mixer_w])
    out_specs = [stream_spec((seq, D_MODEL))] + [stream_spec(s) for s in _STATE_SHAPES]
    out_shape = ([jax.ShapeDtypeStruct((bsz, seq, D_MODEL), F32)]
                 + [jax.ShapeDtypeStruct((bsz,) + s, F32) for s in _STATE_SHAPES])
    return pl.pallas_call(
        functools.partial(_mixer_kernel, seq),
        grid=(bsz,), in_specs=in_specs, out_specs=out_specs, out_shape=out_shape,
        scratch_shapes=_mixer_scratch(seq), name="sample_mixer",
        compiler_params=pltpu.CompilerParams(dimension_semantics=("arbitrary",), vmem_limit_bytes=VMEM_LIMIT),
    )(x, *state0, *mixer_w)


def _ffn_call(h2d, ffn_w):
    n_tok = h2d.shape[0]
    tile = min(FFN_TILE, n_tok)
    tok_spec = pl.BlockSpec((tile, D_MODEL), lambda i: (i, 0))
    return pl.pallas_call(
        _ffn_kernel, grid=(n_tok // tile,),
        in_specs=[tok_spec] + [_const_spec(w.shape) for w in ffn_w],
        out_specs=tok_spec, out_shape=jax.ShapeDtypeStruct((n_tok, D_MODEL), F32), name="sample_ffn",
        scratch_shapes=[pltpu.VMEM((tile, D_MODEL), BF16), pltpu.VMEM((tile, D_FF), BF16)],
        compiler_params=pltpu.CompilerParams(dimension_semantics=("arbitrary",), vmem_limit_bytes=VMEM_LIMIT),
    )(h2d, *ffn_w)


def _state_to_kernel_layout(ssm):
    b = ssm.shape[0]
    return jnp.transpose(ssm, (0, 3, 1, 2)).reshape(b, D_STATE, SSM_WIDTH)


def _state_from_kernel_layout(ssm_t):
    b = ssm_t.shape[0]
    return jnp.transpose(ssm_t.reshape(b, D_STATE, SSM_HEADS, HEAD_DIM), (0, 2, 3, 1))


def kernel(x_prompt, x_sample, state_conv, state_ssm, cache_k, cache_v, w_in, conv_w, conv_b, dt_bias, a_log,
           d_skip, ssm_norm_w, attn_sinks, w_out, ln1_g, ln1_b, w_gate, w_up, w_down, ln2_g, ln2_b):
    w = w_in[0]
    split = np.cumsum([SSM_WIDTH, CONV_DIM, SSM_HEADS, ATTN_WIDTH, KV_W])
    w_z, w_xbc, w_dt, w_q, w_k, w_v = jnp.split(w, split, axis=1)
    w_in_r = jnp.concatenate([w_z.astype(BF16), w_xbc.astype(BF16), (w_q * (HEAD_DIM ** -0.5 * LOG2E)).astype(BF16),
                              w_k.astype(BF16), w_v.astype(BF16)], axis=1)
    rep = LANES // SSM_HEADS
    w_dt_rep = jnp.tile(w_dt, (1, rep)).astype(BF16)
    dt_bias_rep = jnp.tile(dt_bias[0].astype(F32), rep).reshape(1, LANES)
    a_log_rep = jnp.tile(a_log[0].astype(F32), rep).reshape(1, LANES)
    d_row = jnp.repeat(d_skip[0].astype(F32), HEAD_DIM).reshape(1, SSM_WIDTH)
    bias_tables, sink_rows = _attention_tables(attn_sinks[0])
    row = lambda v: v.astype(F32).reshape(1, -1)
    mixer_w = (w_in_r, w_dt_rep, conv_w[0].astype(F32), row(conv_b[0]), dt_bias_rep, a_log_rep, d_row,
               row(ssm_norm_w[0]), _expansion_matrix(), bias_tables, sink_rows,
               w_out[0].astype(BF16), row(ln1_g[0]), row(ln1_b[0]))
    ffn_w = (w_gate[0].astype(BF16), w_up[0].astype(BF16), w_down[0].astype(BF16), row(ln2_g[0]), row(ln2_b[0]))
    assert len(mixer_w) == N_MIXER_W and len(ffn_w) == N_FFN_W

    bsz = x_prompt.shape[0]
    dbsz, dseq, _ = x_sample.shape

    zero_state = tuple(jnp.zeros((bsz,) + s, F32) for s in _STATE_SHAPES)
    y_p, conv_p, ssm_p, k_p, v_p = _prompt_call(x_prompt, zero_state, mixer_w, ffn_w)

    sample_state = (state_conv[0].astype(F32), _state_to_kernel_layout(state_ssm[0].astype(F32)),
                    cache_k[0].astype(F32).reshape(dbsz, WINDOW, KV_W), cache_v[0].astype(F32).reshape(dbsz, WINDOW, KV_W))
    h_s, conv_s, ssm_s, k_s, v_s = _sample_mixer_call(x_sample, sample_state, mixer_w)
    y_s = _ffn_call(h_s.reshape(dbsz * dseq, D_MODEL), ffn_w).reshape(dbsz, dseq, D_MODEL)

    kv_shape = lambda a: a.reshape(1, a.shape[0], WINDOW, KV_HEADS, HEAD_DIM)
    return (y_p, y_s,
            conv_p[None], _state_from_kernel_layout(ssm_p)[None], kv_shape(k_p), kv_shape(v_p),
            conv_s[None], _state_from_kernel_layout(ssm_s)[None], kv_shape(k_s), kv_shape(v_s))
```

```python
import functools
import math

import jax
import jax.numpy as jnp
import numpy as np
from jax import lax
from jax.experimental import pallas as pl
from jax.experimental.pallas import tpu as pltpu

D_MODEL = 1024
CHUNK = 64
HEAD_DIM = 64
SSM_WIDTH = 512
SSM_HEADS = 8
SSM_GROUPS = 2
D_STATE = 128
CONV_W = 4
CONV_DIM = SSM_WIDTH + 2 * SSM_GROUPS * D_STATE
ATTN_WIDTH = 512
ATTN_HEADS = 8
KV_HEADS = 2
WINDOW = 128
D_FF = 2816
LN_EPS = 1e-5
RMS_EPS = 1e-5
DEEPNORM_ALPHA = 2.0 ** 0.25
LOG2E = math.log2(math.e)

LANES = 128
SUBLANES = 8
GROUP_W = SSM_WIDTH // SSM_GROUPS
KV_W = KV_HEADS * HEAD_DIM
BAND = WINDOW + CHUNK
HIST_CHUNKS = WINDOW // CHUNK
HEADS_PER_STACK = 4
STACK_ROWS = HEADS_PER_STACK * CHUNK
COL_Z, COL_XBC, COL_Q, COL_K, COL_V, COL_END = 0, 512, 1536, 2048, 2176, 2304
STACK_A_HEADS = (0, 2, 5, 7)
STACK_B_HEADS = (1, 3, 4, 6)
EXPAND_K = 128
CONV_PAD = SUBLANES

PROMPT_TILE = 256
FFN_TILE = 512
FFN_CHUNK = 256
N_FFN_CHUNKS = D_FF // FFN_CHUNK
VMEM_LIMIT = 56 * 1024 * 1024
UNITS_AT_START = 2
UNITS_AFTER_FIRST_MATMULS = 4
UNITS_AFTER_VECTOR_WORK = 2
UNITS_AFTER_SECOND_MATMULS = 1

F32 = jnp.float32
BF16 = jnp.bfloat16


def _dot(a, b):
    return jnp.dot(a, b, preferred_element_type=F32)


def _dot_nt(a, b):
    return lax.dot_general(a, b, (((1,), (1,)), ((), ())), preferred_element_type=F32)


def _dot_tn(a, b):
    return lax.dot_general(a, b, (((0,), (0,)), ((), ())), preferred_element_type=F32)


def _sigmoid(x):
    return 1.0 / (1.0 + jnp.exp2(x * -LOG2E))


def _silu(x):
    return x * _sigmoid(x)


def _layer_norm(x, g, b):
    mu = jnp.mean(x, axis=-1, keepdims=True)
    xc = x - mu
    var = jnp.mean(xc * xc, axis=-1, keepdims=True)
    return xc * lax.rsqrt(var + LN_EPS) * g + b


def _split3(x):
    hi = x.astype(BF16).astype(F32)
    r = x - hi
    mid = r.astype(BF16).astype(F32)
    lo = r - mid
    return hi, mid, lo


CONV_HIST = slice(CONV_PAD - (CONV_W - 1), CONV_PAD)


def _init_stream_state(ssm0_ref, k0_ref, v0_ref, st_ref, kf_ref, vf_ref, kb_ref, ksb_ref, vb_ref, vsb_ref):
    st_ref[...] = ssm0_ref[0].T
    k0 = k0_ref[0].T
    v0 = v0_ref[0].T
    kf_ref[0:WINDOW, :] = k0
    vf_ref[0:WINDOW, :] = v0
    kb_ref[0:WINDOW, :] = k0.astype(BF16)
    vb_ref[0:WINDOW, :] = v0.astype(BF16)
    ksb_ref[0:WINDOW, :] = pltpu.roll(k0, HEAD_DIM, axis=1).astype(BF16)
    vsb_ref[0:WINDOW, :] = pltpu.roll(v0, HEAD_DIM, axis=1).astype(BF16)


def _write_stream_state(ssm_out_ref, k_out_ref, v_out_ref, st_ref, kf_ref, vf_ref):
    ssm_out_ref[0] = st_ref[...].T
    k_out_ref[0] = kf_ref[0:WINDOW, :].T
    v_out_ref[0] = vf_ref[0:WINDOW, :].T


def _in_projection(tile, xb_ref, w_in_ref, w_dt_ref,
                   convbuf, kf_ref, vf_ref, kb_ref, ksb_ref, vb_ref, vsb_ref, zbuf, qbuf, dtbuf):
    xb = xb_ref[...]
    zbuf[...] = _dot(xb, w_in_ref[:, COL_Z:COL_XBC])
    convbuf[CONV_PAD:CONV_PAD + tile, :] = _dot(xb, w_in_ref[:, COL_XBC:COL_Q])
    qbuf[...] = _dot(xb, w_in_ref[:, COL_Q:COL_K])
    kv = _dot(xb, w_in_ref[:, COL_K:COL_END])
    dtbuf[...] = _dot(xb, w_dt_ref[...])
    k_new = kv[:, :KV_W]
    v_new = kv[:, KV_W:]
    kf_ref[WINDOW:WINDOW + tile, :] = k_new
    vf_ref[WINDOW:WINDOW + tile, :] = v_new
    kb_ref[WINDOW:WINDOW + tile, :] = k_new.astype(BF16)
    vb_ref[WINDOW:WINDOW + tile, :] = v_new.astype(BF16)
    ksb_ref[WINDOW:WINDOW + tile, :] = pltpu.roll(k_new, HEAD_DIM, axis=1).astype(BF16)
    vsb_ref[WINDOW:WINDOW + tile, :] = pltpu.roll(v_new, HEAD_DIM, axis=1).astype(BF16)


def _conv_and_decay(ci, conv_w_ref, conv_b_ref, dt_bias_ref, a_log_ref, convbuf, dtbuf, xsbuf, bcbuf):
    r0 = ci * CHUNK
    rows = slice(r0, r0 + CHUNK)

    conv = conv_b_ref[...]
    for i in range(CONV_W):
        r = CONV_PAD - (CONV_W - 1) + i + r0
        conv = conv + conv_w_ref[i:i + 1, :] * convbuf[r:r + CHUNK, :]
    xc = _silu(conv)
    xsbuf[rows, :] = xc[:, :SSM_WIDTH]
    bcbuf[rows, :] = xc[:, SSM_WIDTH:].astype(BF16)

    row_in_chunk = lax.broadcasted_iota(jnp.int32, (CHUNK, LANES), 0)
    lane = lax.broadcasted_iota(jnp.int32, (CHUNK, LANES), 1)
    dt_in = dtbuf[rows, :] + dt_bias_ref[...]
    dt = jnp.maximum(dt_in, 0.0) + jnp.log1p(jnp.exp(-jnp.abs(dt_in)))
    cum = dt * (-LOG2E * jnp.exp(a_log_ref[...]))
    step = 1
    while step < CHUNK:
        cum = cum + jnp.where(row_in_chunk >= step, pltpu.roll(cum, step, axis=0), 0.0)
        step *= 2
    packed = jnp.zeros((CHUNK, LANES), F32)
    for i, part in enumerate(_split3(cum) + _split3(dt)):
        packed = jnp.where((lane >= SSM_HEADS * i) & (lane < SSM_HEADS * (i + 1)), part, packed)
    return packed.astype(BF16)


def _prepare_tile(tile, x_ref, mixer_w, mixer_s):
    (w_in_ref, w_dt_ref, conv_w_ref, conv_b_ref, dt_bias_ref, a_log_ref) = mixer_w[:6]
    (convbuf, _, kf_ref, vf_ref, kb_ref, ksb_ref, vb_ref, vsb_ref, _, zbuf, qbuf, dtbuf, xsbuf, bcbuf, _, _,
     xb_ref, packbuf) = mixer_s
    if x_ref is not None:
        xb_ref[...] = x_ref[0].astype(BF16)
    _in_projection(tile, xb_ref, w_in_ref, w_dt_ref,
                   convbuf, kf_ref, vf_ref, kb_ref, ksb_ref, vb_ref, vsb_ref, zbuf, qbuf, dtbuf)
    packbuf[...] = _conv_and_decay(0, conv_w_ref, conv_b_ref, dt_bias_ref, a_log_ref, convbuf, dtbuf, xsbuf, bcbuf)


def _mixer_tile(hist_all_valid, tile, first_chunk, x_ref, x_next_ref, switch_stream, mixer_w, mixer_s):
    (_, _, conv_w_ref, conv_b_ref, dt_bias_ref, a_log_ref, d_ref, norm_w_ref,
     expand_ref, bias_ref, sink_ref, w_out_ref, ln_g_ref, ln_b_ref) = mixer_w
    (convbuf, st_ref, kf_ref, vf_ref, kb_ref, ksb_ref, vb_ref, vsb_ref, mixed_ref,
     zbuf, qbuf, dtbuf, xsbuf, bcbuf, cumx_ref, xdt_ref, xb_ref, packbuf) = mixer_s
    n_chunks = tile // CHUNK
    if x_next_ref is not None:
        xb_ref[...] = x_next_ref[0].astype(BF16)
    yield UNITS_AT_START

    lane = lax.broadcasted_iota(jnp.int32, (CHUNK, LANES), 1)
    low_half = lane < HEAD_DIM
    li = lax.broadcasted_iota(jnp.int32, (CHUNK, GROUP_W), 0)
    si = lax.broadcasted_iota(jnp.int32, (CHUNK, GROUP_W), 1) % CHUNK
    diag_mask = li == si
    causal_mask = si <= li
    bd_mask = (lax.broadcasted_iota(jnp.int32, (GROUP_W, GROUP_W), 0) // CHUNK
               == lax.broadcasted_iota(jnp.int32, (GROUP_W, GROUP_W), 1) // CHUNK)

    def conv_and_decay(ci):
        return _conv_and_decay(ci, conv_w_ref, conv_b_ref, dt_bias_ref, a_log_ref, convbuf, dtbuf, xsbuf, bcbuf)

    packed_next = packbuf[...]
    for ci in range(n_chunks):
        r0 = ci * CHUNK
        rows = slice(r0, r0 + CHUNK)
        packed = packed_next

        expanded = _dot(packed, expand_ref[...])
        if hist_all_valid:
            variant = HIST_CHUNKS
        else:
            variant = jnp.minimum(first_chunk + ci, HIST_CHUNKS)
        group_cols = [slice(g * GROUP_W, (g + 1) * GROUP_W) for g in range(SSM_GROUPS)]
        b_gs = [bcbuf[rows, g * D_STATE:(g + 1) * D_STATE] for g in range(SSM_GROUPS)]
        c_gs = [bcbuf[rows, GROUP_W + g * D_STATE:GROUP_W + (g + 1) * D_STATE] for g in range(SSM_GROUPS)]
        cbs = [_dot_nt(c_gs[g], jnp.concatenate([b_gs[g]] * HEADS_PER_STACK, axis=0)) for g in range(SSM_GROUPS)]
        y_offs = [_dot(c_gs[g], st_ref[:, group_cols[g]].astype(BF16)) for g in range(SSM_GROUPS)]
        blocks = [qbuf[rows, j * LANES:(j + 1) * LANES] for j in range(ATTN_HEADS // 2)]
        evens = [jnp.where(low_half, blk, 0.0).astype(BF16) for blk in blocks]
        odds = [jnp.where(low_half, 0.0, blk).astype(BF16) for blk in blocks]
        q_a = jnp.concatenate([evens[0], evens[1], odds[2], odds[3]], axis=0)
        q_b = jnp.concatenate([odds[0], odds[1], evens[2], evens[3]], axis=0)
        scores = [_dot_nt(k_ref[r0:r0 + BAND, :], q_s) for q_s, k_ref in ((q_a, kb_ref), (q_b, ksb_ref))]
        yield UNITS_AFTER_FIRST_MATMULS

        if ci + 1 < n_chunks:
            packed_next = conv_and_decay(ci + 1)
        cumx_ref[...] = expanded[:, :SSM_WIDTH]
        xdt_ref[...] = xsbuf[rows, :] * expanded[:, SSM_WIDTH:]
        m_gs, bds, xws, probs = [], [], [], []
        for g in range(SSM_GROUPS):
            colb = cumx_ref[:, group_cols[g]]
            rowb = jnp.sum(jnp.where(diag_mask, colb, 0.0), axis=0, keepdims=True)
            lmat = jnp.exp2(jnp.where(causal_mask, colb - rowb, -jnp.inf))
            m_gs.append((cbs[g] * lmat).astype(BF16))
            xdt_g = xdt_ref[:, group_cols[g]]
            xdt_b = xdt_g.astype(BF16)
            bds.append(jnp.where(bd_mask, jnp.concatenate([xdt_b] * HEADS_PER_STACK, axis=0), jnp.zeros((), BF16)))
            last = colb[CHUNK - 1:CHUNK, :]
            xws.append((xdt_g * jnp.exp2(last - colb)).astype(BF16))
        for s_idx in range(2):
            s = scores[s_idx] + bias_ref[variant, s_idx]
            sink = sink_ref[s_idx]
            m = jnp.maximum(jnp.max(s, axis=0, keepdims=True), sink)
            p = jnp.exp2(s - m)
            denom = jnp.sum(p, axis=0, keepdims=True) + jnp.exp2(sink - m)
            probs.append((p * (1.0 / denom)).astype(BF16).T)
        b_ts = [b_g.T for b_g in b_gs]
        yield UNITS_AFTER_VECTOR_WORK

        y_diags = [_dot(m_gs[g], bds[g]) for g in range(SSM_GROUPS)]
        updates = [_dot(b_ts[g], xws[g]) for g in range(SSM_GROUPS)]
        outs = [_dot(probs[s_idx], v_ref[r0:r0 + BAND, :])
                for s_idx, v_ref in enumerate((vb_ref, vsb_ref))]
        yield UNITS_AFTER_SECOND_MATMULS

        for g in range(SSM_GROUPS):
            cols = group_cols[g]
            colb = cumx_ref[:, cols]
            last = colb[CHUNK - 1:CHUNK, :]
            st_ref[:, cols] = jnp.exp2(last) * st_ref[:, cols] + updates[g]
            y = y_diags[g] + y_offs[g] * jnp.exp2(colb) + d_ref[:, cols] * xsbuf[rows, cols]
            gg = y * _silu(zbuf[rows, cols])
            ms = jnp.mean(gg * gg, axis=-1, keepdims=True)
            mixed_ref[rows, cols] = (gg * lax.rsqrt(ms + RMS_EPS) * norm_w_ref[:, cols]).astype(BF16)
        o_a, o_b = outs
        sel = [(o_a, o_b), (o_a, o_b), (o_b, o_a), (o_b, o_a)]
        for j in range(ATTN_HEADS // 2):
            ev, od = sel[j]
            blk = jnp.where(low_half, ev[j * CHUNK:(j + 1) * CHUNK], od[j * CHUNK:(j + 1) * CHUNK])
            mixed_ref[rows, SSM_WIDTH + j * LANES:SSM_WIDTH + (j + 1) * LANES] = blk.astype(BF16)

    convbuf[CONV_HIST, :] = convbuf[CONV_PAD + tile - (CONV_W - 1):CONV_PAD + tile, :]
    for ref in (kf_ref, vf_ref, kb_ref, ksb_ref, vb_ref, vsb_ref):
        ref[0:WINDOW, :] = ref[tile:tile + WINDOW, :]

    yield

    mix = _dot(mixed_ref[...], w_out_ref[...])
    if x_next_ref is not None:
        switch_stream()
        _prepare_tile(tile, None, mixer_w, mixer_s)
    return _layer_norm(DEEPNORM_ALPHA * x_ref[0] + mix, ln_g_ref[...], ln_b_ref[...])


FFN_DOWN_AFTER = (6, N_FFN_CHUNKS)
N_FFN_UNITS = 2 * N_FFN_CHUNKS + len(FFN_DOWN_AFTER) * (D_MODEL // FFN_CHUNK)


def _ffn_tile(h_ref, hb_ref, act_ref, w_gate_ref, w_up_ref, w_down_ref, ln_g_ref, ln_b_ref):
    out_cols = [slice(n * FFN_CHUNK, (n + 1) * FFN_CHUNK) for n in range(D_MODEL // FFN_CHUNK)]
    pre = [DEEPNORM_ALPHA * h_ref[:, cols] for cols in out_cols]
    done = 0
    for c in range(N_FFN_CHUNKS):
        cols = slice(c * FFN_CHUNK, (c + 1) * FFN_CHUNK)
        gate = _dot(hb_ref[...], w_gate_ref[:, cols])
        yield
        up = _dot(hb_ref[...], w_up_ref[:, cols])
        act_ref[:, cols] = (_silu(gate) * up).astype(BF16)
        yield
        if c + 1 in FFN_DOWN_AFTER:
            k_rows = slice(done * FFN_CHUNK, (c + 1) * FFN_CHUNK)
            done = c + 1
            for n, cols in enumerate(out_cols):
                pre[n] = pre[n] + _dot(act_ref[:, k_rows], w_down_ref[k_rows, cols])
                yield
    return _layer_norm(jnp.concatenate(pre, axis=1), ln_g_ref[...], ln_b_ref[...])


def _run(gen):
    while True:
        try:
            next(gen)
        except StopIteration as stop:
            return stop.value


def _interleave(mixer, ffn):
    remaining = N_FFN_UNITS
    while True:
        try:
            wanted = next(mixer)
        except StopIteration as stop:
            return stop.value, y
        last = wanted is None
        for _ in range(remaining if last else min(wanted, remaining)):
            next(ffn)
            remaining -= 1
        if last:
            y = _run(ffn)


N_MIXER_W = 14
N_FFN_W = 5
N_CARRIED = 3
N_MIXER_SCRATCH = 18


def _fused_kernel(tile, n_tiles, n_total, *refs):
    x_ref, x_next_ref, conv0_ref, conv0_next_ref = refs[:4]
    o = 4
    carried_in = refs[o:o + N_CARRIED]
    o += N_CARRIED
    mixer_w = refs[o:o + N_MIXER_W]
    o += N_MIXER_W
    ffn_w = refs[o:o + N_FFN_W]
    o += N_FFN_W
    y_ref, conv_out_ref = refs[o:o + 2]
    o += 2
    carried_out = refs[o:o + N_CARRIED]
    o += N_CARRIED
    mixer_s = refs[o:o + N_MIXER_SCRATCH]
    h_s, hb_s, act_s = refs[o + N_MIXER_SCRATCH:]
    convbuf, st_ref, kf_ref, vf_ref = mixer_s[:4]

    g = pl.program_id(0)
    t = jnp.minimum(g, n_total - 1) % n_tiles
    next_starts_stream = (g + 1 < n_total) & ((g + 1) % n_tiles == 0)

    @pl.when(g == 0)
    def _():
        h_s[...] = jnp.zeros(h_s.shape, F32)
        hb_s[...] = jnp.zeros(hb_s.shape, BF16)
        convbuf[CONV_HIST, :] = conv0_ref[0]
        _prepare_tile(tile, x_ref, mixer_w, mixer_s)

    @pl.when(t == 0)
    def _():
        _init_stream_state(*carried_in, *mixer_s[1:8])

    def switch_stream():
        hist = convbuf[CONV_HIST, :]
        conv_out_ref[0] = hist
        convbuf[CONV_HIST, :] = jnp.where(next_starts_stream, conv0_next_ref[0], hist)

    n_chunks = tile // CHUNK
    ffn = _ffn_tile(h_s, hb_s, act_s, *ffn_w)
    mixer = _mixer_tile(False, tile, t * n_chunks, x_ref, x_next_ref, switch_stream, mixer_w, mixer_s)
    h, y = _interleave(mixer, ffn)
    y_ref[0] = y
    h_s[...] = h
    hb_s[...] = h.astype(BF16)

    @pl.when((t == n_tiles - 1) & (g < n_total))
    def _():
        _write_stream_state(*carried_out, st_ref, kf_ref, vf_ref)


def _mixer_kernel(tile, *refs):
    x_ref, conv0_ref = refs[:2]
    o = 2
    carried_in = refs[o:o + N_CARRIED]
    o += N_CARRIED
    mixer_w = refs[o:o + N_MIXER_W]
    o += N_MIXER_W
    h_ref, conv_out_ref = refs[o:o + 2]
    o += 2
    carried_out = refs[o:o + N_CARRIED]
    mixer_s = refs[o + N_CARRIED:]
    convbuf, st_ref, kf_ref, vf_ref = mixer_s[:4]
    convbuf[CONV_HIST, :] = conv0_ref[0]
    _init_stream_state(*carried_in, *mixer_s[1:8])
    _prepare_tile(tile, x_ref, mixer_w, mixer_s)
    h_ref[0] = _run(_mixer_tile(True, tile, 0, x_ref, None, None, mixer_w, mixer_s))
    conv_out_ref[0] = convbuf[CONV_HIST, :]
    _write_stream_state(*carried_out, st_ref, kf_ref, vf_ref)


def _ffn_kernel(h_ref, *refs):
    ffn_w, y_ref = refs[:N_FFN_W], refs[N_FFN_W]
    hb_s, act_s = refs[N_FFN_W + 1:]
    hb_s[...] = h_ref[...].astype(BF16)
    y_ref[...] = _run(_ffn_tile(h_ref, hb_s, act_s, *ffn_w))


def _const_spec(shape):
    zeros = (0,) * len(shape)
    return pl.BlockSpec(shape, lambda *_: zeros, pipeline_mode=pl.Buffered(1))


def _mixer_scratch(tile):
    return [pltpu.VMEM((CONV_PAD + tile, CONV_DIM), F32),
            pltpu.VMEM((D_STATE, SSM_WIDTH), F32),
            pltpu.VMEM((WINDOW + tile, KV_W), F32), pltpu.VMEM((WINDOW + tile, KV_W), F32),
            pltpu.VMEM((WINDOW + tile, KV_W), BF16), pltpu.VMEM((WINDOW + tile, KV_W), BF16),
            pltpu.VMEM((WINDOW + tile, KV_W), BF16), pltpu.VMEM((WINDOW + tile, KV_W), BF16),
            pltpu.VMEM((tile, D_MODEL), BF16),
            pltpu.VMEM((tile, SSM_WIDTH), F32),
            pltpu.VMEM((tile, ATTN_WIDTH), F32),
            pltpu.VMEM((tile, LANES), F32),
            pltpu.VMEM((tile, SSM_WIDTH), F32),
            pltpu.VMEM((tile, CONV_DIM - SSM_WIDTH), BF16),
            pltpu.VMEM((CHUNK, SSM_WIDTH), F32),
            pltpu.VMEM((CHUNK, SSM_WIDTH), F32),
            pltpu.VMEM((tile, D_MODEL), BF16),
            pltpu.VMEM((CHUNK, LANES), BF16)]


_STATE_SHAPES = ((CONV_W - 1, CONV_DIM), (SSM_WIDTH, D_STATE), (KV_W, WINDOW), (KV_W, WINDOW))


def _attention_tables(attn_sinks):
    slopes = np.exp2(-8.0 * np.arange(1, ATTN_HEADS + 1, dtype=np.float64) / ATTN_HEADS)
    qi = np.arange(CHUNK)[None, :] + WINDOW
    kj = np.arange(BAND)[:, None]
    dist = np.abs(qi - kj).astype(np.float64)
    key_chunk = kj // CHUNK
    tables = np.zeros((HIST_CHUNKS + 1, 2, BAND, STACK_ROWS), np.float32)
    for variant in range(HIST_CHUNKS + 1):
        valid = key_chunk >= (HIST_CHUNKS - variant)
        for s_idx, heads in enumerate((STACK_A_HEADS, STACK_B_HEADS)):
            for i, h in enumerate(heads):
                tables[variant, s_idx, :, i * CHUNK:(i + 1) * CHUNK] = np.where(
                    valid, -slopes[h] * dist * LOG2E, -np.inf)
    order = np.array(STACK_A_HEADS + STACK_B_HEADS)
    sink_rows = jnp.repeat(attn_sinks.astype(F32)[order] * LOG2E, CHUNK).reshape(2, 1, STACK_ROWS)
    return jnp.asarray(tables), sink_rows


def _expansion_matrix():
    e = np.zeros((EXPAND_K, 2 * SSM_WIDTH), np.float32)
    for part in range(6):
        for h in range(SSM_HEADS):
            c0 = (part // 3) * SSM_WIDTH + h * HEAD_DIM
            e[part * SSM_HEADS + h, c0:c0 + HEAD_DIM] = 1.0
    return jnp.asarray(e, BF16)


def _prompt_call(x, state0, mixer_w, ffn_w):
    bsz, seq, _ = x.shape
    tile = PROMPT_TILE
    n_tiles = seq // tile
    n_total = bsz * n_tiles

    def mixer_tile_idx(g):
        gc = jnp.minimum(g, n_total - 1)
        return gc // n_tiles, gc % n_tiles

    def next_tile_idx(g):
        return mixer_tile_idx(g + 1)

    def ffn_tile_idx(g):
        gp = jnp.maximum(g - 1, 0)
        return gp // n_tiles, gp % n_tiles

    tile_spec = lambda idx: pl.BlockSpec((1, tile, D_MODEL), lambda g: (*idx(g), 0))
    stream_spec = lambda shape: pl.BlockSpec((1,) + shape, lambda g: (mixer_tile_idx(g)[0], 0, 0))
    next_stream_spec = lambda shape: pl.BlockSpec((1,) + shape, lambda g: (next_tile_idx(g)[0], 0, 0))
    in_specs = ([tile_spec(mixer_tile_idx), tile_spec(next_tile_idx),
                 stream_spec(_STATE_SHAPES[0]), next_stream_spec(_STATE_SHAPES[0])]
                + [stream_spec(s) for s in _STATE_SHAPES[1:]] + [_const_spec(w.shape) for w in mixer_w + ffn_w])
    out_specs = [tile_spec(ffn_tile_idx)] + [stream_spec(s) for s in _STATE_SHAPES]
    out_shape = ([jax.ShapeDtypeStruct((bsz, seq, D_MODEL), F32)]
                 + [jax.ShapeDtypeStruct((bsz,) + s, F32) for s in _STATE_SHAPES])
    scratch = _mixer_scratch(tile) + [pltpu.VMEM((tile, D_MODEL), F32), pltpu.VMEM((tile, D_MODEL), BF16),
                                      pltpu.VMEM((tile, D_FF), BF16)]
    return pl.pallas_call(
        functools.partial(_fused_kernel, tile, n_tiles, n_total),
        grid=(n_total + 1,), in_specs=in_specs, out_specs=out_specs, out_shape=out_shape,
        scratch_shapes=scratch, name="prompt_layer",
        compiler_params=pltpu.CompilerParams(dimension_semantics=("arbitrary",), vmem_limit_bytes=VMEM_LIMIT),
    )(x, x, state0[0], *state0, *mixer_w, *ffn_w)


def _sample_mixer_call(x, state0, mixer_w):
    bsz, seq, _ = x.shape
    stream_spec = lambda shape: pl.BlockSpec((1,) + shape, lambda b: (b, 0, 0))
    in_specs = ([stream_spec((seq, D_MODEL))] + [stream_spec(s) for s in _STATE_SHAPES]
                + [_const_spec(w.shape) for w in mixer_w])
    out_specs = [stream_spec((seq, D_MODEL))] + [stream_spec(s) for s in _STATE_SHAPES]
    out_shape = ([jax.ShapeDtypeStruct((bsz, seq, D_MODEL), F32)]
                 + [jax.ShapeDtypeStruct((bsz,) + s, F32) for s in _STATE_SHAPES])
    return pl.pallas_call(
        functools.partial(_mixer_kernel, seq),
        grid=(bsz,), in_specs=in_specs, out_specs=out_specs, out_shape=out_shape,
        scratch_shapes=_mixer_scratch(seq), name="sample_mixer",
        compiler_params=pltpu.CompilerParams(dimension_semantics=("arbitrary",), vmem_limit_bytes=VMEM_LIMIT),
    )(x, *state0, *mixer_w)


def _ffn_call(h2d, ffn_w):
    n_tok = h2d.shape[0]
    tile = min(FFN_TILE, n_tok)
    tok_spec = pl.BlockSpec((tile, D_MODEL), lambda i: (i, 0))
    return pl.pallas_call(
        _ffn_kernel, grid=(n_tok // tile,),
        in_specs=[tok_spec] + [_const_spec(w.shape) for w in ffn_w],
        out_specs=tok_spec, out_shape=jax.ShapeDtypeStruct((n_tok, D_MODEL), F32), name="sample_ffn",
        scratch_shapes=[pltpu.VMEM((tile, D_MODEL), BF16), pltpu.VMEM((tile, D_FF), BF16)],
        compiler_params=pltpu.CompilerParams(dimension_semantics=("arbitrary",), vmem_limit_bytes=VMEM_LIMIT),
    )(h2d, *ffn_w)


def _window_to_kernel_layout(cache):
    b = cache.shape[0]
    return jnp.transpose(cache.astype(F32).reshape(b, WINDOW, KV_W), (0, 2, 1))


def _window_from_kernel_layout(win_t):
    b = win_t.shape[0]
    return jnp.transpose(win_t, (0, 2, 1)).reshape(1, b, WINDOW, KV_HEADS, HEAD_DIM)


def kernel(x_prompt, x_sample, state_conv, state_ssm, cache_k, cache_v, w_in, conv_w, conv_b, dt_bias, a_log,
           d_skip, ssm_norm_w, attn_sinks, w_out, ln1_g, ln1_b, w_gate, w_up, w_down, ln2_g, ln2_b):
    w_t = jnp.transpose(w_in[0])
    split = np.cumsum([SSM_WIDTH, CONV_DIM, SSM_HEADS, ATTN_WIDTH, KV_W])
    w_z, w_xbc, w_dt, w_q, w_k, w_v = jnp.split(w_t, split, axis=0)
    w_in_r = jnp.transpose(jnp.concatenate(
        [w_z.astype(BF16), w_xbc.astype(BF16), (w_q * (HEAD_DIM ** -0.5 * LOG2E)).astype(BF16),
         w_k.astype(BF16), w_v.astype(BF16)], axis=0))
    rep = LANES // SSM_HEADS
    w_dt_rep = jnp.transpose(jnp.tile(w_dt, (rep, 1)).astype(BF16))
    dt_bias_rep = jnp.tile(dt_bias[0].astype(F32), rep).reshape(1, LANES)
    a_log_rep = jnp.tile(a_log[0].astype(F32), rep).reshape(1, LANES)
    d_row = jnp.repeat(d_skip[0].astype(F32), HEAD_DIM).reshape(1, SSM_WIDTH)
    bias_tables, sink_rows = _attention_tables(attn_sinks[0])
    row = lambda v: v.astype(F32).reshape(1, -1)
    mixer_w = (w_in_r, w_dt_rep, conv_w[0].astype(F32), row(conv_b[0]), dt_bias_rep, a_log_rep, d_row,
               row(ssm_norm_w[0]), _expansion_matrix(), bias_tables, sink_rows,
               w_out[0].astype(BF16), row(ln1_g[0]), row(ln1_b[0]))
    ffn_w = (w_gate[0].astype(BF16), w_up[0].astype(BF16), w_down[0].astype(BF16), row(ln2_g[0]), row(ln2_b[0]))
    assert len(mixer_w) == N_MIXER_W and len(ffn_w) == N_FFN_W

    bsz = x_prompt.shape[0]
    dbsz, dseq, _ = x_sample.shape

    zero_state = tuple(jnp.zeros((bsz,) + s, F32) for s in _STATE_SHAPES)
    y_p, conv_p, ssm_p, k_p, v_p = _prompt_call(x_prompt, zero_state, mixer_w, ffn_w)

    sample_state = (state_conv[0].astype(F32), state_ssm[0].astype(F32).reshape(dbsz, SSM_WIDTH, D_STATE),
                    _window_to_kernel_layout(cache_k[0]), _window_to_kernel_layout(cache_v[0]))
    h_s, conv_s, ssm_s, k_s, v_s = _sample_mixer_call(x_sample, sample_state, mixer_w)
    y_s = _ffn_call(h_s.reshape(dbsz * dseq, D_MODEL), ffn_w).reshape(dbsz, dseq, D_MODEL)

    ssm_shape = lambda a: a.reshape(1, a.shape[0], SSM_HEADS, HEAD_DIM, D_STATE)
    kv_shape = _window_from_kernel_layout
    return (y_p, y_s,
            conv_p[None], ssm_shape(ssm_p), kv_shape(k_p), kv_shape(v_p),
            conv_s[None], ssm_shape(ssm_s), kv_shape(k_s), kv_shape(v_s))
```

```python
import functools
import math

import jax
import jax.numpy as jnp
import numpy as np
from jax import lax
from jax.experimental import pallas as pl
from jax.experimental.pallas import tpu as pltpu

D_MODEL = 1024
CHUNK = 64
HEAD_DIM = 64
SSM_WIDTH = 512
SSM_HEADS = 8
SSM_GROUPS = 2
D_STATE = 128
CONV_W = 4
CONV_DIM = SSM_WIDTH + 2 * SSM_GROUPS * D_STATE
ATTN_WIDTH = 512
ATTN_HEADS = 8
KV_HEADS = 2
WINDOW = 128
D_FF = 2816
LN_EPS = 1e-5
RMS_EPS = 1e-5
DEEPNORM_ALPHA = 2.0 ** 0.25
LOG2E = math.log2(math.e)

LANES = 128
SUBLANES = 8
GROUP_W = SSM_WIDTH // SSM_GROUPS
KV_W = KV_HEADS * HEAD_DIM
BAND = WINDOW + CHUNK
HIST_CHUNKS = WINDOW // CHUNK
HEADS_PER_STACK = 4
STACK_ROWS = HEADS_PER_STACK * CHUNK
COL_Z, COL_XBC, COL_Q, COL_K, COL_V, COL_END = 0, 512, 1536, 2048, 2176, 2304
STACK_A_HEADS = (0, 2, 5, 7)
STACK_B_HEADS = (1, 3, 4, 6)
EXPAND_K = 128
CONV_PAD = SUBLANES

PROMPT_TILE = 256
FFN_TILE = 512
FFN_CHUNK = 256
N_FFN_CHUNKS = D_FF // FFN_CHUNK
VMEM_LIMIT = 56 * 1024 * 1024
UNITS_AT_START = 2
UNITS_AFTER_FIRST_MATMULS = 4
UNITS_AFTER_VECTOR_WORK = 2
UNITS_AFTER_SECOND_MATMULS = 1

F32 = jnp.float32
BF16 = jnp.bfloat16


def _dot(a, b):
    return jnp.dot(a, b, preferred_element_type=F32)


def _dot_nt(a, b):
    return lax.dot_general(a, b, (((1,), (1,)), ((), ())), preferred_element_type=F32)


def _dot_tn(a, b):
    return lax.dot_general(a, b, (((0,), (0,)), ((), ())), preferred_element_type=F32)


def _sigmoid(x):
    return 1.0 / (1.0 + jnp.exp2(x * -LOG2E))


def _silu(x):
    return x * _sigmoid(x)


def _layer_norm(x, g, b):
    mu = jnp.mean(x, axis=-1, keepdims=True)
    xc = x - mu
    var = jnp.mean(xc * xc, axis=-1, keepdims=True)
    return xc * lax.rsqrt(var + LN_EPS) * g + b


def _split3(x):
    hi = x.astype(BF16).astype(F32)
    r = x - hi
    mid = r.astype(BF16).astype(F32)
    lo = r - mid
    return hi, mid, lo


CONV_HIST = slice(CONV_PAD - (CONV_W - 1), CONV_PAD)


def _init_stream_state(ssm0_ref, k0_ref, v0_ref, st_ref, kf_ref, vf_ref, kb_ref, ksb_ref, vb_ref, vsb_ref):
    st_ref[...] = ssm0_ref[0].T
    k0 = k0_ref[0].T
    v0 = v0_ref[0].T
    kf_ref[0:WINDOW, :] = k0
    vf_ref[0:WINDOW, :] = v0
    kb_ref[0:WINDOW, :] = k0.astype(BF16)
    vb_ref[0:WINDOW, :] = v0.astype(BF16)
    ksb_ref[0:WINDOW, :] = pltpu.roll(k0, HEAD_DIM, axis=1).astype(BF16)
    vsb_ref[0:WINDOW, :] = pltpu.roll(v0, HEAD_DIM, axis=1).astype(BF16)


def _write_stream_state(ssm_out_ref, k_out_ref, v_out_ref, st_ref, kf_ref, vf_ref):
    ssm_out_ref[0] = st_ref[...].T
    k_out_ref[0] = kf_ref[0:WINDOW, :].T
    v_out_ref[0] = vf_ref[0:WINDOW, :].T


def _in_projection(tile, xb_ref, w_in_ref, w_dt_ref,
                   convbuf, kf_ref, vf_ref, kb_ref, ksb_ref, vb_ref, vsb_ref, zbuf, qbuf, dtbuf):
    xb = xb_ref[...]
    zbuf[...] = _dot(xb, w_in_ref[:, COL_Z:COL_XBC])
    convbuf[CONV_PAD:CONV_PAD + tile, :] = _dot(xb, w_in_ref[:, COL_XBC:COL_Q])
    qbuf[...] = _dot(xb, w_in_ref[:, COL_Q:COL_K])
    kv = _dot(xb, w_in_ref[:, COL_K:COL_END])
    dtbuf[...] = _dot(xb, w_dt_ref[...])
    k_new = kv[:, :KV_W]
    v_new = kv[:, KV_W:]
    kf_ref[WINDOW:WINDOW + tile, :] = k_new
    vf_ref[WINDOW:WINDOW + tile, :] = v_new
    kb_ref[WINDOW:WINDOW + tile, :] = k_new.astype(BF16)
    vb_ref[WINDOW:WINDOW + tile, :] = v_new.astype(BF16)
    ksb_ref[WINDOW:WINDOW + tile, :] = pltpu.roll(k_new, HEAD_DIM, axis=1).astype(BF16)
    vsb_ref[WINDOW:WINDOW + tile, :] = pltpu.roll(v_new, HEAD_DIM, axis=1).astype(BF16)


def _conv_and_decay(ci, conv_w_ref, conv_b_ref, dt_bias_ref, a_log_ref, convbuf, dtbuf, xsbuf, bcbuf):
    r0 = ci * CHUNK
    rows = slice(r0, r0 + CHUNK)

    conv = conv_b_ref[...]
    for i in range(CONV_W):
        r = CONV_PAD - (CONV_W - 1) + i + r0
        conv = conv + conv_w_ref[i][...] * convbuf[r:r + CHUNK, :]
    xc = _silu(conv)
    xsbuf[rows, :] = xc[:, :SSM_WIDTH]
    bcbuf[rows, :] = xc[:, SSM_WIDTH:].astype(BF16)

    row_in_chunk = lax.broadcasted_iota(jnp.int32, (CHUNK, LANES), 0)
    lane = lax.broadcasted_iota(jnp.int32, (CHUNK, LANES), 1)
    dt_in = dtbuf[rows, :] + dt_bias_ref[...]
    dt = jnp.maximum(dt_in, 0.0) + jnp.log1p(jnp.exp(-jnp.abs(dt_in)))
    cum = dt * (-LOG2E * jnp.exp(a_log_ref[...]))
    step = 1
    while step < CHUNK:
        cum = cum + jnp.where(row_in_chunk >= step, pltpu.roll(cum, step, axis=0), 0.0)
        step *= 2
    packed = jnp.zeros((CHUNK, LANES), F32)
    for i, part in enumerate(_split3(cum) + _split3(dt)):
        packed = jnp.where((lane >= SSM_HEADS * i) & (lane < SSM_HEADS * (i + 1)), part, packed)
    return packed.astype(BF16)


def _prepare_tile(tile, x_ref, mixer_w, mixer_s):
    (w_in_ref, w_dt_ref, conv_w_ref, conv_b_ref, dt_bias_ref, a_log_ref) = mixer_w[:6]
    (convbuf, _, kf_ref, vf_ref, kb_ref, ksb_ref, vb_ref, vsb_ref, _, zbuf, qbuf, dtbuf, xsbuf, bcbuf, _, _,
     xb_ref, packbuf) = mixer_s
    if x_ref is not None:
        xb_ref[...] = x_ref[0].astype(BF16)
    _in_projection(tile, xb_ref, w_in_ref, w_dt_ref,
                   convbuf, kf_ref, vf_ref, kb_ref, ksb_ref, vb_ref, vsb_ref, zbuf, qbuf, dtbuf)
    packbuf[...] = _conv_and_decay(0, conv_w_ref, conv_b_ref, dt_bias_ref, a_log_ref, convbuf, dtbuf, xsbuf, bcbuf)


def _mixer_tile(hist_all_valid, tile, first_chunk, x_ref, x_next_ref, switch_stream, mixer_w, mixer_s):
    (_, _, conv_w_ref, conv_b_ref, dt_bias_ref, a_log_ref, d_ref, norm_w_ref,
     expand_ref, bias_ref, sink_ref, w_out_ref, ln_g_ref, ln_b_ref) = mixer_w
    (convbuf, st_ref, kf_ref, vf_ref, kb_ref, ksb_ref, vb_ref, vsb_ref, mixed_ref,
     zbuf, qbuf, dtbuf, xsbuf, bcbuf, cumx_ref, xdt_ref, xb_ref, packbuf) = mixer_s
    n_chunks = tile // CHUNK
    if x_next_ref is not None:
        xb_ref[...] = x_next_ref[0].astype(BF16)
    yield UNITS_AT_START

    lane = lax.broadcasted_iota(jnp.int32, (CHUNK, LANES), 1)
    low_half = lane < HEAD_DIM
    li = lax.broadcasted_iota(jnp.int32, (CHUNK, GROUP_W), 0)
    si = lax.broadcasted_iota(jnp.int32, (CHUNK, GROUP_W), 1) % CHUNK
    diag_mask = li == si
    causal_mask = si <= li
    bd_mask = (lax.broadcasted_iota(jnp.int32, (GROUP_W, GROUP_W), 0) // CHUNK
               == lax.broadcasted_iota(jnp.int32, (GROUP_W, GROUP_W), 1) // CHUNK)

    def conv_and_decay(ci):
        return _conv_and_decay(ci, conv_w_ref, conv_b_ref, dt_bias_ref, a_log_ref, convbuf, dtbuf, xsbuf, bcbuf)

    packed_next = packbuf[...]
    for ci in range(n_chunks):
        r0 = ci * CHUNK
        rows = slice(r0, r0 + CHUNK)
        packed = packed_next

        expanded = _dot(packed, expand_ref[...])
        if hist_all_valid:
            variant = HIST_CHUNKS
        else:
            variant = jnp.minimum(first_chunk + ci, HIST_CHUNKS)
        group_cols = [slice(g * GROUP_W, (g + 1) * GROUP_W) for g in range(SSM_GROUPS)]
        b_gs = [bcbuf[rows, g * D_STATE:(g + 1) * D_STATE] for g in range(SSM_GROUPS)]
        c_gs = [bcbuf[rows, GROUP_W + g * D_STATE:GROUP_W + (g + 1) * D_STATE] for g in range(SSM_GROUPS)]
        cbs = [_dot_nt(c_gs[g], jnp.concatenate([b_gs[g]] * HEADS_PER_STACK, axis=0)) for g in range(SSM_GROUPS)]
        y_offs = [_dot(c_gs[g], st_ref[:, group_cols[g]].astype(BF16)) for g in range(SSM_GROUPS)]
        blocks = [qbuf[rows, j * LANES:(j + 1) * LANES] for j in range(ATTN_HEADS // 2)]
        evens = [jnp.where(low_half, blk, 0.0).astype(BF16) for blk in blocks]
        odds = [jnp.where(low_half, 0.0, blk).astype(BF16) for blk in blocks]
        q_a = jnp.concatenate([evens[0], evens[1], odds[2], odds[3]], axis=0)
        q_b = jnp.concatenate([odds[0], odds[1], evens[2], evens[3]], axis=0)
        scores = [_dot_nt(k_ref[r0:r0 + BAND, :], q_s) for q_s, k_ref in ((q_a, kb_ref), (q_b, ksb_ref))]
        yield UNITS_AFTER_FIRST_MATMULS

        if ci + 1 < n_chunks:
            packed_next = conv_and_decay(ci + 1)
        cumx_ref[...] = expanded[:, :SSM_WIDTH]
        xdt_ref[...] = xsbuf[rows, :] * expanded[:, SSM_WIDTH:]
        m_gs, bds, xws, probs = [], [], [], []
        for g in range(SSM_GROUPS):
            colb = cumx_ref[:, group_cols[g]]
            rowb = jnp.sum(jnp.where(diag_mask, colb, 0.0), axis=0, keepdims=True)
            lmat = jnp.exp2(jnp.where(causal_mask, colb - rowb, -jnp.inf))
            m_gs.append((cbs[g] * lmat).astype(BF16))
            xdt_g = xdt_ref[:, group_cols[g]]
            xdt_b = xdt_g.astype(BF16)
            bds.append(jnp.where(bd_mask, jnp.concatenate([xdt_b] * HEADS_PER_STACK, axis=0), jnp.zeros((), BF16)))
            last = colb[CHUNK - 1:CHUNK, :]
            xws.append((xdt_g * jnp.exp2(last - colb)).astype(BF16))
        for s_idx in range(2):
            s = scores[s_idx] + bias_ref[variant, s_idx]
            sink = sink_ref[s_idx][...]
            m = jnp.maximum(jnp.max(s, axis=0, keepdims=True), sink)
            p = jnp.exp2(s - m)
            denom = jnp.sum(p, axis=0, keepdims=True) + jnp.exp2(sink - m)
            probs.append((p * (1.0 / denom)).astype(BF16).T)
        b_ts = [b_g.T for b_g in b_gs]
        yield UNITS_AFTER_VECTOR_WORK

        y_diags = [_dot(m_gs[g], bds[g]) for g in range(SSM_GROUPS)]
        updates = [_dot(b_ts[g], xws[g]) for g in range(SSM_GROUPS)]
        outs = [_dot(probs[s_idx], v_ref[r0:r0 + BAND, :])
                for s_idx, v_ref in enumerate((vb_ref, vsb_ref))]
        yield UNITS_AFTER_SECOND_MATMULS

        for g in range(SSM_GROUPS):
            cols = group_cols[g]
            colb = cumx_ref[:, cols]
            last = colb[CHUNK - 1:CHUNK, :]
            st_ref[:, cols] = jnp.exp2(last) * st_ref[:, cols] + updates[g]
            y = y_diags[g] + y_offs[g] * jnp.exp2(colb) + d_ref[:, cols] * xsbuf[rows, cols]
            gg = y * _silu(zbuf[rows, cols])
            ms = jnp.mean(gg * gg, axis=-1, keepdims=True)
            mixed_ref[rows, cols] = (gg * lax.rsqrt(ms + RMS_EPS) * norm_w_ref[:, cols]).astype(BF16)
        o_a, o_b = outs
        sel = [(o_a, o_b), (o_a, o_b), (o_b, o_a), (o_b, o_a)]
        for j in range(ATTN_HEADS // 2):
            ev, od = sel[j]
            blk = jnp.where(low_half, ev[j * CHUNK:(j + 1) * CHUNK], od[j * CHUNK:(j + 1) * CHUNK])
            mixed_ref[rows, SSM_WIDTH + j * LANES:SSM_WIDTH + (j + 1) * LANES] = blk.astype(BF16)

    convbuf[CONV_HIST, :] = convbuf[CONV_PAD + tile - (CONV_W - 1):CONV_PAD + tile, :]
    for ref in (kf_ref, vf_ref, kb_ref, ksb_ref, vb_ref, vsb_ref):
        ref[0:WINDOW, :] = ref[tile:tile + WINDOW, :]

    yield

    mix = _dot(mixed_ref[...], w_out_ref[...])
    if x_next_ref is not None:
        switch_stream()
        _prepare_tile(tile, None, mixer_w, mixer_s)
    return _layer_norm(DEEPNORM_ALPHA * x_ref[0] + mix, ln_g_ref[...], ln_b_ref[...])


FFN_DOWN_AFTER = (6, N_FFN_CHUNKS)
N_FFN_UNITS = 2 * N_FFN_CHUNKS + len(FFN_DOWN_AFTER) * (D_MODEL // FFN_CHUNK)


def _ffn_tile(h_ref, hb_ref, act_ref, w_gate_ref, w_up_ref, w_down_ref, ln_g_ref, ln_b_ref):
    out_cols = [slice(n * FFN_CHUNK, (n + 1) * FFN_CHUNK) for n in range(D_MODEL // FFN_CHUNK)]
    pre = [DEEPNORM_ALPHA * h_ref[:, cols] for cols in out_cols]
    done = 0
    for c in range(N_FFN_CHUNKS):
        cols = slice(c * FFN_CHUNK, (c + 1) * FFN_CHUNK)
        gate = _dot(hb_ref[...], w_gate_ref[:, cols])
        yield
        up = _dot(hb_ref[...], w_up_ref[:, cols])
        act_ref[:, cols] = (_silu(gate) * up).astype(BF16)
        yield
        if c + 1 in FFN_DOWN_AFTER:
            k_rows = slice(done * FFN_CHUNK, (c + 1) * FFN_CHUNK)
            done = c + 1
            for n, cols in enumerate(out_cols):
                pre[n] = pre[n] + _dot(act_ref[:, k_rows], w_down_ref[k_rows, cols])
                yield
    return _layer_norm(jnp.concatenate(pre, axis=1), ln_g_ref[...], ln_b_ref[...])


def _run(gen):
    while True:
        try:
            next(gen)
        except StopIteration as stop:
            return stop.value


def _interleave(mixer, ffn):
    remaining = N_FFN_UNITS
    while True:
        try:
            wanted = next(mixer)
        except StopIteration as stop:
            return stop.value, y
        last = wanted is None
        for _ in range(remaining if last else min(wanted, remaining)):
            next(ffn)
            remaining -= 1
        if last:
            y = _run(ffn)


N_MIXER_SCRATCH = 18

VEC_LAYOUT = (tuple(("conv_w%d" % i, CONV_DIM) for i in range(CONV_W))
              + (("conv_b", CONV_DIM), ("dt_bias", LANES), ("a_log", LANES), ("d_skip", SSM_WIDTH),
                 ("norm_w", SSM_WIDTH), ("sink_a", STACK_ROWS), ("sink_b", STACK_ROWS),
                 ("ln1_g", D_MODEL), ("ln1_b", D_MODEL), ("ln2_g", D_MODEL), ("ln2_b", D_MODEL)))
VEC_WIDTH = sum(width for _, width in VEC_LAYOUT)


def _vec_views(vec_ref):
    views, offset = {}, 0
    for name, width in VEC_LAYOUT:
        views[name] = vec_ref.at[:, offset:offset + width]
        offset += width
    return views


def _mixer_weights(w_in_ref, vec_ref, expand_ref, bias_ref, w_out_ref):
    v = _vec_views(vec_ref)
    return (w_in_ref.at[:, :COL_END], w_in_ref.at[:, COL_END:COL_END + LANES],
            tuple(v["conv_w%d" % i] for i in range(CONV_W)), v["conv_b"], v["dt_bias"], v["a_log"], v["d_skip"],
            v["norm_w"], expand_ref, bias_ref, (v["sink_a"], v["sink_b"]), w_out_ref, v["ln1_g"], v["ln1_b"])


def _ffn_weights(vec_ref, w_gate_ref, w_up_ref, w_down_ref):
    v = _vec_views(vec_ref)
    return (w_gate_ref, w_up_ref, w_down_ref, v["ln2_g"], v["ln2_b"])


def _fused_kernel(tile, n_tiles, n_total,
                  x_ref, x_next_ref, w_in_ref, vec_ref, expand_ref, bias_ref, w_out_ref, w_gate_ref, w_up_ref, w_down_ref,
                  y_ref, conv_out_ref, ssm_out_ref, k_out_ref, v_out_ref, *scratch):
    mixer_w = _mixer_weights(w_in_ref, vec_ref, expand_ref, bias_ref, w_out_ref)
    ffn_w = _ffn_weights(vec_ref, w_gate_ref, w_up_ref, w_down_ref)
    mixer_s = scratch[:N_MIXER_SCRATCH]
    h_s, hb_s, act_s = scratch[N_MIXER_SCRATCH:]
    convbuf, st_ref, kf_ref, vf_ref = mixer_s[:4]

    g = pl.program_id(0)
    t = jnp.minimum(g, n_total - 1) % n_tiles
    next_starts_stream = (g + 1 < n_total) & ((g + 1) % n_tiles == 0)

    @pl.when(g == 0)
    def _():
        h_s[...] = jnp.zeros(h_s.shape, F32)
        hb_s[...] = jnp.zeros(hb_s.shape, BF16)
        convbuf[CONV_HIST, :] = jnp.zeros((CONV_W - 1, CONV_DIM), F32)
        _prepare_tile(tile, x_ref, mixer_w, mixer_s)

    @pl.when(t == 0)
    def _():
        st_ref[...] = jnp.zeros(st_ref.shape, F32)
        for ref in mixer_s[2:8]:
            ref[0:WINDOW, :] = jnp.zeros((WINDOW, KV_W), ref.dtype)

    def switch_stream():
        hist = convbuf[CONV_HIST, :]
        conv_out_ref[0] = hist
        convbuf[CONV_HIST, :] = jnp.where(next_starts_stream, 0.0, hist)

    n_chunks = tile // CHUNK
    ffn = _ffn_tile(h_s, hb_s, act_s, *ffn_w)
    mixer = _mixer_tile(False, tile, t * n_chunks, x_ref, x_next_ref, switch_stream, mixer_w, mixer_s)
    h, y = _interleave(mixer, ffn)
    y_ref[0] = y
    h_s[...] = h
    hb_s[...] = h.astype(BF16)

    @pl.when((t == n_tiles - 1) & (g < n_total))
    def _():
        _write_stream_state(ssm_out_ref, k_out_ref, v_out_ref, st_ref, kf_ref, vf_ref)


def _mixer_kernel(tile, x_ref, conv0_ref, ssm0_ref, k0_ref, v0_ref,
                  w_in_ref, vec_ref, expand_ref, bias_ref, w_out_ref,
                  h_ref, conv_out_ref, ssm_out_ref, k_out_ref, v_out_ref, *mixer_s):
    mixer_w = _mixer_weights(w_in_ref, vec_ref, expand_ref, bias_ref, w_out_ref)
    convbuf, st_ref, kf_ref, vf_ref = mixer_s[:4]
    convbuf[CONV_HIST, :] = conv0_ref[0]
    _init_stream_state(ssm0_ref, k0_ref, v0_ref, *mixer_s[1:8])
    _prepare_tile(tile, x_ref, mixer_w, mixer_s)
    h_ref[0] = _run(_mixer_tile(True, tile, 0, x_ref, None, None, mixer_w, mixer_s))
    conv_out_ref[0] = convbuf[CONV_HIST, :]
    _write_stream_state(ssm_out_ref, k_out_ref, v_out_ref, st_ref, kf_ref, vf_ref)


def _ffn_kernel(h_ref, vec_ref, w_gate_ref, w_up_ref, w_down_ref, y_ref, hb_s, act_s):
    hb_s[...] = h_ref[...].astype(BF16)
    y_ref[...] = _run(_ffn_tile(h_ref, hb_s, act_s, *_ffn_weights(vec_ref, w_gate_ref, w_up_ref, w_down_ref)))


def _const_spec(shape):
    zeros = (0,) * len(shape)
    return pl.BlockSpec(shape, lambda *_: zeros, pipeline_mode=pl.Buffered(1))


def _mixer_scratch(tile):
    return [pltpu.VMEM((CONV_PAD + tile, CONV_DIM), F32),
            pltpu.VMEM((D_STATE, SSM_WIDTH), F32),
            pltpu.VMEM((WINDOW + tile, KV_W), F32), pltpu.VMEM((WINDOW + tile, KV_W), F32),
            pltpu.VMEM((WINDOW + tile, KV_W), BF16), pltpu.VMEM((WINDOW + tile, KV_W), BF16),
            pltpu.VMEM((WINDOW + tile, KV_W), BF16), pltpu.VMEM((WINDOW + tile, KV_W), BF16),
            pltpu.VMEM((tile, D_MODEL), BF16),
            pltpu.VMEM((tile, SSM_WIDTH), F32),
            pltpu.VMEM((tile, ATTN_WIDTH), F32),
            pltpu.VMEM((tile, LANES), F32),
            pltpu.VMEM((tile, SSM_WIDTH), F32),
            pltpu.VMEM((tile, CONV_DIM - SSM_WIDTH), BF16),
            pltpu.VMEM((CHUNK, SSM_WIDTH), F32),
            pltpu.VMEM((CHUNK, SSM_WIDTH), F32),
            pltpu.VMEM((tile, D_MODEL), BF16),
            pltpu.VMEM((CHUNK, LANES), BF16)]


_STATE_SHAPES = ((CONV_W - 1, CONV_DIM), (SSM_WIDTH, D_STATE), (KV_W, WINDOW), (KV_W, WINDOW))


def _attention_tables(attn_sinks):
    slopes = np.exp2(-8.0 * np.arange(1, ATTN_HEADS + 1, dtype=np.float64) / ATTN_HEADS)
    qi = np.arange(CHUNK)[None, :] + WINDOW
    kj = np.arange(BAND)[:, None]
    dist = np.abs(qi - kj).astype(np.float64)
    key_chunk = kj // CHUNK
    tables = np.zeros((HIST_CHUNKS + 1, 2, BAND, STACK_ROWS), np.float32)
    for variant in range(HIST_CHUNKS + 1):
        valid = key_chunk >= (HIST_CHUNKS - variant)
        for s_idx, heads in enumerate((STACK_A_HEADS, STACK_B_HEADS)):
            for i, h in enumerate(heads):
                tables[variant, s_idx, :, i * CHUNK:(i + 1) * CHUNK] = np.where(
                    valid, -slopes[h] * dist * LOG2E, -np.inf)
    order = np.array(STACK_A_HEADS + STACK_B_HEADS)
    sink_rows = jnp.repeat(attn_sinks.astype(F32)[order] * LOG2E, CHUNK).reshape(2, STACK_ROWS)
    return jnp.asarray(tables), sink_rows


def _expansion_matrix():
    e = np.zeros((EXPAND_K, 2 * SSM_WIDTH), np.float32)
    for part in range(6):
        for h in range(SSM_HEADS):
            c0 = (part // 3) * SSM_WIDTH + h * HEAD_DIM
            e[part * SSM_HEADS + h, c0:c0 + HEAD_DIM] = 1.0
    return jnp.asarray(e, BF16)


def _prompt_call(x, mixer_ops, ffn_ops):
    bsz, seq, _ = x.shape
    tile = PROMPT_TILE
    n_tiles = seq // tile
    n_total = bsz * n_tiles

    def mixer_tile_idx(g):
        gc = jnp.minimum(g, n_total - 1)
        return gc // n_tiles, gc % n_tiles

    def next_tile_idx(g):
        return mixer_tile_idx(g + 1)

    def ffn_tile_idx(g):
        gp = jnp.maximum(g - 1, 0)
        return gp // n_tiles, gp % n_tiles

    tile_spec = lambda idx: pl.BlockSpec((1, tile, D_MODEL), lambda g: (*idx(g), 0))
    stream_spec = lambda shape: pl.BlockSpec((1,) + shape, lambda g: (mixer_tile_idx(g)[0], 0, 0))
    in_specs = ([tile_spec(mixer_tile_idx), tile_spec(next_tile_idx)]
                + [_const_spec(w.shape) for w in mixer_ops + ffn_ops])
    out_specs = [tile_spec(ffn_tile_idx)] + [stream_spec(s) for s in _STATE_SHAPES]
    out_shape = ([jax.ShapeDtypeStruct((bsz, seq, D_MODEL), F32)]
                 + [jax.ShapeDtypeStruct((bsz,) + s, F32) for s in _STATE_SHAPES])
    scratch = _mixer_scratch(tile) + [pltpu.VMEM((tile, D_MODEL), F32), pltpu.VMEM((tile, D_MODEL), BF16),
                                      pltpu.VMEM((tile, D_FF), BF16)]
    return pl.pallas_call(
        functools.partial(_fused_kernel, tile, n_tiles, n_total),
        grid=(n_total + 1,), in_specs=in_specs, out_specs=out_specs, out_shape=out_shape,
        scratch_shapes=scratch, name="prompt_layer",
        compiler_params=pltpu.CompilerParams(dimension_semantics=("arbitrary",), vmem_limit_bytes=VMEM_LIMIT),
    )(x, x, *mixer_ops, *ffn_ops)


def _sample_mixer_call(x, state0, mixer_ops):
    bsz, seq, _ = x.shape
    stream_spec = lambda shape: pl.BlockSpec((1,) + shape, lambda b: (b, 0, 0))
    in_specs = ([stream_spec((seq, D_MODEL))] + [stream_spec(s) for s in _STATE_SHAPES]
                + [_const_spec(w.shape) for w in mixer_ops])
    out_specs = [stream_spec((seq, D_MODEL))] + [stream_spec(s) for s in _STATE_SHAPES]
    out_shape = ([jax.ShapeDtypeStruct((bsz, seq, D_MODEL), F32)]
                 + [jax.ShapeDtypeStruct((bsz,) + s, F32) for s in _STATE_SHAPES])
    return pl.pallas_call(
        functools.partial(_mixer_kernel, seq),
        grid=(bsz,), in_specs=in_specs, out_specs=out_specs, out_shape=out_shape,
        scratch_shapes=_mixer_scratch(seq), name="sample_mixer",
        compiler_params=pltpu.CompilerParams(dimension_semantics=("arbitrary",), vmem_limit_bytes=VMEM_LIMIT),
    )(x, *state0, *mixer_ops)


def _ffn_call(h2d, ffn_ops):
    n_tok = h2d.shape[0]
    tile = min(FFN_TILE, n_tok)
    tok_spec = pl.BlockSpec((tile, D_MODEL), lambda i: (i, 0))
    return pl.pallas_call(
        _ffn_kernel, grid=(n_tok // tile,),
        in_specs=[tok_spec] + [_const_spec(w.shape) for w in ffn_ops],
        out_specs=tok_spec, out_shape=jax.ShapeDtypeStruct((n_tok, D_MODEL), F32), name="sample_ffn",
        scratch_shapes=[pltpu.VMEM((tile, D_MODEL), BF16), pltpu.VMEM((tile, D_FF), BF16)],
        compiler_params=pltpu.CompilerParams(dimension_semantics=("arbitrary",), vmem_limit_bytes=VMEM_LIMIT),
    )(h2d, *ffn_ops)


def _window_to_kernel_layout(cache):
    b = cache.shape[0]
    return jnp.transpose(cache.astype(F32).reshape(b, WINDOW, KV_W), (0, 2, 1))


def _window_from_kernel_layout(win_t):
    b = win_t.shape[0]
    return jnp.transpose(win_t, (0, 2, 1)).reshape(1, b, WINDOW, KV_HEADS, HEAD_DIM)


def kernel(x_prompt, x_sample, state_conv, state_ssm, cache_k, cache_v, w_in, conv_w, conv_b, dt_bias, a_log,
           d_skip, ssm_norm_w, attn_sinks, w_out, ln1_g, ln1_b, w_gate, w_up, w_down, ln2_g, ln2_b):
    w_t = jnp.transpose(w_in[0])
    split = np.cumsum([SSM_WIDTH, CONV_DIM, SSM_HEADS, ATTN_WIDTH, KV_W])
    w_z, w_xbc, w_dt, w_q, w_k, w_v = jnp.split(w_t, split, axis=0)
    rep = LANES // SSM_HEADS
    w_dt_rep = jnp.tile(w_dt, (rep, 1)).astype(BF16)
    w_in_r = jnp.transpose(jnp.concatenate(
        [w_z.astype(BF16), w_xbc.astype(BF16), (w_q * (HEAD_DIM ** -0.5 * LOG2E)).astype(BF16),
         w_k.astype(BF16), w_v.astype(BF16), w_dt_rep], axis=0))
    bias_tables, sink_rows = _attention_tables(attn_sinks[0])
    vec_parts = {"conv_b": conv_b[0], "dt_bias": jnp.tile(dt_bias[0], rep), "a_log": jnp.tile(a_log[0], rep),
                 "d_skip": jnp.repeat(d_skip[0], HEAD_DIM), "norm_w": ssm_norm_w[0],
                 "sink_a": sink_rows[0], "sink_b": sink_rows[1],
                 "ln1_g": ln1_g[0], "ln1_b": ln1_b[0], "ln2_g": ln2_g[0], "ln2_b": ln2_b[0]}
    vec_parts.update({"conv_w%d" % i: conv_w[0, i] for i in range(CONV_W)})
    vecs = jnp.concatenate([vec_parts[name].astype(F32).reshape(width) for name, width in VEC_LAYOUT])
    vecs = vecs.reshape(1, VEC_WIDTH)
    mixer_ops = (w_in_r, vecs, _expansion_matrix(), bias_tables, w_out[0].astype(BF16))
    ffn_mats = (w_gate[0].astype(BF16), w_up[0].astype(BF16), w_down[0].astype(BF16))

    dbsz, dseq, _ = x_sample.shape

    y_p, conv_p, ssm_p, k_p, v_p = _prompt_call(x_prompt, mixer_ops, ffn_mats)

    sample_state = (state_conv[0].astype(F32), state_ssm[0].astype(F32).reshape(dbsz, SSM_WIDTH, D_STATE),
                    _window_to_kernel_layout(cache_k[0]), _window_to_kernel_layout(cache_v[0]))
    h_s, conv_s, ssm_s, k_s, v_s = _sample_mixer_call(x_sample, sample_state, mixer_ops)
    y_s = _ffn_call(h_s.reshape(dbsz * dseq, D_MODEL), (vecs,) + ffn_mats).reshape(dbsz, dseq, D_MODEL)

    ssm_shape = lambda a: a.reshape(1, a.shape[0], SSM_HEADS, HEAD_DIM, D_STATE)
    kv_shape = _window_from_kernel_layout
    return (y_p, y_s,
            conv_p[None], ssm_shape(ssm_p), kv_shape(k_p), kv_shape(v_p),
            conv_s[None], ssm_shape(ssm_s), kv_shape(k_s), kv_shape(v_s))
```

```python
import functools
import math

import jax
import jax.numpy as jnp
import numpy as np
from jax import lax
from jax.experimental import pallas as pl
from jax.experimental.pallas import tpu as pltpu

D_MODEL = 1024
CHUNK = 64
HEAD_DIM = 64
SSM_WIDTH = 512
SSM_HEADS = 8
SSM_GROUPS = 2
D_STATE = 128
CONV_W = 4
CONV_DIM = SSM_WIDTH + 2 * SSM_GROUPS * D_STATE
ATTN_WIDTH = 512
ATTN_HEADS = 8
KV_HEADS = 2
WINDOW = 128
D_FF = 2816
LN_EPS = 1e-5
RMS_EPS = 1e-5
DEEPNORM_ALPHA = 2.0 ** 0.25
LOG2E = math.log2(math.e)

LANES = 128
SUBLANES = 8
GROUP_W = SSM_WIDTH // SSM_GROUPS
KV_W = KV_HEADS * HEAD_DIM
BAND = WINDOW + CHUNK
HIST_CHUNKS = WINDOW // CHUNK
HEADS_PER_STACK = 4
STACK_ROWS = HEADS_PER_STACK * CHUNK
COL_Z, COL_XBC, COL_Q, COL_K, COL_V, COL_END = 0, 512, 1536, 2048, 2176, 2304
STACK_A_HEADS = (0, 2, 5, 7)
STACK_B_HEADS = (1, 3, 4, 6)
EXPAND_K = 128
CONV_PAD = SUBLANES

PROMPT_TILE = 256
FFN_TILE = 512
FFN_CHUNK = 256
N_FFN_CHUNKS = D_FF // FFN_CHUNK
VMEM_LIMIT = 56 * 1024 * 1024
UNITS_AT_START = 2
UNITS_AFTER_FIRST_MATMULS = 4
UNITS_AFTER_VECTOR_WORK = 2
UNITS_AFTER_SECOND_MATMULS = 1

F32 = jnp.float32
BF16 = jnp.bfloat16


def _dot(a, b):
    return jnp.dot(a, b, preferred_element_type=F32)


def _dot_nt(a, b):
    return lax.dot_general(a, b, (((1,), (1,)), ((), ())), preferred_element_type=F32)


def _dot_tn(a, b):
    return lax.dot_general(a, b, (((0,), (0,)), ((), ())), preferred_element_type=F32)


def _sigmoid(x):
    return 1.0 / (1.0 + jnp.exp2(x * -LOG2E))


def _silu(x):
    return x * _sigmoid(x)


def _layer_norm(x, g, b):
    mu = jnp.mean(x, axis=-1, keepdims=True)
    xc = x - mu
    var = jnp.mean(xc * xc, axis=-1, keepdims=True)
    return xc * lax.rsqrt(var + LN_EPS) * g + b


def _split3(x):
    hi = x.astype(BF16).astype(F32)
    r = x - hi
    mid = r.astype(BF16).astype(F32)
    lo = r - mid
    return hi, mid, lo


CONV_HIST = slice(CONV_PAD - (CONV_W - 1), CONV_PAD)


def _init_stream_state(ssm0_ref, k0_ref, v0_ref, st_ref, kf_ref, vf_ref, kb_ref, ksb_ref, vb_ref):
    st_ref[...] = ssm0_ref[0].T
    k0 = k0_ref[0].T
    v0 = v0_ref[0].T
    kf_ref[0:WINDOW, :] = k0
    vf_ref[0:WINDOW, :] = v0
    kb_ref[0:WINDOW, :] = k0.astype(BF16)
    vb_ref[0:WINDOW, :] = v0.astype(BF16)
    ksb_ref[0:WINDOW, :] = pltpu.roll(k0, HEAD_DIM, axis=1).astype(BF16)


def _write_stream_state(ssm_out_ref, k_out_ref, v_out_ref, st_ref, kf_ref, vf_ref):
    ssm_out_ref[0] = st_ref[...].T
    k_out_ref[0] = kf_ref[0:WINDOW, :].T
    v_out_ref[0] = vf_ref[0:WINDOW, :].T


def _in_projection(tile, xb_ref, w_in_ref, w_dt_ref,
                   convbuf, kf_ref, vf_ref, kb_ref, ksb_ref, vb_ref, zbuf, qbuf, dtbuf):
    xb = xb_ref[...]
    zbuf[...] = _dot(xb, w_in_ref[:, COL_Z:COL_XBC])
    convbuf[CONV_PAD:CONV_PAD + tile, :] = _dot(xb, w_in_ref[:, COL_XBC:COL_Q])
    qbuf[...] = _dot(xb, w_in_ref[:, COL_Q:COL_K])
    kv = _dot(xb, w_in_ref[:, COL_K:COL_END])
    dtbuf[...] = _dot(xb, w_dt_ref[...])
    k_new = kv[:, :KV_W]
    v_new = kv[:, KV_W:]
    kf_ref[WINDOW:WINDOW + tile, :] = k_new
    vf_ref[WINDOW:WINDOW + tile, :] = v_new
    kb_ref[WINDOW:WINDOW + tile, :] = k_new.astype(BF16)
    vb_ref[WINDOW:WINDOW + tile, :] = v_new.astype(BF16)
    ksb_ref[WINDOW:WINDOW + tile, :] = pltpu.roll(k_new, HEAD_DIM, axis=1).astype(BF16)


def _conv_and_decay(ci, conv_w_ref, conv_b_ref, dt_bias_ref, a_log_ref, convbuf, dtbuf, xsbuf, bcbuf):
    r0 = ci * CHUNK
    rows = slice(r0, r0 + CHUNK)

    conv = conv_b_ref[...]
    for i in range(CONV_W):
        r = CONV_PAD - (CONV_W - 1) + i + r0
        conv = conv + conv_w_ref[i][...] * convbuf[r:r + CHUNK, :]
    xc = _silu(conv)
    xsbuf[rows, :] = xc[:, :SSM_WIDTH]
    bcbuf[rows, :] = xc[:, SSM_WIDTH:].astype(BF16)

    row_in_chunk = lax.broadcasted_iota(jnp.int32, (CHUNK, LANES), 0)
    lane = lax.broadcasted_iota(jnp.int32, (CHUNK, LANES), 1)
    dt_in = dtbuf[rows, :] + dt_bias_ref[...]
    dt = jnp.maximum(dt_in, 0.0) + jnp.log1p(jnp.exp(-jnp.abs(dt_in)))
    cum = dt * (-LOG2E * jnp.exp(a_log_ref[...]))
    step = 1
    while step < CHUNK:
        cum = cum + jnp.where(row_in_chunk >= step, pltpu.roll(cum, step, axis=0), 0.0)
        step *= 2
    packed = jnp.zeros((CHUNK, LANES), F32)
    for i, part in enumerate(_split3(cum) + _split3(dt)):
        packed = jnp.where((lane >= SSM_HEADS * i) & (lane < SSM_HEADS * (i + 1)), part, packed)
    return packed.astype(BF16)


def _prepare_tile(tile, x_ref, mixer_w, mixer_s):
    (w_in_ref, w_dt_ref, conv_w_ref, conv_b_ref, dt_bias_ref, a_log_ref) = mixer_w[:6]
    (convbuf, _, kf_ref, vf_ref, kb_ref, ksb_ref, vb_ref, _, _, zbuf, qbuf, dtbuf, xsbuf, bcbuf, _, _,
     xb_ref, packbuf, _) = mixer_s
    if x_ref is not None:
        xb_ref[...] = x_ref[0].astype(BF16)
    _in_projection(tile, xb_ref, w_in_ref, w_dt_ref,
                   convbuf, kf_ref, vf_ref, kb_ref, ksb_ref, vb_ref, zbuf, qbuf, dtbuf)
    packbuf[...] = _conv_and_decay(0, conv_w_ref, conv_b_ref, dt_bias_ref, a_log_ref, convbuf, dtbuf, xsbuf, bcbuf)


def _mixer_tile(hist_all_valid, tile, first_chunk, x_ref, x_next_ref, switch_stream, mixer_w, mixer_s):
    (_, _, conv_w_ref, conv_b_ref, dt_bias_ref, a_log_ref, d_ref, norm_w_ref,
     expand_ref, bias_ref, sink_ref, w_out_ref, ln_g_ref, ln_b_ref) = mixer_w
    (convbuf, st_ref, kf_ref, vf_ref, kb_ref, ksb_ref, vb_ref, vt_even, mixed_ref,
     zbuf, qbuf, dtbuf, xsbuf, bcbuf, cumx_ref, xdt_ref, xb_ref, packbuf, vt_odd) = mixer_s
    n_chunks = tile // CHUNK
    if x_next_ref is not None:
        xb_ref[...] = x_next_ref[0].astype(BF16)
    vt_even[...] = vb_ref[...].T
    if n_chunks > 1:
        vt_odd[...] = vb_ref[CHUNK:CHUNK + vt_odd.shape[1], :].T
    yield UNITS_AT_START

    lane = lax.broadcasted_iota(jnp.int32, (CHUNK, LANES), 1)
    low_half = lane < HEAD_DIM
    li = lax.broadcasted_iota(jnp.int32, (CHUNK, GROUP_W), 0)
    si = lax.broadcasted_iota(jnp.int32, (CHUNK, GROUP_W), 1) % CHUNK
    diag_mask = li == si
    causal_mask = si <= li
    bd_mask = (lax.broadcasted_iota(jnp.int32, (GROUP_W, GROUP_W), 0) // CHUNK
               == lax.broadcasted_iota(jnp.int32, (GROUP_W, GROUP_W), 1) // CHUNK)

    def conv_and_decay(ci):
        return _conv_and_decay(ci, conv_w_ref, conv_b_ref, dt_bias_ref, a_log_ref, convbuf, dtbuf, xsbuf, bcbuf)

    packed_next = packbuf[...]
    for ci in range(n_chunks):
        r0 = ci * CHUNK
        rows = slice(r0, r0 + CHUNK)
        packed = packed_next

        expanded = _dot(packed, expand_ref[...])
        if hist_all_valid:
            variant = HIST_CHUNKS
        else:
            variant = jnp.minimum(first_chunk + ci, HIST_CHUNKS)
        group_cols = [slice(g * GROUP_W, (g + 1) * GROUP_W) for g in range(SSM_GROUPS)]
        b_gs = [bcbuf[rows, g * D_STATE:(g + 1) * D_STATE] for g in range(SSM_GROUPS)]
        c_gs = [bcbuf[rows, GROUP_W + g * D_STATE:GROUP_W + (g + 1) * D_STATE] for g in range(SSM_GROUPS)]
        cbs = [_dot_nt(c_gs[g], jnp.concatenate([b_gs[g]] * HEADS_PER_STACK, axis=0)) for g in range(SSM_GROUPS)]
        y_offs = [_dot(c_gs[g], st_ref[:, group_cols[g]].astype(BF16)) for g in range(SSM_GROUPS)]
        blocks = [qbuf[rows, j * LANES:(j + 1) * LANES] for j in range(ATTN_HEADS // 2)]
        evens = [jnp.where(low_half, blk, 0.0).astype(BF16) for blk in blocks]
        odds = [jnp.where(low_half, 0.0, blk).astype(BF16) for blk in blocks]
        q_a = jnp.concatenate([evens[0], evens[1], odds[2], odds[3]], axis=0)
        q_b = jnp.concatenate([odds[0], odds[1], evens[2], evens[3]], axis=0)
        scores = [_dot_nt(k_ref[r0:r0 + BAND, :], q_s) for q_s, k_ref in ((q_a, kb_ref), (q_b, ksb_ref))]
        yield UNITS_AFTER_FIRST_MATMULS

        if ci + 1 < n_chunks:
            packed_next = conv_and_decay(ci + 1)
        cumx_ref[...] = expanded[:, :SSM_WIDTH]
        xdt_ref[...] = xsbuf[rows, :] * expanded[:, SSM_WIDTH:]
        m_gs, bds, xws, probs = [], [], [], []
        for g in range(SSM_GROUPS):
            colb = cumx_ref[:, group_cols[g]]
            rowb = jnp.sum(jnp.where(diag_mask, colb, 0.0), axis=0, keepdims=True)
            lmat = jnp.exp2(jnp.where(causal_mask, colb - rowb, -jnp.inf))
            m_gs.append((cbs[g] * lmat).astype(BF16))
            xdt_g = xdt_ref[:, group_cols[g]]
            xdt_b = xdt_g.astype(BF16)
            bds.append(jnp.where(bd_mask, jnp.concatenate([xdt_b] * HEADS_PER_STACK, axis=0), jnp.zeros((), BF16)))
            last = colb[CHUNK - 1:CHUNK, :]
            xws.append((xdt_g * jnp.exp2(last - colb)).astype(BF16))
        for s_idx in range(2):
            s = scores[s_idx] + bias_ref[variant, s_idx]
            sink = sink_ref[s_idx][...]
            m = jnp.maximum(jnp.max(s, axis=0, keepdims=True), sink)
            p = jnp.exp2(s - m)
            denom = jnp.sum(p, axis=0, keepdims=True) + jnp.exp2(sink - m)
            probs.append((p * (1.0 / denom)).astype(BF16))
        b_ts = [b_g.T for b_g in b_gs]
        yield UNITS_AFTER_VECTOR_WORK

        y_diags = [_dot(m_gs[g], bds[g]) for g in range(SSM_GROUPS)]
        updates = [_dot(b_ts[g], xws[g]) for g in range(SSM_GROUPS)]
        vt_src, c0 = (vt_even, r0) if ci % 2 == 0 else (vt_odd, r0 - CHUNK)
        vband_t = vt_src[:, c0:c0 + BAND]
        vband_t_swapped = jnp.concatenate([vband_t[HEAD_DIM:], vband_t[:HEAD_DIM]], axis=0)
        outs_t = [_dot(vband_t, probs[0]), _dot(vband_t_swapped, probs[1])]
        yield UNITS_AFTER_SECOND_MATMULS

        for g in range(SSM_GROUPS):
            cols = group_cols[g]
            colb = cumx_ref[:, cols]
            last = colb[CHUNK - 1:CHUNK, :]
            st_ref[:, cols] = jnp.exp2(last) * st_ref[:, cols] + updates[g]
            y = y_diags[g] + y_offs[g] * jnp.exp2(colb) + d_ref[:, cols] * xsbuf[rows, cols]
            gg = y * _silu(zbuf[rows, cols])
            ms = jnp.mean(gg * gg, axis=-1, keepdims=True)
            mixed_ref[rows, cols] = (gg * lax.rsqrt(ms + RMS_EPS) * norm_w_ref[:, cols]).astype(BF16)
        o_a, o_b = [o_t.T for o_t in outs_t]
        sel = [(o_a, o_b), (o_a, o_b), (o_b, o_a), (o_b, o_a)]
        for j in range(ATTN_HEADS // 2):
            ev, od = sel[j]
            blk = jnp.where(low_half, ev[j * CHUNK:(j + 1) * CHUNK], od[j * CHUNK:(j + 1) * CHUNK])
            mixed_ref[rows, SSM_WIDTH + j * LANES:SSM_WIDTH + (j + 1) * LANES] = blk.astype(BF16)

    convbuf[CONV_HIST, :] = convbuf[CONV_PAD + tile - (CONV_W - 1):CONV_PAD + tile, :]
    for ref in (kf_ref, vf_ref, kb_ref, ksb_ref, vb_ref):
        ref[0:WINDOW, :] = ref[tile:tile + WINDOW, :]

    yield

    mix = _dot(mixed_ref[...], w_out_ref[...])
    if x_next_ref is not None:
        switch_stream()
        _prepare_tile(tile, None, mixer_w, mixer_s)
    return _layer_norm(DEEPNORM_ALPHA * x_ref[0] + mix, ln_g_ref[...], ln_b_ref[...])


FFN_DOWN_AFTER = (6, N_FFN_CHUNKS)
N_FFN_UNITS = 2 * N_FFN_CHUNKS + len(FFN_DOWN_AFTER) * (D_MODEL // FFN_CHUNK)


def _ffn_tile(h_ref, hb_ref, act_ref, w_gate_ref, w_up_ref, w_down_ref, ln_g_ref, ln_b_ref):
    out_cols = [slice(n * FFN_CHUNK, (n + 1) * FFN_CHUNK) for n in range(D_MODEL // FFN_CHUNK)]
    pre = [DEEPNORM_ALPHA * h_ref[:, cols] for cols in out_cols]
    done = 0
    for c in range(N_FFN_CHUNKS):
        cols = slice(c * FFN_CHUNK, (c + 1) * FFN_CHUNK)
        gate = _dot(hb_ref[...], w_gate_ref[:, cols])
        yield
        up = _dot(hb_ref[...], w_up_ref[:, cols])
        act_ref[:, cols] = (_silu(gate) * up).astype(BF16)
        yield
        if c + 1 in FFN_DOWN_AFTER:
            k_rows = slice(done * FFN_CHUNK, (c + 1) * FFN_CHUNK)
            done = c + 1
            for n, cols in enumerate(out_cols):
                pre[n] = pre[n] + _dot(act_ref[:, k_rows], w_down_ref[k_rows, cols])
                yield
    return _layer_norm(jnp.concatenate(pre, axis=1), ln_g_ref[...], ln_b_ref[...])


def _run(gen):
    while True:
        try:
            next(gen)
        except StopIteration as stop:
            return stop.value


def _interleave(mixer, ffn):
    remaining = N_FFN_UNITS
    while True:
        try:
            wanted = next(mixer)
        except StopIteration as stop:
            return stop.value, y
        last = wanted is None
        for _ in range(remaining if last else min(wanted, remaining)):
            next(ffn)
            remaining -= 1
        if last:
            y = _run(ffn)


N_MIXER_SCRATCH = 19

VEC_LAYOUT = (tuple(("conv_w%d" % i, CONV_DIM) for i in range(CONV_W))
              + (("conv_b", CONV_DIM), ("dt_bias", LANES), ("a_log", LANES), ("d_skip", SSM_WIDTH),
                 ("norm_w", SSM_WIDTH), ("sink_a", STACK_ROWS), ("sink_b", STACK_ROWS),
                 ("ln1_g", D_MODEL), ("ln1_b", D_MODEL), ("ln2_g", D_MODEL), ("ln2_b", D_MODEL)))
VEC_WIDTH = sum(width for _, width in VEC_LAYOUT)


def _vec_views(vec_ref):
    views, offset = {}, 0
    for name, width in VEC_LAYOUT:
        views[name] = vec_ref.at[:, offset:offset + width]
        offset += width
    return views


def _mixer_weights(w_in_ref, vec_ref, expand_ref, bias_ref, w_out_ref):
    v = _vec_views(vec_ref)
    return (w_in_ref.at[:, :COL_END], w_in_ref.at[:, COL_END:COL_END + LANES],
            tuple(v["conv_w%d" % i] for i in range(CONV_W)), v["conv_b"], v["dt_bias"], v["a_log"], v["d_skip"],
            v["norm_w"], expand_ref, bias_ref, (v["sink_a"], v["sink_b"]), w_out_ref, v["ln1_g"], v["ln1_b"])


def _ffn_weights(vec_ref, w_gate_ref, w_up_ref, w_down_ref):
    v = _vec_views(vec_ref)
    return (w_gate_ref, w_up_ref, w_down_ref, v["ln2_g"], v["ln2_b"])


def _fused_kernel(tile, n_tiles, n_total,
                  x_ref, x_next_ref, w_in_ref, vec_ref, expand_ref, bias_ref, w_out_ref, w_gate_ref, w_up_ref, w_down_ref,
                  y_ref, conv_out_ref, ssm_out_ref, k_out_ref, v_out_ref, *scratch):
    mixer_w = _mixer_weights(w_in_ref, vec_ref, expand_ref, bias_ref, w_out_ref)
    ffn_w = _ffn_weights(vec_ref, w_gate_ref, w_up_ref, w_down_ref)
    mixer_s = scratch[:N_MIXER_SCRATCH]
    h_s, hb_s, act_s = scratch[N_MIXER_SCRATCH:]
    convbuf, st_ref, kf_ref, vf_ref = mixer_s[:4]

    g = pl.program_id(0)
    t = jnp.minimum(g, n_total - 1) % n_tiles
    next_starts_stream = (g + 1 < n_total) & ((g + 1) % n_tiles == 0)

    @pl.when(g == 0)
    def _():
        h_s[...] = jnp.zeros(h_s.shape, F32)
        hb_s[...] = jnp.zeros(hb_s.shape, BF16)
        convbuf[CONV_HIST, :] = jnp.zeros((CONV_W - 1, CONV_DIM), F32)
        _prepare_tile(tile, x_ref, mixer_w, mixer_s)

    @pl.when(t == 0)
    def _():
        st_ref[...] = jnp.zeros(st_ref.shape, F32)
        for ref in mixer_s[2:7]:
            ref[0:WINDOW, :] = jnp.zeros((WINDOW, KV_W), ref.dtype)

    def switch_stream():
        hist = convbuf[CONV_HIST, :]
        conv_out_ref[0] = hist
        convbuf[CONV_HIST, :] = jnp.where(next_starts_stream, 0.0, hist)

    n_chunks = tile // CHUNK
    ffn = _ffn_tile(h_s, hb_s, act_s, *ffn_w)
    mixer = _mixer_tile(False, tile, t * n_chunks, x_ref, x_next_ref, switch_stream, mixer_w, mixer_s)
    h, y = _interleave(mixer, ffn)
    y_ref[0] = y
    h_s[...] = h
    hb_s[...] = h.astype(BF16)

    @pl.when((t == n_tiles - 1) & (g < n_total))
    def _():
        _write_stream_state(ssm_out_ref, k_out_ref, v_out_ref, st_ref, kf_ref, vf_ref)


def _mixer_kernel(tile, x_ref, conv0_ref, ssm0_ref, k0_ref, v0_ref,
                  w_in_ref, vec_ref, expand_ref, bias_ref, w_out_ref,
                  h_ref, conv_out_ref, ssm_out_ref, k_out_ref, v_out_ref, *mixer_s):
    mixer_w = _mixer_weights(w_in_ref, vec_ref, expand_ref, bias_ref, w_out_ref)
    convbuf, st_ref, kf_ref, vf_ref = mixer_s[:4]
    convbuf[CONV_HIST, :] = conv0_ref[0]
    _init_stream_state(ssm0_ref, k0_ref, v0_ref, *mixer_s[1:7])
    _prepare_tile(tile, x_ref, mixer_w, mixer_s)
    h_ref[0] = _run(_mixer_tile(True, tile, 0, x_ref, None, None, mixer_w, mixer_s))
    conv_out_ref[0] = convbuf[CONV_HIST, :]
    _write_stream_state(ssm_out_ref, k_out_ref, v_out_ref, st_ref, kf_ref, vf_ref)


def _ffn_kernel(h_ref, vec_ref, w_gate_ref, w_up_ref, w_down_ref, y_ref, hb_s, act_s):
    hb_s[...] = h_ref[...].astype(BF16)
    y_ref[...] = _run(_ffn_tile(h_ref, hb_s, act_s, *_ffn_weights(vec_ref, w_gate_ref, w_up_ref, w_down_ref)))


def _const_spec(shape):
    zeros = (0,) * len(shape)
    return pl.BlockSpec(shape, lambda *_: zeros, pipeline_mode=pl.Buffered(1))


def _mixer_scratch(tile):
    return [pltpu.VMEM((CONV_PAD + tile, CONV_DIM), F32),
            pltpu.VMEM((D_STATE, SSM_WIDTH), F32),
            pltpu.VMEM((WINDOW + tile, KV_W), F32), pltpu.VMEM((WINDOW + tile, KV_W), F32),
            pltpu.VMEM((WINDOW + tile, KV_W), BF16), pltpu.VMEM((WINDOW + tile, KV_W), BF16),
            pltpu.VMEM((WINDOW + tile, KV_W), BF16), pltpu.VMEM((KV_W, WINDOW + tile), BF16),
            pltpu.VMEM((tile, D_MODEL), BF16),
            pltpu.VMEM((tile, SSM_WIDTH), F32),
            pltpu.VMEM((tile, ATTN_WIDTH), F32),
            pltpu.VMEM((tile, LANES), F32),
            pltpu.VMEM((tile, SSM_WIDTH), F32),
            pltpu.VMEM((tile, CONV_DIM - SSM_WIDTH), BF16),
            pltpu.VMEM((CHUNK, SSM_WIDTH), F32),
            pltpu.VMEM((CHUNK, SSM_WIDTH), F32),
            pltpu.VMEM((tile, D_MODEL), BF16),
            pltpu.VMEM((CHUNK, LANES), BF16),
            pltpu.VMEM((KV_W, max(tile, LANES) + CHUNK), BF16)]


_STATE_SHAPES = ((CONV_W - 1, CONV_DIM), (SSM_WIDTH, D_STATE), (KV_W, WINDOW), (KV_W, WINDOW))


def _attention_tables(attn_sinks):
    slopes = np.exp2(-8.0 * np.arange(1, ATTN_HEADS + 1, dtype=np.float64) / ATTN_HEADS)
    qi = np.arange(CHUNK)[None, :] + WINDOW
    kj = np.arange(BAND)[:, None]
    dist = np.abs(qi - kj).astype(np.float64)
    key_chunk = kj // CHUNK
    tables = np.zeros((HIST_CHUNKS + 1, 2, BAND, STACK_ROWS), np.float32)
    for variant in range(HIST_CHUNKS + 1):
        valid = key_chunk >= (HIST_CHUNKS - variant)
        for s_idx, heads in enumerate((STACK_A_HEADS, STACK_B_HEADS)):
            for i, h in enumerate(heads):
                tables[variant, s_idx, :, i * CHUNK:(i + 1) * CHUNK] = np.where(
                    valid, -slopes[h] * dist * LOG2E, -np.inf)
    order = np.array(STACK_A_HEADS + STACK_B_HEADS)
    sink_rows = jnp.repeat(attn_sinks.astype(F32)[order] * LOG2E, CHUNK).reshape(2, STACK_ROWS)
    return jnp.asarray(tables), sink_rows


def _expansion_matrix():
    e = np.zeros((EXPAND_K, 2 * SSM_WIDTH), np.float32)
    for part in range(6):
        for h in range(SSM_HEADS):
            c0 = (part // 3) * SSM_WIDTH + h * HEAD_DIM
            e[part * SSM_HEADS + h, c0:c0 + HEAD_DIM] = 1.0
    return jnp.asarray(e, BF16)


def _prompt_call(x, mixer_ops, ffn_ops):
    bsz, seq, _ = x.shape
    tile = PROMPT_TILE
    n_tiles = seq // tile
    n_total = bsz * n_tiles

    def mixer_tile_idx(g):
        gc = jnp.minimum(g, n_total - 1)
        return gc // n_tiles, gc % n_tiles

    def next_tile_idx(g):
        return mixer_tile_idx(g + 1)

    def ffn_tile_idx(g):
        gp = jnp.maximum(g - 1, 0)
        return gp // n_tiles, gp % n_tiles

    tile_spec = lambda idx: pl.BlockSpec((1, tile, D_MODEL), lambda g: (*idx(g), 0))
    stream_spec = lambda shape: pl.BlockSpec((1,) + shape, lambda g: (mixer_tile_idx(g)[0], 0, 0))
    in_specs = ([tile_spec(mixer_tile_idx), tile_spec(next_tile_idx)]
                + [_const_spec(w.shape) for w in mixer_ops + ffn_ops])
    out_specs = [tile_spec(ffn_tile_idx)] + [stream_spec(s) for s in _STATE_SHAPES]
    out_shape = ([jax.ShapeDtypeStruct((bsz, seq, D_MODEL), F32)]
                 + [jax.ShapeDtypeStruct((bsz,) + s, F32) for s in _STATE_SHAPES])
    scratch = _mixer_scratch(tile) + [pltpu.VMEM((tile, D_MODEL), F32), pltpu.VMEM((tile, D_MODEL), BF16),
                                      pltpu.VMEM((tile, D_FF), BF16)]
    return pl.pallas_call(
        functools.partial(_fused_kernel, tile, n_tiles, n_total),
        grid=(n_total + 1,), in_specs=in_specs, out_specs=out_specs, out_shape=out_shape,
        scratch_shapes=scratch, name="prompt_layer",
        compiler_params=pltpu.CompilerParams(dimension_semantics=("arbitrary",), vmem_limit_bytes=VMEM_LIMIT),
    )(x, x, *mixer_ops, *ffn_ops)


def _sample_mixer_call(x, state0, mixer_ops):
    bsz, seq, _ = x.shape
    stream_spec = lambda shape: pl.BlockSpec((1,) + shape, lambda b: (b, 0, 0))
    in_specs = ([stream_spec((seq, D_MODEL))] + [stream_spec(s) for s in _STATE_SHAPES]
                + [_const_spec(w.shape) for w in mixer_ops])
    out_specs = [stream_spec((seq, D_MODEL))] + [stream_spec(s) for s in _STATE_SHAPES]
    out_shape = ([jax.ShapeDtypeStruct((bsz, seq, D_MODEL), F32)]
                 + [jax.ShapeDtypeStruct((bsz,) + s, F32) for s in _STATE_SHAPES])
    return pl.pallas_call(
        functools.partial(_mixer_kernel, seq),
        grid=(bsz,), in_specs=in_specs, out_specs=out_specs, out_shape=out_shape,
        scratch_shapes=_mixer_scratch(seq), name="sample_mixer",
        compiler_params=pltpu.CompilerParams(dimension_semantics=("arbitrary",), vmem_limit_bytes=VMEM_LIMIT),
    )(x, *state0, *mixer_ops)


def _ffn_call(h2d, ffn_ops):
    n_tok = h2d.shape[0]
    tile = min(FFN_TILE, n_tok)
    tok_spec = pl.BlockSpec((tile, D_MODEL), lambda i: (i, 0))
    return pl.pallas_call(
        _ffn_kernel, grid=(n_tok // tile,),
        in_specs=[tok_spec] + [_const_spec(w.shape) for w in ffn_ops],
        out_specs=tok_spec, out_shape=jax.ShapeDtypeStruct((n_tok, D_MODEL), F32), name="sample_ffn",
        scratch_shapes=[pltpu.VMEM((tile, D_MODEL), BF16), pltpu.VMEM((tile, D_FF), BF16)],
        compiler_params=pltpu.CompilerParams(dimension_semantics=("arbitrary",), vmem_limit_bytes=VMEM_LIMIT),
    )(h2d, *ffn_ops)


def _window_to_kernel_layout(cache):
    b = cache.shape[0]
    return jnp.transpose(cache.astype(F32).reshape(b, WINDOW, KV_W), (0, 2, 1))


def _window_from_kernel_layout(win_t):
    b = win_t.shape[0]
    return jnp.transpose(win_t, (0, 2, 1)).reshape(1, b, WINDOW, KV_HEADS, HEAD_DIM)


def kernel(x_prompt, x_sample, state_conv, state_ssm, cache_k, cache_v, w_in, conv_w, conv_b, dt_bias, a_log,
           d_skip, ssm_norm_w, attn_sinks, w_out, ln1_g, ln1_b, w_gate, w_up, w_down, ln2_g, ln2_b):
    w_t = jnp.transpose(w_in[0])
    split = np.cumsum([SSM_WIDTH, CONV_DIM, SSM_HEADS, ATTN_WIDTH, KV_W])
    w_z, w_xbc, w_dt, w_q, w_k, w_v = jnp.split(w_t, split, axis=0)
    rep = LANES // SSM_HEADS
    w_dt_rep = jnp.tile(w_dt, (rep, 1)).astype(BF16)
    w_in_r = jnp.transpose(jnp.concatenate(
        [w_z.astype(BF16), w_xbc.astype(BF16), (w_q * (HEAD_DIM ** -0.5 * LOG2E)).astype(BF16),
         w_k.astype(BF16), w_v.astype(BF16), w_dt_rep], axis=0))
    bias_tables, sink_rows = _attention_tables(attn_sinks[0])
    vec_parts = {"conv_b": conv_b[0], "dt_bias": jnp.tile(dt_bias[0], rep), "a_log": jnp.tile(a_log[0], rep),
                 "d_skip": jnp.repeat(d_skip[0], HEAD_DIM), "norm_w": ssm_norm_w[0],
                 "sink_a": sink_rows[0], "sink_b": sink_rows[1],
                 "ln1_g": ln1_g[0], "ln1_b": ln1_b[0], "ln2_g": ln2_g[0], "ln2_b": ln2_b[0]}
    vec_parts.update({"conv_w%d" % i: conv_w[0, i] for i in range(CONV_W)})
    vecs = jnp.concatenate([vec_parts[name].astype(F32).reshape(width) for name, width in VEC_LAYOUT])
    vecs = vecs.reshape(1, VEC_WIDTH)
    mixer_ops = (w_in_r, vecs, _expansion_matrix(), bias_tables, w_out[0].astype(BF16))
    ffn_mats = (w_gate[0].astype(BF16), w_up[0].astype(BF16), w_down[0].astype(BF16))

    dbsz, dseq, _ = x_sample.shape

    y_p, conv_p, ssm_p, k_p, v_p = _prompt_call(x_prompt, mixer_ops, ffn_mats)

    sample_state = (state_conv[0].astype(F32), state_ssm[0].astype(F32).reshape(dbsz, SSM_WIDTH, D_STATE),
                    _window_to_kernel_layout(cache_k[0]), _window_to_kernel_layout(cache_v[0]))
    h_s, conv_s, ssm_s, k_s, v_s = _sample_mixer_call(x_sample, sample_state, mixer_ops)
    y_s = _ffn_call(h_s.reshape(dbsz * dseq, D_MODEL), (vecs,) + ffn_mats).reshape(dbsz, dseq, D_MODEL)

    ssm_shape = lambda a: a.reshape(1, a.shape[0], SSM_HEADS, HEAD_DIM, D_STATE)
    kv_shape = _window_from_kernel_layout
    return (y_p, y_s,
            conv_p[None], ssm_shape(ssm_p), kv_shape(k_p), kv_shape(v_p),
            conv_s[None], ssm_shape(ssm_s), kv_shape(k_s), kv_shape(v_s))
```

```python
import functools
import math

import jax
import jax.numpy as jnp
import numpy as np
from jax import lax
from jax.experimental import pallas as pl
from jax.experimental.pallas import tpu as pltpu

D_MODEL = 1024
CHUNK = 64
HEAD_DIM = 64
SSM_WIDTH = 512
SSM_HEADS = 8
SSM_GROUPS = 2
D_STATE = 128
CONV_W = 4
CONV_DIM = SSM_WIDTH + 2 * SSM_GROUPS * D_STATE
ATTN_WIDTH = 512
ATTN_HEADS = 8
KV_HEADS = 2
WINDOW = 128
D_FF = 2816
LN_EPS = 1e-5
RMS_EPS = 1e-5
DEEPNORM_ALPHA = 2.0 ** 0.25
LOG2E = math.log2(math.e)

LANES = 128
SUBLANES = 8
GROUP_W = SSM_WIDTH // SSM_GROUPS
KV_W = KV_HEADS * HEAD_DIM
BAND = WINDOW + CHUNK
HIST_CHUNKS = WINDOW // CHUNK
HEADS_PER_STACK = 4
STACK_ROWS = HEADS_PER_STACK * CHUNK
COL_Z, COL_XBC, COL_Q, COL_K, COL_V, COL_END = 0, 512, 1536, 2048, 2176, 2304
STACK_A_HEADS = (0, 2, 5, 7)
STACK_B_HEADS = (1, 3, 4, 6)
EXPAND_K = 128
CONV_PAD = SUBLANES

PROMPT_TILE = 256
FFN_TILE = 512
FFN_CHUNK = 256
N_FFN_CHUNKS = D_FF // FFN_CHUNK
VMEM_LIMIT = 56 * 1024 * 1024
UNITS_AT_START = 2
UNITS_AFTER_FIRST_MATMULS = 4
UNITS_AFTER_VECTOR_WORK = 2
UNITS_AFTER_SECOND_MATMULS = 1

F32 = jnp.float32
BF16 = jnp.bfloat16


def _dot(a, b):
    return jnp.dot(a, b, preferred_element_type=F32)


def _dot_nt(a, b):
    return lax.dot_general(a, b, (((1,), (1,)), ((), ())), preferred_element_type=F32)


def _dot_tn(a, b):
    return lax.dot_general(a, b, (((0,), (0,)), ((), ())), preferred_element_type=F32)


def _sigmoid(x):
    return 1.0 / (1.0 + jnp.exp2(x * -LOG2E))


def _silu(x):
    return x * _sigmoid(x)


def _layer_norm(x, g, b):
    mu = jnp.mean(x, axis=-1, keepdims=True)
    xc = x - mu
    var = jnp.mean(xc * xc, axis=-1, keepdims=True)
    return xc * lax.rsqrt(var + LN_EPS) * g + b


def _split3(x):
    hi = x.astype(BF16).astype(F32)
    r = x - hi
    mid = r.astype(BF16).astype(F32)
    lo = r - mid
    return hi, mid, lo


CONV_HIST = slice(CONV_PAD - (CONV_W - 1), CONV_PAD)


def _init_stream_state(ssm0_ref, k0_ref, v0_ref, st_ref, kf_ref, vf_ref, kb_ref, ksb_ref, vb_ref):
    st_ref[...] = ssm0_ref[0].T
    k0 = k0_ref[0].T
    v0 = v0_ref[0].T
    kf_ref[0:WINDOW, :] = k0
    vf_ref[0:WINDOW, :] = v0
    kb_ref[0:WINDOW, :] = k0.astype(BF16)
    vb_ref[0:WINDOW, :] = v0.astype(BF16)
    ksb_ref[0:WINDOW, :] = pltpu.roll(k0, HEAD_DIM, axis=1).astype(BF16)


def _write_stream_state(ssm_out_ref, k_out_ref, v_out_ref, st_ref, kf_ref, vf_ref):
    ssm_out_ref[0] = st_ref[...].T
    k_out_ref[0] = kf_ref[0:WINDOW, :].T
    v_out_ref[0] = vf_ref[0:WINDOW, :].T


def _in_projection(tile, xb_ref, w_in_ref, w_dt_ref,
                   convbuf, kf_ref, vf_ref, kb_ref, ksb_ref, vb_ref, zbuf, qbuf, dtbuf):
    xb = xb_ref[...]
    zbuf[...] = _dot(xb, w_in_ref[:, COL_Z:COL_XBC])
    convbuf[CONV_PAD:CONV_PAD + tile, :] = _dot(xb, w_in_ref[:, COL_XBC:COL_Q])
    qbuf[...] = _dot(xb, w_in_ref[:, COL_Q:COL_K])
    kv = _dot(xb, w_in_ref[:, COL_K:COL_END])
    dtbuf[...] = _dot(xb, w_dt_ref[...])
    k_new = kv[:, :KV_W]
    v_new = kv[:, KV_W:]
    kf_ref[WINDOW:WINDOW + tile, :] = k_new
    vf_ref[WINDOW:WINDOW + tile, :] = v_new
    kb_ref[WINDOW:WINDOW + tile, :] = k_new.astype(BF16)
    vb_ref[WINDOW:WINDOW + tile, :] = v_new.astype(BF16)
    ksb_ref[WINDOW:WINDOW + tile, :] = pltpu.roll(k_new, HEAD_DIM, axis=1).astype(BF16)


def _conv_and_decay(ci, conv_w_ref, conv_b_ref, dt_bias_ref, a_log_ref, convbuf, dtbuf, xsbuf, bcbuf):
    r0 = ci * CHUNK
    rows = slice(r0, r0 + CHUNK)

    conv = conv_b_ref[...]
    for i in range(CONV_W):
        r = CONV_PAD - (CONV_W - 1) + i + r0
        conv = conv + conv_w_ref[i][...] * convbuf[r:r + CHUNK, :]
    xc = _silu(conv)
    xsbuf[rows, :] = xc[:, :SSM_WIDTH]
    bcbuf[rows, :] = xc[:, SSM_WIDTH:].astype(BF16)

    row_in_chunk = lax.broadcasted_iota(jnp.int32, (CHUNK, LANES), 0)
    lane = lax.broadcasted_iota(jnp.int32, (CHUNK, LANES), 1)
    dt_in = dtbuf[rows, :] + dt_bias_ref[...]
    dt = jnp.maximum(dt_in, 0.0) + jnp.log1p(jnp.exp(-jnp.abs(dt_in)))
    cum = dt * (-LOG2E * jnp.exp(a_log_ref[...]))
    step = 1
    while step < CHUNK:
        cum = cum + jnp.where(row_in_chunk >= step, pltpu.roll(cum, step, axis=0), 0.0)
        step *= 2
    packed = jnp.zeros((CHUNK, LANES), F32)
    for i, part in enumerate(_split3(cum) + _split3(dt)):
        packed = jnp.where((lane >= SSM_HEADS * i) & (lane < SSM_HEADS * (i + 1)), part, packed)
    return packed.astype(BF16)


def _prepare_tile(tile, x_ref, mixer_w, mixer_s):
    (w_in_ref, w_dt_ref, conv_w_ref, conv_b_ref, dt_bias_ref, a_log_ref) = mixer_w[:6]
    (convbuf, _, kf_ref, vf_ref, kb_ref, ksb_ref, vb_ref, _, _, zbuf, qbuf, dtbuf, xsbuf, bcbuf, _, _,
     xb_ref, packbuf, _) = mixer_s
    if x_ref is not None:
        xb_ref[...] = x_ref[0].astype(BF16)
    _in_projection(tile, xb_ref, w_in_ref, w_dt_ref,
                   convbuf, kf_ref, vf_ref, kb_ref, ksb_ref, vb_ref, zbuf, qbuf, dtbuf)
    packbuf[...] = _conv_and_decay(0, conv_w_ref, conv_b_ref, dt_bias_ref, a_log_ref, convbuf, dtbuf, xsbuf, bcbuf)


def _mixer_tile(hist_all_valid, tile, first_chunk, x_ref, x_next_ref, switch_stream, mixer_w, mixer_s):
    (_, _, conv_w_ref, conv_b_ref, dt_bias_ref, a_log_ref, d_ref, norm_w_ref,
     expand_ref, bias_ref, sink_ref, w_out_ref, ln_g_ref, ln_b_ref) = mixer_w
    (convbuf, st_ref, kf_ref, vf_ref, kb_ref, ksb_ref, vb_ref, vt_even, mixed_ref,
     zbuf, qbuf, dtbuf, xsbuf, bcbuf, cumx_ref, xdt_ref, xb_ref, packbuf, vt_odd) = mixer_s
    n_chunks = tile // CHUNK
    if x_next_ref is not None:
        xb_ref[...] = x_next_ref[0].astype(BF16)
    vt_even[...] = vb_ref[...].T
    if n_chunks > 1:
        vt_odd[...] = vb_ref[CHUNK:CHUNK + vt_odd.shape[1], :].T
    yield UNITS_AT_START

    lane = lax.broadcasted_iota(jnp.int32, (CHUNK, LANES), 1)
    low_half = lane < HEAD_DIM
    li = lax.broadcasted_iota(jnp.int32, (CHUNK, GROUP_W), 0)
    si = lax.broadcasted_iota(jnp.int32, (CHUNK, GROUP_W), 1) % CHUNK
    diag_mask = li == si
    causal_mask = si <= li
    bd_mask = (lax.broadcasted_iota(jnp.int32, (GROUP_W, GROUP_W), 0) // CHUNK
               == lax.broadcasted_iota(jnp.int32, (GROUP_W, GROUP_W), 1) // CHUNK)

    def conv_and_decay(ci):
        return _conv_and_decay(ci, conv_w_ref, conv_b_ref, dt_bias_ref, a_log_ref, convbuf, dtbuf, xsbuf, bcbuf)

    packed_next = packbuf[...]
    for ci in range(n_chunks):
        r0 = ci * CHUNK
        rows = slice(r0, r0 + CHUNK)
        packed = packed_next

        expanded = _dot(packed, expand_ref[...])
        if hist_all_valid:
            variant = HIST_CHUNKS
        else:
            variant = jnp.minimum(first_chunk + ci, HIST_CHUNKS)
        group_cols = [slice(g * GROUP_W, (g + 1) * GROUP_W) for g in range(SSM_GROUPS)]
        b_gs = [bcbuf[rows, g * D_STATE:(g + 1) * D_STATE] for g in range(SSM_GROUPS)]
        c_gs = [bcbuf[rows, GROUP_W + g * D_STATE:GROUP_W + (g + 1) * D_STATE] for g in range(SSM_GROUPS)]
        cbs = [_dot_nt(c_gs[g], jnp.concatenate([b_gs[g]] * HEADS_PER_STACK, axis=0)) for g in range(SSM_GROUPS)]
        y_offs = [_dot(c_gs[g], st_ref[:, group_cols[g]].astype(BF16)) for g in range(SSM_GROUPS)]
        blocks = [qbuf[rows, j * LANES:(j + 1) * LANES] for j in range(ATTN_HEADS // 2)]
        evens = [jnp.where(low_half, blk, 0.0).astype(BF16) for blk in blocks]
        odds = [jnp.where(low_half, 0.0, blk).astype(BF16) for blk in blocks]
        q_a = jnp.concatenate([evens[0], evens[1], odds[2], odds[3]], axis=0)
        q_b = jnp.concatenate([odds[0], odds[1], evens[2], evens[3]], axis=0)
        scores = [_dot_nt(k_ref[r0:r0 + BAND, :], q_s) for q_s, k_ref in ((q_a, kb_ref), (q_b, ksb_ref))]
        yield UNITS_AFTER_FIRST_MATMULS

        if ci + 1 < n_chunks:
            packed_next = conv_and_decay(ci + 1)
        cumx_ref[...] = expanded[:, :SSM_WIDTH]
        xdt_ref[...] = xsbuf[rows, :] * expanded[:, SSM_WIDTH:]
        m_gs, bds, xws, probs = [], [], [], []
        for g in range(SSM_GROUPS):
            colb = cumx_ref[:, group_cols[g]]
            rowb = jnp.sum(jnp.where(diag_mask, colb, 0.0), axis=0, keepdims=True)
            lmat = jnp.exp2(jnp.where(causal_mask, colb - rowb, -jnp.inf))
            m_gs.append((cbs[g] * lmat).astype(BF16))
            xdt_g = xdt_ref[:, group_cols[g]]
            xdt_b = xdt_g.astype(BF16)
            bds.append(jnp.where(bd_mask, jnp.concatenate([xdt_b] * HEADS_PER_STACK, axis=0), jnp.zeros((), BF16)))
            last = colb[CHUNK - 1:CHUNK, :]
            xws.append((xdt_g * jnp.exp2(last - colb)).astype(BF16))
        for s_idx in range(2):
            s = scores[s_idx] + bias_ref[variant, s_idx]
            sink = sink_ref[s_idx][...]
            m = jnp.maximum(jnp.max(s, axis=0, keepdims=True), sink)
            p = jnp.exp2(s - m)
            denom = jnp.sum(p, axis=0, keepdims=True) + jnp.exp2(sink - m)
            probs.append((p * (1.0 / denom)).astype(BF16))
        b_ts = [b_g.T for b_g in b_gs]
        yield UNITS_AFTER_VECTOR_WORK

        y_diags = [_dot(m_gs[g], bds[g]) for g in range(SSM_GROUPS)]
        updates = [_dot(b_ts[g], xws[g]) for g in range(SSM_GROUPS)]
        vt_src, c0 = (vt_even, r0) if ci % 2 == 0 else (vt_odd, r0 - CHUNK)
        vband_t = vt_src[:, c0:c0 + BAND]
        vband_t_swapped = jnp.concatenate([vband_t[HEAD_DIM:], vband_t[:HEAD_DIM]], axis=0)
        outs_t = [_dot(vband_t, probs[0]), _dot(vband_t_swapped, probs[1])]
        yield UNITS_AFTER_SECOND_MATMULS

        for g in range(SSM_GROUPS):
            cols = group_cols[g]
            colb = cumx_ref[:, cols]
            last = colb[CHUNK - 1:CHUNK, :]
            st_ref[:, cols] = jnp.exp2(last) * st_ref[:, cols] + updates[g]
            y = y_diags[g] + y_offs[g] * jnp.exp2(colb) + d_ref[:, cols] * xsbuf[rows, cols]
            gg = y * _silu(zbuf[rows, cols])
            ms = jnp.mean(gg * gg, axis=-1, keepdims=True)
            mixed_ref[rows, cols] = (gg * lax.rsqrt(ms + RMS_EPS) * norm_w_ref[:, cols]).astype(BF16)
        o_a, o_b = [o_t.T for o_t in outs_t]
        sel = [(o_a, o_b), (o_a, o_b), (o_b, o_a), (o_b, o_a)]
        for j in range(ATTN_HEADS // 2):
            ev, od = sel[j]
            blk = jnp.where(low_half, ev[j * CHUNK:(j + 1) * CHUNK], od[j * CHUNK:(j + 1) * CHUNK])
            mixed_ref[rows, SSM_WIDTH + j * LANES:SSM_WIDTH + (j + 1) * LANES] = blk.astype(BF16)

    convbuf[CONV_HIST, :] = convbuf[CONV_PAD + tile - (CONV_W - 1):CONV_PAD + tile, :]
    for ref in (kf_ref, vf_ref, kb_ref, ksb_ref, vb_ref):
        ref[0:WINDOW, :] = ref[tile:tile + WINDOW, :]

    yield

    mix = _dot(mixed_ref[...], w_out_ref[...])
    if x_next_ref is not None:
        switch_stream()
        _prepare_tile(tile, None, mixer_w, mixer_s)
    return _layer_norm(DEEPNORM_ALPHA * x_ref[0] + mix, ln_g_ref[...], ln_b_ref[...])


FFN_DOWN_AFTER = (6, N_FFN_CHUNKS)
N_FFN_UNITS = 2 * N_FFN_CHUNKS + len(FFN_DOWN_AFTER) * (D_MODEL // FFN_CHUNK)


def _ffn_tile(h_ref, hb_ref, act_ref, w_gate_ref, w_up_ref, w_down_ref, ln_g_ref, ln_b_ref):
    out_cols = [slice(n * FFN_CHUNK, (n + 1) * FFN_CHUNK) for n in range(D_MODEL // FFN_CHUNK)]
    pre = [DEEPNORM_ALPHA * h_ref[:, cols] for cols in out_cols]
    done = 0
    for c in range(N_FFN_CHUNKS):
        cols = slice(c * FFN_CHUNK, (c + 1) * FFN_CHUNK)
        gate = _dot(hb_ref[...], w_gate_ref[:, cols])
        yield
        up = _dot(hb_ref[...], w_up_ref[:, cols])
        act_ref[:, cols] = (_silu(gate) * up).astype(BF16)
        yield
        if c + 1 in FFN_DOWN_AFTER:
            k_rows = slice(done * FFN_CHUNK, (c + 1) * FFN_CHUNK)
            done = c + 1
            for n, cols in enumerate(out_cols):
                pre[n] = pre[n] + _dot(act_ref[:, k_rows], w_down_ref[k_rows, cols])
                yield
    return _layer_norm(jnp.concatenate(pre, axis=1), ln_g_ref[...], ln_b_ref[...])


def _run(gen):
    while True:
        try:
            next(gen)
        except StopIteration as stop:
            return stop.value


def _interleave(mixer, ffn):
    remaining = N_FFN_UNITS
    while True:
        try:
            wanted = next(mixer)
        except StopIteration as stop:
            return stop.value, y
        last = wanted is None
        for _ in range(remaining if last else min(wanted, remaining)):
            next(ffn)
            remaining -= 1
        if last:
            y = _run(ffn)


N_MIXER_SCRATCH = 19

VEC_LAYOUT = (tuple(("conv_w%d" % i, CONV_DIM) for i in range(CONV_W))
              + (("conv_b", CONV_DIM), ("dt_bias", LANES), ("a_log", LANES), ("d_skip", SSM_WIDTH),
                 ("norm_w", SSM_WIDTH), ("sink_a", STACK_ROWS), ("sink_b", STACK_ROWS),
                 ("ln1_g", D_MODEL), ("ln1_b", D_MODEL), ("ln2_g", D_MODEL), ("ln2_b", D_MODEL)))
VEC_WIDTH = sum(width for _, width in VEC_LAYOUT)


def _vec_views(vec_ref):
    views, offset = {}, 0
    for name, width in VEC_LAYOUT:
        views[name] = vec_ref.at[:, offset:offset + width]
        offset += width
    return views


def _mixer_weights(w_in_ref, vec_ref, expand_ref, bias_ref, w_out_ref):
    v = _vec_views(vec_ref)
    return (w_in_ref.at[:, :COL_END], w_in_ref.at[:, COL_END:COL_END + LANES],
            tuple(v["conv_w%d" % i] for i in range(CONV_W)), v["conv_b"], v["dt_bias"], v["a_log"], v["d_skip"],
            v["norm_w"], expand_ref, bias_ref, (v["sink_a"], v["sink_b"]), w_out_ref, v["ln1_g"], v["ln1_b"])


def _ffn_weights(vec_ref, w_gate_ref, w_up_ref, w_down_ref):
    v = _vec_views(vec_ref)
    return (w_gate_ref, w_up_ref, w_down_ref, v["ln2_g"], v["ln2_b"])


def _fused_kernel(tile, n_tiles, n_total,
                  x_ref, x_next_ref, w_in_ref, vec_ref, expand_ref, bias_ref, w_out_ref, w_gate_ref, w_up_ref, w_down_ref,
                  y_ref, conv_out_ref, ssm_out_ref, k_out_ref, v_out_ref, *scratch):
    mixer_w = _mixer_weights(w_in_ref, vec_ref, expand_ref, bias_ref, w_out_ref)
    ffn_w = _ffn_weights(vec_ref, w_gate_ref, w_up_ref, w_down_ref)
    mixer_s = scratch[:N_MIXER_SCRATCH]
    h_s, hb_s, act_s = scratch[N_MIXER_SCRATCH:]
    convbuf, st_ref, kf_ref, vf_ref = mixer_s[:4]

    g = pl.program_id(0)
    t = jnp.minimum(g, n_total - 1) % n_tiles
    next_starts_stream = (g + 1 < n_total) & ((g + 1) % n_tiles == 0)

    @pl.when(g == 0)
    def _():
        h_s[...] = jnp.zeros(h_s.shape, F32)
        hb_s[...] = jnp.zeros(hb_s.shape, BF16)
        convbuf[CONV_HIST, :] = jnp.zeros((CONV_W - 1, CONV_DIM), F32)
        _prepare_tile(tile, x_ref, mixer_w, mixer_s)

    @pl.when(t == 0)
    def _():
        st_ref[...] = jnp.zeros(st_ref.shape, F32)
        for ref in mixer_s[2:7]:
            ref[0:WINDOW, :] = jnp.zeros((WINDOW, KV_W), ref.dtype)

    def switch_stream():
        hist = convbuf[CONV_HIST, :]
        conv_out_ref[0] = hist
        convbuf[CONV_HIST, :] = jnp.where(next_starts_stream, 0.0, hist)

    n_chunks = tile // CHUNK
    ffn = _ffn_tile(h_s, hb_s, act_s, *ffn_w)
    mixer = _mixer_tile(False, tile, t * n_chunks, x_ref, x_next_ref, switch_stream, mixer_w, mixer_s)
    h, y = _interleave(mixer, ffn)
    y_ref[0] = y
    h_s[...] = h
    hb_s[...] = h.astype(BF16)

    @pl.when((t == n_tiles - 1) & (g < n_total))
    def _():
        _write_stream_state(ssm_out_ref, k_out_ref, v_out_ref, st_ref, kf_ref, vf_ref)


def _mixer_kernel(tile, x_ref, conv0_ref, ssm0_ref, k0_ref, v0_ref,
                  w_in_ref, vec_ref, expand_ref, bias_ref, w_out_ref,
                  h_ref, conv_out_ref, ssm_out_ref, k_out_ref, v_out_ref, *mixer_s):
    mixer_w = _mixer_weights(w_in_ref, vec_ref, expand_ref, bias_ref, w_out_ref)
    convbuf, st_ref, kf_ref, vf_ref = mixer_s[:4]
    convbuf[CONV_HIST, :] = conv0_ref[0]
    _init_stream_state(ssm0_ref, k0_ref, v0_ref, *mixer_s[1:7])
    _prepare_tile(tile, x_ref, mixer_w, mixer_s)
    h_ref[0] = _run(_mixer_tile(True, tile, 0, x_ref, None, None, mixer_w, mixer_s))
    conv_out_ref[0] = convbuf[CONV_HIST, :]
    _write_stream_state(ssm_out_ref, k_out_ref, v_out_ref, st_ref, kf_ref, vf_ref)


def _ffn_kernel(h_ref, vec_ref, w_gate_ref, w_up_ref, w_down_ref, y_ref, hb_s, act_s):
    hb_s[...] = h_ref[...].astype(BF16)
    y_ref[...] = _run(_ffn_tile(h_ref, hb_s, act_s, *_ffn_weights(vec_ref, w_gate_ref, w_up_ref, w_down_ref)))


def _const_spec(shape):
    zeros = (0,) * len(shape)
    return pl.BlockSpec(shape, lambda *_: zeros, pipeline_mode=pl.Buffered(1))


def _mixer_scratch(tile):
    return [pltpu.VMEM((CONV_PAD + tile, CONV_DIM), F32),
            pltpu.VMEM((D_STATE, SSM_WIDTH), F32),
            pltpu.VMEM((WINDOW + tile, KV_W), F32), pltpu.VMEM((WINDOW + tile, KV_W), F32),
            pltpu.VMEM((WINDOW + tile, KV_W), BF16), pltpu.VMEM((WINDOW + tile, KV_W), BF16),
            pltpu.VMEM((WINDOW + tile, KV_W), BF16), pltpu.VMEM((KV_W, WINDOW + tile), BF16),
            pltpu.VMEM((tile, D_MODEL), BF16),
            pltpu.VMEM((tile, SSM_WIDTH), F32),
            pltpu.VMEM((tile, ATTN_WIDTH), F32),
            pltpu.VMEM((tile, LANES), F32),
            pltpu.VMEM((tile, SSM_WIDTH), F32),
            pltpu.VMEM((tile, CONV_DIM - SSM_WIDTH), BF16),
            pltpu.VMEM((CHUNK, SSM_WIDTH), F32),
            pltpu.VMEM((CHUNK, SSM_WIDTH), F32),
            pltpu.VMEM((tile, D_MODEL), BF16),
            pltpu.VMEM((CHUNK, LANES), BF16),
            pltpu.VMEM((KV_W, max(tile, LANES) + CHUNK), BF16)]


_STATE_SHAPES = ((CONV_W - 1, CONV_DIM), (SSM_WIDTH, D_STATE), (KV_W, WINDOW), (KV_W, WINDOW))


def _attention_tables(attn_sinks):
    slopes = np.exp2(-8.0 * np.arange(1, ATTN_HEADS + 1, dtype=np.float64) / ATTN_HEADS)
    qi = np.arange(CHUNK)[None, :] + WINDOW
    kj = np.arange(BAND)[:, None]
    dist = np.abs(qi - kj).astype(np.float64)
    key_chunk = kj // CHUNK
    tables = np.zeros((HIST_CHUNKS + 1, 2, BAND, STACK_ROWS), np.float32)
    for variant in range(HIST_CHUNKS + 1):
        valid = key_chunk >= (HIST_CHUNKS - variant)
        for s_idx, heads in enumerate((STACK_A_HEADS, STACK_B_HEADS)):
            for i, h in enumerate(heads):
                tables[variant, s_idx, :, i * CHUNK:(i + 1) * CHUNK] = np.where(
                    valid, -slopes[h] * dist * LOG2E, -np.inf)
    order = np.array(STACK_A_HEADS + STACK_B_HEADS)
    sink_rows = jnp.repeat(attn_sinks.astype(F32)[order] * LOG2E, CHUNK).reshape(2, STACK_ROWS)
    return jnp.asarray(tables), sink_rows


def _expansion_matrix():
    e = np.zeros((EXPAND_K, 2 * SSM_WIDTH), np.float32)
    for part in range(6):
        for h in range(SSM_HEADS):
            c0 = (part // 3) * SSM_WIDTH + h * HEAD_DIM
            e[part * SSM_HEADS + h, c0:c0 + HEAD_DIM] = 1.0
    return jnp.asarray(e, BF16)


def _prompt_call(x, mixer_ops, ffn_ops):
    bsz, seq, _ = x.shape
    tile = PROMPT_TILE
    n_tiles = seq // tile
    n_total = bsz * n_tiles

    def mixer_tile_idx(g):
        gc = jnp.minimum(g, n_total - 1)
        return gc // n_tiles, gc % n_tiles

    def next_tile_idx(g):
        return mixer_tile_idx(g + 1)

    def ffn_tile_idx(g):
        gp = jnp.maximum(g - 1, 0)
        return gp // n_tiles, gp % n_tiles

    tile_spec = lambda idx: pl.BlockSpec((1, tile, D_MODEL), lambda g: (*idx(g), 0))
    stream_spec = lambda shape: pl.BlockSpec((1,) + shape, lambda g: (mixer_tile_idx(g)[0], 0, 0))
    in_specs = ([tile_spec(mixer_tile_idx), tile_spec(next_tile_idx)]
                + [_const_spec(w.shape) for w in mixer_ops + ffn_ops])
    out_specs = [tile_spec(ffn_tile_idx)] + [stream_spec(s) for s in _STATE_SHAPES]
    out_shape = ([jax.ShapeDtypeStruct((bsz, seq, D_MODEL), F32)]
                 + [jax.ShapeDtypeStruct((bsz,) + s, F32) for s in _STATE_SHAPES])
    scratch = _mixer_scratch(tile) + [pltpu.VMEM((tile, D_MODEL), F32), pltpu.VMEM((tile, D_MODEL), BF16),
                                      pltpu.VMEM((tile, D_FF), BF16)]
    return pl.pallas_call(
        functools.partial(_fused_kernel, tile, n_tiles, n_total),
        grid=(n_total + 1,), in_specs=in_specs, out_specs=out_specs, out_shape=out_shape,
        scratch_shapes=scratch, name="prompt_layer",
        compiler_params=pltpu.CompilerParams(dimension_semantics=("arbitrary",), vmem_limit_bytes=VMEM_LIMIT),
    )(x, x, *mixer_ops, *ffn_ops)


def _sample_mixer_call(x, state0, mixer_ops):
    bsz, seq, _ = x.shape
    stream_spec = lambda shape: pl.BlockSpec((1,) + shape, lambda b: (b, 0, 0))
    in_specs = ([stream_spec((seq, D_MODEL))] + [stream_spec(s) for s in _STATE_SHAPES]
                + [_const_spec(w.shape) for w in mixer_ops])
    out_specs = [stream_spec((seq, D_MODEL))] + [stream_spec(s) for s in _STATE_SHAPES]
    out_shape = ([jax.ShapeDtypeStruct((bsz, seq, D_MODEL), F32)]
                 + [jax.ShapeDtypeStruct((bsz,) + s, F32) for s in _STATE_SHAPES])
    return pl.pallas_call(
        functools.partial(_mixer_kernel, seq),
        grid=(bsz,), in_specs=in_specs, out_specs=out_specs, out_shape=out_shape,
        scratch_shapes=_mixer_scratch(seq), name="sample_mixer",
        compiler_params=pltpu.CompilerParams(dimension_semantics=("arbitrary",), vmem_limit_bytes=VMEM_LIMIT),
    )(x, *state0, *mixer_ops)


def _ffn_call(h2d, ffn_ops):
    n_tok = h2d.shape[0]
    tile = min(FFN_TILE, n_tok)
    tok_spec = pl.BlockSpec((tile, D_MODEL), lambda i: (i, 0))
    return pl.pallas_call(
        _ffn_kernel, grid=(n_tok // tile,),
        in_specs=[tok_spec] + [_const_spec(w.shape) for w in ffn_ops],
        out_specs=tok_spec, out_shape=jax.ShapeDtypeStruct((n_tok, D_MODEL), F32), name="sample_ffn",
        scratch_shapes=[pltpu.VMEM((tile, D_MODEL), BF16), pltpu.VMEM((tile, D_FF), BF16)],
        compiler_params=pltpu.CompilerParams(dimension_semantics=("arbitrary",), vmem_limit_bytes=VMEM_LIMIT),
    )(h2d, *ffn_ops)


IN_COLS = COL_END + 8
W_IN_PREP_ROWS = 256


def _w_in_kernel(w_t_ref, out_ref):
    src = {"z": 0, "xbc": SSM_WIDTH, "dt": SSM_WIDTH + CONV_DIM}
    src["q"] = src["dt"] + SSM_HEADS
    src["k"] = src["q"] + ATTN_WIDTH
    src["v"] = src["k"] + KV_W
    pieces = (("z", COL_Z, COL_XBC - COL_Z, 1.0), ("xbc", COL_XBC, COL_Q - COL_XBC, 1.0),
              ("q", COL_Q, COL_K - COL_Q, HEAD_DIM ** -0.5 * LOG2E), ("k", COL_K, KV_W, 1.0), ("v", COL_V, KV_W, 1.0))
    for name, dst, width, scale in pieces:
        for off in range(0, width, W_IN_PREP_ROWS):
            rows = min(W_IN_PREP_ROWS, width - off)
            blk = w_t_ref[src[name] + off:src[name] + off + rows, :]
            if scale != 1.0:
                blk = blk * scale
            out_ref[:, dst + off:dst + off + rows] = blk.astype(BF16).T
    w_dt = w_t_ref[src["dt"]:src["dt"] + SSM_HEADS, :]
    w_dt_rep = jnp.concatenate([w_dt] * (LANES // SSM_HEADS), axis=0)
    out_ref[:, COL_END:COL_END + LANES] = w_dt_rep.astype(BF16).T


def _prepare_w_in(w_t):
    return pl.pallas_call(
        _w_in_kernel, grid=(1,),
        in_specs=[_const_spec((IN_COLS, D_MODEL))], out_specs=pl.BlockSpec((D_MODEL, COL_END + LANES), lambda i: (0, 0)),
        out_shape=jax.ShapeDtypeStruct((D_MODEL, COL_END + LANES), BF16), name="w_in_prep",
        compiler_params=pltpu.CompilerParams(dimension_semantics=("arbitrary",), vmem_limit_bytes=VMEM_LIMIT),
    )(w_t)


def _window_to_kernel_layout(cache):
    b = cache.shape[0]
    return jnp.transpose(cache.astype(F32).reshape(b, WINDOW, KV_W), (0, 2, 1))


def _window_from_kernel_layout(win_t):
    b = win_t.shape[0]
    return jnp.transpose(win_t, (0, 2, 1)).reshape(1, b, WINDOW, KV_HEADS, HEAD_DIM)


def kernel(x_prompt, x_sample, state_conv, state_ssm, cache_k, cache_v, w_in, conv_w, conv_b, dt_bias, a_log,
           d_skip, ssm_norm_w, attn_sinks, w_out, ln1_g, ln1_b, w_gate, w_up, w_down, ln2_g, ln2_b):
    w_in_r = _prepare_w_in(jnp.transpose(w_in[0]))
    rep = LANES // SSM_HEADS
    bias_tables, sink_rows = _attention_tables(attn_sinks[0])
    vec_parts = {"conv_b": conv_b[0], "dt_bias": jnp.tile(dt_bias[0], rep), "a_log": jnp.tile(a_log[0], rep),
                 "d_skip": jnp.repeat(d_skip[0], HEAD_DIM), "norm_w": ssm_norm_w[0],
                 "sink_a": sink_rows[0], "sink_b": sink_rows[1],
                 "ln1_g": ln1_g[0], "ln1_b": ln1_b[0], "ln2_g": ln2_g[0], "ln2_b": ln2_b[0]}
    vec_parts.update({"conv_w%d" % i: conv_w[0, i] for i in range(CONV_W)})
    vecs = jnp.concatenate([vec_parts[name].astype(F32).reshape(width) for name, width in VEC_LAYOUT])
    vecs = vecs.reshape(1, VEC_WIDTH)
    mixer_ops = (w_in_r, vecs, _expansion_matrix(), bias_tables, w_out[0].astype(BF16))
    ffn_mats = (w_gate[0].astype(BF16), w_up[0].astype(BF16), w_down[0].astype(BF16))

    dbsz, dseq, _ = x_sample.shape

    y_p, conv_p, ssm_p, k_p, v_p = _prompt_call(x_prompt, mixer_ops, ffn_mats)

    sample_state = (state_conv[0].astype(F32), state_ssm[0].astype(F32).reshape(dbsz, SSM_WIDTH, D_STATE),
                    _window_to_kernel_layout(cache_k[0]), _window_to_kernel_layout(cache_v[0]))
    h_s, conv_s, ssm_s, k_s, v_s = _sample_mixer_call(x_sample, sample_state, mixer_ops)
    y_s = _ffn_call(h_s.reshape(dbsz * dseq, D_MODEL), (vecs,) + ffn_mats).reshape(dbsz, dseq, D_MODEL)

    ssm_shape = lambda a: a.reshape(1, a.shape[0], SSM_HEADS, HEAD_DIM, D_STATE)
    kv_shape = _window_from_kernel_layout
    return (y_p, y_s,
            conv_p[None], ssm_shape(ssm_p), kv_shape(k_p), kv_shape(v_p),
            conv_s[None], ssm_shape(ssm_s), kv_shape(k_s), kv_shape(v_s))
```

```python
import functools
import math

import jax
import jax.numpy as jnp
import numpy as np
from jax import lax
from jax.experimental import pallas as pl
from jax.experimental.pallas import tpu as pltpu

D_MODEL = 1024
CHUNK = 64
HEAD_DIM = 64
SSM_WIDTH = 512
SSM_HEADS = 8
SSM_GROUPS = 2
D_STATE = 128
CONV_W = 4
CONV_DIM = SSM_WIDTH + 2 * SSM_GROUPS * D_STATE
ATTN_WIDTH = 512
ATTN_HEADS = 8
KV_HEADS = 2
WINDOW = 128
D_FF = 2816
LN_EPS = 1e-5
RMS_EPS = 1e-5
DEEPNORM_ALPHA = 2.0 ** 0.25
LOG2E = math.log2(math.e)

LANES = 128
SUBLANES = 8
GROUP_W = SSM_WIDTH // SSM_GROUPS
KV_W = KV_HEADS * HEAD_DIM
BAND = WINDOW + CHUNK
HIST_CHUNKS = WINDOW // CHUNK
HEADS_PER_STACK = 4
STACK_ROWS = HEADS_PER_STACK * CHUNK
COL_Z = 0
COL_XBC = COL_Z + SSM_WIDTH
COL_Q = COL_XBC + CONV_DIM
COL_K = COL_Q + ATTN_WIDTH
COL_V = COL_K + KV_W
COL_END = COL_V + KV_W
STACK_A_HEADS = (0, 2, 5, 7)
STACK_B_HEADS = (1, 3, 4, 6)
EXPAND_K = 128
CONV_PAD = SUBLANES

PROMPT_TILE = 256
FFN_TILE = 512
FFN_CHUNK = 256
N_FFN_CHUNKS = D_FF // FFN_CHUNK
VMEM_LIMIT = 56 * 1024 * 1024
UNITS_AT_START = 2
UNITS_AFTER_FIRST_MATMULS = 4
UNITS_AFTER_VECTOR_WORK = 2
UNITS_AFTER_SECOND_MATMULS = 1

F32 = jnp.float32
BF16 = jnp.bfloat16


def _dot(a, b):
    return jnp.dot(a, b, preferred_element_type=F32)


def _dot_nt(a, b):
    return lax.dot_general(a, b, (((1,), (1,)), ((), ())), preferred_element_type=F32)


def _sigmoid(x):
    return 1.0 / (1.0 + jnp.exp2(x * -LOG2E))


def _silu(x):
    return x * _sigmoid(x)


def _layer_norm(x, g, b):
    mu = jnp.mean(x, axis=-1, keepdims=True)
    xc = x - mu
    var = jnp.mean(xc * xc, axis=-1, keepdims=True)
    return xc * lax.rsqrt(var + LN_EPS) * g + b


def _split3(x):
    hi = x.astype(BF16).astype(F32)
    r = x - hi
    mid = r.astype(BF16).astype(F32)
    lo = r - mid
    return hi, mid, lo


CONV_HIST = slice(CONV_PAD - (CONV_W - 1), CONV_PAD)


def _init_stream_state(ssm0_ref, k0_ref, v0_ref, st_ref, kf_ref, vf_ref, kb_ref, ksb_ref, vb_ref):
    st_ref[...] = ssm0_ref[0].T
    k0 = k0_ref[0].T
    v0 = v0_ref[0].T
    kf_ref[0:WINDOW, :] = k0
    vf_ref[0:WINDOW, :] = v0
    kb_ref[0:WINDOW, :] = k0.astype(BF16)
    vb_ref[0:WINDOW, :] = v0.astype(BF16)
    ksb_ref[0:WINDOW, :] = pltpu.roll(k0, HEAD_DIM, axis=1).astype(BF16)


def _write_stream_state(ssm_out_ref, k_out_ref, v_out_ref, st_ref, kf_ref, vf_ref):
    ssm_out_ref[0] = st_ref[...].T
    k_out_ref[0] = kf_ref[0:WINDOW, :].T
    v_out_ref[0] = vf_ref[0:WINDOW, :].T


def _in_projection(tile, xb_ref, w_in_ref, w_dt_ref,
                   convbuf, kf_ref, vf_ref, kb_ref, ksb_ref, vb_ref, zbuf, qbuf, dtbuf):
    xb = xb_ref[...]
    zbuf[...] = _dot(xb, w_in_ref[:, COL_Z:COL_XBC])
    convbuf[CONV_PAD:CONV_PAD + tile, :] = _dot(xb, w_in_ref[:, COL_XBC:COL_Q])
    qbuf[...] = _dot(xb, w_in_ref[:, COL_Q:COL_K])
    kv = _dot(xb, w_in_ref[:, COL_K:COL_END])
    dtbuf[...] = _dot(xb, w_dt_ref[...])
    k_new = kv[:, :KV_W]
    v_new = kv[:, KV_W:]
    kf_ref[WINDOW:WINDOW + tile, :] = k_new
    vf_ref[WINDOW:WINDOW + tile, :] = v_new
    kb_ref[WINDOW:WINDOW + tile, :] = k_new.astype(BF16)
    vb_ref[WINDOW:WINDOW + tile, :] = v_new.astype(BF16)
    ksb_ref[WINDOW:WINDOW + tile, :] = pltpu.roll(k_new, HEAD_DIM, axis=1).astype(BF16)


def _conv_and_decay(ci, conv_w_ref, conv_b_ref, dt_bias_ref, a_log_ref, convbuf, dtbuf, xsbuf, bcbuf):
    r0 = ci * CHUNK
    rows = slice(r0, r0 + CHUNK)

    window = convbuf[r0:r0 + CONV_PAD + CHUNK, :]
    conv = conv_b_ref[...] + conv_w_ref[CONV_W - 1][...] * window[CONV_PAD:]
    for back in range(1, CONV_W):
        conv = conv + conv_w_ref[CONV_W - 1 - back][...] * pltpu.roll(window, back, axis=0)[CONV_PAD:]
    xc = _silu(conv)
    xsbuf[rows, :] = xc[:, :SSM_WIDTH]
    bcbuf[rows, :] = xc[:, SSM_WIDTH:].astype(BF16)

    row_in_chunk = lax.broadcasted_iota(jnp.int32, (CHUNK, LANES), 0)
    lane = lax.broadcasted_iota(jnp.int32, (CHUNK, LANES), 1)
    dt_in = dtbuf[rows, :] + dt_bias_ref[...]
    dt = jnp.maximum(dt_in, 0.0) + jnp.log1p(jnp.exp(-jnp.abs(dt_in)))
    cum = dt * (-LOG2E * jnp.exp(a_log_ref[...]))
    step = 1
    while step < CHUNK:
        cum = cum + jnp.where(row_in_chunk >= step, pltpu.roll(cum, step, axis=0), 0.0)
        step *= 2
    packed = jnp.zeros((CHUNK, LANES), F32)
    for i, part in enumerate(_split3(cum) + _split3(dt)):
        packed = jnp.where((lane >= SSM_HEADS * i) & (lane < SSM_HEADS * (i + 1)), part, packed)
    return packed.astype(BF16)


def _prepare_tile(tile, x_ref, mixer_w, mixer_s):
    (w_in_ref, w_dt_ref, conv_w_ref, conv_b_ref, dt_bias_ref, a_log_ref) = mixer_w[:6]
    (convbuf, _, kf_ref, vf_ref, kb_ref, ksb_ref, vb_ref, _, _, zbuf, qbuf, dtbuf, xsbuf, bcbuf, _, _,
     xb_ref, packbuf, _) = mixer_s
    if x_ref is not None:
        xb_ref[...] = x_ref[0].astype(BF16)
    _in_projection(tile, xb_ref, w_in_ref, w_dt_ref,
                   convbuf, kf_ref, vf_ref, kb_ref, ksb_ref, vb_ref, zbuf, qbuf, dtbuf)
    packbuf[...] = _conv_and_decay(0, conv_w_ref, conv_b_ref, dt_bias_ref, a_log_ref, convbuf, dtbuf, xsbuf, bcbuf)


def _mixer_tile(hist_all_valid, tile, first_chunk, x_ref, x_next_ref, switch_stream, mixer_w, mixer_s):
    (_, _, conv_w_ref, conv_b_ref, dt_bias_ref, a_log_ref, d_ref, norm_w_ref,
     expand_ref, bias_ref, sink_ref, w_out_ref, ln_g_ref, ln_b_ref) = mixer_w
    (convbuf, st_ref, kf_ref, vf_ref, kb_ref, ksb_ref, vb_ref, vt_even, mixed_ref,
     zbuf, qbuf, dtbuf, xsbuf, bcbuf, cumx_ref, xdt_ref, xb_ref, packbuf, vt_odd) = mixer_s
    n_chunks = tile // CHUNK
    if x_next_ref is not None:
        xb_ref[...] = x_next_ref[0].astype(BF16)
    vt_even[...] = vb_ref[...].T
    if n_chunks > 1:
        vt_odd[...] = vb_ref[CHUNK:CHUNK + vt_odd.shape[1], :].T
    yield UNITS_AT_START

    lane = lax.broadcasted_iota(jnp.int32, (CHUNK, LANES), 1)
    low_half = lane < HEAD_DIM
    li = lax.broadcasted_iota(jnp.int32, (CHUNK, GROUP_W), 0)
    si = lax.broadcasted_iota(jnp.int32, (CHUNK, GROUP_W), 1) % CHUNK
    diag_mask = li == si
    causal_mask = si <= li
    bd_mask = (lax.broadcasted_iota(jnp.int32, (GROUP_W, GROUP_W), 0) // CHUNK
               == lax.broadcasted_iota(jnp.int32, (GROUP_W, GROUP_W), 1) // CHUNK)

    def conv_and_decay(ci):
        return _conv_and_decay(ci, conv_w_ref, conv_b_ref, dt_bias_ref, a_log_ref, convbuf, dtbuf, xsbuf, bcbuf)

    packed_next = packbuf[...]
    for ci in range(n_chunks):
        r0 = ci * CHUNK
        rows = slice(r0, r0 + CHUNK)
        packed = packed_next

        expanded = _dot(packed, expand_ref[...])
        if hist_all_valid:
            variant = HIST_CHUNKS
        else:
            variant = jnp.minimum(first_chunk + ci, HIST_CHUNKS)
        group_cols = [slice(g * GROUP_W, (g + 1) * GROUP_W) for g in range(SSM_GROUPS)]
        b_gs = [bcbuf[rows, g * D_STATE:(g + 1) * D_STATE] for g in range(SSM_GROUPS)]
        c_gs = [bcbuf[rows, GROUP_W + g * D_STATE:GROUP_W + (g + 1) * D_STATE] for g in range(SSM_GROUPS)]
        cbs = [_dot_nt(c_gs[g], jnp.concatenate([b_gs[g]] * HEADS_PER_STACK, axis=0)) for g in range(SSM_GROUPS)]
        y_offs = [_dot(c_gs[g], st_ref[:, group_cols[g]].astype(BF16)) for g in range(SSM_GROUPS)]
        blocks = [qbuf[rows, j * LANES:(j + 1) * LANES] for j in range(ATTN_HEADS // 2)]
        evens = [jnp.where(low_half, blk, 0.0).astype(BF16) for blk in blocks]
        odds = [jnp.where(low_half, 0.0, blk).astype(BF16) for blk in blocks]
        q_a = jnp.concatenate([evens[0], evens[1], odds[2], odds[3]], axis=0)
        q_b = jnp.concatenate([odds[0], odds[1], evens[2], evens[3]], axis=0)
        scores = [_dot_nt(k_ref[r0:r0 + BAND, :], q_s) for q_s, k_ref in ((q_a, kb_ref), (q_b, ksb_ref))]
        yield UNITS_AFTER_FIRST_MATMULS

        if ci + 1 < n_chunks:
            packed_next = conv_and_decay(ci + 1)
        cumx_ref[...] = expanded[:, :SSM_WIDTH]
        xdt_ref[...] = xsbuf[rows, :] * expanded[:, SSM_WIDTH:]
        m_gs, bds, xws, probs = [], [], [], []
        for g in range(SSM_GROUPS):
            colb = cumx_ref[:, group_cols[g]]
            rowb = jnp.sum(jnp.where(diag_mask, colb, 0.0), axis=0, keepdims=True)
            lmat = jnp.exp2(jnp.where(causal_mask, colb - rowb, -jnp.inf))
            m_gs.append((cbs[g] * lmat).astype(BF16))
            xdt_g = xdt_ref[:, group_cols[g]]
            xdt_b = xdt_g.astype(BF16)
            bds.append(jnp.where(bd_mask, jnp.concatenate([xdt_b] * HEADS_PER_STACK, axis=0), jnp.zeros((), BF16)))
            last = colb[CHUNK - 1:CHUNK, :]
            xws.append((xdt_g * jnp.exp2(last - colb)).astype(BF16))
        for s_idx in range(2):
            s = scores[s_idx] + bias_ref[variant, s_idx]
            sink = sink_ref[s_idx][...]
            m = jnp.maximum(jnp.max(s, axis=0, keepdims=True), sink)
            p = jnp.exp2(s - m)
            denom = jnp.sum(p, axis=0, keepdims=True) + jnp.exp2(sink - m)
            probs.append((p * (1.0 / denom)).astype(BF16))
        b_ts = [b_g.T for b_g in b_gs]
        yield UNITS_AFTER_VECTOR_WORK

        y_diags = [_dot(m_gs[g], bds[g]) for g in range(SSM_GROUPS)]
        updates = [_dot(b_ts[g], xws[g]) for g in range(SSM_GROUPS)]
        vt_src, c0 = (vt_even, r0) if ci % 2 == 0 else (vt_odd, r0 - CHUNK)
        vband_t = vt_src[:, c0:c0 + BAND]
        vband_t_swapped = jnp.concatenate([vband_t[HEAD_DIM:], vband_t[:HEAD_DIM]], axis=0)
        outs_t = [_dot(vband_t, probs[0]), _dot(vband_t_swapped, probs[1])]
        yield UNITS_AFTER_SECOND_MATMULS

        for g in range(SSM_GROUPS):
            cols = group_cols[g]
            colb = cumx_ref[:, cols]
            last = colb[CHUNK - 1:CHUNK, :]
            st_ref[:, cols] = jnp.exp2(last) * st_ref[:, cols] + updates[g]
            y = y_diags[g] + y_offs[g] * jnp.exp2(colb) + d_ref[:, cols] * xsbuf[rows, cols]
            gg = y * _silu(zbuf[rows, cols])
            ms = jnp.mean(gg * gg, axis=-1, keepdims=True)
            mixed_ref[rows, cols] = (gg * lax.rsqrt(ms + RMS_EPS) * norm_w_ref[:, cols]).astype(BF16)
        o_a, o_b = [o_t.T for o_t in outs_t]
        sel = [(o_a, o_b), (o_a, o_b), (o_b, o_a), (o_b, o_a)]
        for j in range(ATTN_HEADS // 2):
            ev, od = sel[j]
            blk = jnp.where(low_half, ev[j * CHUNK:(j + 1) * CHUNK], od[j * CHUNK:(j + 1) * CHUNK])
            mixed_ref[rows, SSM_WIDTH + j * LANES:SSM_WIDTH + (j + 1) * LANES] = blk.astype(BF16)

    convbuf[CONV_HIST, :] = convbuf[CONV_PAD + tile - (CONV_W - 1):CONV_PAD + tile, :]
    for ref in (kf_ref, vf_ref, kb_ref, ksb_ref, vb_ref):
        ref[0:WINDOW, :] = ref[tile:tile + WINDOW, :]

    yield

    mix = _dot(mixed_ref[...], w_out_ref[...])
    if x_next_ref is not None:
        switch_stream()
        _prepare_tile(tile, None, mixer_w, mixer_s)
    return _layer_norm(DEEPNORM_ALPHA * x_ref[0] + mix, ln_g_ref[...], ln_b_ref[...])


FFN_DOWN_AFTER = (6, N_FFN_CHUNKS)
N_FFN_UNITS = 2 * N_FFN_CHUNKS + len(FFN_DOWN_AFTER) * (D_MODEL // FFN_CHUNK)


def _ffn_tile(h_ref, hb_ref, act_ref, w_gate_ref, w_up_ref, w_down_ref, ln_g_ref, ln_b_ref):
    out_cols = [slice(n * FFN_CHUNK, (n + 1) * FFN_CHUNK) for n in range(D_MODEL // FFN_CHUNK)]
    pre = [DEEPNORM_ALPHA * h_ref[:, cols] for cols in out_cols]
    done = 0
    for c in range(N_FFN_CHUNKS):
        cols = slice(c * FFN_CHUNK, (c + 1) * FFN_CHUNK)
        gate = _dot(hb_ref[...], w_gate_ref[:, cols])
        yield
        up = _dot(hb_ref[...], w_up_ref[:, cols])
        act_ref[:, cols] = (_silu(gate) * up).astype(BF16)
        yield
        if c + 1 in FFN_DOWN_AFTER:
            k_rows = slice(done * FFN_CHUNK, (c + 1) * FFN_CHUNK)
            done = c + 1
            for n, cols in enumerate(out_cols):
                pre[n] = pre[n] + _dot(act_ref[:, k_rows], w_down_ref[k_rows, cols])
                yield
    return _layer_norm(jnp.concatenate(pre, axis=1), ln_g_ref[...], ln_b_ref[...])


def _run(gen):
    while True:
        try:
            next(gen)
        except StopIteration as stop:
            return stop.value


def _interleave(mixer, ffn):
    remaining = N_FFN_UNITS
    while True:
        try:
            wanted = next(mixer)
        except StopIteration as stop:
            return stop.value, y
        last = wanted is None
        for _ in range(remaining if last else min(wanted, remaining)):
            next(ffn)
            remaining -= 1
        if last:
            y = _run(ffn)


N_MIXER_SCRATCH = 19

VEC_LAYOUT = (tuple(("conv_w%d" % i, CONV_DIM) for i in range(CONV_W))
              + (("conv_b", CONV_DIM), ("dt_bias", LANES), ("a_log", LANES), ("d_skip", SSM_WIDTH),
                 ("norm_w", SSM_WIDTH), ("sink_a", STACK_ROWS), ("sink_b", STACK_ROWS),
                 ("ln1_g", D_MODEL), ("ln1_b", D_MODEL), ("ln2_g", D_MODEL), ("ln2_b", D_MODEL)))
VEC_WIDTH = sum(width for _, width in VEC_LAYOUT)


def _vec_views(vec_ref):
    views, offset = {}, 0
    for name, width in VEC_LAYOUT:
        views[name] = vec_ref.at[:, offset:offset + width]
        offset += width
    return views


def _mixer_weights(w_in_ref, vec_ref, expand_ref, bias_ref, w_out_ref):
    v = _vec_views(vec_ref)
    return (w_in_ref.at[:, :COL_END], w_in_ref.at[:, COL_END:COL_END + LANES],
            tuple(v["conv_w%d" % i] for i in range(CONV_W)), v["conv_b"], v["dt_bias"], v["a_log"], v["d_skip"],
            v["norm_w"], expand_ref, bias_ref, (v["sink_a"], v["sink_b"]), w_out_ref, v["ln1_g"], v["ln1_b"])


def _ffn_weights(vec_ref, w_gate_ref, w_up_ref, w_down_ref):
    v = _vec_views(vec_ref)
    return (w_gate_ref, w_up_ref, w_down_ref, v["ln2_g"], v["ln2_b"])


def _fused_kernel(tile, n_tiles, n_total,
                  x_ref, x_next_ref, w_in_ref, vec_ref, expand_ref, bias_ref, w_out_ref, w_gate_ref, w_up_ref, w_down_ref,
                  y_ref, conv_out_ref, ssm_out_ref, k_out_ref, v_out_ref, *scratch):
    mixer_w = _mixer_weights(w_in_ref, vec_ref, expand_ref, bias_ref, w_out_ref)
    ffn_w = _ffn_weights(vec_ref, w_gate_ref, w_up_ref, w_down_ref)
    mixer_s = scratch[:N_MIXER_SCRATCH]
    h_s, hb_s, act_s = scratch[N_MIXER_SCRATCH:]
    convbuf, st_ref, kf_ref, vf_ref = mixer_s[:4]

    g = pl.program_id(0)
    t = jnp.minimum(g, n_total - 1) % n_tiles
    next_starts_stream = (g + 1 < n_total) & ((g + 1) % n_tiles == 0)

    @pl.when(g == 0)
    def _():
        h_s[...] = jnp.zeros(h_s.shape, F32)
        hb_s[...] = jnp.zeros(hb_s.shape, BF16)
        convbuf[0:CONV_PAD, :] = jnp.zeros((CONV_PAD, CONV_DIM), F32)
        _prepare_tile(tile, x_ref, mixer_w, mixer_s)

    @pl.when(t == 0)
    def _():
        st_ref[...] = jnp.zeros(st_ref.shape, F32)
        for ref in mixer_s[2:7]:
            ref[0:WINDOW, :] = jnp.zeros((WINDOW, KV_W), ref.dtype)

    def switch_stream():
        hist = convbuf[CONV_HIST, :]
        conv_out_ref[0] = hist
        convbuf[CONV_HIST, :] = jnp.where(next_starts_stream, 0.0, hist)

    n_chunks = tile // CHUNK
    ffn = _ffn_tile(h_s, hb_s, act_s, *ffn_w)
    mixer = _mixer_tile(False, tile, t * n_chunks, x_ref, x_next_ref, switch_stream, mixer_w, mixer_s)
    h, y = _interleave(mixer, ffn)
    y_ref[0] = y
    h_s[...] = h
    hb_s[...] = h.astype(BF16)

    @pl.when((t == n_tiles - 1) & (g < n_total))
    def _():
        _write_stream_state(ssm_out_ref, k_out_ref, v_out_ref, st_ref, kf_ref, vf_ref)


def _mixer_kernel(tile, x_ref, conv0_ref, ssm0_ref, k0_ref, v0_ref,
                  w_in_ref, vec_ref, expand_ref, bias_ref, w_out_ref,
                  h_ref, conv_out_ref, ssm_out_ref, k_out_ref, v_out_ref, *mixer_s):
    mixer_w = _mixer_weights(w_in_ref, vec_ref, expand_ref, bias_ref, w_out_ref)
    convbuf, st_ref, kf_ref, vf_ref = mixer_s[:4]
    convbuf[0:CONV_PAD, :] = jnp.zeros((CONV_PAD, CONV_DIM), F32)
    convbuf[CONV_HIST, :] = conv0_ref[0]
    _init_stream_state(ssm0_ref, k0_ref, v0_ref, *mixer_s[1:7])
    _prepare_tile(tile, x_ref, mixer_w, mixer_s)
    h_ref[0] = _run(_mixer_tile(True, tile, 0, x_ref, None, None, mixer_w, mixer_s))
    conv_out_ref[0] = convbuf[CONV_HIST, :]
    _write_stream_state(ssm_out_ref, k_out_ref, v_out_ref, st_ref, kf_ref, vf_ref)


def _ffn_kernel(h_ref, vec_ref, w_gate_ref, w_up_ref, w_down_ref, y_ref, hb_s, act_s):
    hb_s[...] = h_ref[...].astype(BF16)
    y_ref[...] = _run(_ffn_tile(h_ref, hb_s, act_s, *_ffn_weights(vec_ref, w_gate_ref, w_up_ref, w_down_ref)))


def _const_spec(shape):
    zeros = (0,) * len(shape)
    return pl.BlockSpec(shape, lambda *_: zeros, pipeline_mode=pl.Buffered(1))


def _mixer_scratch(tile):
    return [pltpu.VMEM((CONV_PAD + tile, CONV_DIM), F32),
            pltpu.VMEM((D_STATE, SSM_WIDTH), F32),
            pltpu.VMEM((WINDOW + tile, KV_W), F32), pltpu.VMEM((WINDOW + tile, KV_W), F32),
            pltpu.VMEM((WINDOW + tile, KV_W), BF16), pltpu.VMEM((WINDOW + tile, KV_W), BF16),
            pltpu.VMEM((WINDOW + tile, KV_W), BF16), pltpu.VMEM((KV_W, WINDOW + tile), BF16),
            pltpu.VMEM((tile, D_MODEL), BF16),
            pltpu.VMEM((tile, SSM_WIDTH), F32),
            pltpu.VMEM((tile, ATTN_WIDTH), F32),
            pltpu.VMEM((tile, LANES), F32),
            pltpu.VMEM((tile, SSM_WIDTH), F32),
            pltpu.VMEM((tile, CONV_DIM - SSM_WIDTH), BF16),
            pltpu.VMEM((CHUNK, SSM_WIDTH), F32),
            pltpu.VMEM((CHUNK, SSM_WIDTH), F32),
            pltpu.VMEM((tile, D_MODEL), BF16),
            pltpu.VMEM((CHUNK, LANES), BF16),
            pltpu.VMEM((KV_W, max(tile, LANES) + CHUNK), BF16)]


_STATE_SHAPES = ((CONV_W - 1, CONV_DIM), (SSM_WIDTH, D_STATE), (KV_W, WINDOW), (KV_W, WINDOW))


def _attention_tables(attn_sinks):
    slopes = np.exp2(-8.0 * np.arange(1, ATTN_HEADS + 1, dtype=np.float64) / ATTN_HEADS)
    qi = np.arange(CHUNK)[None, :] + WINDOW
    kj = np.arange(BAND)[:, None]
    dist = np.abs(qi - kj).astype(np.float64)
    key_chunk = kj // CHUNK
    tables = np.zeros((HIST_CHUNKS + 1, 2, BAND, STACK_ROWS), np.float32)
    for variant in range(HIST_CHUNKS + 1):
        valid = key_chunk >= (HIST_CHUNKS - variant)
        for s_idx, heads in enumerate((STACK_A_HEADS, STACK_B_HEADS)):
            for i, h in enumerate(heads):
                tables[variant, s_idx, :, i * CHUNK:(i + 1) * CHUNK] = np.where(
                    valid, -slopes[h] * dist * LOG2E, -np.inf)
    order = np.array(STACK_A_HEADS + STACK_B_HEADS)
    sink_rows = jnp.repeat(attn_sinks.astype(F32)[order] * LOG2E, CHUNK).reshape(2, STACK_ROWS)
    return jnp.asarray(tables), sink_rows


def _expansion_matrix():
    e = np.zeros((EXPAND_K, 2 * SSM_WIDTH), np.float32)
    for part in range(6):
        for h in range(SSM_HEADS):
            c0 = (part // 3) * SSM_WIDTH + h * HEAD_DIM
            e[part * SSM_HEADS + h, c0:c0 + HEAD_DIM] = 1.0
    return jnp.asarray(e, BF16)


def _prompt_call(x, mixer_ops, ffn_ops):
    bsz, seq, _ = x.shape
    tile = PROMPT_TILE
    n_tiles = seq // tile
    n_total = bsz * n_tiles

    def mixer_tile_idx(g):
        gc = jnp.minimum(g, n_total - 1)
        return gc // n_tiles, gc % n_tiles

    def next_tile_idx(g):
        return mixer_tile_idx(g + 1)

    def ffn_tile_idx(g):
        gp = jnp.maximum(g - 1, 0)
        return gp // n_tiles, gp % n_tiles

    tile_spec = lambda idx: pl.BlockSpec((1, tile, D_MODEL), lambda g: (*idx(g), 0))
    stream_spec = lambda shape: pl.BlockSpec((1,) + shape, lambda g: (mixer_tile_idx(g)[0], 0, 0))
    in_specs = ([tile_spec(mixer_tile_idx), tile_spec(next_tile_idx)]
                + [_const_spec(w.shape) for w in mixer_ops + ffn_ops])
    out_specs = [tile_spec(ffn_tile_idx)] + [stream_spec(s) for s in _STATE_SHAPES]
    out_shape = ([jax.ShapeDtypeStruct((bsz, seq, D_MODEL), F32)]
                 + [jax.ShapeDtypeStruct((bsz,) + s, F32) for s in _STATE_SHAPES])
    scratch = _mixer_scratch(tile) + [pltpu.VMEM((tile, D_MODEL), F32), pltpu.VMEM((tile, D_MODEL), BF16),
                                      pltpu.VMEM((tile, D_FF), BF16)]
    return pl.pallas_call(
        functools.partial(_fused_kernel, tile, n_tiles, n_total),
        grid=(n_total + 1,), in_specs=in_specs, out_specs=out_specs, out_shape=out_shape,
        scratch_shapes=scratch, name="prompt_layer",
        compiler_params=pltpu.CompilerParams(dimension_semantics=("arbitrary",), vmem_limit_bytes=VMEM_LIMIT),
    )(x, x, *mixer_ops, *ffn_ops)


def _sample_mixer_call(x, state0, mixer_ops):
    bsz, seq, _ = x.shape
    stream_spec = lambda shape: pl.BlockSpec((1,) + shape, lambda b: (b, 0, 0))
    in_specs = ([stream_spec((seq, D_MODEL))] + [stream_spec(s) for s in _STATE_SHAPES]
                + [_const_spec(w.shape) for w in mixer_ops])
    out_specs = [stream_spec((seq, D_MODEL))] + [stream_spec(s) for s in _STATE_SHAPES]
    out_shape = ([jax.ShapeDtypeStruct((bsz, seq, D_MODEL), F32)]
                 + [jax.ShapeDtypeStruct((bsz,) + s, F32) for s in _STATE_SHAPES])
    return pl.pallas_call(
        functools.partial(_mixer_kernel, seq),
        grid=(bsz,), in_specs=in_specs, out_specs=out_specs, out_shape=out_shape,
        scratch_shapes=_mixer_scratch(seq), name="sample_mixer",
        compiler_params=pltpu.CompilerParams(dimension_semantics=("arbitrary",), vmem_limit_bytes=VMEM_LIMIT),
    )(x, *state0, *mixer_ops)


def _ffn_call(h2d, ffn_ops):
    n_tok = h2d.shape[0]
    tile = min(FFN_TILE, n_tok)
    tok_spec = pl.BlockSpec((tile, D_MODEL), lambda i: (i, 0))
    return pl.pallas_call(
        _ffn_kernel, grid=(n_tok // tile,),
        in_specs=[tok_spec] + [_const_spec(w.shape) for w in ffn_ops],
        out_specs=tok_spec, out_shape=jax.ShapeDtypeStruct((n_tok, D_MODEL), F32), name="sample_ffn",
        scratch_shapes=[pltpu.VMEM((tile, D_MODEL), BF16), pltpu.VMEM((tile, D_FF), BF16)],
        compiler_params=pltpu.CompilerParams(dimension_semantics=("arbitrary",), vmem_limit_bytes=VMEM_LIMIT),
    )(h2d, *ffn_ops)


IN_COLS = COL_END + 8
W_IN_PREP_ROWS = 256


def _w_in_kernel(w_t_ref, out_ref):
    src = {"z": 0, "xbc": SSM_WIDTH, "dt": SSM_WIDTH + CONV_DIM}
    src["q"] = src["dt"] + SSM_HEADS
    src["k"] = src["q"] + ATTN_WIDTH
    src["v"] = src["k"] + KV_W
    pieces = (("z", COL_Z, COL_XBC - COL_Z, 1.0), ("xbc", COL_XBC, COL_Q - COL_XBC, 1.0),
              ("q", COL_Q, COL_K - COL_Q, HEAD_DIM ** -0.5 * LOG2E), ("k", COL_K, KV_W, 1.0), ("v", COL_V, KV_W, 1.0))
    for name, dst, width, scale in pieces:
        for off in range(0, width, W_IN_PREP_ROWS):
            rows = min(W_IN_PREP_ROWS, width - off)
            blk = w_t_ref[src[name] + off:src[name] + off + rows, :]
            if scale != 1.0:
                blk = blk * scale
            out_ref[:, dst + off:dst + off + rows] = blk.astype(BF16).T
    w_dt = w_t_ref[src["dt"]:src["dt"] + SSM_HEADS, :]
    w_dt_rep = jnp.concatenate([w_dt] * (LANES // SSM_HEADS), axis=0)
    out_ref[:, COL_END:COL_END + LANES] = w_dt_rep.astype(BF16).T


def _prepare_w_in(w_t):
    return pl.pallas_call(
        _w_in_kernel, grid=(1,),
        in_specs=[_const_spec((IN_COLS, D_MODEL))], out_specs=pl.BlockSpec((D_MODEL, COL_END + LANES), lambda i: (0, 0)),
        out_shape=jax.ShapeDtypeStruct((D_MODEL, COL_END + LANES), BF16), name="w_in_prep",
        compiler_params=pltpu.CompilerParams(dimension_semantics=("arbitrary",), vmem_limit_bytes=VMEM_LIMIT),
    )(w_t)


def _window_to_kernel_layout(cache):
    b = cache.shape[0]
    return jnp.transpose(cache.astype(F32).reshape(b, WINDOW, KV_W), (0, 2, 1))


def _window_from_kernel_layout(win_t):
    b = win_t.shape[0]
    return jnp.transpose(win_t, (0, 2, 1)).reshape(1, b, WINDOW, KV_HEADS, HEAD_DIM)


def kernel(x_prompt, x_sample, state_conv, state_ssm, cache_k, cache_v, w_in, conv_w, conv_b, dt_bias, a_log,
           d_skip, ssm_norm_w, attn_sinks, w_out, ln1_g, ln1_b, w_gate, w_up, w_down, ln2_g, ln2_b):
    w_in_r = _prepare_w_in(jnp.transpose(w_in[0]))
    rep = LANES // SSM_HEADS
    bias_tables, sink_rows = _attention_tables(attn_sinks[0])
    vec_parts = {"conv_b": conv_b[0], "dt_bias": jnp.tile(dt_bias[0], rep), "a_log": jnp.tile(a_log[0], rep),
                 "d_skip": jnp.repeat(d_skip[0], HEAD_DIM), "norm_w": ssm_norm_w[0],
                 "sink_a": sink_rows[0], "sink_b": sink_rows[1],
                 "ln1_g": ln1_g[0], "ln1_b": ln1_b[0], "ln2_g": ln2_g[0], "ln2_b": ln2_b[0]}
    vec_parts.update({"conv_w%d" % i: conv_w[0, i] for i in range(CONV_W)})
    vecs = jnp.concatenate([vec_parts[name].astype(F32).reshape(width) for name, width in VEC_LAYOUT])
    vecs = vecs.reshape(1, VEC_WIDTH)
    mixer_ops = (w_in_r, vecs, _expansion_matrix(), bias_tables, w_out[0].astype(BF16))
    ffn_mats = (w_gate[0].astype(BF16), w_up[0].astype(BF16), w_down[0].astype(BF16))

    dbsz, dseq, _ = x_sample.shape

    y_p, conv_p, ssm_p, k_p, v_p = _prompt_call(x_prompt, mixer_ops, ffn_mats)

    sample_state = (state_conv[0].astype(F32), state_ssm[0].astype(F32).reshape(dbsz, SSM_WIDTH, D_STATE),
                    _window_to_kernel_layout(cache_k[0]), _window_to_kernel_layout(cache_v[0]))
    h_s, conv_s, ssm_s, k_s, v_s = _sample_mixer_call(x_sample, sample_state, mixer_ops)
    y_s = _ffn_call(h_s.reshape(dbsz * dseq, D_MODEL), (vecs,) + ffn_mats).reshape(dbsz, dseq, D_MODEL)

    ssm_shape = lambda a: a.reshape(1, a.shape[0], SSM_HEADS, HEAD_DIM, D_STATE)
    kv_shape = _window_from_kernel_layout
    return (y_p, y_s,
            conv_p[None], ssm_shape(ssm_p), kv_shape(k_p), kv_shape(v_p),
            conv_s[None], ssm_shape(ssm_s), kv_shape(k_s), kv_shape(v_s))
```

```python
import functools
import math

import jax
import jax.numpy as jnp
import numpy as np
from jax import lax
from jax.experimental import pallas as pl
from jax.experimental.pallas import tpu as pltpu

D_MODEL = 1024
CHUNK = 64
HEAD_DIM = 64
SSM_WIDTH = 512
SSM_HEADS = 8
SSM_GROUPS = 2
D_STATE = 128
CONV_W = 4
CONV_DIM = SSM_WIDTH + 2 * SSM_GROUPS * D_STATE
ATTN_WIDTH = 512
ATTN_HEADS = 8
KV_HEADS = 2
WINDOW = 128
D_FF = 2816
LN_EPS = 1e-5
RMS_EPS = 1e-5
DEEPNORM_ALPHA = 2.0 ** 0.25
LOG2E = math.log2(math.e)

LANES = 128
SUBLANES = 8
GROUP_W = SSM_WIDTH // SSM_GROUPS
KV_W = KV_HEADS * HEAD_DIM
BAND = WINDOW + CHUNK
HIST_CHUNKS = WINDOW // CHUNK
HEADS_PER_STACK = 4
STACK_ROWS = HEADS_PER_STACK * CHUNK
COL_Z = 0
COL_XBC = COL_Z + SSM_WIDTH
COL_Q = COL_XBC + CONV_DIM
COL_K = COL_Q + ATTN_WIDTH
COL_V = COL_K + KV_W
COL_END = COL_V + KV_W
STACK_A_HEADS = (0, 2, 5, 7)
STACK_B_HEADS = (1, 3, 4, 6)
EXPAND_K = 128
CONV_PAD = SUBLANES

PROMPT_TILE = 256
FFN_TILE = 512
FFN_CHUNK = 256
N_FFN_CHUNKS = D_FF // FFN_CHUNK
VMEM_LIMIT = 56 * 1024 * 1024
UNITS_AT_START = 1
UNITS_AFTER_FIRST_MATMULS = 4
UNITS_AFTER_VECTOR_WORK = 2
UNITS_AFTER_SECOND_MATMULS = 1

F32 = jnp.float32
BF16 = jnp.bfloat16


def _dot(a, b):
    return jnp.dot(a, b, preferred_element_type=F32)


def _dot_nt(a, b):
    return lax.dot_general(a, b, (((1,), (1,)), ((), ())), preferred_element_type=F32)


def _sigmoid(x):
    return 1.0 / (1.0 + jnp.exp2(x * -LOG2E))


def _silu(x):
    return x * _sigmoid(x)


def _layer_norm(x, g, b):
    mu = jnp.mean(x, axis=-1, keepdims=True)
    xc = x - mu
    var = jnp.mean(xc * xc, axis=-1, keepdims=True)
    return xc * lax.rsqrt(var + LN_EPS) * g + b


def _split3(x):
    hi = x.astype(BF16).astype(F32)
    r = x - hi
    mid = r.astype(BF16).astype(F32)
    lo = r - mid
    return hi, mid, lo


CONV_HIST = slice(CONV_PAD - (CONV_W - 1), CONV_PAD)


def _init_stream_state(ssm0_ref, k0_ref, v0_ref, st_ref, kf_ref, vf_ref, kb_ref, ksb_ref, vb_ref):
    st_ref[...] = ssm0_ref[0].T
    k0 = k0_ref[0].T
    v0 = v0_ref[0].T
    kf_ref[0:WINDOW, :] = k0
    vf_ref[0:WINDOW, :] = v0
    kb_ref[0:WINDOW, :] = k0.astype(BF16)
    vb_ref[0:WINDOW, :] = v0.astype(BF16)
    ksb_ref[0:WINDOW, :] = pltpu.roll(k0, HEAD_DIM, axis=1).astype(BF16)


def _write_stream_state(ssm_out_ref, k_out_ref, v_out_ref, st_ref, kf_ref, vf_ref):
    ssm_out_ref[0] = st_ref[...].T
    k_out_ref[0] = kf_ref[0:WINDOW, :].T
    v_out_ref[0] = vf_ref[0:WINDOW, :].T


def _in_projection(tile, xb_ref, w_in_ref, w_dt_ref,
                   convbuf, kf_ref, vf_ref, kb_ref, ksb_ref, vb_ref, zbuf, qbuf, dtbuf):
    xb = xb_ref[...]
    zbuf[...] = _dot(xb, w_in_ref[:, COL_Z:COL_XBC])
    convbuf[CONV_PAD:CONV_PAD + tile, :] = _dot(xb, w_in_ref[:, COL_XBC:COL_Q])
    qbuf[...] = _dot(xb, w_in_ref[:, COL_Q:COL_K])
    kv = _dot(xb, w_in_ref[:, COL_K:COL_END])
    dtbuf[...] = _dot(xb, w_dt_ref[...])
    k_new = kv[:, :KV_W]
    v_new = kv[:, KV_W:]
    kf_ref[WINDOW:WINDOW + tile, :] = k_new
    vf_ref[WINDOW:WINDOW + tile, :] = v_new
    kb_ref[WINDOW:WINDOW + tile, :] = k_new.astype(BF16)
    vb_ref[WINDOW:WINDOW + tile, :] = v_new.astype(BF16)
    ksb_ref[WINDOW:WINDOW + tile, :] = pltpu.roll(k_new, HEAD_DIM, axis=1).astype(BF16)


def _conv_and_decay(ci, conv_w_ref, conv_b_ref, dt_bias_ref, a_log_ref, convbuf, dtbuf, xsbuf, bcbuf):
    r0 = ci * CHUNK
    rows = slice(r0, r0 + CHUNK)

    window = convbuf[r0:r0 + CONV_PAD + CHUNK, :]
    conv = conv_b_ref[...] + conv_w_ref[CONV_W - 1][...] * window[CONV_PAD:]
    for back in range(1, CONV_W):
        conv = conv + conv_w_ref[CONV_W - 1 - back][...] * pltpu.roll(window, back, axis=0)[CONV_PAD:]
    xc = _silu(conv)
    xsbuf[rows, :] = xc[:, :SSM_WIDTH]
    bcbuf[rows, :] = xc[:, SSM_WIDTH:].astype(BF16)

    row_in_chunk = lax.broadcasted_iota(jnp.int32, (CHUNK, LANES), 0)
    lane = lax.broadcasted_iota(jnp.int32, (CHUNK, LANES), 1)
    dt_in = dtbuf[rows, :] + dt_bias_ref[...]
    dt = jnp.maximum(dt_in, 0.0) + jnp.log1p(jnp.exp(-jnp.abs(dt_in)))
    cum = dt * (-LOG2E * jnp.exp(a_log_ref[...]))
    step = 1
    while step < CHUNK:
        cum = cum + jnp.where(row_in_chunk >= step, pltpu.roll(cum, step, axis=0), 0.0)
        step *= 2
    packed = jnp.zeros((CHUNK, LANES), F32)
    for i, part in enumerate(_split3(cum) + _split3(dt)):
        packed = jnp.where((lane >= SSM_HEADS * i) & (lane < SSM_HEADS * (i + 1)), part, packed)
    return packed.astype(BF16)


def _prepare_tile(tile, x_ref, mixer_w, mixer_s):
    (w_in_ref, w_dt_ref, conv_w_ref, conv_b_ref, dt_bias_ref, a_log_ref) = mixer_w[:6]
    (convbuf, _, kf_ref, vf_ref, kb_ref, ksb_ref, vb_ref, _, _, zbuf, qbuf, dtbuf, xsbuf, bcbuf, _, _,
     xb_ref, packbuf, _) = mixer_s
    if x_ref is not None:
        xb_ref[...] = x_ref[0].astype(BF16)
    _in_projection(tile, xb_ref, w_in_ref, w_dt_ref,
                   convbuf, kf_ref, vf_ref, kb_ref, ksb_ref, vb_ref, zbuf, qbuf, dtbuf)
    packbuf[...] = _conv_and_decay(0, conv_w_ref, conv_b_ref, dt_bias_ref, a_log_ref, convbuf, dtbuf, xsbuf, bcbuf)


def _mixer_tile(hist_all_valid, tile, first_chunk, x_ref, x_next_ref, switch_stream, mixer_w, mixer_s):
    (_, _, conv_w_ref, conv_b_ref, dt_bias_ref, a_log_ref, d_ref, norm_w_ref,
     expand_ref, bias_ref, sink_ref, w_out_ref, ln_g_ref, ln_b_ref) = mixer_w
    (convbuf, st_ref, kf_ref, vf_ref, kb_ref, ksb_ref, vb_ref, vt_even, mixed_ref,
     zbuf, qbuf, dtbuf, xsbuf, bcbuf, cumx_ref, xdt_ref, xb_ref, packbuf, vt_odd) = mixer_s
    n_chunks = tile // CHUNK
    if x_next_ref is not None:
        xb_ref[...] = x_next_ref[0].astype(BF16)
    vt_even[...] = vb_ref[...].T
    if n_chunks > 1:
        vt_odd[...] = vb_ref[CHUNK:CHUNK + vt_odd.shape[1], :].T
    yield UNITS_AT_START

    lane = lax.broadcasted_iota(jnp.int32, (CHUNK, LANES), 1)
    low_half = lane < HEAD_DIM
    li = lax.broadcasted_iota(jnp.int32, (CHUNK, GROUP_W), 0)
    si = lax.broadcasted_iota(jnp.int32, (CHUNK, GROUP_W), 1) % CHUNK
    diag_mask = li == si
    causal_mask = si <= li
    bd_mask = (lax.broadcasted_iota(jnp.int32, (GROUP_W, GROUP_W), 0) // CHUNK
               == lax.broadcasted_iota(jnp.int32, (GROUP_W, GROUP_W), 1) // CHUNK)

    def conv_and_decay(ci):
        return _conv_and_decay(ci, conv_w_ref, conv_b_ref, dt_bias_ref, a_log_ref, convbuf, dtbuf, xsbuf, bcbuf)

    packed_next = packbuf[...]
    for ci in range(n_chunks):
        r0 = ci * CHUNK
        rows = slice(r0, r0 + CHUNK)
        packed = packed_next

        expanded = _dot(packed, expand_ref[...])
        if hist_all_valid:
            variant = HIST_CHUNKS
        else:
            variant = jnp.minimum(first_chunk + ci, HIST_CHUNKS)
        group_cols = [slice(g * GROUP_W, (g + 1) * GROUP_W) for g in range(SSM_GROUPS)]
        b_gs = [bcbuf[rows, g * D_STATE:(g + 1) * D_STATE] for g in range(SSM_GROUPS)]
        c_gs = [bcbuf[rows, GROUP_W + g * D_STATE:GROUP_W + (g + 1) * D_STATE] for g in range(SSM_GROUPS)]
        cbs = [_dot_nt(c_gs[g], jnp.concatenate([b_gs[g]] * HEADS_PER_STACK, axis=0)) for g in range(SSM_GROUPS)]
        y_offs = [_dot(c_gs[g], st_ref[:, group_cols[g]].astype(BF16)) for g in range(SSM_GROUPS)]
        blocks = [qbuf[rows, j * LANES:(j + 1) * LANES] for j in range(ATTN_HEADS // 2)]
        evens = [jnp.where(low_half, blk, 0.0).astype(BF16) for blk in blocks]
        odds = [jnp.where(low_half, 0.0, blk).astype(BF16) for blk in blocks]
        q_a = jnp.concatenate([evens[0], evens[1], odds[2], odds[3]], axis=0)
        q_b = jnp.concatenate([odds[0], odds[1], evens[2], evens[3]], axis=0)
        scores = [_dot_nt(k_ref[r0:r0 + BAND, :], q_s) for q_s, k_ref in ((q_a, kb_ref), (q_b, ksb_ref))]
        yield UNITS_AFTER_FIRST_MATMULS

        if ci + 1 < n_chunks:
            packed_next = conv_and_decay(ci + 1)
        cumx_ref[...] = expanded[:, :SSM_WIDTH]
        xdt_ref[...] = xsbuf[rows, :] * expanded[:, SSM_WIDTH:]
        m_gs, bds, xws, probs = [], [], [], []
        for g in range(SSM_GROUPS):
            colb = cumx_ref[:, group_cols[g]]
            rowb = jnp.sum(jnp.where(diag_mask, colb, 0.0), axis=0, keepdims=True)
            lmat = jnp.exp2(jnp.where(causal_mask, colb - rowb, -jnp.inf))
            m_gs.append((cbs[g] * lmat).astype(BF16))
            xdt_g = xdt_ref[:, group_cols[g]]
            xdt_b = xdt_g.astype(BF16)
            bds.append(jnp.where(bd_mask, jnp.concatenate([xdt_b] * HEADS_PER_STACK, axis=0), jnp.zeros((), BF16)))
            last = colb[CHUNK - 1:CHUNK, :]
            xws.append((xdt_g * jnp.exp2(last - colb)).astype(BF16))
        for s_idx in range(2):
            s = scores[s_idx] + bias_ref[variant, s_idx]
            sink = sink_ref[s_idx][...]
            m = jnp.maximum(jnp.max(s, axis=0, keepdims=True), sink)
            p = jnp.exp2(s - m)
            denom = jnp.sum(p, axis=0, keepdims=True) + jnp.exp2(sink - m)
            probs.append((p * (1.0 / denom)).astype(BF16))
        b_ts = [b_g.T for b_g in b_gs]
        yield UNITS_AFTER_VECTOR_WORK

        y_diags = [_dot(m_gs[g], bds[g]) for g in range(SSM_GROUPS)]
        updates = [_dot(b_ts[g], xws[g]) for g in range(SSM_GROUPS)]
        vt_src, c0 = (vt_even, r0) if ci % 2 == 0 else (vt_odd, r0 - CHUNK)
        vband_t = vt_src[:, c0:c0 + BAND]
        vband_t_swapped = jnp.concatenate([vband_t[HEAD_DIM:], vband_t[:HEAD_DIM]], axis=0)
        outs_t = [_dot(vband_t, probs[0]), _dot(vband_t_swapped, probs[1])]
        yield UNITS_AFTER_SECOND_MATMULS

        for g in range(SSM_GROUPS):
            cols = group_cols[g]
            colb = cumx_ref[:, cols]
            last = colb[CHUNK - 1:CHUNK, :]
            st_ref[:, cols] = jnp.exp2(last) * st_ref[:, cols] + updates[g]
            y = y_diags[g] + y_offs[g] * jnp.exp2(colb) + d_ref[:, cols] * xsbuf[rows, cols]
            gg = y * _silu(zbuf[rows, cols])
            ms = jnp.mean(gg * gg, axis=-1, keepdims=True)
            mixed_ref[rows, cols] = (gg * lax.rsqrt(ms + RMS_EPS) * norm_w_ref[:, cols]).astype(BF16)
        o_a, o_b = [o_t.T for o_t in outs_t]
        sel = [(o_a, o_b), (o_a, o_b), (o_b, o_a), (o_b, o_a)]
        for j in range(ATTN_HEADS // 2):
            ev, od = sel[j]
            blk = jnp.where(low_half, ev[j * CHUNK:(j + 1) * CHUNK], od[j * CHUNK:(j + 1) * CHUNK])
            mixed_ref[rows, SSM_WIDTH + j * LANES:SSM_WIDTH + (j + 1) * LANES] = blk.astype(BF16)

    convbuf[CONV_HIST, :] = convbuf[CONV_PAD + tile - (CONV_W - 1):CONV_PAD + tile, :]
    for ref in (kf_ref, vf_ref, kb_ref, ksb_ref, vb_ref):
        ref[0:WINDOW, :] = ref[tile:tile + WINDOW, :]

    yield

    mix = _dot(mixed_ref[...], w_out_ref[...])
    if x_next_ref is not None:
        switch_stream()
        _prepare_tile(tile, None, mixer_w, mixer_s)
    return _layer_norm(DEEPNORM_ALPHA * x_ref[0] + mix, ln_g_ref[...], ln_b_ref[...])


FFN_DOWN_AFTER = (5, N_FFN_CHUNKS)
N_FFN_UNITS = 2 * N_FFN_CHUNKS + len(FFN_DOWN_AFTER) * (D_MODEL // FFN_CHUNK)


def _ffn_tile(h_ref, hb_ref, act_ref, w_gate_ref, w_up_ref, w_down_ref, ln_g_ref, ln_b_ref):
    out_cols = [slice(n * FFN_CHUNK, (n + 1) * FFN_CHUNK) for n in range(D_MODEL // FFN_CHUNK)]
    pre = [DEEPNORM_ALPHA * h_ref[:, cols] for cols in out_cols]
    done = 0
    for c in range(N_FFN_CHUNKS):
        cols = slice(c * FFN_CHUNK, (c + 1) * FFN_CHUNK)
        gate = _dot(hb_ref[...], w_gate_ref[:, cols])
        yield
        up = _dot(hb_ref[...], w_up_ref[:, cols])
        act_ref[:, cols] = (_silu(gate) * up).astype(BF16)
        yield
        if c + 1 in FFN_DOWN_AFTER:
            k_rows = slice(done * FFN_CHUNK, (c + 1) * FFN_CHUNK)
            done = c + 1
            for n, cols in enumerate(out_cols):
                pre[n] = pre[n] + _dot(act_ref[:, k_rows], w_down_ref[k_rows, cols])
                yield
    return _layer_norm(jnp.concatenate(pre, axis=1), ln_g_ref[...], ln_b_ref[...])


def _run(gen):
    while True:
        try:
            next(gen)
        except StopIteration as stop:
            return stop.value


def _interleave(mixer, ffn):
    remaining = N_FFN_UNITS
    while True:
        try:
            wanted = next(mixer)
        except StopIteration as stop:
            return stop.value, y
        last = wanted is None
        for _ in range(remaining if last else min(wanted, remaining)):
            next(ffn)
            remaining -= 1
        if last:
            y = _run(ffn)


N_MIXER_SCRATCH = 19

VEC_LAYOUT = (tuple(("conv_w%d" % i, CONV_DIM) for i in range(CONV_W))
              + (("conv_b", CONV_DIM), ("dt_bias", LANES), ("a_log", LANES), ("d_skip", SSM_WIDTH),
                 ("norm_w", SSM_WIDTH), ("sink_a", STACK_ROWS), ("sink_b", STACK_ROWS),
                 ("ln1_g", D_MODEL), ("ln1_b", D_MODEL), ("ln2_g", D_MODEL), ("ln2_b", D_MODEL)))
VEC_WIDTH = sum(width for _, width in VEC_LAYOUT)


def _vec_views(vec_ref):
    views, offset = {}, 0
    for name, width in VEC_LAYOUT:
        views[name] = vec_ref.at[:, offset:offset + width]
        offset += width
    return views


def _mixer_weights(w_in_ref, vec_ref, expand_ref, bias_ref, w_out_ref):
    v = _vec_views(vec_ref)
    return (w_in_ref.at[:, :COL_END], w_in_ref.at[:, COL_END:COL_END + LANES],
            tuple(v["conv_w%d" % i] for i in range(CONV_W)), v["conv_b"], v["dt_bias"], v["a_log"], v["d_skip"],
            v["norm_w"], expand_ref, bias_ref, (v["sink_a"], v["sink_b"]), w_out_ref, v["ln1_g"], v["ln1_b"])


def _ffn_weights(vec_ref, w_gate_ref, w_up_ref, w_down_ref):
    v = _vec_views(vec_ref)
    return (w_gate_ref, w_up_ref, w_down_ref, v["ln2_g"], v["ln2_b"])


def _fused_kernel(tile, n_tiles, n_total,
                  x_ref, x_next_ref, w_in_ref, vec_ref, expand_ref, bias_ref, w_out_ref, w_gate_ref, w_up_ref, w_down_ref,
                  y_ref, conv_out_ref, ssm_out_ref, k_out_ref, v_out_ref, *scratch):
    mixer_w = _mixer_weights(w_in_ref, vec_ref, expand_ref, bias_ref, w_out_ref)
    ffn_w = _ffn_weights(vec_ref, w_gate_ref, w_up_ref, w_down_ref)
    mixer_s = scratch[:N_MIXER_SCRATCH]
    h_s, hb_s, act_s = scratch[N_MIXER_SCRATCH:]
    convbuf, st_ref, kf_ref, vf_ref = mixer_s[:4]

    g = pl.program_id(0)
    t = lax.rem(jnp.minimum(g, n_total - 1), n_tiles)
    next_starts_stream = (g + 1 < n_total) & (lax.rem(g + 1, n_tiles) == 0)

    @pl.when(g == 0)
    def _():
        h_s[...] = jnp.zeros(h_s.shape, F32)
        hb_s[...] = jnp.zeros(hb_s.shape, BF16)
        convbuf[0:CONV_PAD, :] = jnp.zeros((CONV_PAD, CONV_DIM), F32)
        _prepare_tile(tile, x_ref, mixer_w, mixer_s)

    @pl.when(t == 0)
    def _():
        st_ref[...] = jnp.zeros(st_ref.shape, F32)
        for ref in mixer_s[2:7]:
            ref[0:WINDOW, :] = jnp.zeros((WINDOW, KV_W), ref.dtype)

    def switch_stream():
        hist = convbuf[CONV_HIST, :]
        conv_out_ref[0] = hist
        convbuf[CONV_HIST, :] = jnp.where(next_starts_stream, 0.0, hist)

    n_chunks = tile // CHUNK
    ffn = _ffn_tile(h_s, hb_s, act_s, *ffn_w)
    mixer = _mixer_tile(False, tile, t * n_chunks, x_ref, x_next_ref, switch_stream, mixer_w, mixer_s)
    h, y = _interleave(mixer, ffn)
    y_ref[0] = y
    h_s[...] = h
    hb_s[...] = h.astype(BF16)

    @pl.when((t == n_tiles - 1) & (g < n_total))
    def _():
        _write_stream_state(ssm_out_ref, k_out_ref, v_out_ref, st_ref, kf_ref, vf_ref)


def _mixer_kernel(tile, x_ref, conv0_ref, ssm0_ref, k0_ref, v0_ref,
                  w_in_ref, vec_ref, expand_ref, bias_ref, w_out_ref,
                  h_ref, conv_out_ref, ssm_out_ref, k_out_ref, v_out_ref, *mixer_s):
    mixer_w = _mixer_weights(w_in_ref, vec_ref, expand_ref, bias_ref, w_out_ref)
    convbuf, st_ref, kf_ref, vf_ref = mixer_s[:4]
    convbuf[0:CONV_PAD, :] = jnp.zeros((CONV_PAD, CONV_DIM), F32)
    convbuf[CONV_HIST, :] = conv0_ref[0]
    _init_stream_state(ssm0_ref, k0_ref, v0_ref, *mixer_s[1:7])
    _prepare_tile(tile, x_ref, mixer_w, mixer_s)
    h_ref[0] = _run(_mixer_tile(True, tile, 0, x_ref, None, None, mixer_w, mixer_s))
    conv_out_ref[0] = convbuf[CONV_HIST, :]
    _write_stream_state(ssm_out_ref, k_out_ref, v_out_ref, st_ref, kf_ref, vf_ref)


def _ffn_kernel(h_ref, vec_ref, w_gate_ref, w_up_ref, w_down_ref, y_ref, hb_s, act_s):
    hb_s[...] = h_ref[...].astype(BF16)
    y_ref[...] = _run(_ffn_tile(h_ref, hb_s, act_s, *_ffn_weights(vec_ref, w_gate_ref, w_up_ref, w_down_ref)))


def _const_spec(shape):
    zeros = (0,) * len(shape)
    return pl.BlockSpec(shape, lambda *_: zeros, pipeline_mode=pl.Buffered(1))


def _mixer_scratch(tile):
    return [pltpu.VMEM((CONV_PAD + tile, CONV_DIM), F32),
            pltpu.VMEM((D_STATE, SSM_WIDTH), F32),
            pltpu.VMEM((WINDOW + tile, KV_W), F32), pltpu.VMEM((WINDOW + tile, KV_W), F32),
            pltpu.VMEM((WINDOW + tile, KV_W), BF16), pltpu.VMEM((WINDOW + tile, KV_W), BF16),
            pltpu.VMEM((WINDOW + tile, KV_W), BF16), pltpu.VMEM((KV_W, WINDOW + tile), BF16),
            pltpu.VMEM((tile, D_MODEL), BF16),
            pltpu.VMEM((tile, SSM_WIDTH), F32),
            pltpu.VMEM((tile, ATTN_WIDTH), F32),
            pltpu.VMEM((tile, LANES), F32),
            pltpu.VMEM((tile, SSM_WIDTH), F32),
            pltpu.VMEM((tile, CONV_DIM - SSM_WIDTH), BF16),
            pltpu.VMEM((CHUNK, SSM_WIDTH), F32),
            pltpu.VMEM((CHUNK, SSM_WIDTH), F32),
            pltpu.VMEM((tile, D_MODEL), BF16),
            pltpu.VMEM((CHUNK, LANES), BF16),
            pltpu.VMEM((KV_W, max(tile, LANES) + CHUNK), BF16)]


_STATE_SHAPES = ((CONV_W - 1, CONV_DIM), (SSM_WIDTH, D_STATE), (KV_W, WINDOW), (KV_W, WINDOW))


def _attention_tables(attn_sinks):
    slopes = np.exp2(-8.0 * np.arange(1, ATTN_HEADS + 1, dtype=np.float64) / ATTN_HEADS)
    qi = np.arange(CHUNK)[None, :] + WINDOW
    kj = np.arange(BAND)[:, None]
    dist = np.abs(qi - kj).astype(np.float64)
    key_chunk = kj // CHUNK
    tables = np.zeros((HIST_CHUNKS + 1, 2, BAND, STACK_ROWS), np.float32)
    for variant in range(HIST_CHUNKS + 1):
        valid = key_chunk >= (HIST_CHUNKS - variant)
        for s_idx, heads in enumerate((STACK_A_HEADS, STACK_B_HEADS)):
            for i, h in enumerate(heads):
                tables[variant, s_idx, :, i * CHUNK:(i + 1) * CHUNK] = np.where(
                    valid, -slopes[h] * dist * LOG2E, -np.inf)
    order = np.array(STACK_A_HEADS + STACK_B_HEADS)
    sink_rows = jnp.repeat(attn_sinks.astype(F32)[order] * LOG2E, CHUNK).reshape(2, STACK_ROWS)
    return jnp.asarray(tables), sink_rows


def _expansion_matrix():
    e = np.zeros((EXPAND_K, 2 * SSM_WIDTH), np.float32)
    for part in range(6):
        for h in range(SSM_HEADS):
            c0 = (part // 3) * SSM_WIDTH + h * HEAD_DIM
            e[part * SSM_HEADS + h, c0:c0 + HEAD_DIM] = 1.0
    return jnp.asarray(e, BF16)


def _prompt_call(x, mixer_ops, ffn_ops):
    bsz, seq, _ = x.shape
    tile = PROMPT_TILE
    n_tiles = seq // tile
    n_total = bsz * n_tiles

    def mixer_tile_idx(g):
        gc = jnp.minimum(g, n_total - 1)
        return lax.div(gc, n_tiles), lax.rem(gc, n_tiles)

    def next_tile_idx(g):
        return mixer_tile_idx(g + 1)

    def ffn_tile_idx(g):
        gp = jnp.maximum(g - 1, 0)
        return lax.div(gp, n_tiles), lax.rem(gp, n_tiles)

    tile_spec = lambda idx: pl.BlockSpec((1, tile, D_MODEL), lambda g: (*idx(g), 0))
    stream_spec = lambda shape: pl.BlockSpec((1,) + shape, lambda g: (mixer_tile_idx(g)[0], 0, 0))
    in_specs = ([tile_spec(mixer_tile_idx), tile_spec(next_tile_idx)]
                + [_const_spec(w.shape) for w in mixer_ops + ffn_ops])
    out_specs = [tile_spec(ffn_tile_idx)] + [stream_spec(s) for s in _STATE_SHAPES]
    out_shape = ([jax.ShapeDtypeStruct((bsz, seq, D_MODEL), F32)]
                 + [jax.ShapeDtypeStruct((bsz,) + s, F32) for s in _STATE_SHAPES])
    scratch = _mixer_scratch(tile) + [pltpu.VMEM((tile, D_MODEL), F32), pltpu.VMEM((tile, D_MODEL), BF16),
                                      pltpu.VMEM((tile, D_FF), BF16)]
    return pl.pallas_call(
        functools.partial(_fused_kernel, tile, n_tiles, n_total),
        grid=(n_total + 1,), in_specs=in_specs, out_specs=out_specs, out_shape=out_shape,
        scratch_shapes=scratch, name="prompt_layer",
        compiler_params=pltpu.CompilerParams(dimension_semantics=("arbitrary",), vmem_limit_bytes=VMEM_LIMIT),
    )(x, x, *mixer_ops, *ffn_ops)


def _sample_mixer_call(x, state0, mixer_ops):
    bsz, seq, _ = x.shape
    stream_spec = lambda shape: pl.BlockSpec((1,) + shape, lambda b: (b, 0, 0))
    in_specs = ([stream_spec((seq, D_MODEL))] + [stream_spec(s) for s in _STATE_SHAPES]
                + [_const_spec(w.shape) for w in mixer_ops])
    out_specs = [stream_spec((seq, D_MODEL))] + [stream_spec(s) for s in _STATE_SHAPES]
    out_shape = ([jax.ShapeDtypeStruct((bsz, seq, D_MODEL), F32)]
                 + [jax.ShapeDtypeStruct((bsz,) + s, F32) for s in _STATE_SHAPES])
    return pl.pallas_call(
        functools.partial(_mixer_kernel, seq),
        grid=(bsz,), in_specs=in_specs, out_specs=out_specs, out_shape=out_shape,
        scratch_shapes=_mixer_scratch(seq), name="sample_mixer",
        compiler_params=pltpu.CompilerParams(dimension_semantics=("arbitrary",), vmem_limit_bytes=VMEM_LIMIT),
    )(x, *state0, *mixer_ops)


def _ffn_call(h2d, ffn_ops):
    n_tok = h2d.shape[0]
    tile = min(FFN_TILE, n_tok)
    tok_spec = pl.BlockSpec((tile, D_MODEL), lambda i: (i, 0))
    return pl.pallas_call(
        _ffn_kernel, grid=(n_tok // tile,),
        in_specs=[tok_spec] + [_const_spec(w.shape) for w in ffn_ops],
        out_specs=tok_spec, out_shape=jax.ShapeDtypeStruct((n_tok, D_MODEL), F32), name="sample_ffn",
        scratch_shapes=[pltpu.VMEM((tile, D_MODEL), BF16), pltpu.VMEM((tile, D_FF), BF16)],
        compiler_params=pltpu.CompilerParams(dimension_semantics=("arbitrary",), vmem_limit_bytes=VMEM_LIMIT),
    )(h2d, *ffn_ops)


IN_COLS = COL_END + 8
W_IN_PREP_ROWS = 256


def _w_in_kernel(w_t_ref, out_ref):
    src = {"z": 0, "xbc": SSM_WIDTH, "dt": SSM_WIDTH + CONV_DIM}
    src["q"] = src["dt"] + SSM_HEADS
    src["k"] = src["q"] + ATTN_WIDTH
    src["v"] = src["k"] + KV_W
    pieces = (("z", COL_Z, COL_XBC - COL_Z, 1.0), ("xbc", COL_XBC, COL_Q - COL_XBC, 1.0),
              ("q", COL_Q, COL_K - COL_Q, HEAD_DIM ** -0.5 * LOG2E), ("k", COL_K, KV_W, 1.0), ("v", COL_V, KV_W, 1.0))
    for name, dst, width, scale in pieces:
        for off in range(0, width, W_IN_PREP_ROWS):
            rows = min(W_IN_PREP_ROWS, width - off)
            blk = w_t_ref[src[name] + off:src[name] + off + rows, :]
            if scale != 1.0:
                blk = blk * scale
            out_ref[:, dst + off:dst + off + rows] = blk.astype(BF16).T
    w_dt = w_t_ref[src["dt"]:src["dt"] + SSM_HEADS, :]
    w_dt_rep = jnp.concatenate([w_dt] * (LANES // SSM_HEADS), axis=0)
    out_ref[:, COL_END:COL_END + LANES] = w_dt_rep.astype(BF16).T


def _prepare_w_in(w_t):
    return pl.pallas_call(
        _w_in_kernel, grid=(1,),
        in_specs=[_const_spec((IN_COLS, D_MODEL))], out_specs=pl.BlockSpec((D_MODEL, COL_END + LANES), lambda i: (0, 0)),
        out_shape=jax.ShapeDtypeStruct((D_MODEL, COL_END + LANES), BF16), name="w_in_prep",
        compiler_params=pltpu.CompilerParams(dimension_semantics=("arbitrary",), vmem_limit_bytes=VMEM_LIMIT),
    )(w_t)


def _window_to_kernel_layout(cache):
    b = cache.shape[0]
    return jnp.transpose(cache.astype(F32).reshape(b, WINDOW, KV_W), (0, 2, 1))


def _window_from_kernel_layout(win_t):
    b = win_t.shape[0]
    return jnp.transpose(win_t, (0, 2, 1)).reshape(1, b, WINDOW, KV_HEADS, HEAD_DIM)


def kernel(x_prompt, x_sample, state_conv, state_ssm, cache_k, cache_v, w_in, conv_w, conv_b, dt_bias, a_log,
           d_skip, ssm_norm_w, attn_sinks, w_out, ln1_g, ln1_b, w_gate, w_up, w_down, ln2_g, ln2_b):
    w_in_r = _prepare_w_in(jnp.transpose(w_in[0]))
    rep = LANES // SSM_HEADS
    bias_tables, sink_rows = _attention_tables(attn_sinks[0])
    vec_parts = {"conv_b": conv_b[0], "dt_bias": jnp.tile(dt_bias[0], rep), "a_log": jnp.tile(a_log[0], rep),
                 "d_skip": jnp.repeat(d_skip[0], HEAD_DIM), "norm_w": ssm_norm_w[0],
                 "sink_a": sink_rows[0], "sink_b": sink_rows[1],
                 "ln1_g": ln1_g[0], "ln1_b": ln1_b[0], "ln2_g": ln2_g[0], "ln2_b": ln2_b[0]}
    vec_parts.update({"conv_w%d" % i: conv_w[0, i] for i in range(CONV_W)})
    vecs = jnp.concatenate([vec_parts[name].astype(F32).reshape(width) for name, width in VEC_LAYOUT])
    vecs = vecs.reshape(1, VEC_WIDTH)
    mixer_ops = (w_in_r, vecs, _expansion_matrix(), bias_tables, w_out[0].astype(BF16))
    ffn_mats = (w_gate[0].astype(BF16), w_up[0].astype(BF16), w_down[0].astype(BF16))

    dbsz, dseq, _ = x_sample.shape

    y_p, conv_p, ssm_p, k_p, v_p = _prompt_call(x_prompt, mixer_ops, ffn_mats)

    sample_state = (state_conv[0].astype(F32), state_ssm[0].astype(F32).reshape(dbsz, SSM_WIDTH, D_STATE),
                    _window_to_kernel_layout(cache_k[0]), _window_to_kernel_layout(cache_v[0]))
    h_s, conv_s, ssm_s, k_s, v_s = _sample_mixer_call(x_sample, sample_state, mixer_ops)
    y_s = _ffn_call(h_s.reshape(dbsz * dseq, D_MODEL), (vecs,) + ffn_mats).reshape(dbsz, dseq, D_MODEL)

    ssm_shape = lambda a: a.reshape(1, a.shape[0], SSM_HEADS, HEAD_DIM, D_STATE)
    kv_shape = _window_from_kernel_layout
    return (y_p, y_s,
            conv_p[None], ssm_shape(ssm_p), kv_shape(k_p), kv_shape(v_p),
            conv_s[None], ssm_shape(ssm_s), kv_shape(k_s), kv_shape(v_s))
```

```python
import functools
import math

import jax
import jax.numpy as jnp
import numpy as np
from jax import lax
from jax.experimental import pallas as pl
from jax.experimental.pallas import tpu as pltpu

D_MODEL = 1024
CHUNK = 64
HEAD_DIM = 64
SSM_WIDTH = 512
SSM_HEADS = 8
SSM_GROUPS = 2
D_STATE = 128
CONV_W = 4
CONV_DIM = SSM_WIDTH + 2 * SSM_GROUPS * D_STATE
ATTN_WIDTH = 512
ATTN_HEADS = 8
KV_HEADS = 2
WINDOW = 128
D_FF = 2816
LN_EPS = 1e-5
RMS_EPS = 1e-5
DEEPNORM_ALPHA = 2.0 ** 0.25
LOG2E = math.log2(math.e)

LANES = 128
SUBLANES = 8
GROUP_W = SSM_WIDTH // SSM_GROUPS
KV_W = KV_HEADS * HEAD_DIM
BAND = WINDOW + CHUNK
HIST_CHUNKS = WINDOW // CHUNK
HEADS_PER_STACK = 4
STACK_ROWS = HEADS_PER_STACK * CHUNK
COL_Z = 0
COL_XBC = COL_Z + SSM_WIDTH
COL_Q = COL_XBC + CONV_DIM
COL_K = COL_Q + ATTN_WIDTH
COL_V = COL_K + KV_W
COL_END = COL_V + KV_W
STACK_A_HEADS = (0, 2, 5, 7)
STACK_B_HEADS = (1, 3, 4, 6)
EXPAND_K = 128
CONV_PAD = SUBLANES

PROMPT_TILE = 256
FFN_TILE = 512
FFN_CHUNK = 256
N_FFN_CHUNKS = D_FF // FFN_CHUNK
VMEM_LIMIT = 56 * 1024 * 1024
UNITS_AT_START = 1
UNITS_AFTER_FIRST_MATMULS = 4
UNITS_AFTER_VECTOR_WORK = 2
UNITS_AFTER_SECOND_MATMULS = 1

F32 = jnp.float32
BF16 = jnp.bfloat16


def _dot(a, b):
    return jnp.dot(a, b, preferred_element_type=F32)


def _dot_nt(a, b):
    return lax.dot_general(a, b, (((1,), (1,)), ((), ())), preferred_element_type=F32)


def _sigmoid(x):
    return 1.0 / (1.0 + jnp.exp2(x * -LOG2E))


def _silu(x):
    return x * _sigmoid(x)


def _layer_norm(x, g, b):
    mu = jnp.mean(x, axis=-1, keepdims=True)
    xc = x - mu
    var = jnp.mean(xc * xc, axis=-1, keepdims=True)
    return xc * lax.rsqrt(var + LN_EPS) * g + b


def _split3(x):
    hi = x.astype(BF16).astype(F32)
    r = x - hi
    mid = r.astype(BF16).astype(F32)
    lo = r - mid
    return hi, mid, lo


CONV_HIST = slice(CONV_PAD - (CONV_W - 1), CONV_PAD)


def _init_stream_state(ssm0_ref, k0_ref, v0_ref, st_ref, kf_ref, vf_ref, kb_ref, ksb_ref, vb_ref):
    st_ref[...] = ssm0_ref[0].T
    k0 = k0_ref[0].T
    v0 = v0_ref[0].T
    kf_ref[0:WINDOW, :] = k0
    vf_ref[0:WINDOW, :] = v0
    kb_ref[0:WINDOW, :] = k0.astype(BF16)
    vb_ref[0:WINDOW, :] = v0.astype(BF16)
    ksb_ref[0:WINDOW, :] = pltpu.roll(k0, HEAD_DIM, axis=1).astype(BF16)


def _write_stream_state(ssm_out_ref, k_out_ref, v_out_ref, st_ref, kf_ref, vf_ref):
    ssm_out_ref[0] = st_ref[...].T
    k_out_ref[0] = kf_ref[0:WINDOW, :].T
    v_out_ref[0] = vf_ref[0:WINDOW, :].T


def _in_projection(tile, xb_ref, w_in_ref, w_dt_ref,
                   convbuf, kf_ref, vf_ref, kb_ref, ksb_ref, vb_ref, zbuf, qbuf, dtbuf):
    xb = xb_ref[...]
    zbuf[...] = _dot(xb, w_in_ref[:, COL_Z:COL_XBC])
    convbuf[CONV_PAD:CONV_PAD + tile, :] = _dot(xb, w_in_ref[:, COL_XBC:COL_Q])
    qbuf[...] = _dot(xb, w_in_ref[:, COL_Q:COL_K])
    kv = _dot(xb, w_in_ref[:, COL_K:COL_END])
    dtbuf[...] = _dot(xb, w_dt_ref[...])
    k_new = kv[:, :KV_W]
    v_new = kv[:, KV_W:]
    kf_ref[WINDOW:WINDOW + tile, :] = k_new
    vf_ref[WINDOW:WINDOW + tile, :] = v_new
    kb_ref[WINDOW:WINDOW + tile, :] = k_new.astype(BF16)
    vb_ref[WINDOW:WINDOW + tile, :] = v_new.astype(BF16)
    ksb_ref[WINDOW:WINDOW + tile, :] = pltpu.roll(k_new, HEAD_DIM, axis=1).astype(BF16)


def _conv_and_decay(ci, conv_w_ref, conv_b_ref, dt_bias_ref, a_log_ref, convbuf, dtbuf, xsbuf, bcbuf):
    r0 = ci * CHUNK
    rows = slice(r0, r0 + CHUNK)

    window = convbuf[r0:r0 + CONV_PAD + CHUNK, :]
    conv = conv_b_ref[...] + conv_w_ref[CONV_W - 1][...] * window[CONV_PAD:]
    for back in range(1, CONV_W):
        conv = conv + conv_w_ref[CONV_W - 1 - back][...] * pltpu.roll(window, back, axis=0)[CONV_PAD:]
    xc = _silu(conv)
    xsbuf[rows, :] = xc[:, :SSM_WIDTH]
    bcbuf[rows, :] = xc[:, SSM_WIDTH:].astype(BF16)

    row_in_chunk = lax.broadcasted_iota(jnp.int32, (CHUNK, LANES), 0)
    lane = lax.broadcasted_iota(jnp.int32, (CHUNK, LANES), 1)
    dt_in = dtbuf[rows, :] + dt_bias_ref[...]
    dt = jnp.maximum(dt_in, 0.0) + jnp.log1p(jnp.exp(-jnp.abs(dt_in)))
    cum = dt * (-LOG2E * jnp.exp(a_log_ref[...]))
    step = 1
    while step < CHUNK:
        cum = cum + jnp.where(row_in_chunk >= step, pltpu.roll(cum, step, axis=0), 0.0)
        step *= 2
    packed = jnp.zeros((CHUNK, LANES), F32)
    for i, part in enumerate(_split3(cum) + _split3(dt)):
        packed = jnp.where((lane >= SSM_HEADS * i) & (lane < SSM_HEADS * (i + 1)), part, packed)
    return packed.astype(BF16)


def _prepare_tile(tile, x_ref, mixer_w, mixer_s):
    (w_in_ref, w_dt_ref, conv_w_ref, conv_b_ref, dt_bias_ref, a_log_ref) = mixer_w[:6]
    (convbuf, _, kf_ref, vf_ref, kb_ref, ksb_ref, vb_ref, _, _, zbuf, qbuf, dtbuf, xsbuf, bcbuf, _, _,
     xb_ref, packbuf, _) = mixer_s
    if x_ref is not None:
        xb_ref[...] = x_ref[0].astype(BF16)
    _in_projection(tile, xb_ref, w_in_ref, w_dt_ref,
                   convbuf, kf_ref, vf_ref, kb_ref, ksb_ref, vb_ref, zbuf, qbuf, dtbuf)
    packbuf[...] = _conv_and_decay(0, conv_w_ref, conv_b_ref, dt_bias_ref, a_log_ref, convbuf, dtbuf, xsbuf, bcbuf)


def _mixer_tile(hist_all_valid, tile, first_chunk, x_ref, x_next_ref, switch_stream, mixer_w, mixer_s):
    (_, _, conv_w_ref, conv_b_ref, dt_bias_ref, a_log_ref, d_ref, norm_w_ref,
     expand_ref, bias_ref, sink_ref, w_out_ref, ln_g_ref, ln_b_ref) = mixer_w
    (convbuf, st_ref, kf_ref, vf_ref, kb_ref, ksb_ref, vb_ref, vt_even, mixed_ref,
     zbuf, qbuf, dtbuf, xsbuf, bcbuf, cumx_ref, xdt_ref, xb_ref, packbuf, vt_odd) = mixer_s
    n_chunks = tile // CHUNK
    if x_next_ref is not None:
        xb_ref[...] = x_next_ref[0].astype(BF16)
    vt_even[...] = vb_ref[...].T
    if n_chunks > 1:
        vt_odd[...] = vb_ref[CHUNK:CHUNK + vt_odd.shape[1], :].T
    yield UNITS_AT_START

    lane = lax.broadcasted_iota(jnp.int32, (CHUNK, LANES), 1)
    low_half = lane < HEAD_DIM
    li = lax.broadcasted_iota(jnp.int32, (CHUNK, GROUP_W), 0)
    si = lax.broadcasted_iota(jnp.int32, (CHUNK, GROUP_W), 1) % CHUNK
    diag_mask = li == si
    causal_mask = si <= li
    bd_mask = (lax.broadcasted_iota(jnp.int32, (GROUP_W, GROUP_W), 0) // CHUNK
               == lax.broadcasted_iota(jnp.int32, (GROUP_W, GROUP_W), 1) // CHUNK)

    def conv_and_decay(ci):
        return _conv_and_decay(ci, conv_w_ref, conv_b_ref, dt_bias_ref, a_log_ref, convbuf, dtbuf, xsbuf, bcbuf)

    packed_next = packbuf[...]
    for ci in range(n_chunks):
        r0 = ci * CHUNK
        rows = slice(r0, r0 + CHUNK)
        packed = packed_next

        expanded = _dot(packed, expand_ref[...])
        if hist_all_valid:
            variant = HIST_CHUNKS
        else:
            variant = jnp.minimum(first_chunk + ci, HIST_CHUNKS)
        group_cols = [slice(g * GROUP_W, (g + 1) * GROUP_W) for g in range(SSM_GROUPS)]
        b_gs = [bcbuf[rows, g * D_STATE:(g + 1) * D_STATE] for g in range(SSM_GROUPS)]
        c_gs = [bcbuf[rows, GROUP_W + g * D_STATE:GROUP_W + (g + 1) * D_STATE] for g in range(SSM_GROUPS)]
        cbs = [_dot_nt(c_gs[g], jnp.concatenate([b_gs[g]] * HEADS_PER_STACK, axis=0)) for g in range(SSM_GROUPS)]
        y_offs = [_dot(c_gs[g], st_ref[:, group_cols[g]].astype(BF16)) for g in range(SSM_GROUPS)]
        yield UNITS_AFTER_FIRST_MATMULS // 2
        blocks = [qbuf[rows, j * LANES:(j + 1) * LANES] for j in range(ATTN_HEADS // 2)]
        evens = [jnp.where(low_half, blk, 0.0).astype(BF16) for blk in blocks]
        odds = [jnp.where(low_half, 0.0, blk).astype(BF16) for blk in blocks]
        q_a = jnp.concatenate([evens[0], evens[1], odds[2], odds[3]], axis=0)
        q_b = jnp.concatenate([odds[0], odds[1], evens[2], evens[3]], axis=0)
        scores = [_dot_nt(k_ref[r0:r0 + BAND, :], q_s) for q_s, k_ref in ((q_a, kb_ref), (q_b, ksb_ref))]
        yield UNITS_AFTER_FIRST_MATMULS - UNITS_AFTER_FIRST_MATMULS // 2

        if ci + 1 < n_chunks:
            packed_next = conv_and_decay(ci + 1)
        cumx_ref[...] = expanded[:, :SSM_WIDTH]
        xdt_ref[...] = xsbuf[rows, :] * expanded[:, SSM_WIDTH:]
        m_gs, bds, xws, probs = [], [], [], []
        for g in range(SSM_GROUPS):
            colb = cumx_ref[:, group_cols[g]]
            rowb = jnp.sum(jnp.where(diag_mask, colb, 0.0), axis=0, keepdims=True)
            lmat = jnp.exp2(jnp.where(causal_mask, colb - rowb, -jnp.inf))
            m_gs.append((cbs[g] * lmat).astype(BF16))
            xdt_g = xdt_ref[:, group_cols[g]]
            xdt_b = xdt_g.astype(BF16)
            bds.append(jnp.where(bd_mask, jnp.concatenate([xdt_b] * HEADS_PER_STACK, axis=0), jnp.zeros((), BF16)))
            last = colb[CHUNK - 1:CHUNK, :]
            xws.append((xdt_g * jnp.exp2(last - colb)).astype(BF16))
        for s_idx in range(2):
            s = scores[s_idx] + bias_ref[variant, s_idx]
            sink = sink_ref[s_idx][...]
            m = jnp.maximum(jnp.max(s, axis=0, keepdims=True), sink)
            p = jnp.exp2(s - m)
            denom = jnp.sum(p, axis=0, keepdims=True) + jnp.exp2(sink - m)
            probs.append((p * (1.0 / denom)).astype(BF16))
        b_ts = [b_g.T for b_g in b_gs]
        yield UNITS_AFTER_VECTOR_WORK

        y_diags = [_dot(m_gs[g], bds[g]) for g in range(SSM_GROUPS)]
        updates = [_dot(b_ts[g], xws[g]) for g in range(SSM_GROUPS)]
        vt_src, c0 = (vt_even, r0) if ci % 2 == 0 else (vt_odd, r0 - CHUNK)
        vband_t = vt_src[:, c0:c0 + BAND]
        vband_t_swapped = jnp.concatenate([vband_t[HEAD_DIM:], vband_t[:HEAD_DIM]], axis=0)
        outs_t = [_dot(vband_t, probs[0]), _dot(vband_t_swapped, probs[1])]
        yield UNITS_AFTER_SECOND_MATMULS

        for g in range(SSM_GROUPS):
            cols = group_cols[g]
            colb = cumx_ref[:, cols]
            last = colb[CHUNK - 1:CHUNK, :]
            st_ref[:, cols] = jnp.exp2(last) * st_ref[:, cols] + updates[g]
            y = y_diags[g] + y_offs[g] * jnp.exp2(colb) + d_ref[:, cols] * xsbuf[rows, cols]
            gg = y * _silu(zbuf[rows, cols])
            ms = jnp.mean(gg * gg, axis=-1, keepdims=True)
            mixed_ref[rows, cols] = (gg * lax.rsqrt(ms + RMS_EPS) * norm_w_ref[:, cols]).astype(BF16)
        o_a, o_b = [o_t.T for o_t in outs_t]
        sel = [(o_a, o_b), (o_a, o_b), (o_b, o_a), (o_b, o_a)]
        for j in range(ATTN_HEADS // 2):
            ev, od = sel[j]
            blk = jnp.where(low_half, ev[j * CHUNK:(j + 1) * CHUNK], od[j * CHUNK:(j + 1) * CHUNK])
            mixed_ref[rows, SSM_WIDTH + j * LANES:SSM_WIDTH + (j + 1) * LANES] = blk.astype(BF16)

    convbuf[CONV_HIST, :] = convbuf[CONV_PAD + tile - (CONV_W - 1):CONV_PAD + tile, :]
    for ref in (kf_ref, vf_ref, kb_ref, ksb_ref, vb_ref):
        ref[0:WINDOW, :] = ref[tile:tile + WINDOW, :]

    yield

    mix = _dot(mixed_ref[...], w_out_ref[...])
    if x_next_ref is not None:
        switch_stream()
        _prepare_tile(tile, None, mixer_w, mixer_s)
    return _layer_norm(DEEPNORM_ALPHA * x_ref[0] + mix, ln_g_ref[...], ln_b_ref[...])


FFN_DOWN_AFTER = (5, N_FFN_CHUNKS)
N_FFN_UNITS = 2 * N_FFN_CHUNKS + len(FFN_DOWN_AFTER) * (D_MODEL // FFN_CHUNK)


def _ffn_tile(h_ref, hb_ref, act_ref, w_gate_ref, w_up_ref, w_down_ref, ln_g_ref, ln_b_ref):
    out_cols = [slice(n * FFN_CHUNK, (n + 1) * FFN_CHUNK) for n in range(D_MODEL // FFN_CHUNK)]
    pre = [DEEPNORM_ALPHA * h_ref[:, cols] for cols in out_cols]
    done = 0
    for c in range(N_FFN_CHUNKS):
        cols = slice(c * FFN_CHUNK, (c + 1) * FFN_CHUNK)
        gate = _dot(hb_ref[...], w_gate_ref[:, cols])
        yield
        up = _dot(hb_ref[...], w_up_ref[:, cols])
        act_ref[:, cols] = (_silu(gate) * up).astype(BF16)
        yield
        if c + 1 in FFN_DOWN_AFTER:
            k_rows = slice(done * FFN_CHUNK, (c + 1) * FFN_CHUNK)
            done = c + 1
            for n, cols in enumerate(out_cols):
                pre[n] = pre[n] + _dot(act_ref[:, k_rows], w_down_ref[k_rows, cols])
                yield
    return _layer_norm(jnp.concatenate(pre, axis=1), ln_g_ref[...], ln_b_ref[...])


def _run(gen):
    while True:
        try:
            next(gen)
        except StopIteration as stop:
            return stop.value


def _interleave(mixer, ffn):
    remaining = N_FFN_UNITS
    while True:
        try:
            wanted = next(mixer)
        except StopIteration as stop:
            return stop.value, y
        last = wanted is None
        for _ in range(remaining if last else min(wanted, remaining)):
            next(ffn)
            remaining -= 1
        if last:
            y = _run(ffn)


N_MIXER_SCRATCH = 19

VEC_LAYOUT = (tuple(("conv_w%d" % i, CONV_DIM) for i in range(CONV_W))
              + (("conv_b", CONV_DIM), ("dt_bias", LANES), ("a_log", LANES), ("d_skip", SSM_WIDTH),
                 ("norm_w", SSM_WIDTH), ("sink_a", STACK_ROWS), ("sink_b", STACK_ROWS),
                 ("ln1_g", D_MODEL), ("ln1_b", D_MODEL), ("ln2_g", D_MODEL), ("ln2_b", D_MODEL)))
VEC_WIDTH = sum(width for _, width in VEC_LAYOUT)


def _vec_views(vec_ref):
    views, offset = {}, 0
    for name, width in VEC_LAYOUT:
        views[name] = vec_ref.at[:, offset:offset + width]
        offset += width
    return views


def _mixer_weights(w_in_ref, vec_ref, expand_ref, bias_ref, w_out_ref):
    v = _vec_views(vec_ref)
    return (w_in_ref.at[:, :COL_END], w_in_ref.at[:, COL_END:COL_END + LANES],
            tuple(v["conv_w%d" % i] for i in range(CONV_W)), v["conv_b"], v["dt_bias"], v["a_log"], v["d_skip"],
            v["norm_w"], expand_ref, bias_ref, (v["sink_a"], v["sink_b"]), w_out_ref, v["ln1_g"], v["ln1_b"])


def _ffn_weights(vec_ref, w_gate_ref, w_up_ref, w_down_ref):
    v = _vec_views(vec_ref)
    return (w_gate_ref, w_up_ref, w_down_ref, v["ln2_g"], v["ln2_b"])


def _fused_kernel(tile, n_tiles, n_total,
                  x_ref, x_next_ref, w_in_ref, vec_ref, expand_ref, bias_ref, w_out_ref, w_gate_ref, w_up_ref, w_down_ref,
                  y_ref, conv_out_ref, ssm_out_ref, k_out_ref, v_out_ref, *scratch):
    mixer_w = _mixer_weights(w_in_ref, vec_ref, expand_ref, bias_ref, w_out_ref)
    ffn_w = _ffn_weights(vec_ref, w_gate_ref, w_up_ref, w_down_ref)
    mixer_s = scratch[:N_MIXER_SCRATCH]
    h_s, hb_s, act_s = scratch[N_MIXER_SCRATCH:]
    convbuf, st_ref, kf_ref, vf_ref = mixer_s[:4]

    g = pl.program_id(0)
    t = lax.rem(jnp.minimum(g, n_total - 1), n_tiles)
    next_starts_stream = (g + 1 < n_total) & (lax.rem(g + 1, n_tiles) == 0)

    @pl.when(g == 0)
    def _():
        h_s[...] = jnp.zeros(h_s.shape, F32)
        hb_s[...] = jnp.zeros(hb_s.shape, BF16)
        convbuf[0:CONV_PAD, :] = jnp.zeros((CONV_PAD, CONV_DIM), F32)
        _prepare_tile(tile, x_ref, mixer_w, mixer_s)

    @pl.when(t == 0)
    def _():
        st_ref[...] = jnp.zeros(st_ref.shape, F32)
        for ref in mixer_s[2:7]:
            ref[0:WINDOW, :] = jnp.zeros((WINDOW, KV_W), ref.dtype)

    def switch_stream():
        hist = convbuf[CONV_HIST, :]
        conv_out_ref[0] = hist
        convbuf[CONV_HIST, :] = jnp.where(next_starts_stream, 0.0, hist)

    n_chunks = tile // CHUNK
    ffn = _ffn_tile(h_s, hb_s, act_s, *ffn_w)
    mixer = _mixer_tile(False, tile, t * n_chunks, x_ref, x_next_ref, switch_stream, mixer_w, mixer_s)
    h, y = _interleave(mixer, ffn)
    y_ref[0] = y
    h_s[...] = h
    hb_s[...] = h.astype(BF16)

    @pl.when((t == n_tiles - 1) & (g < n_total))
    def _():
        _write_stream_state(ssm_out_ref, k_out_ref, v_out_ref, st_ref, kf_ref, vf_ref)


def _mixer_kernel(tile, x_ref, conv0_ref, ssm0_ref, k0_ref, v0_ref,
                  w_in_ref, vec_ref, expand_ref, bias_ref, w_out_ref,
                  h_ref, conv_out_ref, ssm_out_ref, k_out_ref, v_out_ref, *mixer_s):
    mixer_w = _mixer_weights(w_in_ref, vec_ref, expand_ref, bias_ref, w_out_ref)
    convbuf, st_ref, kf_ref, vf_ref = mixer_s[:4]
    convbuf[0:CONV_PAD, :] = jnp.zeros((CONV_PAD, CONV_DIM), F32)
    convbuf[CONV_HIST, :] = conv0_ref[0]
    _init_stream_state(ssm0_ref, k0_ref, v0_ref, *mixer_s[1:7])
    _prepare_tile(tile, x_ref, mixer_w, mixer_s)
    h_ref[0] = _run(_mixer_tile(True, tile, 0, x_ref, None, None, mixer_w, mixer_s))
    conv_out_ref[0] = convbuf[CONV_HIST, :]
    _write_stream_state(ssm_out_ref, k_out_ref, v_out_ref, st_ref, kf_ref, vf_ref)


def _ffn_kernel(h_ref, vec_ref, w_gate_ref, w_up_ref, w_down_ref, y_ref, hb_s, act_s):
    hb_s[...] = h_ref[...].astype(BF16)
    y_ref[...] = _run(_ffn_tile(h_ref, hb_s, act_s, *_ffn_weights(vec_ref, w_gate_ref, w_up_ref, w_down_ref)))


def _const_spec(shape):
    zeros = (0,) * len(shape)
    return pl.BlockSpec(shape, lambda *_: zeros, pipeline_mode=pl.Buffered(1))


def _mixer_scratch(tile):
    return [pltpu.VMEM((CONV_PAD + tile, CONV_DIM), F32),
            pltpu.VMEM((D_STATE, SSM_WIDTH), F32),
            pltpu.VMEM((WINDOW + tile, KV_W), F32), pltpu.VMEM((WINDOW + tile, KV_W), F32),
            pltpu.VMEM((WINDOW + tile, KV_W), BF16), pltpu.VMEM((WINDOW + tile, KV_W), BF16),
            pltpu.VMEM((WINDOW + tile, KV_W), BF16), pltpu.VMEM((KV_W, WINDOW + tile), BF16),
            pltpu.VMEM((tile, D_MODEL), BF16),
            pltpu.VMEM((tile, SSM_WIDTH), F32),
            pltpu.VMEM((tile, ATTN_WIDTH), F32),
            pltpu.VMEM((tile, LANES), F32),
            pltpu.VMEM((tile, SSM_WIDTH), F32),
            pltpu.VMEM((tile, CONV_DIM - SSM_WIDTH), BF16),
            pltpu.VMEM((CHUNK, SSM_WIDTH), F32),
            pltpu.VMEM((CHUNK, SSM_WIDTH), F32),
            pltpu.VMEM((tile, D_MODEL), BF16),
            pltpu.VMEM((CHUNK, LANES), BF16),
            pltpu.VMEM((KV_W, max(tile, LANES) + CHUNK), BF16)]


_STATE_SHAPES = ((CONV_W - 1, CONV_DIM), (SSM_WIDTH, D_STATE), (KV_W, WINDOW), (KV_W, WINDOW))


def _attention_tables(attn_sinks):
    slopes = np.exp2(-8.0 * np.arange(1, ATTN_HEADS + 1, dtype=np.float64) / ATTN_HEADS)
    qi = np.arange(CHUNK)[None, :] + WINDOW
    kj = np.arange(BAND)[:, None]
    dist = np.abs(qi - kj).astype(np.float64)
    key_chunk = kj // CHUNK
    tables = np.zeros((HIST_CHUNKS + 1, 2, BAND, STACK_ROWS), np.float32)
    for variant in range(HIST_CHUNKS + 1):
        valid = key_chunk >= (HIST_CHUNKS - variant)
        for s_idx, heads in enumerate((STACK_A_HEADS, STACK_B_HEADS)):
            for i, h in enumerate(heads):
                tables[variant, s_idx, :, i * CHUNK:(i + 1) * CHUNK] = np.where(
                    valid, -slopes[h] * dist * LOG2E, -np.inf)
    order = np.array(STACK_A_HEADS + STACK_B_HEADS)
    sink_rows = jnp.repeat(attn_sinks.astype(F32)[order] * LOG2E, CHUNK).reshape(2, STACK_ROWS)
    return jnp.asarray(tables), sink_rows


def _expansion_matrix():
    e = np.zeros((EXPAND_K, 2 * SSM_WIDTH), np.float32)
    for part in range(6):
        for h in range(SSM_HEADS):
            c0 = (part // 3) * SSM_WIDTH + h * HEAD_DIM
            e[part * SSM_HEADS + h, c0:c0 + HEAD_DIM] = 1.0
    return jnp.asarray(e, BF16)


def _prompt_call(x, mixer_ops, ffn_ops):
    bsz, seq, _ = x.shape
    tile = PROMPT_TILE
    n_tiles = seq // tile
    n_total = bsz * n_tiles

    def mixer_tile_idx(g):
        gc = jnp.minimum(g, n_total - 1)
        return lax.div(gc, n_tiles), lax.rem(gc, n_tiles)

    def next_tile_idx(g):
        return mixer_tile_idx(g + 1)

    def ffn_tile_idx(g):
        gp = jnp.maximum(g - 1, 0)
        return lax.div(gp, n_tiles), lax.rem(gp, n_tiles)

    tile_spec = lambda idx: pl.BlockSpec((1, tile, D_MODEL), lambda g: (*idx(g), 0))
    stream_spec = lambda shape: pl.BlockSpec((1,) + shape, lambda g: (mixer_tile_idx(g)[0], 0, 0))
    in_specs = ([tile_spec(mixer_tile_idx), tile_spec(next_tile_idx)]
                + [_const_spec(w.shape) for w in mixer_ops + ffn_ops])
    out_specs = [tile_spec(ffn_tile_idx)] + [stream_spec(s) for s in _STATE_SHAPES]
    out_shape = ([jax.ShapeDtypeStruct((bsz, seq, D_MODEL), F32)]
                 + [jax.ShapeDtypeStruct((bsz,) + s, F32) for s in _STATE_SHAPES])
    scratch = _mixer_scratch(tile) + [pltpu.VMEM((tile, D_MODEL), F32), pltpu.VMEM((tile, D_MODEL), BF16),
                                      pltpu.VMEM((tile, D_FF), BF16)]
    return pl.pallas_call(
        functools.partial(_fused_kernel, tile, n_tiles, n_total),
        grid=(n_total + 1,), in_specs=in_specs, out_specs=out_specs, out_shape=out_shape,
        scratch_shapes=scratch, name="prompt_layer",
        compiler_params=pltpu.CompilerParams(dimension_semantics=("arbitrary",), vmem_limit_bytes=VMEM_LIMIT),
    )(x, x, *mixer_ops, *ffn_ops)


def _sample_mixer_call(x, state0, mixer_ops):
    bsz, seq, _ = x.shape
    stream_spec = lambda shape: pl.BlockSpec((1,) + shape, lambda b: (b, 0, 0))
    in_specs = ([stream_spec((seq, D_MODEL))] + [stream_spec(s) for s in _STATE_SHAPES]
                + [_const_spec(w.shape) for w in mixer_ops])
    out_specs = [stream_spec((seq, D_MODEL))] + [stream_spec(s) for s in _STATE_SHAPES]
    out_shape = ([jax.ShapeDtypeStruct((bsz, seq, D_MODEL), F32)]
                 + [jax.ShapeDtypeStruct((bsz,) + s, F32) for s in _STATE_SHAPES])
    return pl.pallas_call(
        functools.partial(_mixer_kernel, seq),
        grid=(bsz,), in_specs=in_specs, out_specs=out_specs, out_shape=out_shape,
        scratch_shapes=_mixer_scratch(seq), name="sample_mixer",
        compiler_params=pltpu.CompilerParams(dimension_semantics=("arbitrary",), vmem_limit_bytes=VMEM_LIMIT),
    )(x, *state0, *mixer_ops)


def _ffn_call(h2d, ffn_ops):
    n_tok = h2d.shape[0]
    tile = min(FFN_TILE, n_tok)
    tok_spec = pl.BlockSpec((tile, D_MODEL), lambda i: (i, 0))
    return pl.pallas_call(
        _ffn_kernel, grid=(n_tok // tile,),
        in_specs=[tok_spec] + [_const_spec(w.shape) for w in ffn_ops],
        out_specs=tok_spec, out_shape=jax.ShapeDtypeStruct((n_tok, D_MODEL), F32), name="sample_ffn",
        scratch_shapes=[pltpu.VMEM((tile, D_MODEL), BF16), pltpu.VMEM((tile, D_FF), BF16)],
        compiler_params=pltpu.CompilerParams(dimension_semantics=("arbitrary",), vmem_limit_bytes=VMEM_LIMIT),
    )(h2d, *ffn_ops)


IN_COLS = COL_END + 8
W_IN_PREP_ROWS = 256


def _w_in_kernel(w_t_ref, out_ref):
    src = {"z": 0, "xbc": SSM_WIDTH, "dt": SSM_WIDTH + CONV_DIM}
    src["q"] = src["dt"] + SSM_HEADS
    src["k"] = src["q"] + ATTN_WIDTH
    src["v"] = src["k"] + KV_W
    pieces = (("z", COL_Z, COL_XBC - COL_Z, 1.0), ("xbc", COL_XBC, COL_Q - COL_XBC, 1.0),
              ("q", COL_Q, COL_K - COL_Q, HEAD_DIM ** -0.5 * LOG2E), ("k", COL_K, KV_W, 1.0), ("v", COL_V, KV_W, 1.0))
    for name, dst, width, scale in pieces:
        for off in range(0, width, W_IN_PREP_ROWS):
            rows = min(W_IN_PREP_ROWS, width - off)
            blk = w_t_ref[src[name] + off:src[name] + off + rows, :]
            if scale != 1.0:
                blk = blk * scale
            out_ref[:, dst + off:dst + off + rows] = blk.astype(BF16).T
    w_dt = w_t_ref[src["dt"]:src["dt"] + SSM_HEADS, :]
    w_dt_rep = jnp.concatenate([w_dt] * (LANES // SSM_HEADS), axis=0)
    out_ref[:, COL_END:COL_END + LANES] = w_dt_rep.astype(BF16).T


def _prepare_w_in(w_t):
    return pl.pallas_call(
        _w_in_kernel, grid=(1,),
        in_specs=[_const_spec((IN_COLS, D_MODEL))], out_specs=pl.BlockSpec((D_MODEL, COL_END + LANES), lambda i: (0, 0)),
        out_shape=jax.ShapeDtypeStruct((D_MODEL, COL_END + LANES), BF16), name="w_in_prep",
        compiler_params=pltpu.CompilerParams(dimension_semantics=("arbitrary",), vmem_limit_bytes=VMEM_LIMIT),
    )(w_t)


def _window_to_kernel_layout(cache):
    b = cache.shape[0]
    return jnp.transpose(cache.astype(F32).reshape(b, WINDOW, KV_W), (0, 2, 1))


def _window_from_kernel_layout(win_t):
    b = win_t.shape[0]
    return jnp.transpose(win_t, (0, 2, 1)).reshape(1, b, WINDOW, KV_HEADS, HEAD_DIM)


def kernel(x_prompt, x_sample, state_conv, state_ssm, cache_k, cache_v, w_in, conv_w, conv_b, dt_bias, a_log,
           d_skip, ssm_norm_w, attn_sinks, w_out, ln1_g, ln1_b, w_gate, w_up, w_down, ln2_g, ln2_b):
    w_in_r = _prepare_w_in(jnp.transpose(w_in[0]))
    rep = LANES // SSM_HEADS
    bias_tables, sink_rows = _attention_tables(attn_sinks[0])
    vec_parts = {"conv_b": conv_b[0], "dt_bias": jnp.tile(dt_bias[0], rep), "a_log": jnp.tile(a_log[0], rep),
                 "d_skip": jnp.repeat(d_skip[0], HEAD_DIM), "norm_w": ssm_norm_w[0],
                 "sink_a": sink_rows[0], "sink_b": sink_rows[1],
                 "ln1_g": ln1_g[0], "ln1_b": ln1_b[0], "ln2_g": ln2_g[0], "ln2_b": ln2_b[0]}
    vec_parts.update({"conv_w%d" % i: conv_w[0, i] for i in range(CONV_W)})
    vecs = jnp.concatenate([vec_parts[name].astype(F32).reshape(width) for name, width in VEC_LAYOUT])
    vecs = vecs.reshape(1, VEC_WIDTH)
    mixer_ops = (w_in_r, vecs, _expansion_matrix(), bias_tables, w_out[0].astype(BF16))
    ffn_mats = (w_gate[0].astype(BF16), w_up[0].astype(BF16), w_down[0].astype(BF16))

    dbsz, dseq, _ = x_sample.shape

    y_p, conv_p, ssm_p, k_p, v_p = _prompt_call(x_prompt, mixer_ops, ffn_mats)

    sample_state = (state_conv[0].astype(F32), state_ssm[0].astype(F32).reshape(dbsz, SSM_WIDTH, D_STATE),
                    _window_to_kernel_layout(cache_k[0]), _window_to_kernel_layout(cache_v[0]))
    h_s, conv_s, ssm_s, k_s, v_s = _sample_mixer_call(x_sample, sample_state, mixer_ops)
    y_s = _ffn_call(h_s.reshape(dbsz * dseq, D_MODEL), (vecs,) + ffn_mats).reshape(dbsz, dseq, D_MODEL)

    ssm_shape = lambda a: a.reshape(1, a.shape[0], SSM_HEADS, HEAD_DIM, D_STATE)
    kv_shape = _window_from_kernel_layout
    return (y_p, y_s,
            conv_p[None], ssm_shape(ssm_p), kv_shape(k_p), kv_shape(v_p),
            conv_s[None], ssm_shape(ssm_s), kv_shape(k_s), kv_shape(v_s))
```

```python
import functools
import math

import jax
import jax.numpy as jnp
import numpy as np
from jax import lax
from jax.experimental import pallas as pl
from jax.experimental.pallas import tpu as pltpu

D_MODEL = 1024
CHUNK = 64
HEAD_DIM = 64
SSM_WIDTH = 512
SSM_HEADS = 8
SSM_GROUPS = 2
D_STATE = 128
CONV_W = 4
CONV_DIM = SSM_WIDTH + 2 * SSM_GROUPS * D_STATE
ATTN_WIDTH = 512
ATTN_HEADS = 8
KV_HEADS = 2
WINDOW = 128
D_FF = 2816
LN_EPS = 1e-5
RMS_EPS = 1e-5
DEEPNORM_ALPHA = 2.0 ** 0.25
LOG2E = math.log2(math.e)

LANES = 128
SUBLANES = 8
GROUP_W = SSM_WIDTH // SSM_GROUPS
KV_W = KV_HEADS * HEAD_DIM
BAND = WINDOW + CHUNK
HIST_CHUNKS = WINDOW // CHUNK
HEADS_PER_STACK = 4
STACK_ROWS = HEADS_PER_STACK * CHUNK
COL_Z = 0
COL_XBC = COL_Z + SSM_WIDTH
COL_Q = COL_XBC + CONV_DIM
COL_K = COL_Q + ATTN_WIDTH
COL_V = COL_K + KV_W
COL_END = COL_V + KV_W
STACK_A_HEADS = (0, 2, 5, 7)
STACK_B_HEADS = (1, 3, 4, 6)
EXPAND_K = 128
CONV_PAD = SUBLANES

PROMPT_TILE = 256
FFN_TILE = 512
FFN_CHUNK = 256
N_FFN_CHUNKS = D_FF // FFN_CHUNK
VMEM_LIMIT = 56 * 1024 * 1024
UNITS_AT_START = 1
UNITS_AFTER_FIRST_MATMULS = 3
UNITS_AFTER_VECTOR_WORK = 2
UNITS_AFTER_SECOND_MATMULS = 1

F32 = jnp.float32
BF16 = jnp.bfloat16


def _dot(a, b):
    return jnp.dot(a, b, preferred_element_type=F32)


def _dot_nt(a, b):
    return lax.dot_general(a, b, (((1,), (1,)), ((), ())), preferred_element_type=F32)


def _sigmoid(x):
    return 1.0 / (1.0 + jnp.exp2(x * -LOG2E))


def _silu(x):
    return x * _sigmoid(x)


def _layer_norm(x, g, b):
    mu = jnp.mean(x, axis=-1, keepdims=True)
    xc = x - mu
    var = jnp.mean(xc * xc, axis=-1, keepdims=True)
    return xc * lax.rsqrt(var + LN_EPS) * g + b


def _split3(x):
    hi = x.astype(BF16).astype(F32)
    r = x - hi
    mid = r.astype(BF16).astype(F32)
    lo = r - mid
    return hi, mid, lo


CONV_HIST = slice(CONV_PAD - (CONV_W - 1), CONV_PAD)


def _init_stream_state(ssm0_ref, k0_ref, v0_ref, st_ref, kf_ref, vf_ref, kb_ref, ksb_ref, vb_ref):
    st_ref[...] = ssm0_ref[0].T
    k0 = k0_ref[0].T
    v0 = v0_ref[0].T
    kf_ref[0:WINDOW, :] = k0
    vf_ref[0:WINDOW, :] = v0
    kb_ref[0:WINDOW, :] = k0.astype(BF16)
    vb_ref[0:WINDOW, :] = v0.astype(BF16)
    ksb_ref[0:WINDOW, :] = pltpu.roll(k0, HEAD_DIM, axis=1).astype(BF16)


def _write_stream_state(ssm_out_ref, k_out_ref, v_out_ref, st_ref, kf_ref, vf_ref):
    ssm_out_ref[0] = st_ref[...].T
    k_out_ref[0] = kf_ref[0:WINDOW, :].T
    v_out_ref[0] = vf_ref[0:WINDOW, :].T


def _in_projection(tile, xb_ref, w_in_ref, w_dt_ref,
                   convbuf, kf_ref, vf_ref, kb_ref, ksb_ref, vb_ref, zbuf, qbuf, dtbuf):
    xb = xb_ref[...]
    zbuf[...] = _dot(xb, w_in_ref[:, COL_Z:COL_XBC])
    convbuf[CONV_PAD:CONV_PAD + tile, :] = _dot(xb, w_in_ref[:, COL_XBC:COL_Q])
    qbuf[...] = _dot(xb, w_in_ref[:, COL_Q:COL_K])
    kv = _dot(xb, w_in_ref[:, COL_K:COL_END])
    dtbuf[...] = _dot(xb, w_dt_ref[...])
    k_new = kv[:, :KV_W]
    v_new = kv[:, KV_W:]
    kf_ref[WINDOW:WINDOW + tile, :] = k_new
    vf_ref[WINDOW:WINDOW + tile, :] = v_new
    kb_ref[WINDOW:WINDOW + tile, :] = k_new.astype(BF16)
    vb_ref[WINDOW:WINDOW + tile, :] = v_new.astype(BF16)
    ksb_ref[WINDOW:WINDOW + tile, :] = pltpu.roll(k_new, HEAD_DIM, axis=1).astype(BF16)


def _conv_and_decay(ci, conv_w_ref, conv_b_ref, dt_bias_ref, a_log_ref, convbuf, dtbuf, xsbuf, bcbuf):
    r0 = ci * CHUNK
    rows = slice(r0, r0 + CHUNK)

    window = convbuf[r0:r0 + CONV_PAD + CHUNK, :]
    conv = conv_b_ref[...] + conv_w_ref[CONV_W - 1][...] * window[CONV_PAD:]
    for back in range(1, CONV_W):
        conv = conv + conv_w_ref[CONV_W - 1 - back][...] * pltpu.roll(window, back, axis=0)[CONV_PAD:]
    xc = _silu(conv)
    xsbuf[rows, :] = xc[:, :SSM_WIDTH]
    bcbuf[rows, :] = xc[:, SSM_WIDTH:].astype(BF16)

    row_in_chunk = lax.broadcasted_iota(jnp.int32, (CHUNK, LANES), 0)
    lane = lax.broadcasted_iota(jnp.int32, (CHUNK, LANES), 1)
    dt_in = dtbuf[rows, :] + dt_bias_ref[...]
    dt = jnp.maximum(dt_in, 0.0) + jnp.log1p(jnp.exp(-jnp.abs(dt_in)))
    cum = dt * (-LOG2E * jnp.exp(a_log_ref[...]))
    step = 1
    while step < CHUNK:
        cum = cum + jnp.where(row_in_chunk >= step, pltpu.roll(cum, step, axis=0), 0.0)
        step *= 2
    packed = jnp.zeros((CHUNK, LANES), F32)
    for i, part in enumerate(_split3(cum) + _split3(dt)):
        packed = jnp.where((lane >= SSM_HEADS * i) & (lane < SSM_HEADS * (i + 1)), part, packed)
    return packed.astype(BF16)


def _prepare_tile(tile, x_ref, mixer_w, mixer_s):
    (w_in_ref, w_dt_ref, conv_w_ref, conv_b_ref, dt_bias_ref, a_log_ref) = mixer_w[:6]
    (convbuf, _, kf_ref, vf_ref, kb_ref, ksb_ref, vb_ref, _, _, zbuf, qbuf, dtbuf, xsbuf, bcbuf, _, _,
     xb_ref, packbuf, _) = mixer_s
    if x_ref is not None:
        xb_ref[...] = x_ref[0].astype(BF16)
    _in_projection(tile, xb_ref, w_in_ref, w_dt_ref,
                   convbuf, kf_ref, vf_ref, kb_ref, ksb_ref, vb_ref, zbuf, qbuf, dtbuf)
    packbuf[...] = _conv_and_decay(0, conv_w_ref, conv_b_ref, dt_bias_ref, a_log_ref, convbuf, dtbuf, xsbuf, bcbuf)


def _mixer_tile(hist_all_valid, tile, first_chunk, x_ref, x_next_ref, switch_stream, mixer_w, mixer_s):
    (_, _, conv_w_ref, conv_b_ref, dt_bias_ref, a_log_ref, d_ref, norm_w_ref,
     expand_ref, bias_ref, sink_ref, w_out_ref, ln_g_ref, ln_b_ref) = mixer_w
    (convbuf, st_ref, kf_ref, vf_ref, kb_ref, ksb_ref, vb_ref, vt_even, mixed_ref,
     zbuf, qbuf, dtbuf, xsbuf, bcbuf, cumx_ref, xdt_ref, xb_ref, packbuf, vt_odd) = mixer_s
    n_chunks = tile // CHUNK
    if x_next_ref is not None:
        xb_ref[...] = x_next_ref[0].astype(BF16)
    vt_even[...] = vb_ref[...].T
    if n_chunks > 1:
        vt_odd[...] = vb_ref[CHUNK:CHUNK + vt_odd.shape[1], :].T
    yield UNITS_AT_START

    lane = lax.broadcasted_iota(jnp.int32, (CHUNK, LANES), 1)
    low_half = lane < HEAD_DIM
    li = lax.broadcasted_iota(jnp.int32, (CHUNK, GROUP_W), 0)
    si = lax.broadcasted_iota(jnp.int32, (CHUNK, GROUP_W), 1) % CHUNK
    diag_mask = li == si
    causal_mask = si <= li
    bd_mask = (lax.broadcasted_iota(jnp.int32, (GROUP_W, GROUP_W), 0) // CHUNK
               == lax.broadcasted_iota(jnp.int32, (GROUP_W, GROUP_W), 1) // CHUNK)

    def conv_and_decay(ci):
        return _conv_and_decay(ci, conv_w_ref, conv_b_ref, dt_bias_ref, a_log_ref, convbuf, dtbuf, xsbuf, bcbuf)

    packed_next = packbuf[...]
    for ci in range(n_chunks):
        r0 = ci * CHUNK
        rows = slice(r0, r0 + CHUNK)
        packed = packed_next

        expanded = _dot(packed, expand_ref[...])
        if hist_all_valid:
            variant = HIST_CHUNKS
        else:
            variant = jnp.minimum(first_chunk + ci, HIST_CHUNKS)
        group_cols = [slice(g * GROUP_W, (g + 1) * GROUP_W) for g in range(SSM_GROUPS)]
        b_gs = [bcbuf[rows, g * D_STATE:(g + 1) * D_STATE] for g in range(SSM_GROUPS)]
        c_gs = [bcbuf[rows, GROUP_W + g * D_STATE:GROUP_W + (g + 1) * D_STATE] for g in range(SSM_GROUPS)]
        cbs = [_dot_nt(c_gs[g], jnp.concatenate([b_gs[g]] * HEADS_PER_STACK, axis=0)) for g in range(SSM_GROUPS)]
        y_offs = [_dot(c_gs[g], st_ref[:, group_cols[g]].astype(BF16)) for g in range(SSM_GROUPS)]
        blocks = [qbuf[rows, j * LANES:(j + 1) * LANES] for j in range(ATTN_HEADS // 2)]
        evens = [jnp.where(low_half, blk, 0.0).astype(BF16) for blk in blocks]
        odds = [jnp.where(low_half, 0.0, blk).astype(BF16) for blk in blocks]
        q_a = jnp.concatenate([evens[0], evens[1], odds[2], odds[3]], axis=0)
        q_b = jnp.concatenate([odds[0], odds[1], evens[2], evens[3]], axis=0)
        scores = [_dot_nt(k_ref[r0:r0 + BAND, :], q_s) for q_s, k_ref in ((q_a, kb_ref), (q_b, ksb_ref))]
        yield UNITS_AFTER_FIRST_MATMULS

        if ci + 1 < n_chunks:
            packed_next = conv_and_decay(ci + 1)
        cumx_ref[...] = expanded[:, :SSM_WIDTH]
        xdt_ref[...] = xsbuf[rows, :] * expanded[:, SSM_WIDTH:]
        m_gs, bds, xws, probs = [], [], [], []
        for g in range(SSM_GROUPS):
            colb = cumx_ref[:, group_cols[g]]
            rowb = jnp.sum(jnp.where(diag_mask, colb, 0.0), axis=0, keepdims=True)
            lmat = jnp.exp2(jnp.where(causal_mask, colb - rowb, -jnp.inf))
            m_gs.append((cbs[g] * lmat).astype(BF16))
            xdt_g = xdt_ref[:, group_cols[g]]
            xdt_b = xdt_g.astype(BF16)
            bds.append(jnp.where(bd_mask, jnp.concatenate([xdt_b] * HEADS_PER_STACK, axis=0), jnp.zeros((), BF16)))
            last = colb[CHUNK - 1:CHUNK, :]
            xws.append((xdt_g * jnp.exp2(last - colb)).astype(BF16))
        for s_idx in range(2):
            s = scores[s_idx] + bias_ref[variant, s_idx]
            sink = sink_ref[s_idx][...]
            m = jnp.maximum(jnp.max(s, axis=0, keepdims=True), sink)
            p = jnp.exp2(s - m)
            denom = jnp.sum(p, axis=0, keepdims=True) + jnp.exp2(sink - m)
            probs.append((p * (1.0 / denom)).astype(BF16))
        b_ts = [b_g.T for b_g in b_gs]
        yield UNITS_AFTER_VECTOR_WORK

        y_diags = [_dot(m_gs[g], bds[g]) for g in range(SSM_GROUPS)]
        updates = [_dot(b_ts[g], xws[g]) for g in range(SSM_GROUPS)]
        vt_src, c0 = (vt_even, r0) if ci % 2 == 0 else (vt_odd, r0 - CHUNK)
        vband_t = vt_src[:, c0:c0 + BAND]
        vband_t_swapped = jnp.concatenate([vband_t[HEAD_DIM:], vband_t[:HEAD_DIM]], axis=0)
        outs_t = [_dot(vband_t, probs[0]), _dot(vband_t_swapped, probs[1])]
        yield UNITS_AFTER_SECOND_MATMULS

        for g in range(SSM_GROUPS):
            cols = group_cols[g]
            colb = cumx_ref[:, cols]
            last = colb[CHUNK - 1:CHUNK, :]
            st_ref[:, cols] = jnp.exp2(last) * st_ref[:, cols] + updates[g]
            y = y_diags[g] + y_offs[g] * jnp.exp2(colb) + d_ref[:, cols] * xsbuf[rows, cols]
            gg = y * _silu(zbuf[rows, cols])
            ms = jnp.mean(gg * gg, axis=-1, keepdims=True)
            mixed_ref[rows, cols] = (gg * lax.rsqrt(ms + RMS_EPS) * norm_w_ref[:, cols]).astype(BF16)
        o_a, o_b = [o_t.T for o_t in outs_t]
        sel = [(o_a, o_b), (o_a, o_b), (o_b, o_a), (o_b, o_a)]
        for j in range(ATTN_HEADS // 2):
            ev, od = sel[j]
            blk = jnp.where(low_half, ev[j * CHUNK:(j + 1) * CHUNK], od[j * CHUNK:(j + 1) * CHUNK])
            mixed_ref[rows, SSM_WIDTH + j * LANES:SSM_WIDTH + (j + 1) * LANES] = blk.astype(BF16)

    convbuf[CONV_HIST, :] = convbuf[CONV_PAD + tile - (CONV_W - 1):CONV_PAD + tile, :]
    for ref in (kf_ref, vf_ref, kb_ref, ksb_ref, vb_ref):
        ref[0:WINDOW, :] = ref[tile:tile + WINDOW, :]

    yield

    mix = _dot(mixed_ref[...], w_out_ref[...])
    if x_next_ref is not None:
        switch_stream()
        _prepare_tile(tile, None, mixer_w, mixer_s)
    return _layer_norm(DEEPNORM_ALPHA * x_ref[0] + mix, ln_g_ref[...], ln_b_ref[...])


FFN_DOWN_AFTER = (5, N_FFN_CHUNKS)
N_FFN_UNITS = 2 * N_FFN_CHUNKS + len(FFN_DOWN_AFTER) * (D_MODEL // FFN_CHUNK)


def _ffn_tile(h_ref, hb_ref, act_ref, w_gate_ref, w_up_ref, w_down_ref, ln_g_ref, ln_b_ref):
    out_cols = [slice(n * FFN_CHUNK, (n + 1) * FFN_CHUNK) for n in range(D_MODEL // FFN_CHUNK)]
    pre = [DEEPNORM_ALPHA * h_ref[:, cols] for cols in out_cols]
    done = 0
    for c in range(N_FFN_CHUNKS):
        cols = slice(c * FFN_CHUNK, (c + 1) * FFN_CHUNK)
        gate = _dot(hb_ref[...], w_gate_ref[:, cols])
        yield
        up = _dot(hb_ref[...], w_up_ref[:, cols])
        act_ref[:, cols] = (_silu(gate) * up).astype(BF16)
        yield
        if c + 1 in FFN_DOWN_AFTER:
            k_rows = slice(done * FFN_CHUNK, (c + 1) * FFN_CHUNK)
            done = c + 1
            for n, cols in enumerate(out_cols):
                pre[n] = pre[n] + _dot(act_ref[:, k_rows], w_down_ref[k_rows, cols])
                yield
    return _layer_norm(jnp.concatenate(pre, axis=1), ln_g_ref[...], ln_b_ref[...])


def _run(gen):
    while True:
        try:
            next(gen)
        except StopIteration as stop:
            return stop.value


def _interleave(mixer, ffn):
    remaining = N_FFN_UNITS
    while True:
        try:
            wanted = next(mixer)
        except StopIteration as stop:
            return stop.value, y
        last = wanted is None
        for _ in range(remaining if last else min(wanted, remaining)):
            next(ffn)
            remaining -= 1
        if last:
            y = _run(ffn)


N_MIXER_SCRATCH = 19

VEC_LAYOUT = (tuple(("conv_w%d" % i, CONV_DIM) for i in range(CONV_W))
              + (("conv_b", CONV_DIM), ("dt_bias", LANES), ("a_log", LANES), ("d_skip", SSM_WIDTH),
                 ("norm_w", SSM_WIDTH), ("sink_a", STACK_ROWS), ("sink_b", STACK_ROWS),
                 ("ln1_g", D_MODEL), ("ln1_b", D_MODEL), ("ln2_g", D_MODEL), ("ln2_b", D_MODEL)))
VEC_WIDTH = sum(width for _, width in VEC_LAYOUT)


def _vec_views(vec_ref):
    views, offset = {}, 0
    for name, width in VEC_LAYOUT:
        views[name] = vec_ref.at[:, offset:offset + width]
        offset += width
    return views


def _mixer_weights(w_in_ref, vec_ref, expand_ref, bias_ref, w_out_ref):
    v = _vec_views(vec_ref)
    return (w_in_ref.at[:, :COL_END], w_in_ref.at[:, COL_END:COL_END + LANES],
            tuple(v["conv_w%d" % i] for i in range(CONV_W)), v["conv_b"], v["dt_bias"], v["a_log"], v["d_skip"],
            v["norm_w"], expand_ref, bias_ref, (v["sink_a"], v["sink_b"]), w_out_ref, v["ln1_g"], v["ln1_b"])


def _ffn_weights(vec_ref, w_gate_ref, w_up_ref, w_down_ref):
    v = _vec_views(vec_ref)
    return (w_gate_ref, w_up_ref, w_down_ref, v["ln2_g"], v["ln2_b"])


def _fused_kernel(tile, n_tiles, n_total,
                  x_ref, x_next_ref, w_in_ref, vec_ref, expand_ref, bias_ref, w_out_ref, w_gate_ref, w_up_ref, w_down_ref,
                  y_ref, conv_out_ref, ssm_out_ref, k_out_ref, v_out_ref, *scratch):
    mixer_w = _mixer_weights(w_in_ref, vec_ref, expand_ref, bias_ref, w_out_ref)
    ffn_w = _ffn_weights(vec_ref, w_gate_ref, w_up_ref, w_down_ref)
    mixer_s = scratch[:N_MIXER_SCRATCH]
    h_s, hb_s, act_s = scratch[N_MIXER_SCRATCH:]
    convbuf, st_ref, kf_ref, vf_ref = mixer_s[:4]

    g = pl.program_id(0)
    t = lax.rem(jnp.minimum(g, n_total - 1), n_tiles)
    next_starts_stream = (g + 1 < n_total) & (lax.rem(g + 1, n_tiles) == 0)

    @pl.when(g == 0)
    def _():
        h_s[...] = jnp.zeros(h_s.shape, F32)
        hb_s[...] = jnp.zeros(hb_s.shape, BF16)
        convbuf[0:CONV_PAD, :] = jnp.zeros((CONV_PAD, CONV_DIM), F32)
        _prepare_tile(tile, x_ref, mixer_w, mixer_s)

    @pl.when(t == 0)
    def _():
        st_ref[...] = jnp.zeros(st_ref.shape, F32)
        for ref in mixer_s[2:7]:
            ref[0:WINDOW, :] = jnp.zeros((WINDOW, KV_W), ref.dtype)

    def switch_stream():
        hist = convbuf[CONV_HIST, :]
        conv_out_ref[0] = hist
        convbuf[CONV_HIST, :] = jnp.where(next_starts_stream, 0.0, hist)

    n_chunks = tile // CHUNK
    ffn = _ffn_tile(h_s, hb_s, act_s, *ffn_w)
    mixer = _mixer_tile(False, tile, t * n_chunks, x_ref, x_next_ref, switch_stream, mixer_w, mixer_s)
    h, y = _interleave(mixer, ffn)
    y_ref[0] = y
    h_s[...] = h
    hb_s[...] = h.astype(BF16)

    @pl.when((t == n_tiles - 1) & (g < n_total))
    def _():
        _write_stream_state(ssm_out_ref, k_out_ref, v_out_ref, st_ref, kf_ref, vf_ref)


def _mixer_kernel(tile, x_ref, conv0_ref, ssm0_ref, k0_ref, v0_ref,
                  w_in_ref, vec_ref, expand_ref, bias_ref, w_out_ref,
                  h_ref, conv_out_ref, ssm_out_ref, k_out_ref, v_out_ref, *mixer_s):
    mixer_w = _mixer_weights(w_in_ref, vec_ref, expand_ref, bias_ref, w_out_ref)
    convbuf, st_ref, kf_ref, vf_ref = mixer_s[:4]
    convbuf[0:CONV_PAD, :] = jnp.zeros((CONV_PAD, CONV_DIM), F32)
    convbuf[CONV_HIST, :] = conv0_ref[0]
    _init_stream_state(ssm0_ref, k0_ref, v0_ref, *mixer_s[1:7])
    _prepare_tile(tile, x_ref, mixer_w, mixer_s)
    h_ref[0] = _run(_mixer_tile(True, tile, 0, x_ref, None, None, mixer_w, mixer_s))
    conv_out_ref[0] = convbuf[CONV_HIST, :]
    _write_stream_state(ssm_out_ref, k_out_ref, v_out_ref, st_ref, kf_ref, vf_ref)


def _ffn_kernel(h_ref, vec_ref, w_gate_ref, w_up_ref, w_down_ref, y_ref, hb_s, act_s):
    hb_s[...] = h_ref[...].astype(BF16)
    y_ref[...] = _run(_ffn_tile(h_ref, hb_s, act_s, *_ffn_weights(vec_ref, w_gate_ref, w_up_ref, w_down_ref)))


def _const_spec(shape):
    zeros = (0,) * len(shape)
    return pl.BlockSpec(shape, lambda *_: zeros, pipeline_mode=pl.Buffered(1))


def _mixer_scratch(tile):
    return [pltpu.VMEM((CONV_PAD + tile, CONV_DIM), F32),
            pltpu.VMEM((D_STATE, SSM_WIDTH), F32),
            pltpu.VMEM((WINDOW + tile, KV_W), F32), pltpu.VMEM((WINDOW + tile, KV_W), F32),
            pltpu.VMEM((WINDOW + tile, KV_W), BF16), pltpu.VMEM((WINDOW + tile, KV_W), BF16),
            pltpu.VMEM((WINDOW + tile, KV_W), BF16), pltpu.VMEM((KV_W, WINDOW + tile), BF16),
            pltpu.VMEM((tile, D_MODEL), BF16),
            pltpu.VMEM((tile, SSM_WIDTH), F32),
            pltpu.VMEM((tile, ATTN_WIDTH), F32),
            pltpu.VMEM((tile, LANES), F32),
            pltpu.VMEM((tile, SSM_WIDTH), F32),
            pltpu.VMEM((tile, CONV_DIM - SSM_WIDTH), BF16),
            pltpu.VMEM((CHUNK, SSM_WIDTH), F32),
            pltpu.VMEM((CHUNK, SSM_WIDTH), F32),
            pltpu.VMEM((tile, D_MODEL), BF16),
            pltpu.VMEM((CHUNK, LANES), BF16),
            pltpu.VMEM((KV_W, max(tile, LANES) + CHUNK), BF16)]


_STATE_SHAPES = ((CONV_W - 1, CONV_DIM), (SSM_WIDTH, D_STATE), (KV_W, WINDOW), (KV_W, WINDOW))


def _attention_tables(attn_sinks):
    slopes = np.exp2(-8.0 * np.arange(1, ATTN_HEADS + 1, dtype=np.float64) / ATTN_HEADS)
    qi = np.arange(CHUNK)[None, :] + WINDOW
    kj = np.arange(BAND)[:, None]
    dist = np.abs(qi - kj).astype(np.float64)
    key_chunk = kj // CHUNK
    tables = np.zeros((HIST_CHUNKS + 1, 2, BAND, STACK_ROWS), np.float32)
    for variant in range(HIST_CHUNKS + 1):
        valid = key_chunk >= (HIST_CHUNKS - variant)
        for s_idx, heads in enumerate((STACK_A_HEADS, STACK_B_HEADS)):
            for i, h in enumerate(heads):
                tables[variant, s_idx, :, i * CHUNK:(i + 1) * CHUNK] = np.where(
                    valid, -slopes[h] * dist * LOG2E, -np.inf)
    order = np.array(STACK_A_HEADS + STACK_B_HEADS)
    sink_rows = jnp.repeat(attn_sinks.astype(F32)[order] * LOG2E, CHUNK).reshape(2, STACK_ROWS)
    return jnp.asarray(tables), sink_rows


def _expansion_matrix():
    e = np.zeros((EXPAND_K, 2 * SSM_WIDTH), np.float32)
    for part in range(6):
        for h in range(SSM_HEADS):
            c0 = (part // 3) * SSM_WIDTH + h * HEAD_DIM
            e[part * SSM_HEADS + h, c0:c0 + HEAD_DIM] = 1.0
    return jnp.asarray(e, BF16)


def _prompt_call(x, mixer_ops, ffn_ops):
    bsz, seq, _ = x.shape
    tile = PROMPT_TILE
    n_tiles = seq // tile
    n_total = bsz * n_tiles

    def mixer_tile_idx(g):
        gc = jnp.minimum(g, n_total - 1)
        return lax.div(gc, n_tiles), lax.rem(gc, n_tiles)

    def next_tile_idx(g):
        return mixer_tile_idx(g + 1)

    def ffn_tile_idx(g):
        gp = jnp.maximum(g - 1, 0)
        return lax.div(gp, n_tiles), lax.rem(gp, n_tiles)

    tile_spec = lambda idx: pl.BlockSpec((1, tile, D_MODEL), lambda g: (*idx(g), 0))
    stream_spec = lambda shape: pl.BlockSpec((1,) + shape, lambda g: (mixer_tile_idx(g)[0], 0, 0))
    in_specs = ([tile_spec(mixer_tile_idx), tile_spec(next_tile_idx)]
                + [_const_spec(w.shape) for w in mixer_ops + ffn_ops])
    out_specs = [tile_spec(ffn_tile_idx)] + [stream_spec(s) for s in _STATE_SHAPES]
    out_shape = ([jax.ShapeDtypeStruct((bsz, seq, D_MODEL), F32)]
                 + [jax.ShapeDtypeStruct((bsz,) + s, F32) for s in _STATE_SHAPES])
    scratch = _mixer_scratch(tile) + [pltpu.VMEM((tile, D_MODEL), F32), pltpu.VMEM((tile, D_MODEL), BF16),
                                      pltpu.VMEM((tile, D_FF), BF16)]
    return pl.pallas_call(
        functools.partial(_fused_kernel, tile, n_tiles, n_total),
        grid=(n_total + 1,), in_specs=in_specs, out_specs=out_specs, out_shape=out_shape,
        scratch_shapes=scratch, name="prompt_layer",
        compiler_params=pltpu.CompilerParams(dimension_semantics=("arbitrary",), vmem_limit_bytes=VMEM_LIMIT),
    )(x, x, *mixer_ops, *ffn_ops)


def _sample_mixer_call(x, state0, mixer_ops):
    bsz, seq, _ = x.shape
    stream_spec = lambda shape: pl.BlockSpec((1,) + shape, lambda b: (b, 0, 0))
    in_specs = ([stream_spec((seq, D_MODEL))] + [stream_spec(s) for s in _STATE_SHAPES]
                + [_const_spec(w.shape) for w in mixer_ops])
    out_specs = [stream_spec((seq, D_MODEL))] + [stream_spec(s) for s in _STATE_SHAPES]
    out_shape = ([jax.ShapeDtypeStruct((bsz, seq, D_MODEL), F32)]
                 + [jax.ShapeDtypeStruct((bsz,) + s, F32) for s in _STATE_SHAPES])
    return pl.pallas_call(
        functools.partial(_mixer_kernel, seq),
        grid=(bsz,), in_specs=in_specs, out_specs=out_specs, out_shape=out_shape,
        scratch_shapes=_mixer_scratch(seq), name="sample_mixer",
        compiler_params=pltpu.CompilerParams(dimension_semantics=("arbitrary",), vmem_limit_bytes=VMEM_LIMIT),
    )(x, *state0, *mixer_ops)


def _ffn_call(h2d, ffn_ops):
    n_tok = h2d.shape[0]
    tile = min(FFN_TILE, n_tok)
    tok_spec = pl.BlockSpec((tile, D_MODEL), lambda i: (i, 0))
    return pl.pallas_call(
        _ffn_kernel, grid=(n_tok // tile,),
        in_specs=[tok_spec] + [_const_spec(w.shape) for w in ffn_ops],
        out_specs=tok_spec, out_shape=jax.ShapeDtypeStruct((n_tok, D_MODEL), F32), name="sample_ffn",
        scratch_shapes=[pltpu.VMEM((tile, D_MODEL), BF16), pltpu.VMEM((tile, D_FF), BF16)],
        compiler_params=pltpu.CompilerParams(dimension_semantics=("arbitrary",), vmem_limit_bytes=VMEM_LIMIT),
    )(h2d, *ffn_ops)


IN_COLS = COL_END + 8
W_IN_PREP_ROWS = 256


def _w_in_kernel(w_t_ref, out_ref):
    src = {"z": 0, "xbc": SSM_WIDTH, "dt": SSM_WIDTH + CONV_DIM}
    src["q"] = src["dt"] + SSM_HEADS
    src["k"] = src["q"] + ATTN_WIDTH
    src["v"] = src["k"] + KV_W
    pieces = (("z", COL_Z, COL_XBC - COL_Z, 1.0), ("xbc", COL_XBC, COL_Q - COL_XBC, 1.0),
              ("q", COL_Q, COL_K - COL_Q, HEAD_DIM ** -0.5 * LOG2E), ("k", COL_K, KV_W, 1.0), ("v", COL_V, KV_W, 1.0))
    for name, dst, width, scale in pieces:
        for off in range(0, width, W_IN_PREP_ROWS):
            rows = min(W_IN_PREP_ROWS, width - off)
            blk = w_t_ref[src[name] + off:src[name] + off + rows, :]
            if scale != 1.0:
                blk = blk * scale
            out_ref[:, dst + off:dst + off + rows] = blk.astype(BF16).T
    w_dt = w_t_ref[src["dt"]:src["dt"] + SSM_HEADS, :]
    w_dt_rep = jnp.concatenate([w_dt] * (LANES // SSM_HEADS), axis=0)
    out_ref[:, COL_END:COL_END + LANES] = w_dt_rep.astype(BF16).T


def _prepare_w_in(w_t):
    return pl.pallas_call(
        _w_in_kernel, grid=(1,),
        in_specs=[_const_spec((IN_COLS, D_MODEL))], out_specs=pl.BlockSpec((D_MODEL, COL_END + LANES), lambda i: (0, 0)),
        out_shape=jax.ShapeDtypeStruct((D_MODEL, COL_END + LANES), BF16), name="w_in_prep",
        compiler_params=pltpu.CompilerParams(dimension_semantics=("arbitrary",), vmem_limit_bytes=VMEM_LIMIT),
    )(w_t)


def _window_to_kernel_layout(cache):
    b = cache.shape[0]
    return jnp.transpose(cache.astype(F32).reshape(b, WINDOW, KV_W), (0, 2, 1))


def _window_from_kernel_layout(win_t):
    b = win_t.shape[0]
    return jnp.transpose(win_t, (0, 2, 1)).reshape(1, b, WINDOW, KV_HEADS, HEAD_DIM)


def kernel(x_prompt, x_sample, state_conv, state_ssm, cache_k, cache_v, w_in, conv_w, conv_b, dt_bias, a_log,
           d_skip, ssm_norm_w, attn_sinks, w_out, ln1_g, ln1_b, w_gate, w_up, w_down, ln2_g, ln2_b):
    w_in_r = _prepare_w_in(jnp.transpose(w_in[0]))
    rep = LANES // SSM_HEADS
    bias_tables, sink_rows = _attention_tables(attn_sinks[0])
    vec_parts = {"conv_b": conv_b[0], "dt_bias": jnp.tile(dt_bias[0], rep), "a_log": jnp.tile(a_log[0], rep),
                 "d_skip": jnp.repeat(d_skip[0], HEAD_DIM), "norm_w": ssm_norm_w[0],
                 "sink_a": sink_rows[0], "sink_b": sink_rows[1],
                 "ln1_g": ln1_g[0], "ln1_b": ln1_b[0], "ln2_g": ln2_g[0], "ln2_b": ln2_b[0]}
    vec_parts.update({"conv_w%d" % i: conv_w[0, i] for i in range(CONV_W)})
    vecs = jnp.concatenate([vec_parts[name].astype(F32).reshape(width) for name, width in VEC_LAYOUT])
    vecs = vecs.reshape(1, VEC_WIDTH)
    mixer_ops = (w_in_r, vecs, _expansion_matrix(), bias_tables, w_out[0].astype(BF16))
    ffn_mats = (w_gate[0].astype(BF16), w_up[0].astype(BF16), w_down[0].astype(BF16))

    dbsz, dseq, _ = x_sample.shape

    y_p, conv_p, ssm_p, k_p, v_p = _prompt_call(x_prompt, mixer_ops, ffn_mats)

    sample_state = (state_conv[0].astype(F32), state_ssm[0].astype(F32).reshape(dbsz, SSM_WIDTH, D_STATE),
                    _window_to_kernel_layout(cache_k[0]), _window_to_kernel_layout(cache_v[0]))
    h_s, conv_s, ssm_s, k_s, v_s = _sample_mixer_call(x_sample, sample_state, mixer_ops)
    y_s = _ffn_call(h_s.reshape(dbsz * dseq, D_MODEL), (vecs,) + ffn_mats).reshape(dbsz, dseq, D_MODEL)

    ssm_shape = lambda a: a.reshape(1, a.shape[0], SSM_HEADS, HEAD_DIM, D_STATE)
    kv_shape = _window_from_kernel_layout
    return (y_p, y_s,
            conv_p[None], ssm_shape(ssm_p), kv_shape(k_p), kv_shape(v_p),
            conv_s[None], ssm_shape(ssm_s), kv_shape(k_s), kv_shape(v_s))
```

```python
import functools
import math

import jax
import jax.numpy as jnp
import numpy as np
from jax import lax
from jax.experimental import pallas as pl
from jax.experimental.pallas import tpu as pltpu

D_MODEL = 1024
CHUNK = 64
HEAD_DIM = 64
SSM_WIDTH = 512
SSM_HEADS = 8
SSM_GROUPS = 2
D_STATE = 128
CONV_W = 4
CONV_DIM = SSM_WIDTH + 2 * SSM_GROUPS * D_STATE
ATTN_WIDTH = 512
ATTN_HEADS = 8
KV_HEADS = 2
WINDOW = 128
D_FF = 2816
LN_EPS = 1e-5
RMS_EPS = 1e-5
DEEPNORM_ALPHA = 2.0 ** 0.25
LOG2E = math.log2(math.e)

LANES = 128
SUBLANES = 8
GROUP_W = SSM_WIDTH // SSM_GROUPS
KV_W = KV_HEADS * HEAD_DIM
BAND = WINDOW + CHUNK
HIST_CHUNKS = WINDOW // CHUNK
HEADS_PER_STACK = 4
STACK_ROWS = HEADS_PER_STACK * CHUNK
COL_Z = 0
COL_XBC = COL_Z + SSM_WIDTH
COL_Q = COL_XBC + CONV_DIM
COL_K = COL_Q + ATTN_WIDTH
COL_V = COL_K + KV_W
COL_END = COL_V + KV_W
STACK_A_HEADS = (0, 2, 5, 7)
STACK_B_HEADS = (1, 3, 4, 6)
EXPAND_K = 128
CONV_PAD = SUBLANES

PROMPT_TILE = 256
FFN_TILE = 512
FFN_CHUNK = 256
N_FFN_CHUNKS = D_FF // FFN_CHUNK
VMEM_LIMIT = 56 * 1024 * 1024
UNITS_AT_START = 1
UNITS_AFTER_FIRST_MATMULS = 3
UNITS_AFTER_VECTOR_WORK = 2
UNITS_AFTER_SECOND_MATMULS = 1

F32 = jnp.float32
BF16 = jnp.bfloat16


def _dot(a, b):
    return jnp.dot(a, b, preferred_element_type=F32)


def _dot_nt(a, b):
    return lax.dot_general(a, b, (((1,), (1,)), ((), ())), preferred_element_type=F32)


def _sigmoid(x):
    return 1.0 / (1.0 + jnp.exp2(x * -LOG2E))


def _silu(x):
    return x * _sigmoid(x)


def _layer_norm(x, g, b):
    mu = jnp.mean(x, axis=-1, keepdims=True)
    xc = x - mu
    var = jnp.mean(xc * xc, axis=-1, keepdims=True)
    return xc * lax.rsqrt(var + LN_EPS) * g + b


def _split3(x):
    hi = x.astype(BF16).astype(F32)
    r = x - hi
    mid = r.astype(BF16).astype(F32)
    lo = r - mid
    return hi, mid, lo


CONV_HIST = slice(CONV_PAD - (CONV_W - 1), CONV_PAD)


def _init_stream_state(ssm0_ref, k0_ref, v0_ref, st_ref, kf_ref, vf_ref, kb_ref, ksb_ref, vb_ref):
    st_ref[...] = ssm0_ref[0].T
    k0 = k0_ref[0].T
    v0 = v0_ref[0].T
    kf_ref[0:WINDOW, :] = k0
    vf_ref[0:WINDOW, :] = v0
    kb_ref[0:WINDOW, :] = k0.astype(BF16)
    vb_ref[0:WINDOW, :] = v0.astype(BF16)
    ksb_ref[0:WINDOW, :] = pltpu.roll(k0, HEAD_DIM, axis=1).astype(BF16)


def _write_stream_state(ssm_out_ref, k_out_ref, v_out_ref, st_ref, kf_ref, vf_ref):
    ssm_out_ref[0] = st_ref[...].T
    k_out_ref[0] = kf_ref[0:WINDOW, :].T
    v_out_ref[0] = vf_ref[0:WINDOW, :].T


def _in_projection(tile, xb_ref, w_in_ref, w_dt_ref,
                   convbuf, kf_ref, vf_ref, kb_ref, ksb_ref, vb_ref, zbuf, qbuf, dtbuf):
    xb = xb_ref[...]
    zbuf[...] = _dot(xb, w_in_ref[:, COL_Z:COL_XBC])
    convbuf[CONV_PAD:CONV_PAD + tile, :] = _dot(xb, w_in_ref[:, COL_XBC:COL_Q])
    qbuf[...] = _dot(xb, w_in_ref[:, COL_Q:COL_K])
    kv = _dot(xb, w_in_ref[:, COL_K:COL_END])
    dtbuf[...] = _dot(xb, w_dt_ref[...])
    k_new = kv[:, :KV_W]
    v_new = kv[:, KV_W:]
    kf_ref[WINDOW:WINDOW + tile, :] = k_new
    vf_ref[WINDOW:WINDOW + tile, :] = v_new
    kb_ref[WINDOW:WINDOW + tile, :] = k_new.astype(BF16)
    vb_ref[WINDOW:WINDOW + tile, :] = v_new.astype(BF16)
    ksb_ref[WINDOW:WINDOW + tile, :] = pltpu.roll(k_new, HEAD_DIM, axis=1).astype(BF16)


def _conv_and_decay(ci, conv_w_ref, conv_b_ref, dt_bias_ref, a_log_ref, convbuf, dtbuf, xsbuf, bcbuf):
    r0 = ci * CHUNK
    rows = slice(r0, r0 + CHUNK)

    window = convbuf[r0:r0 + CONV_PAD + CHUNK, :]
    conv = conv_b_ref[...] + conv_w_ref[CONV_W - 1][...] * window[CONV_PAD:]
    for back in range(1, CONV_W):
        conv = conv + conv_w_ref[CONV_W - 1 - back][...] * pltpu.roll(window, back, axis=0)[CONV_PAD:]
    xc = _silu(conv)
    xsbuf[rows, :] = xc[:, :SSM_WIDTH]
    bcbuf[rows, :] = xc[:, SSM_WIDTH:].astype(BF16)

    row_in_chunk = lax.broadcasted_iota(jnp.int32, (CHUNK, LANES), 0)
    lane = lax.broadcasted_iota(jnp.int32, (CHUNK, LANES), 1)
    dt_in = dtbuf[rows, :] + dt_bias_ref[...]
    dt = jnp.maximum(dt_in, 0.0) + jnp.log1p(jnp.exp(-jnp.abs(dt_in)))
    cum = dt * (-LOG2E * jnp.exp(a_log_ref[...]))
    step = 1
    while step < CHUNK:
        cum = cum + jnp.where(row_in_chunk >= step, pltpu.roll(cum, step, axis=0), 0.0)
        step *= 2
    packed = jnp.zeros((CHUNK, LANES), F32)
    for i, part in enumerate(_split3(cum) + _split3(dt)):
        packed = jnp.where((lane >= SSM_HEADS * i) & (lane < SSM_HEADS * (i + 1)), part, packed)
    return packed.astype(BF16)


def _prepare_tile(tile, x_ref, mixer_w, mixer_s):
    (w_in_ref, w_dt_ref, conv_w_ref, conv_b_ref, dt_bias_ref, a_log_ref) = mixer_w[:6]
    (convbuf, _, kf_ref, vf_ref, kb_ref, ksb_ref, vb_ref, _, _, zbuf, qbuf, dtbuf, xsbuf, bcbuf, _, _,
     xb_ref, packbuf, _) = mixer_s
    if x_ref is not None:
        xb_ref[...] = x_ref[0].astype(BF16)
    _in_projection(tile, xb_ref, w_in_ref, w_dt_ref,
                   convbuf, kf_ref, vf_ref, kb_ref, ksb_ref, vb_ref, zbuf, qbuf, dtbuf)
    packbuf[...] = _conv_and_decay(0, conv_w_ref, conv_b_ref, dt_bias_ref, a_log_ref, convbuf, dtbuf, xsbuf, bcbuf)


def _mixer_tile(hist_all_valid, tile, first_chunk, x_ref, x_next_ref, switch_stream, mixer_w, mixer_s):
    (_, _, conv_w_ref, conv_b_ref, dt_bias_ref, a_log_ref, d_ref, norm_w_ref,
     expand_ref, bias_ref, sink_ref, w_out_ref, ln_g_ref, ln_b_ref) = mixer_w
    (convbuf, st_ref, kf_ref, vf_ref, kb_ref, ksb_ref, vb_ref, vt_even, mixed_ref,
     zbuf, qbuf, dtbuf, xsbuf, bcbuf, cumx_ref, xdt_ref, xb_ref, packbuf, vt_odd) = mixer_s
    n_chunks = tile // CHUNK
    if x_next_ref is not None:
        xb_ref[...] = x_next_ref[0].astype(BF16)
    vt_even[...] = vb_ref[...].T
    if n_chunks > 1:
        vt_odd[...] = vb_ref[CHUNK:CHUNK + vt_odd.shape[1], :].T
    yield UNITS_AT_START

    lane = lax.broadcasted_iota(jnp.int32, (CHUNK, LANES), 1)
    low_half = lane < HEAD_DIM
    li = lax.broadcasted_iota(jnp.int32, (CHUNK, GROUP_W), 0)
    si = lax.broadcasted_iota(jnp.int32, (CHUNK, GROUP_W), 1) % CHUNK
    diag_mask = li == si
    causal_mask = si <= li
    bd_mask = (lax.broadcasted_iota(jnp.int32, (GROUP_W, GROUP_W), 0) // CHUNK
               == lax.broadcasted_iota(jnp.int32, (GROUP_W, GROUP_W), 1) // CHUNK)

    def conv_and_decay(ci):
        return _conv_and_decay(ci, conv_w_ref, conv_b_ref, dt_bias_ref, a_log_ref, convbuf, dtbuf, xsbuf, bcbuf)

    packed_next = packbuf[...]
    for ci in range(n_chunks):
        r0 = ci * CHUNK
        rows = slice(r0, r0 + CHUNK)
        packed = packed_next

        expanded = _dot(packed, expand_ref[...])
        if hist_all_valid:
            variant = HIST_CHUNKS
        else:
            variant = jnp.minimum(first_chunk + ci, HIST_CHUNKS)
        group_cols = [slice(g * GROUP_W, (g + 1) * GROUP_W) for g in range(SSM_GROUPS)]
        b_gs = [bcbuf[rows, g * D_STATE:(g + 1) * D_STATE] for g in range(SSM_GROUPS)]
        c_gs = [bcbuf[rows, GROUP_W + g * D_STATE:GROUP_W + (g + 1) * D_STATE] for g in range(SSM_GROUPS)]
        cbs = [_dot_nt(c_gs[g], jnp.concatenate([b_gs[g]] * HEADS_PER_STACK, axis=0)) for g in range(SSM_GROUPS)]
        y_offs = [_dot(c_gs[g], st_ref[:, group_cols[g]].astype(BF16)) for g in range(SSM_GROUPS)]
        blocks = [qbuf[rows, j * LANES:(j + 1) * LANES] for j in range(ATTN_HEADS // 2)]
        evens = [jnp.where(low_half, blk, 0.0).astype(BF16) for blk in blocks]
        odds = [jnp.where(low_half, 0.0, blk).astype(BF16) for blk in blocks]
        q_a = jnp.concatenate([evens[0], evens[1], odds[2], odds[3]], axis=0)
        q_b = jnp.concatenate([odds[0], odds[1], evens[2], evens[3]], axis=0)
        scores = [_dot_nt(k_ref[r0:r0 + BAND, :], q_s) for q_s, k_ref in ((q_a, kb_ref), (q_b, ksb_ref))]
        yield UNITS_AFTER_FIRST_MATMULS

        if ci + 1 < n_chunks:
            packed_next = conv_and_decay(ci + 1)
        cumx_ref[...] = expanded[:, :SSM_WIDTH]
        xdt_ref[...] = xsbuf[rows, :] * expanded[:, SSM_WIDTH:]
        m_gs, bds, xws, probs = [], [], [], []
        for g in range(SSM_GROUPS):
            colb = cumx_ref[:, group_cols[g]]
            rowb = jnp.sum(jnp.where(diag_mask, colb, 0.0), axis=0, keepdims=True)
            lmat = jnp.exp2(jnp.where(causal_mask, colb - rowb, -jnp.inf))
            m_gs.append((cbs[g] * lmat).astype(BF16))
            xdt_g = xdt_ref[:, group_cols[g]]
            xdt_b = xdt_g.astype(BF16)
            bds.append(jnp.where(bd_mask, jnp.concatenate([xdt_b] * HEADS_PER_STACK, axis=0), jnp.zeros((), BF16)))
            last = colb[CHUNK - 1:CHUNK, :]
            xws.append((xdt_g * jnp.exp2(last - colb)).astype(BF16))
        for s_idx in range(2):
            s = scores[s_idx] + bias_ref[variant, s_idx]
            sink = sink_ref[s_idx][...]
            m = jnp.maximum(jnp.max(s, axis=0, keepdims=True), sink)
            p = jnp.exp2(s - m)
            denom = jnp.sum(p, axis=0, keepdims=True) + jnp.exp2(sink - m)
            probs.append((p * (1.0 / denom)).astype(BF16))
        b_ts = [b_g.T for b_g in b_gs]
        yield UNITS_AFTER_VECTOR_WORK

        y_diags = [_dot(m_gs[g], bds[g]) for g in range(SSM_GROUPS)]
        updates = [_dot(b_ts[g], xws[g]) for g in range(SSM_GROUPS)]
        vt_src, c0 = (vt_even, r0) if ci % 2 == 0 else (vt_odd, r0 - CHUNK)
        vband_t = vt_src[:, c0:c0 + BAND]
        vband_t_swapped = jnp.concatenate([vband_t[HEAD_DIM:], vband_t[:HEAD_DIM]], axis=0)
        outs_t = [_dot(vband_t, probs[0]), _dot(vband_t_swapped, probs[1])]
        yield UNITS_AFTER_SECOND_MATMULS

        for g in range(SSM_GROUPS):
            cols = group_cols[g]
            colb = cumx_ref[:, cols]
            last = colb[CHUNK - 1:CHUNK, :]
            st_ref[:, cols] = jnp.exp2(last) * st_ref[:, cols] + updates[g]
            y = y_diags[g] + y_offs[g] * jnp.exp2(colb) + d_ref[:, cols] * xsbuf[rows, cols]
            gg = y * _silu(zbuf[rows, cols])
            ms = jnp.mean(gg * gg, axis=-1, keepdims=True)
            mixed_ref[rows, cols] = (gg * lax.rsqrt(ms + RMS_EPS) * norm_w_ref[:, cols]).astype(BF16)
        o_a, o_b = [o_t.T for o_t in outs_t]
        sel = [(o_a, o_b), (o_a, o_b), (o_b, o_a), (o_b, o_a)]
        for j in range(ATTN_HEADS // 2):
            ev, od = sel[j]
            blk = jnp.where(low_half, ev[j * CHUNK:(j + 1) * CHUNK], od[j * CHUNK:(j + 1) * CHUNK])
            mixed_ref[rows, SSM_WIDTH + j * LANES:SSM_WIDTH + (j + 1) * LANES] = blk.astype(BF16)

    convbuf[CONV_HIST, :] = convbuf[CONV_PAD + tile - (CONV_W - 1):CONV_PAD + tile, :]
    for ref in (kf_ref, vf_ref, kb_ref, ksb_ref, vb_ref):
        ref[0:WINDOW, :] = ref[tile:tile + WINDOW, :]

    yield

    mix = _dot(mixed_ref[...], w_out_ref[...])
    if x_next_ref is not None:
        switch_stream()
        _prepare_tile(tile, None, mixer_w, mixer_s)
    return _layer_norm(DEEPNORM_ALPHA * x_ref[0] + mix, ln_g_ref[...], ln_b_ref[...])


FFN_DOWN_AFTER = (7, N_FFN_CHUNKS)
N_FFN_UNITS = 2 * N_FFN_CHUNKS + len(FFN_DOWN_AFTER) * (D_MODEL // FFN_CHUNK)


def _ffn_tile(h_ref, hb_ref, act_ref, w_gate_ref, w_up_ref, w_down_ref, ln_g_ref, ln_b_ref):
    out_cols = [slice(n * FFN_CHUNK, (n + 1) * FFN_CHUNK) for n in range(D_MODEL // FFN_CHUNK)]
    pre = [DEEPNORM_ALPHA * h_ref[:, cols] for cols in out_cols]
    done = 0
    for c in range(N_FFN_CHUNKS):
        cols = slice(c * FFN_CHUNK, (c + 1) * FFN_CHUNK)
        gate = _dot(hb_ref[...], w_gate_ref[:, cols])
        yield
        up = _dot(hb_ref[...], w_up_ref[:, cols])
        act_ref[:, cols] = (_silu(gate) * up).astype(BF16)
        yield
        if c + 1 in FFN_DOWN_AFTER:
            k_rows = slice(done * FFN_CHUNK, (c + 1) * FFN_CHUNK)
            done = c + 1
            for n, cols in enumerate(out_cols):
                pre[n] = pre[n] + _dot(act_ref[:, k_rows], w_down_ref[k_rows, cols])
                yield
    return _layer_norm(jnp.concatenate(pre, axis=1), ln_g_ref[...], ln_b_ref[...])


def _run(gen):
    while True:
        try:
            next(gen)
        except StopIteration as stop:
            return stop.value


def _interleave(mixer, ffn):
    remaining = N_FFN_UNITS
    while True:
        try:
            wanted = next(mixer)
        except StopIteration as stop:
            return stop.value, y
        last = wanted is None
        for _ in range(remaining if last else min(wanted, remaining)):
            next(ffn)
            remaining -= 1
        if last:
            y = _run(ffn)


N_MIXER_SCRATCH = 19

VEC_LAYOUT = (tuple(("conv_w%d" % i, CONV_DIM) for i in range(CONV_W))
              + (("conv_b", CONV_DIM), ("dt_bias", LANES), ("a_log", LANES), ("d_skip", SSM_WIDTH),
                 ("norm_w", SSM_WIDTH), ("sink_a", STACK_ROWS), ("sink_b", STACK_ROWS),
                 ("ln1_g", D_MODEL), ("ln1_b", D_MODEL), ("ln2_g", D_MODEL), ("ln2_b", D_MODEL)))
VEC_WIDTH = sum(width for _, width in VEC_LAYOUT)


def _vec_views(vec_ref):
    views, offset = {}, 0
    for name, width in VEC_LAYOUT:
        views[name] = vec_ref.at[:, offset:offset + width]
        offset += width
    return views


def _mixer_weights(w_in_ref, vec_ref, expand_ref, bias_ref, w_out_ref):
    v = _vec_views(vec_ref)
    return (w_in_ref.at[:, :COL_END], w_in_ref.at[:, COL_END:COL_END + LANES],
            tuple(v["conv_w%d" % i] for i in range(CONV_W)), v["conv_b"], v["dt_bias"], v["a_log"], v["d_skip"],
            v["norm_w"], expand_ref, bias_ref, (v["sink_a"], v["sink_b"]), w_out_ref, v["ln1_g"], v["ln1_b"])


def _ffn_weights(vec_ref, w_gate_ref, w_up_ref, w_down_ref):
    v = _vec_views(vec_ref)
    return (w_gate_ref, w_up_ref, w_down_ref, v["ln2_g"], v["ln2_b"])


def _fused_kernel(tile, n_tiles, n_total,
                  x_ref, x_next_ref, w_in_ref, vec_ref, expand_ref, bias_ref, w_out_ref, w_gate_ref, w_up_ref, w_down_ref,
                  y_ref, conv_out_ref, ssm_out_ref, k_out_ref, v_out_ref, *scratch):
    mixer_w = _mixer_weights(w_in_ref, vec_ref, expand_ref, bias_ref, w_out_ref)
    ffn_w = _ffn_weights(vec_ref, w_gate_ref, w_up_ref, w_down_ref)
    mixer_s = scratch[:N_MIXER_SCRATCH]
    h_s, hb_s, act_s = scratch[N_MIXER_SCRATCH:]
    convbuf, st_ref, kf_ref, vf_ref = mixer_s[:4]

    g = pl.program_id(0)
    t = lax.rem(jnp.minimum(g, n_total - 1), n_tiles)
    next_starts_stream = (g + 1 < n_total) & (lax.rem(g + 1, n_tiles) == 0)

    @pl.when(g == 0)
    def _():
        h_s[...] = jnp.zeros(h_s.shape, F32)
        hb_s[...] = jnp.zeros(hb_s.shape, BF16)
        convbuf[0:CONV_PAD, :] = jnp.zeros((CONV_PAD, CONV_DIM), F32)
        _prepare_tile(tile, x_ref, mixer_w, mixer_s)

    @pl.when(t == 0)
    def _():
        st_ref[...] = jnp.zeros(st_ref.shape, F32)
        for ref in mixer_s[2:7]:
            ref[0:WINDOW, :] = jnp.zeros((WINDOW, KV_W), ref.dtype)

    def switch_stream():
        hist = convbuf[CONV_HIST, :]
        conv_out_ref[0] = hist
        convbuf[CONV_HIST, :] = jnp.where(next_starts_stream, 0.0, hist)

    n_chunks = tile // CHUNK
    ffn = _ffn_tile(h_s, hb_s, act_s, *ffn_w)
    mixer = _mixer_tile(False, tile, t * n_chunks, x_ref, x_next_ref, switch_stream, mixer_w, mixer_s)
    h, y = _interleave(mixer, ffn)
    y_ref[0] = y
    h_s[...] = h
    hb_s[...] = h.astype(BF16)

    @pl.when((t == n_tiles - 1) & (g < n_total))
    def _():
        _write_stream_state(ssm_out_ref, k_out_ref, v_out_ref, st_ref, kf_ref, vf_ref)


def _mixer_kernel(tile, x_ref, conv0_ref, ssm0_ref, k0_ref, v0_ref,
                  w_in_ref, vec_ref, expand_ref, bias_ref, w_out_ref,
                  h_ref, conv_out_ref, ssm_out_ref, k_out_ref, v_out_ref, *mixer_s):
    mixer_w = _mixer_weights(w_in_ref, vec_ref, expand_ref, bias_ref, w_out_ref)
    convbuf, st_ref, kf_ref, vf_ref = mixer_s[:4]
    convbuf[0:CONV_PAD, :] = jnp.zeros((CONV_PAD, CONV_DIM), F32)
    convbuf[CONV_HIST, :] = conv0_ref[0]
    _init_stream_state(ssm0_ref, k0_ref, v0_ref, *mixer_s[1:7])
    _prepare_tile(tile, x_ref, mixer_w, mixer_s)
    h_ref[0] = _run(_mixer_tile(True, tile, 0, x_ref, None, None, mixer_w, mixer_s))
    conv_out_ref[0] = convbuf[CONV_HIST, :]
    _write_stream_state(ssm_out_ref, k_out_ref, v_out_ref, st_ref, kf_ref, vf_ref)


def _ffn_kernel(h_ref, vec_ref, w_gate_ref, w_up_ref, w_down_ref, y_ref, hb_s, act_s):
    hb_s[...] = h_ref[...].astype(BF16)
    y_ref[...] = _run(_ffn_tile(h_ref, hb_s, act_s, *_ffn_weights(vec_ref, w_gate_ref, w_up_ref, w_down_ref)))


def _const_spec(shape):
    zeros = (0,) * len(shape)
    return pl.BlockSpec(shape, lambda *_: zeros, pipeline_mode=pl.Buffered(1))


def _mixer_scratch(tile):
    return [pltpu.VMEM((CONV_PAD + tile, CONV_DIM), F32),
            pltpu.VMEM((D_STATE, SSM_WIDTH), F32),
            pltpu.VMEM((WINDOW + tile, KV_W), F32), pltpu.VMEM((WINDOW + tile, KV_W), F32),
            pltpu.VMEM((WINDOW + tile, KV_W), BF16), pltpu.VMEM((WINDOW + tile, KV_W), BF16),
            pltpu.VMEM((WINDOW + tile, KV_W), BF16), pltpu.VMEM((KV_W, WINDOW + tile), BF16),
            pltpu.VMEM((tile, D_MODEL), BF16),
            pltpu.VMEM((tile, SSM_WIDTH), F32),
            pltpu.VMEM((tile, ATTN_WIDTH), F32),
            pltpu.VMEM((tile, LANES), F32),
            pltpu.VMEM((tile, SSM_WIDTH), F32),
            pltpu.VMEM((tile, CONV_DIM - SSM_WIDTH), BF16),
            pltpu.VMEM((CHUNK, SSM_WIDTH), F32),
            pltpu.VMEM((CHUNK, SSM_WIDTH), F32),
            pltpu.VMEM((tile, D_MODEL), BF16),
            pltpu.VMEM((CHUNK, LANES), BF16),
            pltpu.VMEM((KV_W, max(tile, LANES) + CHUNK), BF16)]


_STATE_SHAPES = ((CONV_W - 1, CONV_DIM), (SSM_WIDTH, D_STATE), (KV_W, WINDOW), (KV_W, WINDOW))


def _attention_tables(attn_sinks):
    slopes = np.exp2(-8.0 * np.arange(1, ATTN_HEADS + 1, dtype=np.float64) / ATTN_HEADS)
    qi = np.arange(CHUNK)[None, :] + WINDOW
    kj = np.arange(BAND)[:, None]
    dist = np.abs(qi - kj).astype(np.float64)
    key_chunk = kj // CHUNK
    tables = np.zeros((HIST_CHUNKS + 1, 2, BAND, STACK_ROWS), np.float32)
    for variant in range(HIST_CHUNKS + 1):
        valid = key_chunk >= (HIST_CHUNKS - variant)
        for s_idx, heads in enumerate((STACK_A_HEADS, STACK_B_HEADS)):
            for i, h in enumerate(heads):
                tables[variant, s_idx, :, i * CHUNK:(i + 1) * CHUNK] = np.where(
                    valid, -slopes[h] * dist * LOG2E, -np.inf)
    order = np.array(STACK_A_HEADS + STACK_B_HEADS)
    sink_rows = jnp.repeat(attn_sinks.astype(F32)[order] * LOG2E, CHUNK).reshape(2, STACK_ROWS)
    return jnp.asarray(tables), sink_rows


def _expansion_matrix():
    e = np.zeros((EXPAND_K, 2 * SSM_WIDTH), np.float32)
    for part in range(6):
        for h in range(SSM_HEADS):
            c0 = (part // 3) * SSM_WIDTH + h * HEAD_DIM
            e[part * SSM_HEADS + h, c0:c0 + HEAD_DIM] = 1.0
    return jnp.asarray(e, BF16)


def _prompt_call(x, mixer_ops, ffn_ops):
    bsz, seq, _ = x.shape
    tile = PROMPT_TILE
    n_tiles = seq // tile
    n_total = bsz * n_tiles

    def mixer_tile_idx(g):
        gc = jnp.minimum(g, n_total - 1)
        return lax.div(gc, n_tiles), lax.rem(gc, n_tiles)

    def next_tile_idx(g):
        return mixer_tile_idx(g + 1)

    def ffn_tile_idx(g):
        gp = jnp.maximum(g - 1, 0)
        return lax.div(gp, n_tiles), lax.rem(gp, n_tiles)

    tile_spec = lambda idx: pl.BlockSpec((1, tile, D_MODEL), lambda g: (*idx(g), 0))
    stream_spec = lambda shape: pl.BlockSpec((1,) + shape, lambda g: (mixer_tile_idx(g)[0], 0, 0))
    in_specs = ([tile_spec(mixer_tile_idx), tile_spec(next_tile_idx)]
                + [_const_spec(w.shape) for w in mixer_ops + ffn_ops])
    out_specs = [tile_spec(ffn_tile_idx)] + [stream_spec(s) for s in _STATE_SHAPES]
    out_shape = ([jax.ShapeDtypeStruct((bsz, seq, D_MODEL), F32)]
                 + [jax.ShapeDtypeStruct((bsz,) + s, F32) for s in _STATE_SHAPES])
    scratch = _mixer_scratch(tile) + [pltpu.VMEM((tile, D_MODEL), F32), pltpu.VMEM((tile, D_MODEL), BF16),
                                      pltpu.VMEM((tile, D_FF), BF16)]
    return pl.pallas_call(
        functools.partial(_fused_kernel, tile, n_tiles, n_total),
        grid=(n_total + 1,), in_specs=in_specs, out_specs=out_specs, out_shape=out_shape,
        scratch_shapes=scratch, name="prompt_layer",
        compiler_params=pltpu.CompilerParams(dimension_semantics=("arbitrary",), vmem_limit_bytes=VMEM_LIMIT),
    )(x, x, *mixer_ops, *ffn_ops)


def _sample_mixer_call(x, state0, mixer_ops):
    bsz, seq, _ = x.shape
    stream_spec = lambda shape: pl.BlockSpec((1,) + shape, lambda b: (b, 0, 0))
    in_specs = ([stream_spec((seq, D_MODEL))] + [stream_spec(s) for s in _STATE_SHAPES]
                + [_const_spec(w.shape) for w in mixer_ops])
    out_specs = [stream_spec((seq, D_MODEL))] + [stream_spec(s) for s in _STATE_SHAPES]
    out_shape = ([jax.ShapeDtypeStruct((bsz, seq, D_MODEL), F32)]
                 + [jax.ShapeDtypeStruct((bsz,) + s, F32) for s in _STATE_SHAPES])
    return pl.pallas_call(
        functools.partial(_mixer_kernel, seq),
        grid=(bsz,), in_specs=in_specs, out_specs=out_specs, out_shape=out_shape,
        scratch_shapes=_mixer_scratch(seq), name="sample_mixer",
        compiler_params=pltpu.CompilerParams(dimension_semantics=("arbitrary",), vmem_limit_bytes=VMEM_LIMIT),
    )(x, *state0, *mixer_ops)


def _ffn_call(h2d, ffn_ops):
    n_tok = h2d.shape[0]
    tile = min(FFN_TILE, n_tok)
    tok_spec = pl.BlockSpec((tile, D_MODEL), lambda i: (i, 0))
    return pl.pallas_call(
        _ffn_kernel, grid=(n_tok // tile,),
        in_specs=[tok_spec] + [_const_spec(w.shape) for w in ffn_ops],
        out_specs=tok_spec, out_shape=jax.ShapeDtypeStruct((n_tok, D_MODEL), F32), name="sample_ffn",
        scratch_shapes=[pltpu.VMEM((tile, D_MODEL), BF16), pltpu.VMEM((tile, D_FF), BF16)],
        compiler_params=pltpu.CompilerParams(dimension_semantics=("arbitrary",), vmem_limit_bytes=VMEM_LIMIT),
    )(h2d, *ffn_ops)


IN_COLS = COL_END + 8
W_IN_PREP_ROWS = 256


def _w_in_kernel(w_t_ref, out_ref):
    src = {"z": 0, "xbc": SSM_WIDTH, "dt": SSM_WIDTH + CONV_DIM}
    src["q"] = src["dt"] + SSM_HEADS
    src["k"] = src["q"] + ATTN_WIDTH
    src["v"] = src["k"] + KV_W
    pieces = (("z", COL_Z, COL_XBC - COL_Z, 1.0), ("xbc", COL_XBC, COL_Q - COL_XBC, 1.0),
              ("q", COL_Q, COL_K - COL_Q, HEAD_DIM ** -0.5 * LOG2E), ("k", COL_K, KV_W, 1.0), ("v", COL_V, KV_W, 1.0))
    for name, dst, width, scale in pieces:
        for off in range(0, width, W_IN_PREP_ROWS):
            rows = min(W_IN_PREP_ROWS, width - off)
            blk = w_t_ref[src[name] + off:src[name] + off + rows, :]
            if scale != 1.0:
                blk = blk * scale
            out_ref[:, dst + off:dst + off + rows] = blk.astype(BF16).T
    w_dt = w_t_ref[src["dt"]:src["dt"] + SSM_HEADS, :]
    w_dt_rep = jnp.concatenate([w_dt] * (LANES // SSM_HEADS), axis=0)
    out_ref[:, COL_END:COL_END + LANES] = w_dt_rep.astype(BF16).T


def _prepare_w_in(w_t):
    return pl.pallas_call(
        _w_in_kernel, grid=(1,),
        in_specs=[_const_spec((IN_COLS, D_MODEL))], out_specs=pl.BlockSpec((D_MODEL, COL_END + LANES), lambda i: (0, 0)),
        out_shape=jax.ShapeDtypeStruct((D_MODEL, COL_END + LANES), BF16), name="w_in_prep",
        compiler_params=pltpu.CompilerParams(dimension_semantics=("arbitrary",), vmem_limit_bytes=VMEM_LIMIT),
    )(w_t)


def _window_to_kernel_layout(cache):
    b = cache.shape[0]
    return jnp.transpose(cache.astype(F32).reshape(b, WINDOW, KV_W), (0, 2, 1))


def _window_from_kernel_layout(win_t):
    b = win_t.shape[0]
    return jnp.transpose(win_t, (0, 2, 1)).reshape(1, b, WINDOW, KV_HEADS, HEAD_DIM)


def kernel(x_prompt, x_sample, state_conv, state_ssm, cache_k, cache_v, w_in, conv_w, conv_b, dt_bias, a_log,
           d_skip, ssm_norm_w, attn_sinks, w_out, ln1_g, ln1_b, w_gate, w_up, w_down, ln2_g, ln2_b):
    w_in_r = _prepare_w_in(jnp.transpose(w_in[0]))
    rep = LANES // SSM_HEADS
    bias_tables, sink_rows = _attention_tables(attn_sinks[0])
    vec_parts = {"conv_b": conv_b[0], "dt_bias": jnp.tile(dt_bias[0], rep), "a_log": jnp.tile(a_log[0], rep),
                 "d_skip": jnp.repeat(d_skip[0], HEAD_DIM), "norm_w": ssm_norm_w[0],
                 "sink_a": sink_rows[0], "sink_b": sink_rows[1],
                 "ln1_g": ln1_g[0], "ln1_b": ln1_b[0], "ln2_g": ln2_g[0], "ln2_b": ln2_b[0]}
    vec_parts.update({"conv_w%d" % i: conv_w[0, i] for i in range(CONV_W)})
    vecs = jnp.concatenate([vec_parts[name].astype(F32).reshape(width) for name, width in VEC_LAYOUT])
    vecs = vecs.reshape(1, VEC_WIDTH)
    mixer_ops = (w_in_r, vecs, _expansion_matrix(), bias_tables, w_out[0].astype(BF16))
    ffn_mats = (w_gate[0].astype(BF16), w_up[0].astype(BF16), w_down[0].astype(BF16))

    dbsz, dseq, _ = x_sample.shape

    y_p, conv_p, ssm_p, k_p, v_p = _prompt_call(x_prompt, mixer_ops, ffn_mats)

    sample_state = (state_conv[0].astype(F32), state_ssm[0].astype(F32).reshape(dbsz, SSM_WIDTH, D_STATE),
                    _window_to_kernel_layout(cache_k[0]), _window_to_kernel_layout(cache_v[0]))
    h_s, conv_s, ssm_s, k_s, v_s = _sample_mixer_call(x_sample, sample_state, mixer_ops)
    y_s = _ffn_call(h_s.reshape(dbsz * dseq, D_MODEL), (vecs,) + ffn_mats).reshape(dbsz, dseq, D_MODEL)

    ssm_shape = lambda a: a.reshape(1, a.shape[0], SSM_HEADS, HEAD_DIM, D_STATE)
    kv_shape = _window_from_kernel_layout
    return (y_p, y_s,
            conv_p[None], ssm_shape(ssm_p), kv_shape(k_p), kv_shape(v_p),
            conv_s[None], ssm_shape(ssm_s), kv_shape(k_s), kv_shape(v_s))
```

```python
import functools
import math

import jax
import jax.numpy as jnp
import numpy as np
from jax import lax
from jax.experimental import pallas as pl
from jax.experimental.pallas import tpu as pltpu

D_MODEL = 1024
CHUNK = 64
HEAD_DIM = 64
SSM_WIDTH = 512
SSM_HEADS = 8
SSM_GROUPS = 2
D_STATE = 128
CONV_W = 4
CONV_DIM = SSM_WIDTH + 2 * SSM_GROUPS * D_STATE
ATTN_WIDTH = 512
ATTN_HEADS = 8
KV_HEADS = 2
WINDOW = 128
D_FF = 2816
LN_EPS = 1e-5
RMS_EPS = 1e-5
DEEPNORM_ALPHA = 2.0 ** 0.25
LOG2E = math.log2(math.e)

LANES = 128
SUBLANES = 8
GROUP_W = SSM_WIDTH // SSM_GROUPS
KV_W = KV_HEADS * HEAD_DIM
BAND = WINDOW + CHUNK
HIST_CHUNKS = WINDOW // CHUNK
HEADS_PER_STACK = 4
STACK_ROWS = HEADS_PER_STACK * CHUNK
COL_Z = 0
COL_XBC = COL_Z + SSM_WIDTH
COL_Q = COL_XBC + CONV_DIM
COL_K = COL_Q + ATTN_WIDTH
COL_V = COL_K + KV_W
COL_END = COL_V + KV_W
STACK_A_HEADS = (0, 2, 5, 7)
STACK_B_HEADS = (1, 3, 4, 6)
EXPAND_K = 128
CONV_PAD = SUBLANES

PROMPT_TILE = 256
FFN_TILE = 512
FFN_CHUNK = 256
N_FFN_CHUNKS = D_FF // FFN_CHUNK
VMEM_LIMIT = 56 * 1024 * 1024
UNITS_AT_START = 0
UNITS_AFTER_FIRST_MATMULS = 3
UNITS_AFTER_VECTOR_WORK = 2
UNITS_AFTER_SECOND_MATMULS = 1

F32 = jnp.float32
BF16 = jnp.bfloat16


def _dot(a, b):
    return jnp.dot(a, b, preferred_element_type=F32)


def _dot_nt(a, b):
    return lax.dot_general(a, b, (((1,), (1,)), ((), ())), preferred_element_type=F32)


def _sigmoid(x):
    return 1.0 / (1.0 + jnp.exp2(x * -LOG2E))


def _silu(x):
    return x * _sigmoid(x)


def _layer_norm(x, g, b):
    mu = jnp.mean(x, axis=-1, keepdims=True)
    xc = x - mu
    var = jnp.mean(xc * xc, axis=-1, keepdims=True)
    return xc * lax.rsqrt(var + LN_EPS) * g + b


def _split3(x):
    hi = x.astype(BF16).astype(F32)
    r = x - hi
    mid = r.astype(BF16).astype(F32)
    lo = r - mid
    return hi, mid, lo


CONV_HIST = slice(CONV_PAD - (CONV_W - 1), CONV_PAD)


def _init_stream_state(ssm0_ref, k0_ref, v0_ref, st_ref, kf_ref, vf_ref, kb_ref, ksb_ref, vb_ref):
    st_ref[...] = ssm0_ref[0].T
    k0 = k0_ref[0].T
    v0 = v0_ref[0].T
    kf_ref[0:WINDOW, :] = k0
    vf_ref[0:WINDOW, :] = v0
    kb_ref[0:WINDOW, :] = k0.astype(BF16)
    vb_ref[0:WINDOW, :] = v0.astype(BF16)
    ksb_ref[0:WINDOW, :] = pltpu.roll(k0, HEAD_DIM, axis=1).astype(BF16)


def _write_stream_state(ssm_out_ref, k_out_ref, v_out_ref, st_ref, kf_ref, vf_ref):
    ssm_out_ref[0] = st_ref[...].T
    k_out_ref[0] = kf_ref[0:WINDOW, :].T
    v_out_ref[0] = vf_ref[0:WINDOW, :].T


def _in_projection(tile, xb_ref, w_in_ref, w_dt_ref,
                   convbuf, kf_ref, vf_ref, kb_ref, ksb_ref, vb_ref, zbuf, qbuf, dtbuf):
    xb = xb_ref[...]
    zbuf[...] = _dot(xb, w_in_ref[:, COL_Z:COL_XBC])
    convbuf[CONV_PAD:CONV_PAD + tile, :] = _dot(xb, w_in_ref[:, COL_XBC:COL_Q])
    qbuf[...] = _dot(xb, w_in_ref[:, COL_Q:COL_K])
    kv = _dot(xb, w_in_ref[:, COL_K:COL_END])
    dtbuf[...] = _dot(xb, w_dt_ref[...])
    k_new = kv[:, :KV_W]
    v_new = kv[:, KV_W:]
    kf_ref[WINDOW:WINDOW + tile, :] = k_new
    vf_ref[WINDOW:WINDOW + tile, :] = v_new
    kb_ref[WINDOW:WINDOW + tile, :] = k_new.astype(BF16)
    vb_ref[WINDOW:WINDOW + tile, :] = v_new.astype(BF16)
    ksb_ref[WINDOW:WINDOW + tile, :] = pltpu.roll(k_new, HEAD_DIM, axis=1).astype(BF16)


def _conv_and_decay(ci, conv_w_ref, conv_b_ref, dt_bias_ref, a_log_ref, convbuf, dtbuf, xsbuf, bcbuf):
    r0 = ci * CHUNK
    rows = slice(r0, r0 + CHUNK)

    window = convbuf[r0:r0 + CONV_PAD + CHUNK, :]
    conv = conv_b_ref[...] + conv_w_ref[CONV_W - 1][...] * window[CONV_PAD:]
    for back in range(1, CONV_W):
        conv = conv + conv_w_ref[CONV_W - 1 - back][...] * pltpu.roll(window, back, axis=0)[CONV_PAD:]
    xc = _silu(conv)
    xsbuf[rows, :] = xc[:, :SSM_WIDTH]
    bcbuf[rows, :] = xc[:, SSM_WIDTH:].astype(BF16)

    row_in_chunk = lax.broadcasted_iota(jnp.int32, (CHUNK, LANES), 0)
    lane = lax.broadcasted_iota(jnp.int32, (CHUNK, LANES), 1)
    dt_in = dtbuf[rows, :] + dt_bias_ref[...]
    dt = jnp.maximum(dt_in, 0.0) + jnp.log1p(jnp.exp(-jnp.abs(dt_in)))
    cum = dt * (-LOG2E * jnp.exp(a_log_ref[...]))
    step = 1
    while step < CHUNK:
        cum = cum + jnp.where(row_in_chunk >= step, pltpu.roll(cum, step, axis=0), 0.0)
        step *= 2
    packed = jnp.zeros((CHUNK, LANES), F32)
    for i, part in enumerate(_split3(cum) + _split3(dt)):
        packed = jnp.where((lane >= SSM_HEADS * i) & (lane < SSM_HEADS * (i + 1)), part, packed)
    return packed.astype(BF16)


def _prepare_tile(tile, x_ref, mixer_w, mixer_s):
    (w_in_ref, w_dt_ref, conv_w_ref, conv_b_ref, dt_bias_ref, a_log_ref) = mixer_w[:6]
    (convbuf, _, kf_ref, vf_ref, kb_ref, ksb_ref, vb_ref, _, _, zbuf, qbuf, dtbuf, xsbuf, bcbuf, _, _,
     xb_ref, packbuf, _) = mixer_s
    if x_ref is not None:
        xb_ref[...] = x_ref[0].astype(BF16)
    _in_projection(tile, xb_ref, w_in_ref, w_dt_ref,
                   convbuf, kf_ref, vf_ref, kb_ref, ksb_ref, vb_ref, zbuf, qbuf, dtbuf)
    packbuf[...] = _conv_and_decay(0, conv_w_ref, conv_b_ref, dt_bias_ref, a_log_ref, convbuf, dtbuf, xsbuf, bcbuf)


def _mixer_tile(hist_all_valid, tile, first_chunk, x_ref, x_next_ref, switch_stream, mixer_w, mixer_s):
    (_, _, conv_w_ref, conv_b_ref, dt_bias_ref, a_log_ref, d_ref, norm_w_ref,
     expand_ref, bias_ref, sink_ref, w_out_ref, ln_g_ref, ln_b_ref) = mixer_w
    (convbuf, st_ref, kf_ref, vf_ref, kb_ref, ksb_ref, vb_ref, vt_even, mixed_ref,
     zbuf, qbuf, dtbuf, xsbuf, bcbuf, cumx_ref, xdt_ref, xb_ref, packbuf, vt_odd) = mixer_s
    n_chunks = tile // CHUNK
    if x_next_ref is not None:
        xb_ref[...] = x_next_ref[0].astype(BF16)
    vt_even[...] = vb_ref[...].T
    if n_chunks > 1:
        vt_odd[...] = vb_ref[CHUNK:CHUNK + vt_odd.shape[1], :].T
    yield UNITS_AT_START

    lane = lax.broadcasted_iota(jnp.int32, (CHUNK, LANES), 1)
    low_half = lane < HEAD_DIM
    li = lax.broadcasted_iota(jnp.int32, (CHUNK, GROUP_W), 0)
    si = lax.broadcasted_iota(jnp.int32, (CHUNK, GROUP_W), 1) % CHUNK
    diag_mask = li == si
    causal_mask = si <= li
    bd_mask = (lax.broadcasted_iota(jnp.int32, (GROUP_W, GROUP_W), 0) // CHUNK
               == lax.broadcasted_iota(jnp.int32, (GROUP_W, GROUP_W), 1) // CHUNK)

    def conv_and_decay(ci):
        return _conv_and_decay(ci, conv_w_ref, conv_b_ref, dt_bias_ref, a_log_ref, convbuf, dtbuf, xsbuf, bcbuf)

    packed_next = packbuf[...]
    for ci in range(n_chunks):
        r0 = ci * CHUNK
        rows = slice(r0, r0 + CHUNK)
        packed = packed_next

        expanded = _dot(packed, expand_ref[...])
        if hist_all_valid:
            variant = HIST_CHUNKS
        else:
            variant = jnp.minimum(first_chunk + ci, HIST_CHUNKS)
        group_cols = [slice(g * GROUP_W, (g + 1) * GROUP_W) for g in range(SSM_GROUPS)]
        b_gs = [bcbuf[rows, g * D_STATE:(g + 1) * D_STATE] for g in range(SSM_GROUPS)]
        c_gs = [bcbuf[rows, GROUP_W + g * D_STATE:GROUP_W + (g + 1) * D_STATE] for g in range(SSM_GROUPS)]
        cbs = [_dot_nt(c_gs[g], jnp.concatenate([b_gs[g]] * HEADS_PER_STACK, axis=0)) for g in range(SSM_GROUPS)]
        y_offs = [_dot(c_gs[g], st_ref[:, group_cols[g]].astype(BF16)) for g in range(SSM_GROUPS)]
        blocks = [qbuf[rows, j * LANES:(j + 1) * LANES] for j in range(ATTN_HEADS // 2)]
        evens = [jnp.where(low_half, blk, 0.0).astype(BF16) for blk in blocks]
        odds = [jnp.where(low_half, 0.0, blk).astype(BF16) for blk in blocks]
        q_a = jnp.concatenate([evens[0], evens[1], odds[2], odds[3]], axis=0)
        q_b = jnp.concatenate([odds[0], odds[1], evens[2], evens[3]], axis=0)
        scores = [_dot_nt(k_ref[r0:r0 + BAND, :], q_s) for q_s, k_ref in ((q_a, kb_ref), (q_b, ksb_ref))]
        yield UNITS_AFTER_FIRST_MATMULS + (1 if ci == 0 else 0)

        if ci + 1 < n_chunks:
            packed_next = conv_and_decay(ci + 1)
        cumx_ref[...] = expanded[:, :SSM_WIDTH]
        xdt_ref[...] = xsbuf[rows, :] * expanded[:, SSM_WIDTH:]
        m_gs, bds, xws, probs = [], [], [], []
        for g in range(SSM_GROUPS):
            colb = cumx_ref[:, group_cols[g]]
            rowb = jnp.sum(jnp.where(diag_mask, colb, 0.0), axis=0, keepdims=True)
            lmat = jnp.exp2(jnp.where(causal_mask, colb - rowb, -jnp.inf))
            m_gs.append((cbs[g] * lmat).astype(BF16))
            xdt_g = xdt_ref[:, group_cols[g]]
            xdt_b = xdt_g.astype(BF16)
            bds.append(jnp.where(bd_mask, jnp.concatenate([xdt_b] * HEADS_PER_STACK, axis=0), jnp.zeros((), BF16)))
            last = colb[CHUNK - 1:CHUNK, :]
            xws.append((xdt_g * jnp.exp2(last - colb)).astype(BF16))
        for s_idx in range(2):
            s = scores[s_idx] + bias_ref[variant, s_idx]
            sink = sink_ref[s_idx][...]
            m = jnp.maximum(jnp.max(s, axis=0, keepdims=True), sink)
            p = jnp.exp2(s - m)
            denom = jnp.sum(p, axis=0, keepdims=True) + jnp.exp2(sink - m)
            probs.append((p * (1.0 / denom)).astype(BF16))
        b_ts = [b_g.T for b_g in b_gs]
        yield UNITS_AFTER_VECTOR_WORK

        y_diags = [_dot(m_gs[g], bds[g]) for g in range(SSM_GROUPS)]
        updates = [_dot(b_ts[g], xws[g]) for g in range(SSM_GROUPS)]
        vt_src, c0 = (vt_even, r0) if ci % 2 == 0 else (vt_odd, r0 - CHUNK)
        vband_t = vt_src[:, c0:c0 + BAND]
        vband_t_swapped = jnp.concatenate([vband_t[HEAD_DIM:], vband_t[:HEAD_DIM]], axis=0)
        outs_t = [_dot(vband_t, probs[0]), _dot(vband_t_swapped, probs[1])]
        yield UNITS_AFTER_SECOND_MATMULS

        for g in range(SSM_GROUPS):
            cols = group_cols[g]
            colb = cumx_ref[:, cols]
            last = colb[CHUNK - 1:CHUNK, :]
            st_ref[:, cols] = jnp.exp2(last) * st_ref[:, cols] + updates[g]
            y = y_diags[g] + y_offs[g] * jnp.exp2(colb) + d_ref[:, cols] * xsbuf[rows, cols]
            gg = y * _silu(zbuf[rows, cols])
            ms = jnp.mean(gg * gg, axis=-1, keepdims=True)
            mixed_ref[rows, cols] = (gg * lax.rsqrt(ms + RMS_EPS) * norm_w_ref[:, cols]).astype(BF16)
        o_a, o_b = [o_t.T for o_t in outs_t]
        sel = [(o_a, o_b), (o_a, o_b), (o_b, o_a), (o_b, o_a)]
        for j in range(ATTN_HEADS // 2):
            ev, od = sel[j]
            blk = jnp.where(low_half, ev[j * CHUNK:(j + 1) * CHUNK], od[j * CHUNK:(j + 1) * CHUNK])
            mixed_ref[rows, SSM_WIDTH + j * LANES:SSM_WIDTH + (j + 1) * LANES] = blk.astype(BF16)

    convbuf[CONV_HIST, :] = convbuf[CONV_PAD + tile - (CONV_W - 1):CONV_PAD + tile, :]
    for ref in (kf_ref, vf_ref, kb_ref, ksb_ref, vb_ref):
        ref[0:WINDOW, :] = ref[tile:tile + WINDOW, :]

    yield

    mix = _dot(mixed_ref[...], w_out_ref[...])
    if x_next_ref is not None:
        switch_stream()
        _prepare_tile(tile, None, mixer_w, mixer_s)
    return _layer_norm(DEEPNORM_ALPHA * x_ref[0] + mix, ln_g_ref[...], ln_b_ref[...])


FFN_DOWN_AFTER = (7, N_FFN_CHUNKS)
N_FFN_UNITS = 2 * N_FFN_CHUNKS + len(FFN_DOWN_AFTER) * (D_MODEL // FFN_CHUNK)


def _ffn_tile(h_ref, hb_ref, act_ref, w_gate_ref, w_up_ref, w_down_ref, ln_g_ref, ln_b_ref):
    out_cols = [slice(n * FFN_CHUNK, (n + 1) * FFN_CHUNK) for n in range(D_MODEL // FFN_CHUNK)]
    pre = [DEEPNORM_ALPHA * h_ref[:, cols] for cols in out_cols]
    done = 0
    for c in range(N_FFN_CHUNKS):
        cols = slice(c * FFN_CHUNK, (c + 1) * FFN_CHUNK)
        gate = _dot(hb_ref[...], w_gate_ref[:, cols])
        yield
        up = _dot(hb_ref[...], w_up_ref[:, cols])
        act_ref[:, cols] = (_silu(gate) * up).astype(BF16)
        yield
        if c + 1 in FFN_DOWN_AFTER:
            k_rows = slice(done * FFN_CHUNK, (c + 1) * FFN_CHUNK)
            done = c + 1
            for n, cols in enumerate(out_cols):
                pre[n] = pre[n] + _dot(act_ref[:, k_rows], w_down_ref[k_rows, cols])
                yield
    return _layer_norm(jnp.concatenate(pre, axis=1), ln_g_ref[...], ln_b_ref[...])


def _run(gen):
    while True:
        try:
            next(gen)
        except StopIteration as stop:
            return stop.value


def _interleave(mixer, ffn):
    remaining = N_FFN_UNITS
    while True:
        try:
            wanted = next(mixer)
        except StopIteration as stop:
            return stop.value, y
        last = wanted is None
        for _ in range(remaining if last else min(wanted, remaining)):
            next(ffn)
            remaining -= 1
        if last:
            y = _run(ffn)


N_MIXER_SCRATCH = 19

VEC_LAYOUT = (tuple(("conv_w%d" % i, CONV_DIM) for i in range(CONV_W))
              + (("conv_b", CONV_DIM), ("dt_bias", LANES), ("a_log", LANES), ("d_skip", SSM_WIDTH),
                 ("norm_w", SSM_WIDTH), ("sink_a", STACK_ROWS), ("sink_b", STACK_ROWS),
                 ("ln1_g", D_MODEL), ("ln1_b", D_MODEL), ("ln2_g", D_MODEL), ("ln2_b", D_MODEL)))
VEC_WIDTH = sum(width for _, width in VEC_LAYOUT)


def _vec_views(vec_ref):
    views, offset = {}, 0
    for name, width in VEC_LAYOUT:
        views[name] = vec_ref.at[:, offset:offset + width]
        offset += width
    return views


def _mixer_weights(w_in_ref, vec_ref, expand_ref, bias_ref, w_out_ref):
    v = _vec_views(vec_ref)
    return (w_in_ref.at[:, :COL_END], w_in_ref.at[:, COL_END:COL_END + LANES],
            tuple(v["conv_w%d" % i] for i in range(CONV_W)), v["conv_b"], v["dt_bias"], v["a_log"], v["d_skip"],
            v["norm_w"], expand_ref, bias_ref, (v["sink_a"], v["sink_b"]), w_out_ref, v["ln1_g"], v["ln1_b"])


def _ffn_weights(vec_ref, w_gate_ref, w_up_ref, w_down_ref):
    v = _vec_views(vec_ref)
    return (w_gate_ref, w_up_ref, w_down_ref, v["ln2_g"], v["ln2_b"])


def _fused_kernel(tile, n_tiles, n_total,
                  x_ref, x_next_ref, w_in_ref, vec_ref, expand_ref, bias_ref, w_out_ref, w_gate_ref, w_up_ref, w_down_ref,
                  y_ref, conv_out_ref, ssm_out_ref, k_out_ref, v_out_ref, *scratch):
    mixer_w = _mixer_weights(w_in_ref, vec_ref, expand_ref, bias_ref, w_out_ref)
    ffn_w = _ffn_weights(vec_ref, w_gate_ref, w_up_ref, w_down_ref)
    mixer_s = scratch[:N_MIXER_SCRATCH]
    h_s, hb_s, act_s = scratch[N_MIXER_SCRATCH:]
    convbuf, st_ref, kf_ref, vf_ref = mixer_s[:4]

    g = pl.program_id(0)
    t = lax.rem(jnp.minimum(g, n_total - 1), n_tiles)
    next_starts_stream = (g + 1 < n_total) & (lax.rem(g + 1, n_tiles) == 0)

    @pl.when(g == 0)
    def _():
        h_s[...] = jnp.zeros(h_s.shape, F32)
        hb_s[...] = jnp.zeros(hb_s.shape, BF16)
        convbuf[0:CONV_PAD, :] = jnp.zeros((CONV_PAD, CONV_DIM), F32)
        _prepare_tile(tile, x_ref, mixer_w, mixer_s)

    @pl.when(t == 0)
    def _():
        st_ref[...] = jnp.zeros(st_ref.shape, F32)
        for ref in mixer_s[2:7]:
            ref[0:WINDOW, :] = jnp.zeros((WINDOW, KV_W), ref.dtype)

    def switch_stream():
        hist = convbuf[CONV_HIST, :]
        conv_out_ref[0] = hist
        convbuf[CONV_HIST, :] = jnp.where(next_starts_stream, 0.0, hist)

    n_chunks = tile // CHUNK
    ffn = _ffn_tile(h_s, hb_s, act_s, *ffn_w)
    mixer = _mixer_tile(False, tile, t * n_chunks, x_ref, x_next_ref, switch_stream, mixer_w, mixer_s)
    h, y = _interleave(mixer, ffn)
    y_ref[0] = y
    h_s[...] = h
    hb_s[...] = h.astype(BF16)

    @pl.when((t == n_tiles - 1) & (g < n_total))
    def _():
        _write_stream_state(ssm_out_ref, k_out_ref, v_out_ref, st_ref, kf_ref, vf_ref)


def _mixer_kernel(tile, x_ref, conv0_ref, ssm0_ref, k0_ref, v0_ref,
                  w_in_ref, vec_ref, expand_ref, bias_ref, w_out_ref,
                  h_ref, conv_out_ref, ssm_out_ref, k_out_ref, v_out_ref, *mixer_s):
    mixer_w = _mixer_weights(w_in_ref, vec_ref, expand_ref, bias_ref, w_out_ref)
    convbuf, st_ref, kf_ref, vf_ref = mixer_s[:4]
    convbuf[0:CONV_PAD, :] = jnp.zeros((CONV_PAD, CONV_DIM), F32)
    convbuf[CONV_HIST, :] = conv0_ref[0]
    _init_stream_state(ssm0_ref, k0_ref, v0_ref, *mixer_s[1:7])
    _prepare_tile(tile, x_ref, mixer_w, mixer_s)
    h_ref[0] = _run(_mixer_tile(True, tile, 0, x_ref, None, None, mixer_w, mixer_s))
    conv_out_ref[0] = convbuf[CONV_HIST, :]
    _write_stream_state(ssm_out_ref, k_out_ref, v_out_ref, st_ref, kf_ref, vf_ref)


def _ffn_kernel(h_ref, vec_ref, w_gate_ref, w_up_ref, w_down_ref, y_ref, hb_s, act_s):
    hb_s[...] = h_ref[...].astype(BF16)
    y_ref[...] = _run(_ffn_tile(h_ref, hb_s, act_s, *_ffn_weights(vec_ref, w_gate_ref, w_up_ref, w_down_ref)))


def _const_spec(shape):
    zeros = (0,) * len(shape)
    return pl.BlockSpec(shape, lambda *_: zeros, pipeline_mode=pl.Buffered(1))


def _mixer_scratch(tile):
    return [pltpu.VMEM((CONV_PAD + tile, CONV_DIM), F32),
            pltpu.VMEM((D_STATE, SSM_WIDTH), F32),
            pltpu.VMEM((WINDOW + tile, KV_W), F32), pltpu.VMEM((WINDOW + tile, KV_W), F32),
            pltpu.VMEM((WINDOW + tile, KV_W), BF16), pltpu.VMEM((WINDOW + tile, KV_W), BF16),
            pltpu.VMEM((WINDOW + tile, KV_W), BF16), pltpu.VMEM((KV_W, WINDOW + tile), BF16),
            pltpu.VMEM((tile, D_MODEL), BF16),
            pltpu.VMEM((tile, SSM_WIDTH), F32),
            pltpu.VMEM((tile, ATTN_WIDTH), F32),
            pltpu.VMEM((tile, LANES), F32),
            pltpu.VMEM((tile, SSM_WIDTH), F32),
            pltpu.VMEM((tile, CONV_DIM - SSM_WIDTH), BF16),
            pltpu.VMEM((CHUNK, SSM_WIDTH), F32),
            pltpu.VMEM((CHUNK, SSM_WIDTH), F32),
            pltpu.VMEM((tile, D_MODEL), BF16),
            pltpu.VMEM((CHUNK, LANES), BF16),
            pltpu.VMEM((KV_W, max(tile, LANES) + CHUNK), BF16)]


_STATE_SHAPES = ((CONV_W - 1, CONV_DIM), (SSM_WIDTH, D_STATE), (KV_W, WINDOW), (KV_W, WINDOW))


def _attention_tables(attn_sinks):
    slopes = np.exp2(-8.0 * np.arange(1, ATTN_HEADS + 1, dtype=np.float64) / ATTN_HEADS)
    qi = np.arange(CHUNK)[None, :] + WINDOW
    kj = np.arange(BAND)[:, None]
    dist = np.abs(qi - kj).astype(np.float64)
    key_chunk = kj // CHUNK
    tables = np.zeros((HIST_CHUNKS + 1, 2, BAND, STACK_ROWS), np.float32)
    for variant in range(HIST_CHUNKS + 1):
        valid = key_chunk >= (HIST_CHUNKS - variant)
        for s_idx, heads in enumerate((STACK_A_HEADS, STACK_B_HEADS)):
            for i, h in enumerate(heads):
                tables[variant, s_idx, :, i * CHUNK:(i + 1) * CHUNK] = np.where(
                    valid, -slopes[h] * dist * LOG2E, -np.inf)
    order = np.array(STACK_A_HEADS + STACK_B_HEADS)
    sink_rows = jnp.repeat(attn_sinks.astype(F32)[order] * LOG2E, CHUNK).reshape(2, STACK_ROWS)
    return jnp.asarray(tables), sink_rows


def _expansion_matrix():
    e = np.zeros((EXPAND_K, 2 * SSM_WIDTH), np.float32)
    for part in range(6):
        for h in range(SSM_HEADS):
            c0 = (part // 3) * SSM_WIDTH + h * HEAD_DIM
            e[part * SSM_HEADS + h, c0:c0 + HEAD_DIM] = 1.0
    return jnp.asarray(e, BF16)


def _prompt_call(x, mixer_ops, ffn_ops):
    bsz, seq, _ = x.shape
    tile = PROMPT_TILE
    n_tiles = seq // tile
    n_total = bsz * n_tiles

    def mixer_tile_idx(g):
        gc = jnp.minimum(g, n_total - 1)
        return lax.div(gc, n_tiles), lax.rem(gc, n_tiles)

    def next_tile_idx(g):
        return mixer_tile_idx(g + 1)

    def ffn_tile_idx(g):
        gp = jnp.maximum(g - 1, 0)
        return lax.div(gp, n_tiles), lax.rem(gp, n_tiles)

    tile_spec = lambda idx: pl.BlockSpec((1, tile, D_MODEL), lambda g: (*idx(g), 0))
    stream_spec = lambda shape: pl.BlockSpec((1,) + shape, lambda g: (mixer_tile_idx(g)[0], 0, 0))
    in_specs = ([tile_spec(mixer_tile_idx), tile_spec(next_tile_idx)]
                + [_const_spec(w.shape) for w in mixer_ops + ffn_ops])
    out_specs = [tile_spec(ffn_tile_idx)] + [stream_spec(s) for s in _STATE_SHAPES]
    out_shape = ([jax.ShapeDtypeStruct((bsz, seq, D_MODEL), F32)]
                 + [jax.ShapeDtypeStruct((bsz,) + s, F32) for s in _STATE_SHAPES])
    scratch = _mixer_scratch(tile) + [pltpu.VMEM((tile, D_MODEL), F32), pltpu.VMEM((tile, D_MODEL), BF16),
                                      pltpu.VMEM((tile, D_FF), BF16)]
    return pl.pallas_call(
        functools.partial(_fused_kernel, tile, n_tiles, n_total),
        grid=(n_total + 1,), in_specs=in_specs, out_specs=out_specs, out_shape=out_shape,
        scratch_shapes=scratch, name="prompt_layer",
        compiler_params=pltpu.CompilerParams(dimension_semantics=("arbitrary",), vmem_limit_bytes=VMEM_LIMIT),
    )(x, x, *mixer_ops, *ffn_ops)


def _sample_mixer_call(x, state0, mixer_ops):
    bsz, seq, _ = x.shape
    stream_spec = lambda shape: pl.BlockSpec((1,) + shape, lambda b: (b, 0, 0))
    in_specs = ([stream_spec((seq, D_MODEL))] + [stream_spec(s) for s in _STATE_SHAPES]
                + [_const_spec(w.shape) for w in mixer_ops])
    out_specs = [stream_spec((seq, D_MODEL))] + [stream_spec(s) for s in _STATE_SHAPES]
    out_shape = ([jax.ShapeDtypeStruct((bsz, seq, D_MODEL), F32)]
                 + [jax.ShapeDtypeStruct((bsz,) + s, F32) for s in _STATE_SHAPES])
    return pl.pallas_call(
        functools.partial(_mixer_kernel, seq),
        grid=(bsz,), in_specs=in_specs, out_specs=out_specs, out_shape=out_shape,
        scratch_shapes=_mixer_scratch(seq), name="sample_mixer",
        compiler_params=pltpu.CompilerParams(dimension_semantics=("arbitrary",), vmem_limit_bytes=VMEM_LIMIT),
    )(x, *state0, *mixer_ops)


def _ffn_call(h2d, ffn_ops):
    n_tok = h2d.shape[0]
    tile = min(FFN_TILE, n_tok)
    tok_spec = pl.BlockSpec((tile, D_MODEL), lambda i: (i, 0))
    return pl.pallas_call(
        _ffn_kernel, grid=(n_tok // tile,),
        in_specs=[tok_spec] + [_const_spec(w.shape) for w in ffn_ops],
        out_specs=tok_spec, out_shape=jax.ShapeDtypeStruct((n_tok, D_MODEL), F32), name="sample_ffn",
        scratch_shapes=[pltpu.VMEM((tile, D_MODEL), BF16), pltpu.VMEM((tile, D_FF), BF16)],
        compiler_params=pltpu.CompilerParams(dimension_semantics=("arbitrary",), vmem_limit_bytes=VMEM_LIMIT),
    )(h2d, *ffn_ops)


IN_COLS = COL_END + 8
W_IN_PREP_ROWS = 256


def _w_in_kernel(w_t_ref, out_ref):
    src = {"z": 0, "xbc": SSM_WIDTH, "dt": SSM_WIDTH + CONV_DIM}
    src["q"] = src["dt"] + SSM_HEADS
    src["k"] = src["q"] + ATTN_WIDTH
    src["v"] = src["k"] + KV_W
    pieces = (("z", COL_Z, COL_XBC - COL_Z, 1.0), ("xbc", COL_XBC, COL_Q - COL_XBC, 1.0),
              ("q", COL_Q, COL_K - COL_Q, HEAD_DIM ** -0.5 * LOG2E), ("k", COL_K, KV_W, 1.0), ("v", COL_V, KV_W, 1.0))
    for name, dst, width, scale in pieces:
        for off in range(0, width, W_IN_PREP_ROWS):
            rows = min(W_IN_PREP_ROWS, width - off)
            blk = w_t_ref[src[name] + off:src[name] + off + rows, :]
            if scale != 1.0:
                blk = blk * scale
            out_ref[:, dst + off:dst + off + rows] = blk.astype(BF16).T
    w_dt = w_t_ref[src["dt"]:src["dt"] + SSM_HEADS, :]
    w_dt_rep = jnp.concatenate([w_dt] * (LANES // SSM_HEADS), axis=0)
    out_ref[:, COL_END:COL_END + LANES] = w_dt_rep.astype(BF16).T


def _prepare_w_in(w_t):
    return pl.pallas_call(
        _w_in_kernel, grid=(1,),
        in_specs=[_const_spec((IN_COLS, D_MODEL))], out_specs=pl.BlockSpec((D_MODEL, COL_END + LANES), lambda i: (0, 0)),
        out_shape=jax.ShapeDtypeStruct((D_MODEL, COL_END + LANES), BF16), name="w_in_prep",
        compiler_params=pltpu.CompilerParams(dimension_semantics=("arbitrary",), vmem_limit_bytes=VMEM_LIMIT),
    )(w_t)


def _window_to_kernel_layout(cache):
    b = cache.shape[0]
    return jnp.transpose(cache.astype(F32).reshape(b, WINDOW, KV_W), (0, 2, 1))


def _window_from_kernel_layout(win_t):
    b = win_t.shape[0]
    return jnp.transpose(win_t, (0, 2, 1)).reshape(1, b, WINDOW, KV_HEADS, HEAD_DIM)


def kernel(x_prompt, x_sample, state_conv, state_ssm, cache_k, cache_v, w_in, conv_w, conv_b, dt_bias, a_log,
           d_skip, ssm_norm_w, attn_sinks, w_out, ln1_g, ln1_b, w_gate, w_up, w_down, ln2_g, ln2_b):
    w_in_r = _prepare_w_in(jnp.transpose(w_in[0]))
    rep = LANES // SSM_HEADS
    bias_tables, sink_rows = _attention_tables(attn_sinks[0])
    vec_parts = {"conv_b": conv_b[0], "dt_bias": jnp.tile(dt_bias[0], rep), "a_log": jnp.tile(a_log[0], rep),
                 "d_skip": jnp.repeat(d_skip[0], HEAD_DIM), "norm_w": ssm_norm_w[0],
                 "sink_a": sink_rows[0], "sink_b": sink_rows[1],
                 "ln1_g": ln1_g[0], "ln1_b": ln1_b[0], "ln2_g": ln2_g[0], "ln2_b": ln2_b[0]}
    vec_parts.update({"conv_w%d" % i: conv_w[0, i] for i in range(CONV_W)})
    vecs = jnp.concatenate([vec_parts[name].astype(F32).reshape(width) for name, width in VEC_LAYOUT])
    vecs = vecs.reshape(1, VEC_WIDTH)
    mixer_ops = (w_in_r, vecs, _expansion_matrix(), bias_tables, w_out[0].astype(BF16))
    ffn_mats = (w_gate[0].astype(BF16), w_up[0].astype(BF16), w_down[0].astype(BF16))

    dbsz, dseq, _ = x_sample.shape

    y_p, conv_p, ssm_p, k_p, v_p = _prompt_call(x_prompt, mixer_ops, ffn_mats)

    sample_state = (state_conv[0].astype(F32), state_ssm[0].astype(F32).reshape(dbsz, SSM_WIDTH, D_STATE),
                    _window_to_kernel_layout(cache_k[0]), _window_to_kernel_layout(cache_v[0]))
    h_s, conv_s, ssm_s, k_s, v_s = _sample_mixer_call(x_sample, sample_state, mixer_ops)
    y_s = _ffn_call(h_s.reshape(dbsz * dseq, D_MODEL), (vecs,) + ffn_mats).reshape(dbsz, dseq, D_MODEL)

    ssm_shape = lambda a: a.reshape(1, a.shape[0], SSM_HEADS, HEAD_DIM, D_STATE)
    kv_shape = _window_from_kernel_layout
    return (y_p, y_s,
            conv_p[None], ssm_shape(ssm_p), kv_shape(k_p), kv_shape(v_p),
            conv_s[None], ssm_shape(ssm_s), kv_shape(k_s), kv_shape(v_s))
```

```python
import functools
import math

import jax
import jax.numpy as jnp
import numpy as np
from jax import lax
from jax.experimental import pallas as pl
from jax.experimental.pallas import tpu as pltpu

D_MODEL = 1024
CHUNK = 64
HEAD_DIM = 64
SSM_WIDTH = 512
SSM_HEADS = 8
SSM_GROUPS = 2
D_STATE = 128
CONV_W = 4
CONV_DIM = SSM_WIDTH + 2 * SSM_GROUPS * D_STATE
ATTN_WIDTH = 512
ATTN_HEADS = 8
KV_HEADS = 2
WINDOW = 128
D_FF = 2816
LN_EPS = 1e-5
RMS_EPS = 1e-5
DEEPNORM_ALPHA = 2.0 ** 0.25
LOG2E = math.log2(math.e)

LANES = 128
SUBLANES = 8
GROUP_W = SSM_WIDTH // SSM_GROUPS
KV_W = KV_HEADS * HEAD_DIM
BAND = WINDOW + CHUNK
HIST_CHUNKS = WINDOW // CHUNK
HEADS_PER_STACK = 4
STACK_ROWS = HEADS_PER_STACK * CHUNK
COL_Z = 0
COL_XBC = COL_Z + SSM_WIDTH
COL_Q = COL_XBC + CONV_DIM
COL_K = COL_Q + ATTN_WIDTH
COL_V = COL_K + KV_W
COL_END = COL_V + KV_W
STACK_A_HEADS = (0, 2, 5, 7)
STACK_B_HEADS = (1, 3, 4, 6)
EXPAND_K = 128
CONV_PAD = SUBLANES

PROMPT_TILE = 256
FFN_CHUNK = 256
N_FFN_CHUNKS = D_FF // FFN_CHUNK
VMEM_LIMIT = 56 * 1024 * 1024
UNITS_AT_START = 1
UNITS_AFTER_FIRST_MATMULS = 3
UNITS_AFTER_VECTOR_WORK = 2
UNITS_AFTER_SECOND_MATMULS = 1

F32 = jnp.float32
BF16 = jnp.bfloat16


def _dot(a, b):
    return jnp.dot(a, b, preferred_element_type=F32)


def _dot_nt(a, b):
    return lax.dot_general(a, b, (((1,), (1,)), ((), ())), preferred_element_type=F32)


def _sigmoid(x):
    return 1.0 / (1.0 + jnp.exp2(x * -LOG2E))


def _silu(x):
    return x * _sigmoid(x)


def _layer_norm(x, g, b):
    mu = jnp.mean(x, axis=-1, keepdims=True)
    xc = x - mu
    var = jnp.mean(xc * xc, axis=-1, keepdims=True)
    return xc * lax.rsqrt(var + LN_EPS) * g + b


def _split3(x):
    hi = x.astype(BF16).astype(F32)
    r = x - hi
    mid = r.astype(BF16).astype(F32)
    lo = r - mid
    return hi, mid, lo


CONV_HIST = slice(CONV_PAD - (CONV_W - 1), CONV_PAD)


def _init_stream_state(ssm0_ref, k0_ref, v0_ref, st_ref, kf_ref, vf_ref, kb_ref, ksb_ref, vb_ref):
    st_ref[...] = ssm0_ref[0].T
    k0 = k0_ref[0].T
    v0 = v0_ref[0].T
    kf_ref[0:WINDOW, :] = k0
    vf_ref[0:WINDOW, :] = v0
    kb_ref[0:WINDOW, :] = k0.astype(BF16)
    vb_ref[0:WINDOW, :] = v0.astype(BF16)
    ksb_ref[0:WINDOW, :] = pltpu.roll(k0, HEAD_DIM, axis=1).astype(BF16)


def _write_stream_state(ssm_out_ref, k_out_ref, v_out_ref, st_ref, kf_ref, vf_ref):
    ssm_out_ref[0] = st_ref[...].T
    k_out_ref[0] = kf_ref[0:WINDOW, :].T
    v_out_ref[0] = vf_ref[0:WINDOW, :].T


def _in_projection(tile, xb_ref, w_in_ref, w_dt_ref,
                   convbuf, kf_ref, vf_ref, kb_ref, ksb_ref, vb_ref, zbuf, qbuf, dtbuf):
    xb = xb_ref[...]
    zbuf[...] = _dot(xb, w_in_ref[:, COL_Z:COL_XBC])
    convbuf[CONV_PAD:CONV_PAD + tile, :] = _dot(xb, w_in_ref[:, COL_XBC:COL_Q])
    qbuf[...] = _dot(xb, w_in_ref[:, COL_Q:COL_K])
    kv = _dot(xb, w_in_ref[:, COL_K:COL_END])
    dtbuf[...] = _dot(xb, w_dt_ref[...])
    k_new = kv[:, :KV_W]
    v_new = kv[:, KV_W:]
    kf_ref[WINDOW:WINDOW + tile, :] = k_new
    vf_ref[WINDOW:WINDOW + tile, :] = v_new
    kb_ref[WINDOW:WINDOW + tile, :] = k_new.astype(BF16)
    vb_ref[WINDOW:WINDOW + tile, :] = v_new.astype(BF16)
    ksb_ref[WINDOW:WINDOW + tile, :] = pltpu.roll(k_new, HEAD_DIM, axis=1).astype(BF16)


def _conv_and_decay(ci, conv_w_ref, conv_b_ref, dt_bias_ref, a_log_ref, convbuf, dtbuf, xsbuf, bcbuf):
    r0 = ci * CHUNK
    rows = slice(r0, r0 + CHUNK)

    window = convbuf[r0:r0 + CONV_PAD + CHUNK, :]
    conv = conv_b_ref[...] + conv_w_ref[CONV_W - 1][...] * window[CONV_PAD:]
    for back in range(1, CONV_W):
        conv = conv + conv_w_ref[CONV_W - 1 - back][...] * pltpu.roll(window, back, axis=0)[CONV_PAD:]
    xc = _silu(conv)
    xsbuf[rows, :] = xc[:, :SSM_WIDTH]
    bcbuf[rows, :] = xc[:, SSM_WIDTH:].astype(BF16)

    row_in_chunk = lax.broadcasted_iota(jnp.int32, (CHUNK, LANES), 0)
    lane = lax.broadcasted_iota(jnp.int32, (CHUNK, LANES), 1)
    dt_in = dtbuf[rows, :] + dt_bias_ref[...]
    dt = jnp.maximum(dt_in, 0.0) + jnp.log1p(jnp.exp(-jnp.abs(dt_in)))
    cum = dt * (-LOG2E * jnp.exp(a_log_ref[...]))
    step = 1
    while step < CHUNK:
        cum = cum + jnp.where(row_in_chunk >= step, pltpu.roll(cum, step, axis=0), 0.0)
        step *= 2
    packed = jnp.zeros((CHUNK, LANES), F32)
    for i, part in enumerate(_split3(cum) + _split3(dt)):
        packed = jnp.where((lane >= SSM_HEADS * i) & (lane < SSM_HEADS * (i + 1)), part, packed)
    return packed.astype(BF16)


def _prepare_tile(tile, x_ref, mixer_w, mixer_s):
    (w_in_ref, w_dt_ref, conv_w_ref, conv_b_ref, dt_bias_ref, a_log_ref) = mixer_w[:6]
    (convbuf, _, kf_ref, vf_ref, kb_ref, ksb_ref, vb_ref, _, _, zbuf, qbuf, dtbuf, xsbuf, bcbuf, _, _,
     xb_ref, packbuf, _) = mixer_s
    if x_ref is not None:
        xb_ref[...] = x_ref[0].astype(BF16)
    _in_projection(tile, xb_ref, w_in_ref, w_dt_ref,
                   convbuf, kf_ref, vf_ref, kb_ref, ksb_ref, vb_ref, zbuf, qbuf, dtbuf)
    packbuf[...] = _conv_and_decay(0, conv_w_ref, conv_b_ref, dt_bias_ref, a_log_ref, convbuf, dtbuf, xsbuf, bcbuf)


def _mixer_tile(hist_all_valid, tile, first_chunk, x_ref, x_next_ref, switch_stream, mixer_w, mixer_s):
    (_, _, conv_w_ref, conv_b_ref, dt_bias_ref, a_log_ref, d_ref, norm_w_ref,
     expand_ref, bias_ref, sink_ref, w_out_ref, ln_g_ref, ln_b_ref) = mixer_w
    (convbuf, st_ref, kf_ref, vf_ref, kb_ref, ksb_ref, vb_ref, vt_even, mixed_ref,
     zbuf, qbuf, dtbuf, xsbuf, bcbuf, cumx_ref, xdt_ref, xb_ref, packbuf, vt_odd) = mixer_s
    n_chunks = tile // CHUNK
    if x_next_ref is not None:
        xb_ref[...] = x_next_ref[0].astype(BF16)
    vt_even[...] = vb_ref[...].T
    if n_chunks > 1:
        vt_odd[...] = vb_ref[CHUNK:CHUNK + vt_odd.shape[1], :].T
    yield UNITS_AT_START

    lane = lax.broadcasted_iota(jnp.int32, (CHUNK, LANES), 1)
    low_half = lane < HEAD_DIM
    li = lax.broadcasted_iota(jnp.int32, (CHUNK, GROUP_W), 0)
    si = lax.broadcasted_iota(jnp.int32, (CHUNK, GROUP_W), 1) % CHUNK
    diag_mask = li == si
    causal_mask = si <= li
    bd_mask = (lax.broadcasted_iota(jnp.int32, (GROUP_W, GROUP_W), 0) // CHUNK
               == lax.broadcasted_iota(jnp.int32, (GROUP_W, GROUP_W), 1) // CHUNK)

    def conv_and_decay(ci):
        return _conv_and_decay(ci, conv_w_ref, conv_b_ref, dt_bias_ref, a_log_ref, convbuf, dtbuf, xsbuf, bcbuf)

    packed_next = packbuf[...]
    for ci in range(n_chunks):
        r0 = ci * CHUNK
        rows = slice(r0, r0 + CHUNK)
        packed = packed_next

        expanded = _dot(packed, expand_ref[...])
        if hist_all_valid:
            variant = HIST_CHUNKS
        else:
            variant = jnp.minimum(first_chunk + ci, HIST_CHUNKS)
        group_cols = [slice(g * GROUP_W, (g + 1) * GROUP_W) for g in range(SSM_GROUPS)]
        b_gs = [bcbuf[rows, g * D_STATE:(g + 1) * D_STATE] for g in range(SSM_GROUPS)]
        c_gs = [bcbuf[rows, GROUP_W + g * D_STATE:GROUP_W + (g + 1) * D_STATE] for g in range(SSM_GROUPS)]
        cbs = [_dot_nt(c_gs[g], jnp.concatenate([b_gs[g]] * HEADS_PER_STACK, axis=0)) for g in range(SSM_GROUPS)]
        y_offs = [_dot(c_gs[g], st_ref[:, group_cols[g]].astype(BF16)) for g in range(SSM_GROUPS)]
        blocks = [qbuf[rows, j * LANES:(j + 1) * LANES] for j in range(ATTN_HEADS // 2)]
        evens = [jnp.where(low_half, blk, 0.0).astype(BF16) for blk in blocks]
        odds = [jnp.where(low_half, 0.0, blk).astype(BF16) for blk in blocks]
        q_a = jnp.concatenate([evens[0], evens[1], odds[2], odds[3]], axis=0)
        q_b = jnp.concatenate([odds[0], odds[1], evens[2], evens[3]], axis=0)
        scores = [_dot_nt(k_ref[r0:r0 + BAND, :], q_s) for q_s, k_ref in ((q_a, kb_ref), (q_b, ksb_ref))]
        yield UNITS_AFTER_FIRST_MATMULS

        if ci + 1 < n_chunks:
            packed_next = conv_and_decay(ci + 1)
        cumx_ref[...] = expanded[:, :SSM_WIDTH]
        xdt_ref[...] = xsbuf[rows, :] * expanded[:, SSM_WIDTH:]
        m_gs, bds, xws, probs = [], [], [], []
        for g in range(SSM_GROUPS):
            colb = cumx_ref[:, group_cols[g]]
            rowb = jnp.sum(jnp.where(diag_mask, colb, 0.0), axis=0, keepdims=True)
            lmat = jnp.exp2(jnp.where(causal_mask, colb - rowb, -jnp.inf))
            m_gs.append((cbs[g] * lmat).astype(BF16))
            xdt_g = xdt_ref[:, group_cols[g]]
            xdt_b = xdt_g.astype(BF16)
            bds.append(jnp.where(bd_mask, jnp.concatenate([xdt_b] * HEADS_PER_STACK, axis=0), jnp.zeros((), BF16)))
            last = colb[CHUNK - 1:CHUNK, :]
            xws.append((xdt_g * jnp.exp2(last - colb)).astype(BF16))
        for s_idx in range(2):
            s = scores[s_idx] + bias_ref[variant, s_idx]
            sink = sink_ref[s_idx][...]
            m = jnp.maximum(jnp.max(s, axis=0, keepdims=True), sink)
            p = jnp.exp2(s - m)
            denom = jnp.sum(p, axis=0, keepdims=True) + jnp.exp2(sink - m)
            probs.append((p * (1.0 / denom)).astype(BF16))
        b_ts = [b_g.T for b_g in b_gs]
        yield UNITS_AFTER_VECTOR_WORK

        y_diags = [_dot(m_gs[g], bds[g]) for g in range(SSM_GROUPS)]
        updates = [_dot(b_ts[g], xws[g]) for g in range(SSM_GROUPS)]
        vt_src, c0 = (vt_even, r0) if ci % 2 == 0 else (vt_odd, r0 - CHUNK)
        vband_t = vt_src[:, c0:c0 + BAND]
        vband_t_swapped = jnp.concatenate([vband_t[HEAD_DIM:], vband_t[:HEAD_DIM]], axis=0)
        outs_t = [_dot(vband_t, probs[0]), _dot(vband_t_swapped, probs[1])]
        yield UNITS_AFTER_SECOND_MATMULS

        for g in range(SSM_GROUPS):
            cols = group_cols[g]
            colb = cumx_ref[:, cols]
            last = colb[CHUNK - 1:CHUNK, :]
            st_ref[:, cols] = jnp.exp2(last) * st_ref[:, cols] + updates[g]
            y = y_diags[g] + y_offs[g] * jnp.exp2(colb) + d_ref[:, cols] * xsbuf[rows, cols]
            gg = y * _silu(zbuf[rows, cols])
            ms = jnp.mean(gg * gg, axis=-1, keepdims=True)
            mixed_ref[rows, cols] = (gg * lax.rsqrt(ms + RMS_EPS) * norm_w_ref[:, cols]).astype(BF16)
        o_a, o_b = [o_t.T for o_t in outs_t]
        sel = [(o_a, o_b), (o_a, o_b), (o_b, o_a), (o_b, o_a)]
        for j in range(ATTN_HEADS // 2):
            ev, od = sel[j]
            blk = jnp.where(low_half, ev[j * CHUNK:(j + 1) * CHUNK], od[j * CHUNK:(j + 1) * CHUNK])
            mixed_ref[rows, SSM_WIDTH + j * LANES:SSM_WIDTH + (j + 1) * LANES] = blk.astype(BF16)

    convbuf[CONV_HIST, :] = convbuf[CONV_PAD + tile - (CONV_W - 1):CONV_PAD + tile, :]
    for ref in (kf_ref, vf_ref, kb_ref, ksb_ref, vb_ref):
        ref[0:WINDOW, :] = ref[tile:tile + WINDOW, :]

    yield
    if x_ref is None:
        return None

    mix = _dot(mixed_ref[...], w_out_ref[...])
    if x_next_ref is not None:
        switch_stream()
        _prepare_tile(tile, None, mixer_w, mixer_s)
    return _layer_norm(DEEPNORM_ALPHA * x_ref[0] + mix, ln_g_ref[...], ln_b_ref[...])


FFN_DOWN_AFTER = (7, N_FFN_CHUNKS)
N_FFN_UNITS = 2 * N_FFN_CHUNKS + len(FFN_DOWN_AFTER) * (D_MODEL // FFN_CHUNK)


def _ffn_tile(h_ref, hb_ref, act_ref, w_gate_ref, w_up_ref, w_down_ref, ln_g_ref, ln_b_ref):
    out_cols = [slice(n * FFN_CHUNK, (n + 1) * FFN_CHUNK) for n in range(D_MODEL // FFN_CHUNK)]
    pre = [DEEPNORM_ALPHA * h_ref[:, cols] for cols in out_cols]
    done = 0
    for c in range(N_FFN_CHUNKS):
        cols = slice(c * FFN_CHUNK, (c + 1) * FFN_CHUNK)
        gate = _dot(hb_ref[...], w_gate_ref[:, cols])
        yield
        up = _dot(hb_ref[...], w_up_ref[:, cols])
        act_ref[:, cols] = (_silu(gate) * up).astype(BF16)
        yield
        if c + 1 in FFN_DOWN_AFTER:
            k_rows = slice(done * FFN_CHUNK, (c + 1) * FFN_CHUNK)
            done = c + 1
            for n, cols in enumerate(out_cols):
                pre[n] = pre[n] + _dot(act_ref[:, k_rows], w_down_ref[k_rows, cols])
                yield
    return _layer_norm(jnp.concatenate(pre, axis=1), ln_g_ref[...], ln_b_ref[...])


def _run(gen):
    while True:
        try:
            next(gen)
        except StopIteration as stop:
            return stop.value


def _interleave(mixer, ffn):
    remaining = N_FFN_UNITS
    while True:
        try:
            wanted = next(mixer)
        except StopIteration as stop:
            return stop.value, y
        last = wanted is None
        for _ in range(remaining if last else min(wanted, remaining)):
            next(ffn)
            remaining -= 1
        if last:
            y = _run(ffn)


N_MIXER_SCRATCH = 19

VEC_LAYOUT = (tuple(("conv_w%d" % i, CONV_DIM) for i in range(CONV_W))
              + (("conv_b", CONV_DIM), ("dt_bias", LANES), ("a_log", LANES), ("d_skip", SSM_WIDTH),
                 ("norm_w", SSM_WIDTH), ("sink_a", STACK_ROWS), ("sink_b", STACK_ROWS),
                 ("ln1_g", D_MODEL), ("ln1_b", D_MODEL), ("ln2_g", D_MODEL), ("ln2_b", D_MODEL)))
VEC_WIDTH = sum(width for _, width in VEC_LAYOUT)


def _vec_views(vec_ref):
    views, offset = {}, 0
    for name, width in VEC_LAYOUT:
        views[name] = vec_ref.at[:, offset:offset + width]
        offset += width
    return views


def _mixer_weights(w_in_ref, vec_ref, expand_ref, bias_ref, w_out_ref):
    v = _vec_views(vec_ref)
    return (w_in_ref.at[:, :COL_END], w_in_ref.at[:, COL_END:COL_END + LANES],
            tuple(v["conv_w%d" % i] for i in range(CONV_W)), v["conv_b"], v["dt_bias"], v["a_log"], v["d_skip"],
            v["norm_w"], expand_ref, bias_ref, (v["sink_a"], v["sink_b"]), w_out_ref, v["ln1_g"], v["ln1_b"])


def _ffn_weights(vec_ref, w_gate_ref, w_up_ref, w_down_ref):
    v = _vec_views(vec_ref)
    return (w_gate_ref, w_up_ref, w_down_ref, v["ln2_g"], v["ln2_b"])


def _fused_kernel(tile, n_tiles, n_total,
                  x_ref, x_next_ref, w_in_ref, vec_ref, expand_ref, bias_ref, w_out_ref, w_gate_ref, w_up_ref, w_down_ref,
                  y_ref, conv_out_ref, ssm_out_ref, k_out_ref, v_out_ref, *scratch):
    mixer_w = _mixer_weights(w_in_ref, vec_ref, expand_ref, bias_ref, w_out_ref)
    ffn_w = _ffn_weights(vec_ref, w_gate_ref, w_up_ref, w_down_ref)
    mixer_s = scratch[:N_MIXER_SCRATCH]
    h_s, hb_s, act_s = scratch[N_MIXER_SCRATCH:]
    convbuf, st_ref, kf_ref, vf_ref = mixer_s[:4]

    g = pl.program_id(0)
    t = lax.rem(jnp.minimum(g, n_total - 1), n_tiles)
    next_starts_stream = (g + 1 < n_total) & (lax.rem(g + 1, n_tiles) == 0)

    @pl.when(g == 0)
    def _():
        h_s[...] = jnp.zeros(h_s.shape, F32)
        hb_s[...] = jnp.zeros(hb_s.shape, BF16)
        convbuf[0:CONV_PAD, :] = jnp.zeros((CONV_PAD, CONV_DIM), F32)
        _prepare_tile(tile, x_ref, mixer_w, mixer_s)

    @pl.when(t == 0)
    def _():
        st_ref[...] = jnp.zeros(st_ref.shape, F32)
        for ref in mixer_s[2:7]:
            ref[0:WINDOW, :] = jnp.zeros((WINDOW, KV_W), ref.dtype)

    def switch_stream():
        hist = convbuf[CONV_HIST, :]
        conv_out_ref[0] = hist
        convbuf[CONV_HIST, :] = jnp.where(next_starts_stream, 0.0, hist)

    n_chunks = tile // CHUNK
    ffn = _ffn_tile(h_s, hb_s, act_s, *ffn_w)
    mixer = _mixer_tile(False, tile, t * n_chunks, x_ref, x_next_ref, switch_stream, mixer_w, mixer_s)
    h, y = _interleave(mixer, ffn)
    y_ref[0] = y
    h_s[...] = h
    hb_s[...] = h.astype(BF16)

    @pl.when((t == n_tiles - 1) & (g < n_total))
    def _():
        _write_stream_state(ssm_out_ref, k_out_ref, v_out_ref, st_ref, kf_ref, vf_ref)


def _sample_kernel(n_streams, seq, x_ref, conv0_ref, ssm0_ref, k0_ref, v0_ref,
                   w_in_ref, vec_ref, expand_ref, bias_ref, w_out_ref, w_gate_ref, w_up_ref, w_down_ref,
                   y_ref, conv_out_ref, ssm_out_ref, k_out_ref, v_out_ref, *scratch):
    mixer_w = _mixer_weights(w_in_ref, vec_ref, expand_ref, bias_ref, w_out_ref)
    ffn_w = _ffn_weights(vec_ref, w_gate_ref, w_up_ref, w_down_ref)
    (convbuf_a, kf_a, vf_a, kb_a, ksb_a, vb_a, mixed_a, zbuf_a, qbuf_a, dtbuf_a, xb_a, h_s, hb_s, act_s) = scratch[:14]
    one_s = scratch[14:]
    (convbuf, st_ref, kf_ref, vf_ref, kb_ref, ksb_ref, vb_ref, _, mixed_ref,
     zbuf, qbuf, dtbuf, xsbuf, bcbuf, _, _, _, packbuf, _) = one_s
    total = n_streams * seq
    part = min(total, PROMPT_TILE)
    b = pl.program_id(0)

    @pl.when(b == 0)
    def _():
        xb_a[...] = x_ref[...].astype(BF16)
        for r0 in range(0, total, part):
            rows, behind_conv, behind_kv = (pl.ds(r0, part), pl.ds(r0, CONV_PAD + part), pl.ds(r0, WINDOW + part))
            _in_projection(part, xb_a.at[rows], mixer_w[0], mixer_w[1], convbuf_a.at[behind_conv],
                           *(ref.at[behind_kv] for ref in (kf_a, vf_a, kb_a, ksb_a, vb_a)),
                           zbuf_a.at[rows], qbuf_a.at[rows], dtbuf_a.at[rows])

    rows = pl.ds(pl.multiple_of(b * seq, seq), seq)
    conv_rows = pl.ds(pl.multiple_of(CONV_PAD + b * seq, SUBLANES), seq)
    new_rows = pl.ds(pl.multiple_of(WINDOW + b * seq, seq), seq)
    convbuf[0:CONV_PAD, :] = jnp.zeros((CONV_PAD, CONV_DIM), F32)
    convbuf[CONV_HIST, :] = conv0_ref[0]
    convbuf[CONV_PAD:CONV_PAD + seq, :] = convbuf_a[conv_rows, :]
    zbuf[...] = zbuf_a[rows, :]
    qbuf[...] = qbuf_a[rows, :]
    dtbuf[...] = dtbuf_a[rows, :]
    _init_stream_state(ssm0_ref, k0_ref, v0_ref, *one_s[1:7])
    for dst, src in ((kf_ref, kf_a), (vf_ref, vf_a), (kb_ref, kb_a), (ksb_ref, ksb_a), (vb_ref, vb_a)):
        dst[WINDOW:WINDOW + seq, :] = src[new_rows, :]
    packbuf[...] = _conv_and_decay(0, *mixer_w[2:6], convbuf, dtbuf, xsbuf, bcbuf)
    _run(_mixer_tile(True, seq, 0, None, None, None, mixer_w, one_s))
    mixed_a[rows, :] = mixed_ref[...]
    conv_out_ref[0] = convbuf[CONV_HIST, :]
    _write_stream_state(ssm_out_ref, k_out_ref, v_out_ref, st_ref, kf_ref, vf_ref)

    @pl.when(b == n_streams - 1)
    def _():
        for r0 in range(0, total, part):
            out_rows = pl.ds(r0, part)
            mix = _dot(mixed_a[out_rows, :], w_out_ref[...])
            h = _layer_norm(DEEPNORM_ALPHA * x_ref[out_rows, :] + mix, mixer_w[12][...], mixer_w[13][...])
            h_s[out_rows, :] = h
            hb_s[out_rows, :] = h.astype(BF16)
        y_ref[...] = _run(_ffn_tile(h_s, hb_s, act_s, *ffn_w))


def _const_spec(shape):
    zeros = (0,) * len(shape)
    return pl.BlockSpec(shape, lambda *_: zeros, pipeline_mode=pl.Buffered(1))


def _mixer_scratch(tile):
    return [pltpu.VMEM((CONV_PAD + tile, CONV_DIM), F32),
            pltpu.VMEM((D_STATE, SSM_WIDTH), F32),
            pltpu.VMEM((WINDOW + tile, KV_W), F32), pltpu.VMEM((WINDOW + tile, KV_W), F32),
            pltpu.VMEM((WINDOW + tile, KV_W), BF16), pltpu.VMEM((WINDOW + tile, KV_W), BF16),
            pltpu.VMEM((WINDOW + tile, KV_W), BF16), pltpu.VMEM((KV_W, WINDOW + tile), BF16),
            pltpu.VMEM((tile, D_MODEL), BF16),
            pltpu.VMEM((tile, SSM_WIDTH), F32),
            pltpu.VMEM((tile, ATTN_WIDTH), F32),
            pltpu.VMEM((tile, LANES), F32),
            pltpu.VMEM((tile, SSM_WIDTH), F32),
            pltpu.VMEM((tile, CONV_DIM - SSM_WIDTH), BF16),
            pltpu.VMEM((CHUNK, SSM_WIDTH), F32),
            pltpu.VMEM((CHUNK, SSM_WIDTH), F32),
            pltpu.VMEM((tile, D_MODEL), BF16),
            pltpu.VMEM((CHUNK, LANES), BF16),
            pltpu.VMEM((KV_W, max(tile, LANES) + CHUNK), BF16)]


_STATE_SHAPES = ((CONV_W - 1, CONV_DIM), (SSM_WIDTH, D_STATE), (KV_W, WINDOW), (KV_W, WINDOW))


def _attention_tables(attn_sinks):
    slopes = np.exp2(-8.0 * np.arange(1, ATTN_HEADS + 1, dtype=np.float64) / ATTN_HEADS)
    qi = np.arange(CHUNK)[None, :] + WINDOW
    kj = np.arange(BAND)[:, None]
    dist = np.abs(qi - kj).astype(np.float64)
    key_chunk = kj // CHUNK
    tables = np.zeros((HIST_CHUNKS + 1, 2, BAND, STACK_ROWS), np.float32)
    for variant in range(HIST_CHUNKS + 1):
        valid = key_chunk >= (HIST_CHUNKS - variant)
        for s_idx, heads in enumerate((STACK_A_HEADS, STACK_B_HEADS)):
            for i, h in enumerate(heads):
                tables[variant, s_idx, :, i * CHUNK:(i + 1) * CHUNK] = np.where(
                    valid, -slopes[h] * dist * LOG2E, -np.inf)
    order = np.array(STACK_A_HEADS + STACK_B_HEADS)
    sink_rows = jnp.repeat(attn_sinks.astype(F32)[order] * LOG2E, CHUNK).reshape(2, STACK_ROWS)
    return jnp.asarray(tables), sink_rows


def _expansion_matrix():
    e = np.zeros((EXPAND_K, 2 * SSM_WIDTH), np.float32)
    for part in range(6):
        for h in range(SSM_HEADS):
            c0 = (part // 3) * SSM_WIDTH + h * HEAD_DIM
            e[part * SSM_HEADS + h, c0:c0 + HEAD_DIM] = 1.0
    return jnp.asarray(e, BF16)


def _prompt_call(x, mixer_ops, ffn_ops):
    bsz, seq, _ = x.shape
    tile = PROMPT_TILE
    n_tiles = seq // tile
    n_total = bsz * n_tiles

    def mixer_tile_idx(g):
        gc = jnp.minimum(g, n_total - 1)
        return lax.div(gc, n_tiles), lax.rem(gc, n_tiles)

    def next_tile_idx(g):
        return mixer_tile_idx(g + 1)

    def ffn_tile_idx(g):
        gp = jnp.maximum(g - 1, 0)
        return lax.div(gp, n_tiles), lax.rem(gp, n_tiles)

    tile_spec = lambda idx: pl.BlockSpec((1, tile, D_MODEL), lambda g: (*idx(g), 0))
    stream_spec = lambda shape: pl.BlockSpec((1,) + shape, lambda g: (mixer_tile_idx(g)[0], 0, 0))
    in_specs = ([tile_spec(mixer_tile_idx), tile_spec(next_tile_idx)]
                + [_const_spec(w.shape) for w in mixer_ops + ffn_ops])
    out_specs = [tile_spec(ffn_tile_idx)] + [stream_spec(s) for s in _STATE_SHAPES]
    out_shape = ([jax.ShapeDtypeStruct((bsz, seq, D_MODEL), F32)]
                 + [jax.ShapeDtypeStruct((bsz,) + s, F32) for s in _STATE_SHAPES])
    scratch = _mixer_scratch(tile) + [pltpu.VMEM((tile, D_MODEL), F32), pltpu.VMEM((tile, D_MODEL), BF16),
                                      pltpu.VMEM((tile, D_FF), BF16)]
    return pl.pallas_call(
        functools.partial(_fused_kernel, tile, n_tiles, n_total),
        grid=(n_total + 1,), in_specs=in_specs, out_specs=out_specs, out_shape=out_shape,
        scratch_shapes=scratch, name="prompt_layer",
        compiler_params=pltpu.CompilerParams(dimension_semantics=("arbitrary",), vmem_limit_bytes=VMEM_LIMIT),
    )(x, x, *mixer_ops, *ffn_ops)


def _sample_call(x, state0, mixer_ops, ffn_ops):
    n_streams, seq, _ = x.shape
    total = n_streams * seq
    stream_spec = lambda shape: pl.BlockSpec((1,) + shape, lambda b: (b, 0, 0))
    in_specs = ([_const_spec((total, D_MODEL))] + [stream_spec(s) for s in _STATE_SHAPES]
                + [_const_spec(w.shape) for w in mixer_ops + ffn_ops])
    out_specs = ([pl.BlockSpec((total, D_MODEL), lambda b: (0, 0))]
                 + [stream_spec(s) for s in _STATE_SHAPES])
    out_shape = ([jax.ShapeDtypeStruct((total, D_MODEL), F32)]
                 + [jax.ShapeDtypeStruct((n_streams,) + s, F32) for s in _STATE_SHAPES])
    all_rows = [pltpu.VMEM((CONV_PAD + total, CONV_DIM), F32),
                pltpu.VMEM((WINDOW + total, KV_W), F32), pltpu.VMEM((WINDOW + total, KV_W), F32),
                pltpu.VMEM((WINDOW + total, KV_W), BF16), pltpu.VMEM((WINDOW + total, KV_W), BF16),
                pltpu.VMEM((WINDOW + total, KV_W), BF16),
                pltpu.VMEM((total, D_MODEL), BF16),
                pltpu.VMEM((total, SSM_WIDTH), F32), pltpu.VMEM((total, ATTN_WIDTH), F32),
                pltpu.VMEM((total, LANES), F32),
                pltpu.VMEM((total, D_MODEL), BF16),
                pltpu.VMEM((total, D_MODEL), F32), pltpu.VMEM((total, D_MODEL), BF16),
                pltpu.VMEM((total, D_FF), BF16)]
    y, conv, ssm, k, v = pl.pallas_call(
        functools.partial(_sample_kernel, n_streams, seq),
        grid=(n_streams,), in_specs=in_specs, out_specs=out_specs, out_shape=out_shape,
        scratch_shapes=all_rows + _mixer_scratch(seq), name="sample_layer",
        compiler_params=pltpu.CompilerParams(dimension_semantics=("arbitrary",), vmem_limit_bytes=VMEM_LIMIT),
    )(x.reshape(total, D_MODEL), *state0, *mixer_ops, *ffn_ops)
    return y.reshape(n_streams, seq, D_MODEL), conv, ssm, k, v


IN_COLS = COL_END + 8
W_IN_PREP_ROWS = 256


def _w_in_kernel(w_t_ref, out_ref):
    src = {"z": 0, "xbc": SSM_WIDTH, "dt": SSM_WIDTH + CONV_DIM}
    src["q"] = src["dt"] + SSM_HEADS
    src["k"] = src["q"] + ATTN_WIDTH
    src["v"] = src["k"] + KV_W
    pieces = (("z", COL_Z, COL_XBC - COL_Z, 1.0), ("xbc", COL_XBC, COL_Q - COL_XBC, 1.0),
              ("q", COL_Q, COL_K - COL_Q, HEAD_DIM ** -0.5 * LOG2E), ("k", COL_K, KV_W, 1.0), ("v", COL_V, KV_W, 1.0))
    for name, dst, width, scale in pieces:
        for off in range(0, width, W_IN_PREP_ROWS):
            rows = min(W_IN_PREP_ROWS, width - off)
            blk = w_t_ref[src[name] + off:src[name] + off + rows, :]
            if scale != 1.0:
                blk = blk * scale
            out_ref[:, dst + off:dst + off + rows] = blk.astype(BF16).T
    w_dt = w_t_ref[src["dt"]:src["dt"] + SSM_HEADS, :]
    w_dt_rep = jnp.concatenate([w_dt] * (LANES // SSM_HEADS), axis=0)
    out_ref[:, COL_END:COL_END + LANES] = w_dt_rep.astype(BF16).T


def _prepare_w_in(w_t):
    return pl.pallas_call(
        _w_in_kernel, grid=(1,),
        in_specs=[_const_spec((IN_COLS, D_MODEL))], out_specs=pl.BlockSpec((D_MODEL, COL_END + LANES), lambda i: (0, 0)),
        out_shape=jax.ShapeDtypeStruct((D_MODEL, COL_END + LANES), BF16), name="w_in_prep",
        compiler_params=pltpu.CompilerParams(dimension_semantics=("arbitrary",), vmem_limit_bytes=VMEM_LIMIT),
    )(w_t)


def _window_to_kernel_layout(cache):
    b = cache.shape[0]
    return jnp.transpose(cache.astype(F32).reshape(b, WINDOW, KV_W), (0, 2, 1))


def _window_from_kernel_layout(win_t):
    b = win_t.shape[0]
    return jnp.transpose(win_t, (0, 2, 1)).reshape(1, b, WINDOW, KV_HEADS, HEAD_DIM)


def kernel(x_prompt, x_sample, state_conv, state_ssm, cache_k, cache_v, w_in, conv_w, conv_b, dt_bias, a_log,
           d_skip, ssm_norm_w, attn_sinks, w_out, ln1_g, ln1_b, w_gate, w_up, w_down, ln2_g, ln2_b):
    w_in_r = _prepare_w_in(jnp.transpose(w_in[0]))
    rep = LANES // SSM_HEADS
    bias_tables, sink_rows = _attention_tables(attn_sinks[0])
    vec_parts = {"conv_b": conv_b[0], "dt_bias": jnp.tile(dt_bias[0], rep), "a_log": jnp.tile(a_log[0], rep),
                 "d_skip": jnp.repeat(d_skip[0], HEAD_DIM), "norm_w": ssm_norm_w[0],
                 "sink_a": sink_rows[0], "sink_b": sink_rows[1],
                 "ln1_g": ln1_g[0], "ln1_b": ln1_b[0], "ln2_g": ln2_g[0], "ln2_b": ln2_b[0]}
    vec_parts.update({"conv_w%d" % i: conv_w[0, i] for i in range(CONV_W)})
    vecs = jnp.concatenate([vec_parts[name].astype(F32).reshape(width) for name, width in VEC_LAYOUT])
    vecs = vecs.reshape(1, VEC_WIDTH)
    mixer_ops = (w_in_r, vecs, _expansion_matrix(), bias_tables, w_out[0].astype(BF16))
    ffn_mats = (w_gate[0].astype(BF16), w_up[0].astype(BF16), w_down[0].astype(BF16))

    dbsz = x_sample.shape[0]

    y_p, conv_p, ssm_p, k_p, v_p = _prompt_call(x_prompt, mixer_ops, ffn_mats)

    sample_state = (state_conv[0].astype(F32), state_ssm[0].astype(F32).reshape(dbsz, SSM_WIDTH, D_STATE),
                    _window_to_kernel_layout(cache_k[0]), _window_to_kernel_layout(cache_v[0]))
    y_s, conv_s, ssm_s, k_s, v_s = _sample_call(x_sample, sample_state, mixer_ops, ffn_mats)

    ssm_shape = lambda a: a.reshape(1, a.shape[0], SSM_HEADS, HEAD_DIM, D_STATE)
    kv_shape = _window_from_kernel_layout
    return (y_p, y_s,
            conv_p[None], ssm_shape(ssm_p), kv_shape(k_p), kv_shape(v_p),
            conv_s[None], ssm_shape(ssm_s), kv_shape(k_s), kv_shape(v_s))
```

```python
import functools
import math

import jax
import jax.numpy as jnp
import numpy as np
from jax import lax
from jax.experimental import pallas as pl
from jax.experimental.pallas import tpu as pltpu

D_MODEL = 1024
CHUNK = 64
HEAD_DIM = 64
SSM_WIDTH = 512
SSM_HEADS = 8
SSM_GROUPS = 2
D_STATE = 128
CONV_W = 4
CONV_DIM = SSM_WIDTH + 2 * SSM_GROUPS * D_STATE
ATTN_WIDTH = 512
ATTN_HEADS = 8
KV_HEADS = 2
WINDOW = 128
D_FF = 2816
LN_EPS = 1e-5
RMS_EPS = 1e-5
DEEPNORM_ALPHA = 2.0 ** 0.25
LOG2E = math.log2(math.e)

LANES = 128
SUBLANES = 8
GROUP_W = SSM_WIDTH // SSM_GROUPS
KV_W = KV_HEADS * HEAD_DIM
BAND = WINDOW + CHUNK
HIST_CHUNKS = WINDOW // CHUNK
HEADS_PER_STACK = 4
STACK_ROWS = HEADS_PER_STACK * CHUNK
COL_Z = 0
COL_XBC = COL_Z + SSM_WIDTH
COL_Q = COL_XBC + CONV_DIM
COL_K = COL_Q + ATTN_WIDTH
COL_V = COL_K + KV_W
COL_END = COL_V + KV_W
STACK_A_HEADS = (0, 2, 5, 7)
STACK_B_HEADS = (1, 3, 4, 6)
EXPAND_K = 128
CONV_PAD = SUBLANES

PROMPT_TILE = 256
FFN_CHUNK = 256
N_FFN_CHUNKS = D_FF // FFN_CHUNK
VMEM_LIMIT = 56 * 1024 * 1024
UNITS_AT_START = 1
UNITS_AFTER_FIRST_MATMULS = 3
UNITS_AFTER_VECTOR_WORK = 2
UNITS_AFTER_SECOND_MATMULS = 1

F32 = jnp.float32
BF16 = jnp.bfloat16


def _dot(a, b):
    return jnp.dot(a, b, preferred_element_type=F32)


def _dot_nt(a, b):
    return lax.dot_general(a, b, (((1,), (1,)), ((), ())), preferred_element_type=F32)


def _sigmoid(x):
    return 1.0 / (1.0 + jnp.exp2(x * -LOG2E))


def _silu(x):
    return x * _sigmoid(x)


def _layer_norm(x, g, b):
    mu = jnp.mean(x, axis=-1, keepdims=True)
    xc = x - mu
    var = jnp.mean(xc * xc, axis=-1, keepdims=True)
    return xc * lax.rsqrt(var + LN_EPS) * g + b


def _split3(x):
    hi = x.astype(BF16).astype(F32)
    r = x - hi
    mid = r.astype(BF16).astype(F32)
    lo = r - mid
    return hi, mid, lo


CONV_HIST = slice(CONV_PAD - (CONV_W - 1), CONV_PAD)


def _init_stream_state(ssm0_ref, k0_ref, v0_ref, st_ref, kf_ref, vf_ref, kb_ref, ksb_ref, vb_ref):
    st_ref[...] = ssm0_ref[0].T
    k0 = k0_ref[0].T
    v0 = v0_ref[0].T
    kf_ref[0:WINDOW, :] = k0
    vf_ref[0:WINDOW, :] = v0
    kb_ref[0:WINDOW, :] = k0.astype(BF16)
    vb_ref[0:WINDOW, :] = v0.astype(BF16)
    ksb_ref[0:WINDOW, :] = pltpu.roll(k0, HEAD_DIM, axis=1).astype(BF16)


def _write_stream_state(ssm_out_ref, k_out_ref, v_out_ref, st_ref, kf_ref, vf_ref):
    ssm_out_ref[0] = st_ref[...].T
    k_out_ref[0] = kf_ref[0:WINDOW, :].T
    v_out_ref[0] = vf_ref[0:WINDOW, :].T


def _in_projection(tile, xb_ref, w_in_ref, w_dt_ref,
                   convbuf, kf_ref, vf_ref, kb_ref, ksb_ref, vb_ref, zbuf, qbuf, dtbuf):
    xb = xb_ref[...]
    zbuf[...] = _dot(xb, w_in_ref[:, COL_Z:COL_XBC])
    convbuf[CONV_PAD:CONV_PAD + tile, :] = _dot(xb, w_in_ref[:, COL_XBC:COL_Q])
    qbuf[...] = _dot(xb, w_in_ref[:, COL_Q:COL_K])
    kv = _dot(xb, w_in_ref[:, COL_K:COL_END])
    dtbuf[...] = _dot(xb, w_dt_ref[...])
    k_new = kv[:, :KV_W]
    v_new = kv[:, KV_W:]
    kf_ref[WINDOW:WINDOW + tile, :] = k_new
    vf_ref[WINDOW:WINDOW + tile, :] = v_new
    kb_ref[WINDOW:WINDOW + tile, :] = k_new.astype(BF16)
    vb_ref[WINDOW:WINDOW + tile, :] = v_new.astype(BF16)
    ksb_ref[WINDOW:WINDOW + tile, :] = pltpu.roll(k_new, HEAD_DIM, axis=1).astype(BF16)


def _conv_and_decay(ci, conv_w_ref, conv_b_ref, dt_bias_ref, a_log_ref, convbuf, dtbuf, xsbuf, bcbuf):
    r0 = ci * CHUNK
    rows = slice(r0, r0 + CHUNK)

    window = convbuf[r0:r0 + CONV_PAD + CHUNK, :]
    conv = conv_b_ref[...] + conv_w_ref[CONV_W - 1][...] * window[CONV_PAD:]
    for back in range(1, CONV_W):
        conv = conv + conv_w_ref[CONV_W - 1 - back][...] * pltpu.roll(window, back, axis=0)[CONV_PAD:]
    xc = _silu(conv)
    xsbuf[rows, :] = xc[:, :SSM_WIDTH]
    bcbuf[rows, :] = xc[:, SSM_WIDTH:].astype(BF16)

    row_in_chunk = lax.broadcasted_iota(jnp.int32, (CHUNK, LANES), 0)
    lane = lax.broadcasted_iota(jnp.int32, (CHUNK, LANES), 1)
    dt_in = dtbuf[rows, :] + dt_bias_ref[...]
    dt = jnp.maximum(dt_in, 0.0) + jnp.log1p(jnp.exp(-jnp.abs(dt_in)))
    cum = dt * (-LOG2E * jnp.exp(a_log_ref[...]))
    step = 1
    while step < CHUNK:
        cum = cum + jnp.where(row_in_chunk >= step, pltpu.roll(cum, step, axis=0), 0.0)
        step *= 2
    packed = jnp.zeros((CHUNK, LANES), F32)
    for i, part in enumerate(_split3(cum) + _split3(dt)):
        packed = jnp.where((lane >= SSM_HEADS * i) & (lane < SSM_HEADS * (i + 1)), part, packed)
    return packed.astype(BF16)


def _prepare_tile(tile, x_ref, mixer_w, mixer_s):
    (w_in_ref, w_dt_ref, conv_w_ref, conv_b_ref, dt_bias_ref, a_log_ref) = mixer_w[:6]
    (convbuf, _, kf_ref, vf_ref, kb_ref, ksb_ref, vb_ref, _, _, zbuf, qbuf, dtbuf, xsbuf, bcbuf, _, _,
     xb_ref, packbuf, _) = mixer_s
    if x_ref is not None:
        xb_ref[...] = x_ref[0].astype(BF16)
    _in_projection(tile, xb_ref, w_in_ref, w_dt_ref,
                   convbuf, kf_ref, vf_ref, kb_ref, ksb_ref, vb_ref, zbuf, qbuf, dtbuf)
    packbuf[...] = _conv_and_decay(0, conv_w_ref, conv_b_ref, dt_bias_ref, a_log_ref, convbuf, dtbuf, xsbuf, bcbuf)


def _mixer_tile(hist_all_valid, tile, first_chunk, x_ref, x_next_ref, switch_stream, mixer_w, mixer_s):
    (_, _, conv_w_ref, conv_b_ref, dt_bias_ref, a_log_ref, d_ref, norm_w_ref,
     expand_ref, bias_ref, sink_ref, w_out_ref, ln_g_ref, ln_b_ref) = mixer_w
    (convbuf, st_ref, kf_ref, vf_ref, kb_ref, ksb_ref, vb_ref, vt_even, mixed_ref,
     zbuf, qbuf, dtbuf, xsbuf, bcbuf, cumx_ref, xdt_ref, xb_ref, packbuf, vt_odd) = mixer_s
    n_chunks = tile // CHUNK
    if x_next_ref is not None:
        xb_ref[...] = x_next_ref[0].astype(BF16)
    vt_even[...] = vb_ref[...].T
    if n_chunks > 1:
        vt_odd[...] = vb_ref[CHUNK:CHUNK + vt_odd.shape[1], :].T
    yield UNITS_AT_START

    lane = lax.broadcasted_iota(jnp.int32, (CHUNK, LANES), 1)
    low_half = lane < HEAD_DIM
    li = lax.broadcasted_iota(jnp.int32, (CHUNK, GROUP_W), 0)
    si = lax.broadcasted_iota(jnp.int32, (CHUNK, GROUP_W), 1) % CHUNK
    diag_mask = li == si
    causal_mask = si <= li
    bd_mask = (lax.broadcasted_iota(jnp.int32, (GROUP_W, GROUP_W), 0) // CHUNK
               == lax.broadcasted_iota(jnp.int32, (GROUP_W, GROUP_W), 1) // CHUNK)

    def conv_and_decay(ci):
        return _conv_and_decay(ci, conv_w_ref, conv_b_ref, dt_bias_ref, a_log_ref, convbuf, dtbuf, xsbuf, bcbuf)

    packed_next = packbuf[...]
    for ci in range(n_chunks):
        r0 = ci * CHUNK
        rows = slice(r0, r0 + CHUNK)
        packed = packed_next

        expanded = _dot(packed, expand_ref[...])
        if hist_all_valid:
            variant = HIST_CHUNKS
        else:
            variant = jnp.minimum(first_chunk + ci, HIST_CHUNKS)
        group_cols = [slice(g * GROUP_W, (g + 1) * GROUP_W) for g in range(SSM_GROUPS)]
        b_gs = [bcbuf[rows, g * D_STATE:(g + 1) * D_STATE] for g in range(SSM_GROUPS)]
        c_gs = [bcbuf[rows, GROUP_W + g * D_STATE:GROUP_W + (g + 1) * D_STATE] for g in range(SSM_GROUPS)]
        cbs = [_dot_nt(c_gs[g], jnp.concatenate([b_gs[g]] * HEADS_PER_STACK, axis=0)) for g in range(SSM_GROUPS)]
        y_offs = [_dot(c_gs[g], st_ref[:, group_cols[g]].astype(BF16)) for g in range(SSM_GROUPS)]
        blocks = [qbuf[rows, j * LANES:(j + 1) * LANES] for j in range(ATTN_HEADS // 2)]
        evens = [jnp.where(low_half, blk, 0.0).astype(BF16) for blk in blocks]
        odds = [jnp.where(low_half, 0.0, blk).astype(BF16) for blk in blocks]
        q_a = jnp.concatenate([evens[0], evens[1], odds[2], odds[3]], axis=0)
        q_b = jnp.concatenate([odds[0], odds[1], evens[2], evens[3]], axis=0)
        scores = [_dot_nt(k_ref[r0:r0 + BAND, :], q_s) for q_s, k_ref in ((q_a, kb_ref), (q_b, ksb_ref))]
        yield UNITS_AFTER_FIRST_MATMULS

        if ci + 1 < n_chunks:
            packed_next = conv_and_decay(ci + 1)
        cumx_ref[...] = expanded[:, :SSM_WIDTH]
        xdt_ref[...] = xsbuf[rows, :] * expanded[:, SSM_WIDTH:]
        m_gs, bds, xws, probs = [], [], [], []
        for g in range(SSM_GROUPS):
            colb = cumx_ref[:, group_cols[g]]
            rowb = jnp.sum(jnp.where(diag_mask, colb, 0.0), axis=0, keepdims=True)
            lmat = jnp.exp2(jnp.where(causal_mask, colb - rowb, -jnp.inf))
            m_gs.append((cbs[g] * lmat).astype(BF16))
            xdt_g = xdt_ref[:, group_cols[g]]
            xdt_b = xdt_g.astype(BF16)
            bds.append(jnp.where(bd_mask, jnp.concatenate([xdt_b] * HEADS_PER_STACK, axis=0), jnp.zeros((), BF16)))
            last = colb[CHUNK - 1:CHUNK, :]
            xws.append((xdt_g * jnp.exp2(last - colb)).astype(BF16))
        for s_idx in range(2):
            s = scores[s_idx] + bias_ref[variant, s_idx]
            sink = sink_ref[s_idx][...]
            m = jnp.maximum(jnp.max(s, axis=0, keepdims=True), sink)
            p = jnp.exp2(s - m)
            denom = jnp.sum(p, axis=0, keepdims=True) + jnp.exp2(sink - m)
            probs.append((p * (1.0 / denom)).astype(BF16))
        b_ts = [b_g.T for b_g in b_gs]
        yield UNITS_AFTER_VECTOR_WORK

        y_diags = [_dot(m_gs[g], bds[g]) for g in range(SSM_GROUPS)]
        updates = [_dot(b_ts[g], xws[g]) for g in range(SSM_GROUPS)]
        vt_src, c0 = (vt_even, r0) if ci % 2 == 0 else (vt_odd, r0 - CHUNK)
        vband_t = vt_src[:, c0:c0 + BAND]
        vband_t_swapped = jnp.concatenate([vband_t[HEAD_DIM:], vband_t[:HEAD_DIM]], axis=0)
        outs_t = [_dot(vband_t, probs[0]), _dot(vband_t_swapped, probs[1])]
        yield UNITS_AFTER_SECOND_MATMULS

        for g in range(SSM_GROUPS):
            cols = group_cols[g]
            colb = cumx_ref[:, cols]
            last = colb[CHUNK - 1:CHUNK, :]
            st_ref[:, cols] = jnp.exp2(last) * st_ref[:, cols] + updates[g]
            y = y_diags[g] + y_offs[g] * jnp.exp2(colb) + d_ref[:, cols] * xsbuf[rows, cols]
            gg = y * _silu(zbuf[rows, cols])
            ms = jnp.mean(gg * gg, axis=-1, keepdims=True)
            mixed_ref[rows, cols] = (gg * lax.rsqrt(ms + RMS_EPS) * norm_w_ref[:, cols]).astype(BF16)
        o_a, o_b = [o_t.T for o_t in outs_t]
        sel = [(o_a, o_b), (o_a, o_b), (o_b, o_a), (o_b, o_a)]
        for j in range(ATTN_HEADS // 2):
            ev, od = sel[j]
            blk = jnp.where(low_half, ev[j * CHUNK:(j + 1) * CHUNK], od[j * CHUNK:(j + 1) * CHUNK])
            mixed_ref[rows, SSM_WIDTH + j * LANES:SSM_WIDTH + (j + 1) * LANES] = blk.astype(BF16)

    convbuf[CONV_HIST, :] = convbuf[CONV_PAD + tile - (CONV_W - 1):CONV_PAD + tile, :]
    for ref in (kf_ref, vf_ref, kb_ref, ksb_ref, vb_ref):
        ref[0:WINDOW, :] = ref[tile:tile + WINDOW, :]

    yield
    if x_ref is None:
        return None

    mix = _dot(mixed_ref[...], w_out_ref[...])
    if x_next_ref is not None:
        switch_stream()
        _prepare_tile(tile, None, mixer_w, mixer_s)
    return _layer_norm(DEEPNORM_ALPHA * x_ref[0] + mix, ln_g_ref[...], ln_b_ref[...])


FFN_DOWN_AFTER = (7, N_FFN_CHUNKS)
N_FFN_UNITS = 2 * N_FFN_CHUNKS + len(FFN_DOWN_AFTER) * (D_MODEL // FFN_CHUNK)


def _ffn_tile(h_ref, hb_ref, act_ref, w_gate_ref, w_up_ref, w_down_ref, ln_g_ref, ln_b_ref):
    out_cols = [slice(n * FFN_CHUNK, (n + 1) * FFN_CHUNK) for n in range(D_MODEL // FFN_CHUNK)]
    pre = [DEEPNORM_ALPHA * h_ref[:, cols] for cols in out_cols]
    done = 0
    for c in range(N_FFN_CHUNKS):
        cols = slice(c * FFN_CHUNK, (c + 1) * FFN_CHUNK)
        gate = _dot(hb_ref[...], w_gate_ref[:, cols])
        yield
        up = _dot(hb_ref[...], w_up_ref[:, cols])
        act_ref[:, cols] = (_silu(gate) * up).astype(BF16)
        yield
        if c + 1 in FFN_DOWN_AFTER:
            k_rows = slice(done * FFN_CHUNK, (c + 1) * FFN_CHUNK)
            done = c + 1
            for n, cols in enumerate(out_cols):
                pre[n] = pre[n] + _dot(act_ref[:, k_rows], w_down_ref[k_rows, cols])
                yield
    return _layer_norm(jnp.concatenate(pre, axis=1), ln_g_ref[...], ln_b_ref[...])


def _run(gen):
    while True:
        try:
            next(gen)
        except StopIteration as stop:
            return stop.value


def _interleave(mixer, ffn):
    remaining = N_FFN_UNITS
    while True:
        try:
            wanted = next(mixer)
        except StopIteration as stop:
            return stop.value, y
        last = wanted is None
        for _ in range(remaining if last else min(wanted, remaining)):
            next(ffn)
            remaining -= 1
        if last:
            y = _run(ffn)


STREAMS_PER_STEP = 2
N_MIXER_SCRATCH = 19

VEC_LAYOUT = (tuple(("conv_w%d" % i, CONV_DIM) for i in range(CONV_W))
              + (("conv_b", CONV_DIM), ("dt_bias", LANES), ("a_log", LANES), ("d_skip", SSM_WIDTH),
                 ("norm_w", SSM_WIDTH), ("sink_a", STACK_ROWS), ("sink_b", STACK_ROWS),
                 ("ln1_g", D_MODEL), ("ln1_b", D_MODEL), ("ln2_g", D_MODEL), ("ln2_b", D_MODEL)))
VEC_WIDTH = sum(width for _, width in VEC_LAYOUT)


def _vec_views(vec_ref):
    views, offset = {}, 0
    for name, width in VEC_LAYOUT:
        views[name] = vec_ref.at[:, offset:offset + width]
        offset += width
    return views


def _mixer_weights(w_in_ref, vec_ref, expand_ref, bias_ref, w_out_ref):
    v = _vec_views(vec_ref)
    return (w_in_ref.at[:, :COL_END], w_in_ref.at[:, COL_END:COL_END + LANES],
            tuple(v["conv_w%d" % i] for i in range(CONV_W)), v["conv_b"], v["dt_bias"], v["a_log"], v["d_skip"],
            v["norm_w"], expand_ref, bias_ref, (v["sink_a"], v["sink_b"]), w_out_ref, v["ln1_g"], v["ln1_b"])


def _ffn_weights(vec_ref, w_gate_ref, w_up_ref, w_down_ref):
    v = _vec_views(vec_ref)
    return (w_gate_ref, w_up_ref, w_down_ref, v["ln2_g"], v["ln2_b"])


def _fused_kernel(tile, n_tiles, n_total,
                  x_ref, x_next_ref, w_in_ref, vec_ref, expand_ref, bias_ref, w_out_ref, w_gate_ref, w_up_ref, w_down_ref,
                  y_ref, conv_out_ref, ssm_out_ref, k_out_ref, v_out_ref, *scratch):
    mixer_w = _mixer_weights(w_in_ref, vec_ref, expand_ref, bias_ref, w_out_ref)
    ffn_w = _ffn_weights(vec_ref, w_gate_ref, w_up_ref, w_down_ref)
    mixer_s = scratch[:N_MIXER_SCRATCH]
    h_s, hb_s, act_s = scratch[N_MIXER_SCRATCH:]
    convbuf, st_ref, kf_ref, vf_ref = mixer_s[:4]

    g = pl.program_id(0)
    t = lax.rem(jnp.minimum(g, n_total - 1), n_tiles)
    next_starts_stream = (g + 1 < n_total) & (lax.rem(g + 1, n_tiles) == 0)

    @pl.when(g == 0)
    def _():
        h_s[...] = jnp.zeros(h_s.shape, F32)
        hb_s[...] = jnp.zeros(hb_s.shape, BF16)
        convbuf[0:CONV_PAD, :] = jnp.zeros((CONV_PAD, CONV_DIM), F32)
        _prepare_tile(tile, x_ref, mixer_w, mixer_s)

    @pl.when(t == 0)
    def _():
        st_ref[...] = jnp.zeros(st_ref.shape, F32)
        for ref in mixer_s[2:7]:
            ref[0:WINDOW, :] = jnp.zeros((WINDOW, KV_W), ref.dtype)

    def switch_stream():
        hist = convbuf[CONV_HIST, :]
        conv_out_ref[0] = hist
        convbuf[CONV_HIST, :] = jnp.where(next_starts_stream, 0.0, hist)

    n_chunks = tile // CHUNK
    ffn = _ffn_tile(h_s, hb_s, act_s, *ffn_w)
    mixer = _mixer_tile(False, tile, t * n_chunks, x_ref, x_next_ref, switch_stream, mixer_w, mixer_s)
    h, y = _interleave(mixer, ffn)
    y_ref[0] = y
    h_s[...] = h
    hb_s[...] = h.astype(BF16)

    @pl.when((t == n_tiles - 1) & (g < n_total))
    def _():
        _write_stream_state(ssm_out_ref, k_out_ref, v_out_ref, st_ref, kf_ref, vf_ref)


def _sample_kernel(n_streams, seq, x_ref, conv0_ref, ssm0_ref, k0_ref, v0_ref,
                   w_in_ref, vec_ref, expand_ref, bias_ref, w_out_ref, w_gate_ref, w_up_ref, w_down_ref,
                   y_ref, conv_out_ref, ssm_out_ref, k_out_ref, v_out_ref, *scratch):
    mixer_w = _mixer_weights(w_in_ref, vec_ref, expand_ref, bias_ref, w_out_ref)
    ffn_w = _ffn_weights(vec_ref, w_gate_ref, w_up_ref, w_down_ref)
    (convbuf_a, kf_a, vf_a, kb_a, ksb_a, vb_a, mixed_a, zbuf_a, qbuf_a, dtbuf_a, xb_a, h_s, hb_s, act_s) = scratch[:14]
    stream_sets = [scratch[14 + j * N_MIXER_SCRATCH:14 + (j + 1) * N_MIXER_SCRATCH] for j in range(STREAMS_PER_STEP)]
    total = n_streams * seq
    part = min(total, PROMPT_TILE)
    b = pl.program_id(0)

    @pl.when(b == 0)
    def _():
        xb_a[...] = x_ref[...].astype(BF16)
        for r0 in range(0, total, part):
            rows, behind_conv, behind_kv = (pl.ds(r0, part), pl.ds(r0, CONV_PAD + part), pl.ds(r0, WINDOW + part))
            _in_projection(part, xb_a.at[rows], mixer_w[0], mixer_w[1], convbuf_a.at[behind_conv],
                           *(ref.at[behind_kv] for ref in (kf_a, vf_a, kb_a, ksb_a, vb_a)),
                           zbuf_a.at[rows], qbuf_a.at[rows], dtbuf_a.at[rows])

    mixers = []
    for j, one_s in enumerate(stream_sets):
        (convbuf, st_ref, kf_ref, vf_ref, kb_ref, ksb_ref, vb_ref, _, mixed_ref,
         zbuf, qbuf, dtbuf, xsbuf, bcbuf, _, _, _, packbuf, _) = one_s
        first = (b * STREAMS_PER_STEP + j) * seq
        rows = pl.ds(pl.multiple_of(first, seq), seq)
        conv_rows = pl.ds(pl.multiple_of(CONV_PAD + first, SUBLANES), seq)
        new_rows = pl.ds(pl.multiple_of(WINDOW + first, seq), seq)
        convbuf[0:CONV_PAD, :] = jnp.zeros((CONV_PAD, CONV_DIM), F32)
        convbuf[CONV_HIST, :] = conv0_ref[j]
        convbuf[CONV_PAD:CONV_PAD + seq, :] = convbuf_a[conv_rows, :]
        zbuf[...] = zbuf_a[rows, :]
        qbuf[...] = qbuf_a[rows, :]
        dtbuf[...] = dtbuf_a[rows, :]
        _init_stream_state(ssm0_ref.at[j:j + 1], k0_ref.at[j:j + 1], v0_ref.at[j:j + 1], *one_s[1:7])
        for dst, src in ((kf_ref, kf_a), (vf_ref, vf_a), (kb_ref, kb_a), (ksb_ref, ksb_a), (vb_ref, vb_a)):
            dst[WINDOW:WINDOW + seq, :] = src[new_rows, :]
        packbuf[...] = _conv_and_decay(0, *mixer_w[2:6], convbuf, dtbuf, xsbuf, bcbuf)
        mixers.append(_mixer_tile(True, seq, 0, None, None, None, mixer_w, one_s))
    while mixers:
        for mixer in list(mixers):
            try:
                next(mixer)
            except StopIteration:
                mixers.remove(mixer)
    for j, one_s in enumerate(stream_sets):
        convbuf, st_ref, kf_ref, vf_ref = one_s[:4]
        rows = pl.ds(pl.multiple_of((b * STREAMS_PER_STEP + j) * seq, seq), seq)
        mixed_a[rows, :] = one_s[8][...]
        conv_out_ref[j] = convbuf[CONV_HIST, :]
        _write_stream_state(ssm_out_ref.at[j:j + 1], k_out_ref.at[j:j + 1], v_out_ref.at[j:j + 1],
                            st_ref, kf_ref, vf_ref)

    @pl.when(b == n_streams // STREAMS_PER_STEP - 1)
    def _():
        for r0 in range(0, total, part):
            out_rows = pl.ds(r0, part)
            mix = _dot(mixed_a[out_rows, :], w_out_ref[...])
            h = _layer_norm(DEEPNORM_ALPHA * x_ref[out_rows, :] + mix, mixer_w[12][...], mixer_w[13][...])
            h_s[out_rows, :] = h
            hb_s[out_rows, :] = h.astype(BF16)
        y_ref[...] = _run(_ffn_tile(h_s, hb_s, act_s, *ffn_w))


def _const_spec(shape):
    zeros = (0,) * len(shape)
    return pl.BlockSpec(shape, lambda *_: zeros, pipeline_mode=pl.Buffered(1))


def _mixer_scratch(tile):
    return [pltpu.VMEM((CONV_PAD + tile, CONV_DIM), F32),
            pltpu.VMEM((D_STATE, SSM_WIDTH), F32),
            pltpu.VMEM((WINDOW + tile, KV_W), F32), pltpu.VMEM((WINDOW + tile, KV_W), F32),
            pltpu.VMEM((WINDOW + tile, KV_W), BF16), pltpu.VMEM((WINDOW + tile, KV_W), BF16),
            pltpu.VMEM((WINDOW + tile, KV_W), BF16), pltpu.VMEM((KV_W, WINDOW + tile), BF16),
            pltpu.VMEM((tile, D_MODEL), BF16),
            pltpu.VMEM((tile, SSM_WIDTH), F32),
            pltpu.VMEM((tile, ATTN_WIDTH), F32),
            pltpu.VMEM((tile, LANES), F32),
            pltpu.VMEM((tile, SSM_WIDTH), F32),
            pltpu.VMEM((tile, CONV_DIM - SSM_WIDTH), BF16),
            pltpu.VMEM((CHUNK, SSM_WIDTH), F32),
            pltpu.VMEM((CHUNK, SSM_WIDTH), F32),
            pltpu.VMEM((tile, D_MODEL), BF16),
            pltpu.VMEM((CHUNK, LANES), BF16),
            pltpu.VMEM((KV_W, max(tile, LANES) + CHUNK), BF16)]


_STATE_SHAPES = ((CONV_W - 1, CONV_DIM), (SSM_WIDTH, D_STATE), (KV_W, WINDOW), (KV_W, WINDOW))


def _attention_tables(attn_sinks):
    slopes = np.exp2(-8.0 * np.arange(1, ATTN_HEADS + 1, dtype=np.float64) / ATTN_HEADS)
    qi = np.arange(CHUNK)[None, :] + WINDOW
    kj = np.arange(BAND)[:, None]
    dist = np.abs(qi - kj).astype(np.float64)
    key_chunk = kj // CHUNK
    tables = np.zeros((HIST_CHUNKS + 1, 2, BAND, STACK_ROWS), np.float32)
    for variant in range(HIST_CHUNKS + 1):
        valid = key_chunk >= (HIST_CHUNKS - variant)
        for s_idx, heads in enumerate((STACK_A_HEADS, STACK_B_HEADS)):
            for i, h in enumerate(heads):
                tables[variant, s_idx, :, i * CHUNK:(i + 1) * CHUNK] = np.where(
                    valid, -slopes[h] * dist * LOG2E, -np.inf)
    order = np.array(STACK_A_HEADS + STACK_B_HEADS)
    sink_rows = jnp.repeat(attn_sinks.astype(F32)[order] * LOG2E, CHUNK).reshape(2, STACK_ROWS)
    return jnp.asarray(tables), sink_rows


def _expansion_matrix():
    e = np.zeros((EXPAND_K, 2 * SSM_WIDTH), np.float32)
    for part in range(6):
        for h in range(SSM_HEADS):
            c0 = (part // 3) * SSM_WIDTH + h * HEAD_DIM
            e[part * SSM_HEADS + h, c0:c0 + HEAD_DIM] = 1.0
    return jnp.asarray(e, BF16)


def _prompt_call(x, mixer_ops, ffn_ops):
    bsz, seq, _ = x.shape
    tile = PROMPT_TILE
    n_tiles = seq // tile
    n_total = bsz * n_tiles

    def mixer_tile_idx(g):
        gc = jnp.minimum(g, n_total - 1)
        return lax.div(gc, n_tiles), lax.rem(gc, n_tiles)

    def next_tile_idx(g):
        return mixer_tile_idx(g + 1)

    def ffn_tile_idx(g):
        gp = jnp.maximum(g - 1, 0)
        return lax.div(gp, n_tiles), lax.rem(gp, n_tiles)

    tile_spec = lambda idx: pl.BlockSpec((1, tile, D_MODEL), lambda g: (*idx(g), 0))
    stream_spec = lambda shape: pl.BlockSpec((1,) + shape, lambda g: (mixer_tile_idx(g)[0], 0, 0))
    in_specs = ([tile_spec(mixer_tile_idx), tile_spec(next_tile_idx)]
                + [_const_spec(w.shape) for w in mixer_ops + ffn_ops])
    out_specs = [tile_spec(ffn_tile_idx)] + [stream_spec(s) for s in _STATE_SHAPES]
    out_shape = ([jax.ShapeDtypeStruct((bsz, seq, D_MODEL), F32)]
                 + [jax.ShapeDtypeStruct((bsz,) + s, F32) for s in _STATE_SHAPES])
    scratch = _mixer_scratch(tile) + [pltpu.VMEM((tile, D_MODEL), F32), pltpu.VMEM((tile, D_MODEL), BF16),
                                      pltpu.VMEM((tile, D_FF), BF16)]
    return pl.pallas_call(
        functools.partial(_fused_kernel, tile, n_tiles, n_total),
        grid=(n_total + 1,), in_specs=in_specs, out_specs=out_specs, out_shape=out_shape,
        scratch_shapes=scratch, name="prompt_layer",
        compiler_params=pltpu.CompilerParams(dimension_semantics=("arbitrary",), vmem_limit_bytes=VMEM_LIMIT),
    )(x, x, *mixer_ops, *ffn_ops)


def _sample_call(x, state0, mixer_ops, ffn_ops):
    n_streams, seq, _ = x.shape
    assert n_streams % STREAMS_PER_STEP == 0
    total = n_streams * seq
    stream_spec = lambda shape: pl.BlockSpec((STREAMS_PER_STEP,) + shape, lambda b: (b, 0, 0))
    in_specs = ([_const_spec((total, D_MODEL))] + [stream_spec(s) for s in _STATE_SHAPES]
                + [_const_spec(w.shape) for w in mixer_ops + ffn_ops])
    out_specs = ([pl.BlockSpec((total, D_MODEL), lambda b: (0, 0))]
                 + [stream_spec(s) for s in _STATE_SHAPES])
    out_shape = ([jax.ShapeDtypeStruct((total, D_MODEL), F32)]
                 + [jax.ShapeDtypeStruct((n_streams,) + s, F32) for s in _STATE_SHAPES])
    all_rows = [pltpu.VMEM((CONV_PAD + total, CONV_DIM), F32),
                pltpu.VMEM((WINDOW + total, KV_W), F32), pltpu.VMEM((WINDOW + total, KV_W), F32),
                pltpu.VMEM((WINDOW + total, KV_W), BF16), pltpu.VMEM((WINDOW + total, KV_W), BF16),
                pltpu.VMEM((WINDOW + total, KV_W), BF16),
                pltpu.VMEM((total, D_MODEL), BF16),
                pltpu.VMEM((total, SSM_WIDTH), F32), pltpu.VMEM((total, ATTN_WIDTH), F32),
                pltpu.VMEM((total, LANES), F32),
                pltpu.VMEM((total, D_MODEL), BF16),
                pltpu.VMEM((total, D_MODEL), F32), pltpu.VMEM((total, D_MODEL), BF16),
                pltpu.VMEM((total, D_FF), BF16)]
    y, conv, ssm, k, v = pl.pallas_call(
        functools.partial(_sample_kernel, n_streams, seq),
        grid=(n_streams // STREAMS_PER_STEP,), in_specs=in_specs, out_specs=out_specs, out_shape=out_shape,
        scratch_shapes=all_rows + STREAMS_PER_STEP * _mixer_scratch(seq), name="sample_layer",
        compiler_params=pltpu.CompilerParams(dimension_semantics=("arbitrary",), vmem_limit_bytes=VMEM_LIMIT),
    )(x.reshape(total, D_MODEL), *state0, *mixer_ops, *ffn_ops)
    return y.reshape(n_streams, seq, D_MODEL), conv, ssm, k, v


IN_COLS = COL_END + 8
W_IN_PREP_ROWS = 256


def _w_in_kernel(w_t_ref, out_ref):
    src = {"z": 0, "xbc": SSM_WIDTH, "dt": SSM_WIDTH + CONV_DIM}
    src["q"] = src["dt"] + SSM_HEADS
    src["k"] = src["q"] + ATTN_WIDTH
    src["v"] = src["k"] + KV_W
    pieces = (("z", COL_Z, COL_XBC - COL_Z, 1.0), ("xbc", COL_XBC, COL_Q - COL_XBC, 1.0),
              ("q", COL_Q, COL_K - COL_Q, HEAD_DIM ** -0.5 * LOG2E), ("k", COL_K, KV_W, 1.0), ("v", COL_V, KV_W, 1.0))
    for name, dst, width, scale in pieces:
        for off in range(0, width, W_IN_PREP_ROWS):
            rows = min(W_IN_PREP_ROWS, width - off)
            blk = w_t_ref[src[name] + off:src[name] + off + rows, :]
            if scale != 1.0:
                blk = blk * scale
            out_ref[:, dst + off:dst + off + rows] = blk.astype(BF16).T
    w_dt = w_t_ref[src["dt"]:src["dt"] + SSM_HEADS, :]
    w_dt_rep = jnp.concatenate([w_dt] * (LANES // SSM_HEADS), axis=0)
    out_ref[:, COL_END:COL_END + LANES] = w_dt_rep.astype(BF16).T


def _prepare_w_in(w_t):
    return pl.pallas_call(
        _w_in_kernel, grid=(1,),
        in_specs=[_const_spec((IN_COLS, D_MODEL))], out_specs=pl.BlockSpec((D_MODEL, COL_END + LANES), lambda i: (0, 0)),
        out_shape=jax.ShapeDtypeStruct((D_MODEL, COL_END + LANES), BF16), name="w_in_prep",
        compiler_params=pltpu.CompilerParams(dimension_semantics=("arbitrary",), vmem_limit_bytes=VMEM_LIMIT),
    )(w_t)


def _window_to_kernel_layout(cache):
    b = cache.shape[0]
    return jnp.transpose(cache.astype(F32).reshape(b, WINDOW, KV_W), (0, 2, 1))


def _window_from_kernel_layout(win_t):
    b = win_t.shape[0]
    return jnp.transpose(win_t, (0, 2, 1)).reshape(1, b, WINDOW, KV_HEADS, HEAD_DIM)


def kernel(x_prompt, x_sample, state_conv, state_ssm, cache_k, cache_v, w_in, conv_w, conv_b, dt_bias, a_log,
           d_skip, ssm_norm_w, attn_sinks, w_out, ln1_g, ln1_b, w_gate, w_up, w_down, ln2_g, ln2_b):
    w_in_r = _prepare_w_in(jnp.transpose(w_in[0]))
    rep = LANES // SSM_HEADS
    bias_tables, sink_rows = _attention_tables(attn_sinks[0])
    vec_parts = {"conv_b": conv_b[0], "dt_bias": jnp.tile(dt_bias[0], rep), "a_log": jnp.tile(a_log[0], rep),
                 "d_skip": jnp.repeat(d_skip[0], HEAD_DIM), "norm_w": ssm_norm_w[0],
                 "sink_a": sink_rows[0], "sink_b": sink_rows[1],
                 "ln1_g": ln1_g[0], "ln1_b": ln1_b[0], "ln2_g": ln2_g[0], "ln2_b": ln2_b[0]}
    vec_parts.update({"conv_w%d" % i: conv_w[0, i] for i in range(CONV_W)})
    vecs = jnp.concatenate([vec_parts[name].astype(F32).reshape(width) for name, width in VEC_LAYOUT])
    vecs = vecs.reshape(1, VEC_WIDTH)
    mixer_ops = (w_in_r, vecs, _expansion_matrix(), bias_tables, w_out[0].astype(BF16))
    ffn_mats = (w_gate[0].astype(BF16), w_up[0].astype(BF16), w_down[0].astype(BF16))

    dbsz = x_sample.shape[0]

    y_p, conv_p, ssm_p, k_p, v_p = _prompt_call(x_prompt, mixer_ops, ffn_mats)

    sample_state = (state_conv[0].astype(F32), state_ssm[0].astype(F32).reshape(dbsz, SSM_WIDTH, D_STATE),
                    _window_to_kernel_layout(cache_k[0]), _window_to_kernel_layout(cache_v[0]))
    y_s, conv_s, ssm_s, k_s, v_s = _sample_call(x_sample, sample_state, mixer_ops, ffn_mats)

    ssm_shape = lambda a: a.reshape(1, a.shape[0], SSM_HEADS, HEAD_DIM, D_STATE)
    kv_shape = _window_from_kernel_layout
    return (y_p, y_s,
            conv_p[None], ssm_shape(ssm_p), kv_shape(k_p), kv_shape(v_p),
            conv_s[None], ssm_shape(ssm_s), kv_shape(k_s), kv_shape(v_s))
```

```python
import functools
import math

import jax
import jax.numpy as jnp
import numpy as np
from jax import lax
from jax.experimental import pallas as pl
from jax.experimental.pallas import tpu as pltpu

D_MODEL = 1024
CHUNK = 64
HEAD_DIM = 64
SSM_WIDTH = 512
SSM_HEADS = 8
SSM_GROUPS = 2
D_STATE = 128
CONV_W = 4
CONV_DIM = SSM_WIDTH + 2 * SSM_GROUPS * D_STATE
ATTN_WIDTH = 512
ATTN_HEADS = 8
KV_HEADS = 2
WINDOW = 128
D_FF = 2816
LN_EPS = 1e-5
RMS_EPS = 1e-5
DEEPNORM_ALPHA = 2.0 ** 0.25
LOG2E = math.log2(math.e)

LANES = 128
SUBLANES = 8
GROUP_W = SSM_WIDTH // SSM_GROUPS
KV_W = KV_HEADS * HEAD_DIM
BAND = WINDOW + CHUNK
HIST_CHUNKS = WINDOW // CHUNK
HEADS_PER_STACK = 4
STACK_ROWS = HEADS_PER_STACK * CHUNK
COL_Z = 0
COL_XBC = COL_Z + SSM_WIDTH
COL_Q = COL_XBC + CONV_DIM
COL_K = COL_Q + ATTN_WIDTH
COL_V = COL_K + KV_W
COL_END = COL_V + KV_W
STACK_A_HEADS = (0, 2, 5, 7)
STACK_B_HEADS = (1, 3, 4, 6)
EXPAND_K = 128
CONV_PAD = SUBLANES

PROMPT_TILE = 256
FFN_CHUNK = 256
N_FFN_CHUNKS = D_FF // FFN_CHUNK
VMEM_LIMIT = 56 * 1024 * 1024
UNITS_AT_START = 1
UNITS_AFTER_FIRST_MATMULS = 3
UNITS_AFTER_VECTOR_WORK = 2
UNITS_AFTER_SECOND_MATMULS = 1

F32 = jnp.float32
BF16 = jnp.bfloat16


def _dot(a, b):
    return jnp.dot(a, b, preferred_element_type=F32)


def _dot_nt(a, b):
    return lax.dot_general(a, b, (((1,), (1,)), ((), ())), preferred_element_type=F32)


def _sigmoid(x):
    return 1.0 / (1.0 + jnp.exp2(x * -LOG2E))


def _silu(x):
    return x * _sigmoid(x)


def _layer_norm(x, g, b):
    mu = jnp.mean(x, axis=-1, keepdims=True)
    xc = x - mu
    var = jnp.mean(xc * xc, axis=-1, keepdims=True)
    return xc * lax.rsqrt(var + LN_EPS) * g + b


def _split3(x):
    hi = x.astype(BF16).astype(F32)
    r = x - hi
    mid = r.astype(BF16).astype(F32)
    lo = r - mid
    return hi, mid, lo


CONV_HIST = slice(CONV_PAD - (CONV_W - 1), CONV_PAD)


def _init_stream_state(ssm0_ref, k0_ref, v0_ref, st_ref, kf_ref, vf_ref, kb_ref, ksb_ref, vb_ref):
    st_ref[...] = ssm0_ref[0].T
    k0 = k0_ref[0].T
    v0 = v0_ref[0].T
    kf_ref[0:WINDOW, :] = k0
    vf_ref[0:WINDOW, :] = v0
    kb_ref[0:WINDOW, :] = k0.astype(BF16)
    vb_ref[0:WINDOW, :] = v0.astype(BF16)
    ksb_ref[0:WINDOW, :] = pltpu.roll(k0, HEAD_DIM, axis=1).astype(BF16)


def _write_stream_state(ssm_out_ref, k_out_ref, v_out_ref, st_ref, kf_ref, vf_ref):
    ssm_out_ref[0] = st_ref[...].T
    k_out_ref[0] = kf_ref[0:WINDOW, :].T
    v_out_ref[0] = vf_ref[0:WINDOW, :].T


def _in_projection(tile, xb_ref, w_in_ref, w_dt_ref,
                   convbuf, kf_ref, vf_ref, kb_ref, ksb_ref, vb_ref, zbuf, qbuf, dtbuf):
    xb = xb_ref[...]
    zbuf[...] = _dot(xb, w_in_ref[:, COL_Z:COL_XBC])
    convbuf[CONV_PAD:CONV_PAD + tile, :] = _dot(xb, w_in_ref[:, COL_XBC:COL_Q])
    qbuf[...] = _dot(xb, w_in_ref[:, COL_Q:COL_K])
    kv = _dot(xb, w_in_ref[:, COL_K:COL_END])
    dtbuf[...] = _dot(xb, w_dt_ref[...])
    k_new = kv[:, :KV_W]
    v_new = kv[:, KV_W:]
    kf_ref[WINDOW:WINDOW + tile, :] = k_new
    vf_ref[WINDOW:WINDOW + tile, :] = v_new
    kb_ref[WINDOW:WINDOW + tile, :] = k_new.astype(BF16)
    vb_ref[WINDOW:WINDOW + tile, :] = v_new.astype(BF16)
    ksb_ref[WINDOW:WINDOW + tile, :] = pltpu.roll(k_new, HEAD_DIM, axis=1).astype(BF16)


def _conv_and_decay(ci, conv_w_ref, conv_b_ref, dt_bias_ref, a_log_ref, convbuf, dtbuf, xsbuf, bcbuf):
    r0 = ci * CHUNK
    rows = slice(r0, r0 + CHUNK)

    window = convbuf[r0:r0 + CONV_PAD + CHUNK, :]
    conv = conv_b_ref[...] + conv_w_ref[CONV_W - 1][...] * window[CONV_PAD:]
    for back in range(1, CONV_W):
        conv = conv + conv_w_ref[CONV_W - 1 - back][...] * pltpu.roll(window, back, axis=0)[CONV_PAD:]
    xc = _silu(conv)
    xsbuf[rows, :] = xc[:, :SSM_WIDTH]
    bcbuf[rows, :] = xc[:, SSM_WIDTH:].astype(BF16)

    row_in_chunk = lax.broadcasted_iota(jnp.int32, (CHUNK, LANES), 0)
    lane = lax.broadcasted_iota(jnp.int32, (CHUNK, LANES), 1)
    dt_in = dtbuf[rows, :] + dt_bias_ref[...]
    dt = jnp.maximum(dt_in, 0.0) + jnp.log1p(jnp.exp(-jnp.abs(dt_in)))
    cum = dt * (-LOG2E * jnp.exp(a_log_ref[...]))
    step = 1
    while step < CHUNK:
        cum = cum + jnp.where(row_in_chunk >= step, pltpu.roll(cum, step, axis=0), 0.0)
        step *= 2
    packed = jnp.zeros((CHUNK, LANES), F32)
    for i, part in enumerate(_split3(cum) + _split3(dt)):
        packed = jnp.where((lane >= SSM_HEADS * i) & (lane < SSM_HEADS * (i + 1)), part, packed)
    return packed.astype(BF16)


def _prepare_tile(tile, x_ref, mixer_w, mixer_s):
    (w_in_ref, w_dt_ref, conv_w_ref, conv_b_ref, dt_bias_ref, a_log_ref) = mixer_w[:6]
    (convbuf, _, kf_ref, vf_ref, kb_ref, ksb_ref, vb_ref, _, _, zbuf, qbuf, dtbuf, xsbuf, bcbuf, _, _,
     xb_ref, packbuf, _) = mixer_s
    if x_ref is not None:
        xb_ref[...] = x_ref[0].astype(BF16)
    _in_projection(tile, xb_ref, w_in_ref, w_dt_ref,
                   convbuf, kf_ref, vf_ref, kb_ref, ksb_ref, vb_ref, zbuf, qbuf, dtbuf)
    packbuf[...] = _conv_and_decay(0, conv_w_ref, conv_b_ref, dt_bias_ref, a_log_ref, convbuf, dtbuf, xsbuf, bcbuf)


def _mixer_tile(hist_all_valid, tile, first_chunk, x_ref, x_next_ref, switch_stream, mixer_w, mixer_s):
    (_, _, conv_w_ref, conv_b_ref, dt_bias_ref, a_log_ref, d_ref, norm_w_ref,
     expand_ref, bias_ref, sink_ref, w_out_ref, ln_g_ref, ln_b_ref) = mixer_w
    (convbuf, st_ref, kf_ref, vf_ref, kb_ref, ksb_ref, vb_ref, vt_even, mixed_ref,
     zbuf, qbuf, dtbuf, xsbuf, bcbuf, cumx_ref, xdt_ref, xb_ref, packbuf, vt_odd) = mixer_s
    n_chunks = tile // CHUNK
    if x_next_ref is not None:
        xb_ref[...] = x_next_ref[0].astype(BF16)
    vt_even[...] = vb_ref[...].T
    if n_chunks > 1:
        vt_odd[...] = vb_ref[CHUNK:CHUNK + vt_odd.shape[1], :].T
    yield UNITS_AT_START

    lane = lax.broadcasted_iota(jnp.int32, (CHUNK, LANES), 1)
    low_half = lane < HEAD_DIM
    li = lax.broadcasted_iota(jnp.int32, (CHUNK, GROUP_W), 0)
    si = lax.broadcasted_iota(jnp.int32, (CHUNK, GROUP_W), 1) % CHUNK
    diag_mask = li == si
    causal_mask = si <= li
    bd_mask = (lax.broadcasted_iota(jnp.int32, (GROUP_W, GROUP_W), 0) // CHUNK
               == lax.broadcasted_iota(jnp.int32, (GROUP_W, GROUP_W), 1) // CHUNK)

    def conv_and_decay(ci):
        return _conv_and_decay(ci, conv_w_ref, conv_b_ref, dt_bias_ref, a_log_ref, convbuf, dtbuf, xsbuf, bcbuf)

    packed_next = packbuf[...]
    for ci in range(n_chunks):
        r0 = ci * CHUNK
        rows = slice(r0, r0 + CHUNK)
        packed = packed_next

        expanded = _dot(packed, expand_ref[...])
        if hist_all_valid:
            variant = HIST_CHUNKS
        else:
            variant = jnp.minimum(first_chunk + ci, HIST_CHUNKS)
        group_cols = [slice(g * GROUP_W, (g + 1) * GROUP_W) for g in range(SSM_GROUPS)]
        b_gs = [bcbuf[rows, g * D_STATE:(g + 1) * D_STATE] for g in range(SSM_GROUPS)]
        c_gs = [bcbuf[rows, GROUP_W + g * D_STATE:GROUP_W + (g + 1) * D_STATE] for g in range(SSM_GROUPS)]
        cbs = [_dot_nt(c_gs[g], jnp.concatenate([b_gs[g]] * HEADS_PER_STACK, axis=0)) for g in range(SSM_GROUPS)]
        y_offs = [_dot(c_gs[g], st_ref[:, group_cols[g]].astype(BF16)) for g in range(SSM_GROUPS)]
        blocks = [qbuf[rows, j * LANES:(j + 1) * LANES] for j in range(ATTN_HEADS // 2)]
        evens = [jnp.where(low_half, blk, 0.0).astype(BF16) for blk in blocks]
        odds = [jnp.where(low_half, 0.0, blk).astype(BF16) for blk in blocks]
        q_a = jnp.concatenate([evens[0], evens[1], odds[2], odds[3]], axis=0)
        q_b = jnp.concatenate([odds[0], odds[1], evens[2], evens[3]], axis=0)
        scores = [_dot_nt(k_ref[r0:r0 + BAND, :], q_s) for q_s, k_ref in ((q_a, kb_ref), (q_b, ksb_ref))]
        yield UNITS_AFTER_FIRST_MATMULS

        if ci + 1 < n_chunks:
            packed_next = conv_and_decay(ci + 1)
        cumx_ref[...] = expanded[:, :SSM_WIDTH]
        xdt_ref[...] = xsbuf[rows, :] * expanded[:, SSM_WIDTH:]
        m_gs, bds, xws, probs = [], [], [], []
        for g in range(SSM_GROUPS):
            colb = cumx_ref[:, group_cols[g]]
            rowb = jnp.sum(jnp.where(diag_mask, colb, 0.0), axis=0, keepdims=True)
            lmat = jnp.exp2(jnp.where(causal_mask, colb - rowb, -jnp.inf))
            m_gs.append((cbs[g] * lmat).astype(BF16))
            xdt_g = xdt_ref[:, group_cols[g]]
            xdt_b = xdt_g.astype(BF16)
            bds.append(jnp.where(bd_mask, jnp.concatenate([xdt_b] * HEADS_PER_STACK, axis=0), jnp.zeros((), BF16)))
            last = colb[CHUNK - 1:CHUNK, :]
            xws.append((xdt_g * jnp.exp2(last - colb)).astype(BF16))
        for s_idx in range(2):
            s = scores[s_idx] + bias_ref[variant, s_idx]
            sink = sink_ref[s_idx][...]
            m = jnp.maximum(jnp.max(s, axis=0, keepdims=True), sink)
            p = jnp.exp2(s - m)
            denom = jnp.sum(p, axis=0, keepdims=True) + jnp.exp2(sink - m)
            probs.append((p * (1.0 / denom)).astype(BF16))
        b_ts = [b_g.T for b_g in b_gs]
        yield UNITS_AFTER_VECTOR_WORK

        y_diags = [_dot(m_gs[g], bds[g]) for g in range(SSM_GROUPS)]
        updates = [_dot(b_ts[g], xws[g]) for g in range(SSM_GROUPS)]
        vt_src, c0 = (vt_even, r0) if ci % 2 == 0 else (vt_odd, r0 - CHUNK)
        vband_t = vt_src[:, c0:c0 + BAND]
        vband_t_swapped = jnp.concatenate([vband_t[HEAD_DIM:], vband_t[:HEAD_DIM]], axis=0)
        outs_t = [_dot(vband_t, probs[0]), _dot(vband_t_swapped, probs[1])]
        yield UNITS_AFTER_SECOND_MATMULS

        for g in range(SSM_GROUPS):
            cols = group_cols[g]
            colb = cumx_ref[:, cols]
            last = colb[CHUNK - 1:CHUNK, :]
            st_ref[:, cols] = jnp.exp2(last) * st_ref[:, cols] + updates[g]
            y = y_diags[g] + y_offs[g] * jnp.exp2(colb) + d_ref[:, cols] * xsbuf[rows, cols]
            gg = y * _silu(zbuf[rows, cols])
            ms = jnp.mean(gg * gg, axis=-1, keepdims=True)
            mixed_ref[rows, cols] = (gg * lax.rsqrt(ms + RMS_EPS) * norm_w_ref[:, cols]).astype(BF16)
        o_a, o_b = [o_t.T for o_t in outs_t]
        sel = [(o_a, o_b), (o_a, o_b), (o_b, o_a), (o_b, o_a)]
        for j in range(ATTN_HEADS // 2):
            ev, od = sel[j]
            blk = jnp.where(low_half, ev[j * CHUNK:(j + 1) * CHUNK], od[j * CHUNK:(j + 1) * CHUNK])
            mixed_ref[rows, SSM_WIDTH + j * LANES:SSM_WIDTH + (j + 1) * LANES] = blk.astype(BF16)

    convbuf[CONV_HIST, :] = convbuf[CONV_PAD + tile - (CONV_W - 1):CONV_PAD + tile, :]
    for ref in (kf_ref, vf_ref, kb_ref, ksb_ref, vb_ref):
        ref[0:WINDOW, :] = ref[tile:tile + WINDOW, :]

    yield
    if x_ref is None:
        return None

    mix = _dot(mixed_ref[...], w_out_ref[...])
    if x_next_ref is not None:
        switch_stream()
        _prepare_tile(tile, None, mixer_w, mixer_s)
    return _layer_norm(DEEPNORM_ALPHA * x_ref[0] + mix, ln_g_ref[...], ln_b_ref[...])


FFN_DOWN_AFTER = (7, N_FFN_CHUNKS)
N_FFN_UNITS = 2 * N_FFN_CHUNKS + len(FFN_DOWN_AFTER) * (D_MODEL // FFN_CHUNK)


def _ffn_tile(h_ref, hb_ref, act_ref, w_gate_ref, w_up_ref, w_down_ref, ln_g_ref, ln_b_ref):
    out_cols = [slice(n * FFN_CHUNK, (n + 1) * FFN_CHUNK) for n in range(D_MODEL // FFN_CHUNK)]
    pre = [DEEPNORM_ALPHA * h_ref[:, cols] for cols in out_cols]
    done = 0
    for c in range(N_FFN_CHUNKS):
        cols = slice(c * FFN_CHUNK, (c + 1) * FFN_CHUNK)
        gate = _dot(hb_ref[...], w_gate_ref[:, cols])
        yield
        up = _dot(hb_ref[...], w_up_ref[:, cols])
        act_ref[:, cols] = (_silu(gate) * up).astype(BF16)
        yield
        if c + 1 in FFN_DOWN_AFTER:
            k_rows = slice(done * FFN_CHUNK, (c + 1) * FFN_CHUNK)
            done = c + 1
            for n, cols in enumerate(out_cols):
                pre[n] = pre[n] + _dot(act_ref[:, k_rows], w_down_ref[k_rows, cols])
                yield
    return _layer_norm(jnp.concatenate(pre, axis=1), ln_g_ref[...], ln_b_ref[...])


def _run(gen):
    while True:
        try:
            next(gen)
        except StopIteration as stop:
            return stop.value


def _interleave(mixer, ffn):
    remaining = N_FFN_UNITS
    while True:
        try:
            wanted = next(mixer)
        except StopIteration as stop:
            return stop.value, y
        last = wanted is None
        for _ in range(remaining if last else min(wanted, remaining)):
            next(ffn)
            remaining -= 1
        if last:
            y = _run(ffn)


STREAMS_PER_STEP = 2
N_MIXER_SCRATCH = 19

VEC_LAYOUT = (tuple(("conv_w%d" % i, CONV_DIM) for i in range(CONV_W))
              + (("conv_b", CONV_DIM), ("dt_bias", LANES), ("a_log", LANES), ("d_skip", SSM_WIDTH),
                 ("norm_w", SSM_WIDTH), ("sink_a", STACK_ROWS), ("sink_b", STACK_ROWS),
                 ("ln1_g", D_MODEL), ("ln1_b", D_MODEL), ("ln2_g", D_MODEL), ("ln2_b", D_MODEL)))
VEC_WIDTH = sum(width for _, width in VEC_LAYOUT)


def _vec_views(vec_ref):
    views, offset = {}, 0
    for name, width in VEC_LAYOUT:
        views[name] = vec_ref.at[:, offset:offset + width]
        offset += width
    return views


def _mixer_weights(w_in_ref, vec_ref, expand_ref, bias_ref, w_out_ref):
    v = _vec_views(vec_ref)
    return (w_in_ref.at[:, :COL_END], w_in_ref.at[:, COL_END:COL_END + LANES],
            tuple(v["conv_w%d" % i] for i in range(CONV_W)), v["conv_b"], v["dt_bias"], v["a_log"], v["d_skip"],
            v["norm_w"], expand_ref, bias_ref, (v["sink_a"], v["sink_b"]), w_out_ref, v["ln1_g"], v["ln1_b"])


def _ffn_weights(vec_ref, w_gate_ref, w_up_ref, w_down_ref):
    v = _vec_views(vec_ref)
    return (w_gate_ref, w_up_ref, w_down_ref, v["ln2_g"], v["ln2_b"])


def _fused_kernel(tile, n_tiles, n_total,
                  x_ref, x_next_ref, w_in_ref, vec_ref, expand_ref, bias_ref, w_out_ref, w_gate_ref, w_up_ref, w_down_ref,
                  y_ref, conv_out_ref, ssm_out_ref, k_out_ref, v_out_ref, *scratch):
    mixer_w = _mixer_weights(w_in_ref, vec_ref, expand_ref, bias_ref, w_out_ref)
    ffn_w = _ffn_weights(vec_ref, w_gate_ref, w_up_ref, w_down_ref)
    mixer_s = scratch[:N_MIXER_SCRATCH]
    h_s, hb_s, act_s = scratch[N_MIXER_SCRATCH:]
    convbuf, st_ref, kf_ref, vf_ref = mixer_s[:4]

    g = pl.program_id(0)
    t = lax.rem(jnp.minimum(g, n_total - 1), n_tiles)
    next_starts_stream = (g + 1 < n_total) & (lax.rem(g + 1, n_tiles) == 0)

    @pl.when(g == 0)
    def _():
        h_s[...] = jnp.zeros(h_s.shape, F32)
        hb_s[...] = jnp.zeros(hb_s.shape, BF16)
        convbuf[0:CONV_PAD, :] = jnp.zeros((CONV_PAD, CONV_DIM), F32)
        _prepare_tile(tile, x_ref, mixer_w, mixer_s)

    @pl.when(t == 0)
    def _():
        st_ref[...] = jnp.zeros(st_ref.shape, F32)
        for ref in mixer_s[2:7]:
            ref[0:WINDOW, :] = jnp.zeros((WINDOW, KV_W), ref.dtype)

    def switch_stream():
        hist = convbuf[CONV_HIST, :]
        conv_out_ref[0] = hist
        convbuf[CONV_HIST, :] = jnp.where(next_starts_stream, 0.0, hist)

    n_chunks = tile // CHUNK
    ffn = _ffn_tile(h_s, hb_s, act_s, *ffn_w)
    mixer = _mixer_tile(False, tile, t * n_chunks, x_ref, x_next_ref, switch_stream, mixer_w, mixer_s)
    h, y = _interleave(mixer, ffn)
    y_ref[0] = y
    h_s[...] = h
    hb_s[...] = h.astype(BF16)

    @pl.when((t == n_tiles - 1) & (g < n_total))
    def _():
        _write_stream_state(ssm_out_ref, k_out_ref, v_out_ref, st_ref, kf_ref, vf_ref)


def _sample_kernel(n_streams, seq, x_ref, conv0_ref, ssm0_ref, k0_ref, v0_ref,
                   w_in_ref, vec_ref, expand_ref, bias_ref, w_out_ref, w_gate_ref, w_up_ref, w_down_ref,
                   y_ref, conv_out_ref, ssm_out_ref, k_out_ref, v_out_ref, *scratch):
    mixer_w = _mixer_weights(w_in_ref, vec_ref, expand_ref, bias_ref, w_out_ref)
    ffn_w = _ffn_weights(vec_ref, w_gate_ref, w_up_ref, w_down_ref)
    (convbuf_a, kf_a, vf_a, kb_a, ksb_a, vb_a, mixed_a, zbuf_a, qbuf_a, dtbuf_a, xb_a, h_s, hb_s, act_s) = scratch[:14]
    stream_sets = [scratch[14 + j * N_MIXER_SCRATCH:14 + (j + 1) * N_MIXER_SCRATCH] for j in range(STREAMS_PER_STEP)]
    total = n_streams * seq
    part = min(total, PROMPT_TILE)
    b = pl.program_id(0)

    @pl.when(b == 0)
    def _():
        xb_a[...] = x_ref[...].astype(BF16)
        for r0 in range(0, total, part):
            rows, behind_conv, behind_kv = (pl.ds(r0, part), pl.ds(r0, CONV_PAD + part), pl.ds(r0, WINDOW + part))
            _in_projection(part, xb_a.at[rows], mixer_w[0], mixer_w[1], convbuf_a.at[behind_conv],
                           *(ref.at[behind_kv] for ref in (kf_a, vf_a, kb_a, ksb_a, vb_a)),
                           zbuf_a.at[rows], qbuf_a.at[rows], dtbuf_a.at[rows])

    mixers = []
    for j, one_s in enumerate(stream_sets):
        (convbuf, st_ref, kf_ref, vf_ref, kb_ref, ksb_ref, vb_ref, _, mixed_ref,
         zbuf, qbuf, dtbuf, xsbuf, bcbuf, _, _, _, packbuf, _) = one_s
        first = (b * STREAMS_PER_STEP + j) * seq
        rows = pl.ds(pl.multiple_of(first, seq), seq)
        conv_rows = pl.ds(pl.multiple_of(CONV_PAD + first, SUBLANES), seq)
        new_rows = pl.ds(pl.multiple_of(WINDOW + first, seq), seq)
        convbuf[0:CONV_PAD, :] = jnp.zeros((CONV_PAD, CONV_DIM), F32)
        convbuf[CONV_HIST, :] = conv0_ref[j]
        convbuf[CONV_PAD:CONV_PAD + seq, :] = convbuf_a[conv_rows, :]
        zbuf[...] = zbuf_a[rows, :]
        qbuf[...] = qbuf_a[rows, :]
        dtbuf[...] = dtbuf_a[rows, :]
        _init_stream_state(ssm0_ref.at[j:j + 1], k0_ref.at[j:j + 1], v0_ref.at[j:j + 1], *one_s[1:7])
        for dst, src in ((kf_ref, kf_a), (vf_ref, vf_a), (kb_ref, kb_a), (ksb_ref, ksb_a), (vb_ref, vb_a)):
            dst[WINDOW:WINDOW + seq, :] = src[new_rows, :]
        packbuf[...] = _conv_and_decay(0, *mixer_w[2:6], convbuf, dtbuf, xsbuf, bcbuf)
        mixers.append(_mixer_tile(True, seq, 0, None, None, None, mixer_w, one_s))
    while mixers:
        for mixer in list(mixers):
            try:
                next(mixer)
            except StopIteration:
                mixers.remove(mixer)
    for j, one_s in enumerate(stream_sets):
        convbuf, st_ref, kf_ref, vf_ref = one_s[:4]
        rows = pl.ds(pl.multiple_of((b * STREAMS_PER_STEP + j) * seq, seq), seq)
        mixed_a[rows, :] = one_s[8][...]
        conv_out_ref[j] = convbuf[CONV_HIST, :]
        _write_stream_state(ssm_out_ref.at[j:j + 1], k_out_ref.at[j:j + 1], v_out_ref.at[j:j + 1],
                            st_ref, kf_ref, vf_ref)

    @pl.when(b == n_streams // STREAMS_PER_STEP - 1)
    def _():
        for r0 in range(0, total, part):
            out_rows = pl.ds(r0, part)
            mix = _dot(mixed_a[out_rows, :], w_out_ref[...])
            h = _layer_norm(DEEPNORM_ALPHA * x_ref[out_rows, :] + mix, mixer_w[12][...], mixer_w[13][...])
            h_s[out_rows, :] = h
            hb_s[out_rows, :] = h.astype(BF16)
        y_ref[...] = _run(_ffn_tile(h_s, hb_s, act_s, *ffn_w))


def _const_spec(shape):
    zeros = (0,) * len(shape)
    return pl.BlockSpec(shape, lambda *_: zeros, pipeline_mode=pl.Buffered(1))


def _mixer_scratch(tile):
    return [pltpu.VMEM((CONV_PAD + tile, CONV_DIM), F32),
            pltpu.VMEM((D_STATE, SSM_WIDTH), F32),
            pltpu.VMEM((WINDOW + tile, KV_W), F32), pltpu.VMEM((WINDOW + tile, KV_W), F32),
            pltpu.VMEM((WINDOW + tile, KV_W), BF16), pltpu.VMEM((WINDOW + tile, KV_W), BF16),
            pltpu.VMEM((WINDOW + tile, KV_W), BF16), pltpu.VMEM((KV_W, WINDOW + tile), BF16),
            pltpu.VMEM((tile, D_MODEL), BF16),
            pltpu.VMEM((tile, SSM_WIDTH), F32),
            pltpu.VMEM((tile, ATTN_WIDTH), F32),
            pltpu.VMEM((tile, LANES), F32),
            pltpu.VMEM((tile, SSM_WIDTH), F32),
            pltpu.VMEM((tile, CONV_DIM - SSM_WIDTH), BF16),
            pltpu.VMEM((CHUNK, SSM_WIDTH), F32),
            pltpu.VMEM((CHUNK, SSM_WIDTH), F32),
            pltpu.VMEM((tile, D_MODEL), BF16),
            pltpu.VMEM((CHUNK, LANES), BF16),
            pltpu.VMEM((KV_W, max(tile, LANES) + CHUNK), BF16)]


_STATE_SHAPES = ((CONV_W - 1, CONV_DIM), (SSM_WIDTH, D_STATE), (KV_W, WINDOW), (KV_W, WINDOW))


def _attention_tables(attn_sinks):
    slopes = np.exp2(-8.0 * np.arange(1, ATTN_HEADS + 1, dtype=np.float64) / ATTN_HEADS)
    qi = np.arange(CHUNK)[None, :] + WINDOW
    kj = np.arange(BAND)[:, None]
    dist = np.abs(qi - kj).astype(np.float64)
    key_chunk = kj // CHUNK
    tables = np.zeros((HIST_CHUNKS + 1, 2, BAND, STACK_ROWS), np.float32)
    for variant in range(HIST_CHUNKS + 1):
        valid = key_chunk >= (HIST_CHUNKS - variant)
        for s_idx, heads in enumerate((STACK_A_HEADS, STACK_B_HEADS)):
            for i, h in enumerate(heads):
                tables[variant, s_idx, :, i * CHUNK:(i + 1) * CHUNK] = np.where(
                    valid, -slopes[h] * dist * LOG2E, -np.inf)
    order = np.array(STACK_A_HEADS + STACK_B_HEADS)
    sink_rows = jnp.repeat(attn_sinks.astype(F32)[order] * LOG2E, CHUNK).reshape(2, STACK_ROWS)
    return jnp.asarray(tables), sink_rows


def _expansion_matrix():
    e = np.zeros((EXPAND_K, 2 * SSM_WIDTH), np.float32)
    for part in range(6):
        for h in range(SSM_HEADS):
            c0 = (part // 3) * SSM_WIDTH + h * HEAD_DIM
            e[part * SSM_HEADS + h, c0:c0 + HEAD_DIM] = 1.0
    return jnp.asarray(e, BF16)


def _prompt_call(x, mixer_ops, ffn_ops):
    bsz, seq, _ = x.shape
    tile = PROMPT_TILE
    n_tiles = seq // tile
    n_total = bsz * n_tiles

    def mixer_tile_idx(g):
        gc = jnp.minimum(g, n_total - 1)
        return lax.div(gc, n_tiles), lax.rem(gc, n_tiles)

    def next_tile_idx(g):
        return mixer_tile_idx(g + 1)

    def ffn_tile_idx(g):
        gp = jnp.maximum(g - 1, 0)
        return lax.div(gp, n_tiles), lax.rem(gp, n_tiles)

    tile_spec = lambda idx: pl.BlockSpec((1, tile, D_MODEL), lambda g: (*idx(g), 0))
    stream_spec = lambda shape: pl.BlockSpec((1,) + shape, lambda g: (mixer_tile_idx(g)[0], 0, 0))
    in_specs = ([tile_spec(mixer_tile_idx), tile_spec(next_tile_idx)]
                + [_const_spec(w.shape) for w in mixer_ops + ffn_ops])
    out_specs = [tile_spec(ffn_tile_idx)] + [stream_spec(s) for s in _STATE_SHAPES]
    out_shape = ([jax.ShapeDtypeStruct((bsz, seq, D_MODEL), F32)]
                 + [jax.ShapeDtypeStruct((bsz,) + s, F32) for s in _STATE_SHAPES])
    scratch = _mixer_scratch(tile) + [pltpu.VMEM((tile, D_MODEL), F32), pltpu.VMEM((tile, D_MODEL), BF16),
                                      pltpu.VMEM((tile, D_FF), BF16)]
    return pl.pallas_call(
        functools.partial(_fused_kernel, tile, n_tiles, n_total),
        grid=(n_total + 1,), in_specs=in_specs, out_specs=out_specs, out_shape=out_shape,
        scratch_shapes=scratch, name="prompt_layer",
        compiler_params=pltpu.CompilerParams(dimension_semantics=("arbitrary",), vmem_limit_bytes=VMEM_LIMIT),
    )(x, x, *mixer_ops, *ffn_ops)


def _sample_call(x, state0, mixer_ops, ffn_ops):
    n_streams, seq, _ = x.shape
    assert n_streams % STREAMS_PER_STEP == 0
    total = n_streams * seq
    stream_spec = lambda shape: pl.BlockSpec((STREAMS_PER_STEP,) + shape, lambda b: (b, 0, 0))
    in_specs = ([_const_spec((total, D_MODEL))] + [stream_spec(s) for s in _STATE_SHAPES]
                + [_const_spec(w.shape) for w in mixer_ops + ffn_ops])
    out_specs = ([pl.BlockSpec((total, D_MODEL), lambda b: (0, 0))]
                 + [stream_spec(s) for s in _STATE_SHAPES])
    out_shape = ([jax.ShapeDtypeStruct((total, D_MODEL), F32)]
                 + [jax.ShapeDtypeStruct((n_streams,) + s, F32) for s in _STATE_SHAPES])
    all_rows = [pltpu.VMEM((CONV_PAD + total, CONV_DIM), F32),
                pltpu.VMEM((WINDOW + total, KV_W), F32), pltpu.VMEM((WINDOW + total, KV_W), F32),
                pltpu.VMEM((WINDOW + total, KV_W), BF16), pltpu.VMEM((WINDOW + total, KV_W), BF16),
                pltpu.VMEM((WINDOW + total, KV_W), BF16),
                pltpu.VMEM((total, D_MODEL), BF16),
                pltpu.VMEM((total, SSM_WIDTH), F32), pltpu.VMEM((total, ATTN_WIDTH), F32),
                pltpu.VMEM((total, LANES), F32),
                pltpu.VMEM((total, D_MODEL), BF16),
                pltpu.VMEM((total, D_MODEL), F32), pltpu.VMEM((total, D_MODEL), BF16),
                pltpu.VMEM((total, D_FF), BF16)]
    y, conv, ssm, k, v = pl.pallas_call(
        functools.partial(_sample_kernel, n_streams, seq),
        grid=(n_streams // STREAMS_PER_STEP,), in_specs=in_specs, out_specs=out_specs, out_shape=out_shape,
        scratch_shapes=all_rows + STREAMS_PER_STEP * _mixer_scratch(seq), name="sample_layer",
        compiler_params=pltpu.CompilerParams(dimension_semantics=("arbitrary",), vmem_limit_bytes=VMEM_LIMIT),
    )(x.reshape(total, D_MODEL), *state0, *mixer_ops, *ffn_ops)
    return y.reshape(n_streams, seq, D_MODEL), conv, ssm, k, v


IN_COLS = COL_END + 8
W_IN_PREP_ROWS = 256


def _w_in_kernel(w_t_ref, out_ref):
    src = {"z": 0, "xbc": SSM_WIDTH, "dt": SSM_WIDTH + CONV_DIM}
    src["q"] = src["dt"] + SSM_HEADS
    src["k"] = src["q"] + ATTN_WIDTH
    src["v"] = src["k"] + KV_W
    pieces = (("z", COL_Z, COL_XBC - COL_Z, 1.0), ("xbc", COL_XBC, COL_Q - COL_XBC, 1.0),
              ("q", COL_Q, COL_K - COL_Q, HEAD_DIM ** -0.5 * LOG2E), ("k", COL_K, KV_W, 1.0), ("v", COL_V, KV_W, 1.0))
    for name, dst, width, scale in pieces:
        for off in range(0, width, W_IN_PREP_ROWS):
            rows = min(W_IN_PREP_ROWS, width - off)
            blk = w_t_ref[src[name] + off:src[name] + off + rows, :]
            if scale != 1.0:
                blk = blk * scale
            out_ref[:, dst + off:dst + off + rows] = blk.astype(BF16).T
    w_dt = w_t_ref[src["dt"]:src["dt"] + SSM_HEADS, :]
    w_dt_rep = jnp.concatenate([w_dt] * (LANES // SSM_HEADS), axis=0)
    out_ref[:, COL_END:COL_END + LANES] = w_dt_rep.astype(BF16).T


def _prepare_w_in(w_t):
    return pl.pallas_call(
        _w_in_kernel, grid=(1,),
        in_specs=[_const_spec((IN_COLS, D_MODEL))], out_specs=pl.BlockSpec((D_MODEL, COL_END + LANES), lambda i: (0, 0)),
        out_shape=jax.ShapeDtypeStruct((D_MODEL, COL_END + LANES), BF16), name="w_in_prep",
        compiler_params=pltpu.CompilerParams(dimension_semantics=("arbitrary",), vmem_limit_bytes=VMEM_LIMIT),
    )(w_t)


CAST_STEPS = 8


def _cast_kernel(*refs):
    n = len(refs) // 2
    for src, dst in zip(refs[:n], refs[n:]):
        dst[...] = src[...].astype(BF16)


def _cast_weights(*mats):
    specs = [pl.BlockSpec((m.shape[0] // CAST_STEPS, m.shape[1]), lambda i: (i, 0)) for m in mats]
    return pl.pallas_call(
        _cast_kernel, grid=(CAST_STEPS,), in_specs=specs, out_specs=specs,
        out_shape=[jax.ShapeDtypeStruct(m.shape, BF16) for m in mats], name="weight_cast",
        compiler_params=pltpu.CompilerParams(dimension_semantics=("arbitrary",), vmem_limit_bytes=VMEM_LIMIT),
    )(*mats)


def _window_to_kernel_layout(cache):
    b = cache.shape[0]
    return jnp.transpose(cache.astype(F32).reshape(b, WINDOW, KV_W), (0, 2, 1))


def _window_from_kernel_layout(win_t):
    b = win_t.shape[0]
    return jnp.transpose(win_t, (0, 2, 1)).reshape(1, b, WINDOW, KV_HEADS, HEAD_DIM)


def kernel(x_prompt, x_sample, state_conv, state_ssm, cache_k, cache_v, w_in, conv_w, conv_b, dt_bias, a_log,
           d_skip, ssm_norm_w, attn_sinks, w_out, ln1_g, ln1_b, w_gate, w_up, w_down, ln2_g, ln2_b):
    w_in_r = _prepare_w_in(jnp.transpose(w_in[0]))
    rep = LANES // SSM_HEADS
    bias_tables, sink_rows = _attention_tables(attn_sinks[0])
    vec_parts = {"conv_b": conv_b[0], "dt_bias": jnp.tile(dt_bias[0], rep), "a_log": jnp.tile(a_log[0], rep),
                 "d_skip": jnp.repeat(d_skip[0], HEAD_DIM), "norm_w": ssm_norm_w[0],
                 "sink_a": sink_rows[0], "sink_b": sink_rows[1],
                 "ln1_g": ln1_g[0], "ln1_b": ln1_b[0], "ln2_g": ln2_g[0], "ln2_b": ln2_b[0]}
    vec_parts.update({"conv_w%d" % i: conv_w[0, i] for i in range(CONV_W)})
    vecs = jnp.concatenate([vec_parts[name].astype(F32).reshape(width) for name, width in VEC_LAYOUT])
    vecs = vecs.reshape(1, VEC_WIDTH)
    w_out_b, *ffn_mats = _cast_weights(w_out[0].astype(F32), w_gate[0].astype(F32), w_up[0].astype(F32),
                                       w_down[0].astype(F32))
    mixer_ops = (w_in_r, vecs, _expansion_matrix(), bias_tables, w_out_b)
    ffn_mats = tuple(ffn_mats)

    dbsz = x_sample.shape[0]

    y_p, conv_p, ssm_p, k_p, v_p = _prompt_call(x_prompt, mixer_ops, ffn_mats)

    sample_state = (state_conv[0].astype(F32), state_ssm[0].astype(F32).reshape(dbsz, SSM_WIDTH, D_STATE),
                    _window_to_kernel_layout(cache_k[0]), _window_to_kernel_layout(cache_v[0]))
    y_s, conv_s, ssm_s, k_s, v_s = _sample_call(x_sample, sample_state, mixer_ops, ffn_mats)

    ssm_shape = lambda a: a.reshape(1, a.shape[0], SSM_HEADS, HEAD_DIM, D_STATE)
    kv_shape = _window_from_kernel_layout
    return (y_p, y_s,
            conv_p[None], ssm_shape(ssm_p), kv_shape(k_p), kv_shape(v_p),
            conv_s[None], ssm_shape(ssm_s), kv_shape(k_s), kv_shape(v_s))
```
